```python
import jax, jax.numpy as jnp
from jax import lax
import numpy as np

D_MODEL = 2048
BATCH = 8
SEQ = 8192
DEPTH = 2

GRID_W = 64
CTX_LEN = 256
N_MOD = 6
EPS = 1e-6
D_CONV = D_MODEL // 2
CONV_WIDTH = 31
D_POOL = D_MODEL // 2
POOL_WINDOWS = (2, 4, 8, 16)
N_POOL_GROUPS = len(POOL_WINDOWS)
POOL_GROUP = D_POOL // N_POOL_GROUPS
D_IN0 = 2 * D_CONV + D_POOL
HEAD_DIM = 128
N_HEADS = D_MODEL // HEAD_DIM
N_KV_HEADS = 4
GROUP = N_HEADS // N_KV_HEADS
D_Q = N_HEADS * HEAD_DIM
D_KV = N_KV_HEADS * HEAD_DIM
Q_BLOCK = 128
ROPE_AXIS = HEAD_DIM // 2
ROPE_THETA = 10000.0
ATTN_SCALE = HEAD_DIM ** -0.5
D_FF = 4 * D_MODEL

kernel_name = "hybrid_conv_pool_gqa_diffusion_block"


def _rmsnorm(x, g=None):
    xf = x.astype(jnp.float32)
    y = (xf * lax.rsqrt(jnp.mean(xf * xf, axis=-1, keepdims=True) + EPS)).astype(x.dtype)
    return y if g is None else y * g


def _layernorm(x, g, b):
    xf = x.astype(jnp.float32)
    mu = jnp.mean(xf, axis=-1, keepdims=True)
    var = jnp.mean(jnp.square(xf - mu), axis=-1, keepdims=True)
    return ((xf - mu) * lax.rsqrt(var + EPS)).astype(x.dtype) * g + b


def _adaln(cond, w, b):
    return jnp.split(jax.nn.silu(cond) @ w + b, N_MOD, axis=-1)


def _modulate(h, shift, scale):
    return h * (1 + scale) + shift


def _conformer_conv(a, gate, conv_w, conv_b, ln_g, ln_b):
    u = a * jax.nn.sigmoid(gate)
    u = lax.conv_general_dilated(
        u, conv_w[:, None, :], window_strides=(1,),
        padding=((CONV_WIDTH // 2, CONV_WIDTH // 2),),
        dimension_numbers=("NWC", "WIO", "NWC"),
        feature_group_count=D_CONV) + conv_b
    return jax.nn.silu(_layernorm(u, ln_g, ln_b))


def _pool_mixer(u, pool_w, pool_scale):
    b, n, _ = u.shape
    uf = u.astype(jnp.float32)
    cs = jnp.concatenate([jnp.zeros((b, 1, D_POOL), jnp.float32), jnp.cumsum(uf, axis=1)], axis=1)
    t = jnp.arange(n)
    means = []
    for g, w in enumerate(POOL_WINDOWS):
        lo = jnp.maximum(t - w // 2, 0)
        hi = jnp.minimum(t - w // 2 + w, n)
        cs_g = cs[..., g * POOL_GROUP:(g + 1) * POOL_GROUP]
        cnt = (hi - lo).astype(jnp.float32)[None, :, None]
        means.append((jnp.take(cs_g, hi, axis=1) - jnp.take(cs_g, lo, axis=1)) / cnt)
    d = (jnp.concatenate(means, axis=-1) - uf).astype(u.dtype)
    d = d.reshape(b, n, N_POOL_GROUPS, POOL_GROUP)
    y = jnp.einsum("bngc,gcd->bngd", d, pool_w).reshape(b, n, D_POOL)
    return y * pool_scale


def _conv_pool_mixer(h, in_w, conv_w, conv_b, ln_g, ln_b, pool_w, pool_scale, out_w):
    a, gate, u = jnp.split(h @ in_w, [D_CONV, 2 * D_CONV], axis=-1)
    y = jnp.concatenate([_conformer_conv(a, gate, conv_w, conv_b, ln_g, ln_b),
                         _pool_mixer(u, pool_w, pool_scale)], axis=-1)
    return y @ out_w


def _rope_axis(xa, pos):
    half = ROPE_AXIS // 2
    freqs = ROPE_THETA ** (-jnp.arange(half, dtype=jnp.float32) / half)
    ang = pos[:, None] * freqs[None, :]
    cos = jnp.cos(ang)[None, :, None, :]
    sin = jnp.sin(ang)[None, :, None, :]
    x1 = xa[..., :half].astype(jnp.float32)
    x2 = xa[..., half:].astype(jnp.float32)
    return jnp.concatenate([x1 * cos - x2 * sin, x1 * sin + x2 * cos], axis=-1).astype(xa.dtype)


def _rope_2d(x, pos_row, pos_col):
    return jnp.concatenate([_rope_axis(x[..., :ROPE_AXIS], pos_row),
                            _rope_axis(x[..., ROPE_AXIS:], pos_col)], axis=-1)


def _gqa_block(qb, k, v):
    s = jnp.einsum("bkgqd,bksd->bkgqs", qb, k, preferred_element_type=jnp.float32) * ATTN_SCALE
    p = jax.nn.softmax(s, axis=-1).astype(v.dtype)
    return jnp.einsum("bkgqs,bksd->bkgqd", p, v)


def _attention_mixer(h, hc, pos_row, pos_col, qkv_w, q_norm_g, k_norm_g, out_w, with_ctx_out):
    b, n, _ = h.shape
    nc = hc.shape[1]
    q, k, v = jnp.split(h @ qkv_w, [D_Q, D_Q + D_KV], axis=-1)
    q = _rope_2d(_rmsnorm(q.reshape(b, n, N_HEADS, HEAD_DIM), q_norm_g), pos_row, pos_col)
    k = _rope_2d(_rmsnorm(k.reshape(b, n, N_KV_HEADS, HEAD_DIM), k_norm_g), pos_row, pos_col)
    v = v.reshape(b, n, N_KV_HEADS, HEAD_DIM)
    if with_ctx_out:
        qc, kc, vc = jnp.split(hc @ qkv_w, [D_Q, D_Q + D_KV], axis=-1)
    else:
        kc, vc = jnp.split(hc @ qkv_w[:, D_Q:], [D_KV], axis=-1)
    kc = _rmsnorm(kc.reshape(b, nc, N_KV_HEADS, HEAD_DIM), k_norm_g)
    vc = vc.reshape(b, nc, N_KV_HEADS, HEAD_DIM)
    k_all = jnp.concatenate([kc, k], axis=1).transpose(0, 2, 1, 3)
    v_all = jnp.concatenate([vc, v], axis=1).transpose(0, 2, 1, 3)
    nb = n // Q_BLOCK
    qb = q.reshape(b, nb, Q_BLOCK, N_KV_HEADS, GROUP, HEAD_DIM).transpose(1, 0, 3, 4, 2, 5)
    o = lax.map(lambda blk: _gqa_block(blk, k_all, v_all), qb)
    o = o.transpose(1, 0, 4, 2, 3, 5).reshape(b, n, D_Q)
    y = o @ out_w
    if with_ctx_out:
        qc = _rmsnorm(qc.reshape(b, nc, N_KV_HEADS, GROUP, HEAD_DIM), q_norm_g).transpose(0, 2, 3, 1, 4)
        oc = _gqa_block(qc, kc.transpose(0, 2, 1, 3), vc.transpose(0, 2, 1, 3))
        yc = oc.transpose(0, 3, 1, 2, 4).reshape(b, nc, D_Q) @ out_w
        return y, yc
    return y, None


def _sqrelu_mlp(h, w1, w2):
    return jnp.square(jax.nn.relu(h @ w1)) @ w2


def _fwd_setup_inputs(seed: int = 0) -> dict:
    key = jax.random.key(seed)
    ks = jax.random.split(key, 32)

    def nrm(k, shape, s):
        return jax.random.normal(k, shape, jnp.float32) * s

    return {
        "x": nrm(ks[0], (BATCH, SEQ, D_MODEL), 1.0),
        "c": nrm(ks[1], (BATCH, D_MODEL), 1.0),
        "ctx": nrm(ks[2], (BATCH, CTX_LEN, D_MODEL), 1.0),
        "c_ctx": nrm(ks[3], (D_MODEL,), 1.0),
        "l0_ada_w": nrm(ks[4], (D_MODEL, N_MOD * D_MODEL), 0.5 * D_MODEL ** -0.5),
        "l0_ada_b": nrm(ks[5], (N_MOD * D_MODEL,), 0.02),
        "l0_in_w": nrm(ks[6], (D_MODEL, D_IN0), D_MODEL ** -0.5),
        "l0_conv_w": nrm(ks[7], (CONV_WIDTH, D_CONV), CONV_WIDTH ** -0.5),
        "l0_conv_b": nrm(ks[8], (D_CONV,), 0.02),
        "l0_conv_ln_g": 1.0 + nrm(ks[9], (D_CONV,), 0.05),
        "l0_conv_ln_b": nrm(ks[10], (D_CONV,), 0.02),
        "l0_pool_w": nrm(ks[11], (N_POOL_GROUPS, POOL_GROUP, POOL_GROUP), POOL_GROUP ** -0.5),
        "l0_pool_scale": 1.0 + nrm(ks[12], (D_POOL,), 0.1),
        "l0_out_w": nrm(ks[13], (D_CONV + D_POOL, D_MODEL), (D_CONV + D_POOL) ** -0.5),
        "l0_mlp_w1": nrm(ks[14], (D_MODEL, D_FF), D_MODEL ** -0.5),
        "l0_mlp_w2": nrm(ks[15], (D_FF, D_MODEL), D_FF ** -0.5),
        "l1_ada_w": nrm(ks[16], (D_MODEL, N_MOD * D_MODEL), 0.5 * D_MODEL ** -0.5),
        "l1_ada_b": nrm(ks[17], (N_MOD * D_MODEL,), 0.02),
        "l1_qkv_w": nrm(ks[18], (D_MODEL, D_Q + 2 * D_KV), D_MODEL ** -0.5),
        "l1_q_norm_g": 1.0 + nrm(ks[19], (HEAD_DIM,), 0.05),
        "l1_k_norm_g": 1.0 + nrm(ks[20], (HEAD_DIM,), 0.05),
        "l1_out_w": nrm(ks[21], (D_Q, D_MODEL), D_Q ** -0.5),
        "l1_mlp_w1": nrm(ks[22], (D_MODEL, D_FF), D_MODEL ** -0.5),
        "l1_mlp_w2": nrm(ks[23], (D_FF, D_MODEL), D_FF ** -0.5),
        "final_g": 1.0 + nrm(ks[24], (D_MODEL,), 0.05),
    }


def _fwd_reference(x, c, ctx, c_ctx,
              l0_ada_w, l0_ada_b, l0_in_w, l0_conv_w, l0_conv_b, l0_conv_ln_g, l0_conv_ln_b,
              l0_pool_w, l0_pool_scale, l0_out_w, l0_mlp_w1, l0_mlp_w2,
              l1_ada_w, l1_ada_b, l1_qkv_w, l1_q_norm_g, l1_k_norm_g, l1_out_w,
              l1_mlp_w1, l1_mlp_w2, final_g):
    n = x.shape[1]
    rows = n // GRID_W
    pos_row = jnp.repeat(jnp.arange(rows), GRID_W).astype(jnp.float32)
    pos_col = jnp.tile(jnp.arange(GRID_W), rows).astype(jnp.float32)

    layers = (
        (l0_ada_w, l0_ada_b,
         (l0_in_w, l0_conv_w, l0_conv_b, l0_conv_ln_g, l0_conv_ln_b, l0_pool_w, l0_pool_scale, l0_out_w),
         l0_mlp_w1, l0_mlp_w2),
        (l1_ada_w, l1_ada_b,
         (l1_qkv_w, l1_q_norm_g, l1_k_norm_g, l1_out_w),
         l1_mlp_w1, l1_mlp_w2),
    )

    for i in range(DEPTH):
        ada_w, ada_b, mix_p, w1, w2 = layers[i]
        last = i == DEPTH - 1
        sh1, sc1, g1, sh2, sc2, g2 = [m[:, None, :] for m in _adaln(c, ada_w, ada_b)]
        csh1, csc1, cg1, csh2, csc2, cg2 = _adaln(c_ctx, ada_w, ada_b)
        h = _modulate(_rmsnorm(x), sh1, sc1)
        hc = _modulate(_rmsnorm(ctx), csh1, csc1)
        if i % 2 == 0:
            y = _conv_pool_mixer(h, *mix_p)
            yc = None if last else _conv_pool_mixer(hc, *mix_p)
        else:
            y, yc = _attention_mixer(h, hc, pos_row, pos_col, *mix_p, with_ctx_out=not last)
        x = x + g1 * y
        x = x + g2 * _sqrelu_mlp(_modulate(_rmsnorm(x), sh2, sc2), w1, w2)
        if not last:
            ctx = ctx + cg1 * yc
            ctx = ctx + cg2 * _sqrelu_mlp(_modulate(_rmsnorm(ctx), csh2, csc2), w1, w2)

    return _rmsnorm(x, final_g)


import jax as _jax
import jax.numpy as _jnp

TWIN_FORMAT = 'train_step'
FWD_PARAMS = ['x', 'c', 'ctx', 'c_ctx', 'l0_ada_w', 'l0_ada_b', 'l0_in_w', 'l0_conv_w', 'l0_conv_b', 'l0_conv_ln_g', 'l0_conv_ln_b', 'l0_pool_w', 'l0_pool_scale', 'l0_out_w', 'l0_mlp_w1', 'l0_mlp_w2', 'l1_ada_w', 'l1_ada_b', 'l1_qkv_w', 'l1_q_norm_g', 'l1_k_norm_g', 'l1_out_w', 'l1_mlp_w1', 'l1_mlp_w2', 'final_g']
TWIN_WEIGHTS = ['c_ctx', 'l0_ada_w', 'l0_ada_b', 'l0_in_w', 'l0_conv_w', 'l0_conv_b', 'l0_conv_ln_g', 'l0_conv_ln_b', 'l0_pool_w', 'l0_pool_scale', 'l0_out_w', 'l0_mlp_w1', 'l0_mlp_w2', 'l1_ada_w', 'l1_ada_b', 'l1_qkv_w', 'l1_q_norm_g', 'l1_k_norm_g', 'l1_out_w', 'l1_mlp_w1', 'l1_mlp_w2', 'final_g']
TWIN_DIFF_INPUT = 'x'
TWIN_INPUTS = ['x', 'c', 'ctx', 'c_ctx', 'l0_ada_w', 'l0_ada_b', 'l0_in_w', 'l0_conv_w', 'l0_conv_b', 'l0_conv_ln_g', 'l0_conv_ln_b', 'l0_pool_w', 'l0_pool_scale', 'l0_out_w', 'l0_mlp_w1', 'l0_mlp_w2', 'l1_ada_w', 'l1_ada_b', 'l1_qkv_w', 'l1_q_norm_g', 'l1_k_norm_g', 'l1_out_w', 'l1_mlp_w1', 'l1_mlp_w2', 'final_g', 'loss_target', 'm_c_ctx', 'm_l0_ada_w', 'm_l0_ada_b', 'm_l0_in_w', 'm_l0_conv_w', 'm_l0_conv_b', 'm_l0_conv_ln_g', 'm_l0_conv_ln_b', 'm_l0_pool_w', 'm_l0_pool_scale', 'm_l0_out_w', 'm_l0_mlp_w1', 'm_l0_mlp_w2', 'm_l1_ada_w', 'm_l1_ada_b', 'm_l1_qkv_w', 'm_l1_q_norm_g', 'm_l1_k_norm_g', 'm_l1_out_w', 'm_l1_mlp_w1', 'm_l1_mlp_w2', 'm_final_g', 'v_c_ctx', 'v_l0_ada_w', 'v_l0_ada_b', 'v_l0_in_w', 'v_l0_conv_w', 'v_l0_conv_b', 'v_l0_conv_ln_g', 'v_l0_conv_ln_b', 'v_l0_pool_w', 'v_l0_pool_scale', 'v_l0_out_w', 'v_l0_mlp_w1', 'v_l0_mlp_w2', 'v_l1_ada_w', 'v_l1_ada_b', 'v_l1_qkv_w', 'v_l1_q_norm_g', 'v_l1_k_norm_g', 'v_l1_out_w', 'v_l1_mlp_w1', 'v_l1_mlp_w2', 'v_final_g']
TWIN_OUTPUTS = ['loss', 'grad_x', 'grad_c_ctx', 'grad_l0_ada_w', 'grad_l0_ada_b', 'grad_l0_in_w', 'grad_l0_conv_w', 'grad_l0_conv_b', 'grad_l0_conv_ln_g', 'grad_l0_conv_ln_b', 'grad_l0_pool_w', 'grad_l0_pool_scale', 'grad_l0_out_w', 'grad_l0_mlp_w1', 'grad_l0_mlp_w2', 'grad_l1_ada_w', 'grad_l1_ada_b', 'grad_l1_qkv_w', 'grad_l1_q_norm_g', 'grad_l1_k_norm_g', 'grad_l1_out_w', 'grad_l1_mlp_w1', 'grad_l1_mlp_w2', 'grad_final_g', 'delta_c_ctx', 'delta_l0_ada_w', 'delta_l0_ada_b', 'delta_l0_in_w', 'delta_l0_conv_w', 'delta_l0_conv_b', 'delta_l0_conv_ln_g', 'delta_l0_conv_ln_b', 'delta_l0_pool_w', 'delta_l0_pool_scale', 'delta_l0_out_w', 'delta_l0_mlp_w1', 'delta_l0_mlp_w2', 'delta_l1_ada_w', 'delta_l1_ada_b', 'delta_l1_qkv_w', 'delta_l1_q_norm_g', 'delta_l1_k_norm_g', 'delta_l1_out_w', 'delta_l1_mlp_w1', 'delta_l1_mlp_w2', 'delta_final_g', 'new_m_c_ctx', 'new_m_l0_ada_w', 'new_m_l0_ada_b', 'new_m_l0_in_w', 'new_m_l0_conv_w', 'new_m_l0_conv_b', 'new_m_l0_conv_ln_g', 'new_m_l0_conv_ln_b', 'new_m_l0_pool_w', 'new_m_l0_pool_scale', 'new_m_l0_out_w', 'new_m_l0_mlp_w1', 'new_m_l0_mlp_w2', 'new_m_l1_ada_w', 'new_m_l1_ada_b', 'new_m_l1_qkv_w', 'new_m_l1_q_norm_g', 'new_m_l1_k_norm_g', 'new_m_l1_out_w', 'new_m_l1_mlp_w1', 'new_m_l1_mlp_w2', 'new_m_final_g', 'new_v_c_ctx', 'new_v_l0_ada_w', 'new_v_l0_ada_b', 'new_v_l0_in_w', 'new_v_l0_conv_w', 'new_v_l0_conv_b', 'new_v_l0_conv_ln_g', 'new_v_l0_conv_ln_b', 'new_v_l0_pool_w', 'new_v_l0_pool_scale', 'new_v_l0_out_w', 'new_v_l0_mlp_w1', 'new_v_l0_mlp_w2', 'new_v_l1_ada_w', 'new_v_l1_ada_b', 'new_v_l1_qkv_w', 'new_v_l1_q_norm_g', 'new_v_l1_k_norm_g', 'new_v_l1_out_w', 'new_v_l1_mlp_w1', 'new_v_l1_mlp_w2', 'new_v_final_g']
TWIN_LEAF_KINDS = {'loss': 'loss', 'grad_x': 'grad_x', 'grad_c_ctx': 'grad_w', 'grad_l0_ada_w': 'grad_w', 'grad_l0_ada_b': 'grad_w', 'grad_l0_in_w': 'grad_w', 'grad_l0_conv_w': 'grad_w', 'grad_l0_conv_b': 'grad_w', 'grad_l0_conv_ln_g': 'grad_w', 'grad_l0_conv_ln_b': 'grad_w', 'grad_l0_pool_w': 'grad_w', 'grad_l0_pool_scale': 'grad_w', 'grad_l0_out_w': 'grad_w', 'grad_l0_mlp_w1': 'grad_w', 'grad_l0_mlp_w2': 'grad_w', 'grad_l1_ada_w': 'grad_w', 'grad_l1_ada_b': 'grad_w', 'grad_l1_qkv_w': 'grad_w', 'grad_l1_q_norm_g': 'grad_w', 'grad_l1_k_norm_g': 'grad_w', 'grad_l1_out_w': 'grad_w', 'grad_l1_mlp_w1': 'grad_w', 'grad_l1_mlp_w2': 'grad_w', 'grad_final_g': 'grad_w', 'delta_c_ctx': 'delta_w', 'delta_l0_ada_w': 'delta_w', 'delta_l0_ada_b': 'delta_w', 'delta_l0_in_w': 'delta_w', 'delta_l0_conv_w': 'delta_w', 'delta_l0_conv_b': 'delta_w', 'delta_l0_conv_ln_g': 'delta_w', 'delta_l0_conv_ln_b': 'delta_w', 'delta_l0_pool_w': 'delta_w', 'delta_l0_pool_scale': 'delta_w', 'delta_l0_out_w': 'delta_w', 'delta_l0_mlp_w1': 'delta_w', 'delta_l0_mlp_w2': 'delta_w', 'delta_l1_ada_w': 'delta_w', 'delta_l1_ada_b': 'delta_w', 'delta_l1_qkv_w': 'delta_w', 'delta_l1_q_norm_g': 'delta_w', 'delta_l1_k_norm_g': 'delta_w', 'delta_l1_out_w': 'delta_w', 'delta_l1_mlp_w1': 'delta_w', 'delta_l1_mlp_w2': 'delta_w', 'delta_final_g': 'delta_w', 'new_m_c_ctx': 'new_m', 'new_m_l0_ada_w': 'new_m', 'new_m_l0_ada_b': 'new_m', 'new_m_l0_in_w': 'new_m', 'new_m_l0_conv_w': 'new_m', 'new_m_l0_conv_b': 'new_m', 'new_m_l0_conv_ln_g': 'new_m', 'new_m_l0_conv_ln_b': 'new_m', 'new_m_l0_pool_w': 'new_m', 'new_m_l0_pool_scale': 'new_m', 'new_m_l0_out_w': 'new_m', 'new_m_l0_mlp_w1': 'new_m', 'new_m_l0_mlp_w2': 'new_m', 'new_m_l1_ada_w': 'new_m', 'new_m_l1_ada_b': 'new_m', 'new_m_l1_qkv_w': 'new_m', 'new_m_l1_q_norm_g': 'new_m', 'new_m_l1_k_norm_g': 'new_m', 'new_m_l1_out_w': 'new_m', 'new_m_l1_mlp_w1': 'new_m', 'new_m_l1_mlp_w2': 'new_m', 'new_m_final_g': 'new_m', 'new_v_c_ctx': 'new_v', 'new_v_l0_ada_w': 'new_v', 'new_v_l0_ada_b': 'new_v', 'new_v_l0_in_w': 'new_v', 'new_v_l0_conv_w': 'new_v', 'new_v_l0_conv_b': 'new_v', 'new_v_l0_conv_ln_g': 'new_v', 'new_v_l0_conv_ln_b': 'new_v', 'new_v_l0_pool_w': 'new_v', 'new_v_l0_pool_scale': 'new_v', 'new_v_l0_out_w': 'new_v', 'new_v_l0_mlp_w1': 'new_v', 'new_v_l0_mlp_w2': 'new_v', 'new_v_l1_ada_w': 'new_v', 'new_v_l1_ada_b': 'new_v', 'new_v_l1_qkv_w': 'new_v', 'new_v_l1_q_norm_g': 'new_v', 'new_v_l1_k_norm_g': 'new_v', 'new_v_l1_out_w': 'new_v', 'new_v_l1_mlp_w1': 'new_v', 'new_v_l1_mlp_w2': 'new_v', 'new_v_final_g': 'new_v'}


def _forward(args):
    return _fwd_reference(*[args[k] for k in FWD_PARAMS])


def _output_shape():
    def fwd():
        inp = _fwd_setup_inputs(0)
        return _fwd_reference(*[inp[k] for k in FWD_PARAMS])
    out = _jax.eval_shape(fwd)
    return out.shape, out.dtype

N_MICROBATCH = 1
ADAM_LR = 0.001
ADAM_B1 = 0.9
ADAM_B2 = 0.999
ADAM_EPS = 1e-08
ADAM_WD = 0.01
ADAM_STEP = 10
PER_EXAMPLE_BATCH_AXIS = {'x': 0, 'c': 0, 'ctx': 0, 'loss_target': 0}
SHARED_INPUTS = []
_WEIGHT_DTYPES = {'c_ctx': _jnp.float32, 'l0_ada_w': _jnp.float32, 'l0_ada_b': _jnp.float32, 'l0_in_w': _jnp.float32, 'l0_conv_w': _jnp.float32, 'l0_conv_b': _jnp.float32, 'l0_conv_ln_g': _jnp.float32, 'l0_conv_ln_b': _jnp.float32, 'l0_pool_w': _jnp.float32, 'l0_pool_scale': _jnp.float32, 'l0_out_w': _jnp.float32, 'l0_mlp_w1': _jnp.float32, 'l0_mlp_w2': _jnp.float32, 'l1_ada_w': _jnp.float32, 'l1_ada_b': _jnp.float32, 'l1_qkv_w': _jnp.float32, 'l1_q_norm_g': _jnp.float32, 'l1_k_norm_g': _jnp.float32, 'l1_out_w': _jnp.float32, 'l1_mlp_w1': _jnp.float32, 'l1_mlp_w2': _jnp.float32, 'final_g': _jnp.float32}
MOMENT_SCALE = {'c_ctx': 7.177756e-03, 'l0_ada_w': 1.018874e-01, 'l0_ada_b': 2.138592e-01, 'l0_in_w': 2.525515e-02, 'l0_conv_w': 2.465902e-02, 'l0_conv_b': 5.914923e-02, 'l0_conv_ln_g': 3.751215e-02, 'l0_conv_ln_b': 3.581318e-02, 'l0_pool_w': 3.519592e-02, 'l0_pool_scale': 3.536531e-02, 'l0_out_w': 3.065594e-02, 'l0_mlp_w1': 2.972069e-02, 'l0_mlp_w2': 7.358098e-02, 'l1_ada_w': 9.581633e-02, 'l1_ada_b': 1.996569e-01, 'l1_qkv_w': 1.448460e-02, 'l1_q_norm_g': 1.107995e-02, 'l1_k_norm_g': 1.138871e-02, 'l1_out_w': 1.986076e-02, 'l1_mlp_w1': 2.778393e-02, 'l1_mlp_w2': 6.747337e-02, 'final_g': 3.228399e+01}


def _to_microbatches(a, axis):
    t = _jnp.moveaxis(a, axis, 0)
    t = t.reshape((N_MICROBATCH, t.shape[0] // N_MICROBATCH) + t.shape[1:])
    return _jnp.moveaxis(t, 1, axis + 1)


def setup_inputs(seed: int = 0) -> dict:
    inp = _fwd_setup_inputs(seed)
    key = _jax.random.fold_in(_jax.random.key(seed), 7919)
    shape, _ = _output_shape()
    out = dict(inp)
    out["loss_target"] = _jax.random.normal(_jax.random.fold_in(key, 0), shape, _jnp.float32)
    for i, name in enumerate(TWIN_WEIGHTS):
        w = inp[name].astype(_jnp.float32)
        if MOMENT_SCALE is None:
            s = _jnp.sqrt(_jnp.mean(_jnp.square(w)) + 1e-30)
        else:
            s = MOMENT_SCALE[name]
        km, kv = _jax.random.split(_jax.random.fold_in(key, i + 1))
        out[name] = w
        out["m_" + name] = s * _jax.random.normal(km, w.shape, _jnp.float32)
        out["v_" + name] = (s * s) * _jax.random.uniform(kv, w.shape, _jnp.float32, 0.5, 1.5)
    if N_MICROBATCH > 1:
        for name, axis in PER_EXAMPLE_BATCH_AXIS.items():
            out[name] = _to_microbatches(out[name], axis)
    return {'x': out['x'], 'c': out['c'], 'ctx': out['ctx'], 'c_ctx': out['c_ctx'], 'l0_ada_w': out['l0_ada_w'], 'l0_ada_b': out['l0_ada_b'], 'l0_in_w': out['l0_in_w'], 'l0_conv_w': out['l0_conv_w'], 'l0_conv_b': out['l0_conv_b'], 'l0_conv_ln_g': out['l0_conv_ln_g'], 'l0_conv_ln_b': out['l0_conv_ln_b'], 'l0_pool_w': out['l0_pool_w'], 'l0_pool_scale': out['l0_pool_scale'], 'l0_out_w': out['l0_out_w'], 'l0_mlp_w1': out['l0_mlp_w1'], 'l0_mlp_w2': out['l0_mlp_w2'], 'l1_ada_w': out['l1_ada_w'], 'l1_ada_b': out['l1_ada_b'], 'l1_qkv_w': out['l1_qkv_w'], 'l1_q_norm_g': out['l1_q_norm_g'], 'l1_k_norm_g': out['l1_k_norm_g'], 'l1_out_w': out['l1_out_w'], 'l1_mlp_w1': out['l1_mlp_w1'], 'l1_mlp_w2': out['l1_mlp_w2'], 'final_g': out['final_g'], 'loss_target': out['loss_target'], 'm_c_ctx': out['m_c_ctx'], 'm_l0_ada_w': out['m_l0_ada_w'], 'm_l0_ada_b': out['m_l0_ada_b'], 'm_l0_in_w': out['m_l0_in_w'], 'm_l0_conv_w': out['m_l0_conv_w'], 'm_l0_conv_b': out['m_l0_conv_b'], 'm_l0_conv_ln_g': out['m_l0_conv_ln_g'], 'm_l0_conv_ln_b': out['m_l0_conv_ln_b'], 'm_l0_pool_w': out['m_l0_pool_w'], 'm_l0_pool_scale': out['m_l0_pool_scale'], 'm_l0_out_w': out['m_l0_out_w'], 'm_l0_mlp_w1': out['m_l0_mlp_w1'], 'm_l0_mlp_w2': out['m_l0_mlp_w2'], 'm_l1_ada_w': out['m_l1_ada_w'], 'm_l1_ada_b': out['m_l1_ada_b'], 'm_l1_qkv_w': out['m_l1_qkv_w'], 'm_l1_q_norm_g': out['m_l1_q_norm_g'], 'm_l1_k_norm_g': out['m_l1_k_norm_g'], 'm_l1_out_w': out['m_l1_out_w'], 'm_l1_mlp_w1': out['m_l1_mlp_w1'], 'm_l1_mlp_w2': out['m_l1_mlp_w2'], 'm_final_g': out['m_final_g'], 'v_c_ctx': out['v_c_ctx'], 'v_l0_ada_w': out['v_l0_ada_w'], 'v_l0_ada_b': out['v_l0_ada_b'], 'v_l0_in_w': out['v_l0_in_w'], 'v_l0_conv_w': out['v_l0_conv_w'], 'v_l0_conv_b': out['v_l0_conv_b'], 'v_l0_conv_ln_g': out['v_l0_conv_ln_g'], 'v_l0_conv_ln_b': out['v_l0_conv_ln_b'], 'v_l0_pool_w': out['v_l0_pool_w'], 'v_l0_pool_scale': out['v_l0_pool_scale'], 'v_l0_out_w': out['v_l0_out_w'], 'v_l0_mlp_w1': out['v_l0_mlp_w1'], 'v_l0_mlp_w2': out['v_l0_mlp_w2'], 'v_l1_ada_w': out['v_l1_ada_w'], 'v_l1_ada_b': out['v_l1_ada_b'], 'v_l1_qkv_w': out['v_l1_qkv_w'], 'v_l1_q_norm_g': out['v_l1_q_norm_g'], 'v_l1_k_norm_g': out['v_l1_k_norm_g'], 'v_l1_out_w': out['v_l1_out_w'], 'v_l1_mlp_w1': out['v_l1_mlp_w1'], 'v_l1_mlp_w2': out['v_l1_mlp_w2'], 'v_final_g': out['v_final_g']}


def _loss(weights, diff, rest, loss_target):
    with _jax.named_scope("forward"):
        args = {**rest, TWIN_DIFF_INPUT: diff, **{k: w.astype(_WEIGHT_DTYPES[k]) for k, w in weights.items()}}
        y = _forward(args)
    with _jax.named_scope("loss_head"):
        err = _jnp.square(y.astype(_jnp.float32) - loss_target)
        return 0.5 * _jnp.sum(_jnp.mean(err, axis=-1)) if err.ndim else 0.5 * err


def _adamw(w, g, m, v):
    m = ADAM_B1 * m + (1.0 - ADAM_B1) * g
    v = ADAM_B2 * v + (1.0 - ADAM_B2) * _jnp.square(g)
    m_hat = m / (1.0 - ADAM_B1 ** ADAM_STEP)
    v_hat = v / (1.0 - ADAM_B2 ** ADAM_STEP)
    delta = -ADAM_LR * (m_hat / (_jnp.sqrt(v_hat) + ADAM_EPS) + ADAM_WD * w)
    return delta, m, v


def reference(x, c, ctx, c_ctx, l0_ada_w, l0_ada_b, l0_in_w, l0_conv_w, l0_conv_b, l0_conv_ln_g, l0_conv_ln_b, l0_pool_w, l0_pool_scale, l0_out_w, l0_mlp_w1, l0_mlp_w2, l1_ada_w, l1_ada_b, l1_qkv_w, l1_q_norm_g, l1_k_norm_g, l1_out_w, l1_mlp_w1, l1_mlp_w2, final_g, loss_target, m_c_ctx, m_l0_ada_w, m_l0_ada_b, m_l0_in_w, m_l0_conv_w, m_l0_conv_b, m_l0_conv_ln_g, m_l0_conv_ln_b, m_l0_pool_w, m_l0_pool_scale, m_l0_out_w, m_l0_mlp_w1, m_l0_mlp_w2, m_l1_ada_w, m_l1_ada_b, m_l1_qkv_w, m_l1_q_norm_g, m_l1_k_norm_g, m_l1_out_w, m_l1_mlp_w1, m_l1_mlp_w2, m_final_g, v_c_ctx, v_l0_ada_w, v_l0_ada_b, v_l0_in_w, v_l0_conv_w, v_l0_conv_b, v_l0_conv_ln_g, v_l0_conv_ln_b, v_l0_pool_w, v_l0_pool_scale, v_l0_out_w, v_l0_mlp_w1, v_l0_mlp_w2, v_l1_ada_w, v_l1_ada_b, v_l1_qkv_w, v_l1_q_norm_g, v_l1_k_norm_g, v_l1_out_w, v_l1_mlp_w1, v_l1_mlp_w2, v_final_g):
    given = dict(x=x, c=c, ctx=ctx, c_ctx=c_ctx, l0_ada_w=l0_ada_w, l0_ada_b=l0_ada_b, l0_in_w=l0_in_w, l0_conv_w=l0_conv_w, l0_conv_b=l0_conv_b, l0_conv_ln_g=l0_conv_ln_g, l0_conv_ln_b=l0_conv_ln_b, l0_pool_w=l0_pool_w, l0_pool_scale=l0_pool_scale, l0_out_w=l0_out_w, l0_mlp_w1=l0_mlp_w1, l0_mlp_w2=l0_mlp_w2, l1_ada_w=l1_ada_w, l1_ada_b=l1_ada_b, l1_qkv_w=l1_qkv_w, l1_q_norm_g=l1_q_norm_g, l1_k_norm_g=l1_k_norm_g, l1_out_w=l1_out_w, l1_mlp_w1=l1_mlp_w1, l1_mlp_w2=l1_mlp_w2, final_g=final_g, loss_target=loss_target, m_c_ctx=m_c_ctx, m_l0_ada_w=m_l0_ada_w, m_l0_ada_b=m_l0_ada_b, m_l0_in_w=m_l0_in_w, m_l0_conv_w=m_l0_conv_w, m_l0_conv_b=m_l0_conv_b, m_l0_conv_ln_g=m_l0_conv_ln_g, m_l0_conv_ln_b=m_l0_conv_ln_b, m_l0_pool_w=m_l0_pool_w, m_l0_pool_scale=m_l0_pool_scale, m_l0_out_w=m_l0_out_w, m_l0_mlp_w1=m_l0_mlp_w1, m_l0_mlp_w2=m_l0_mlp_w2, m_l1_ada_w=m_l1_ada_w, m_l1_ada_b=m_l1_ada_b, m_l1_qkv_w=m_l1_qkv_w, m_l1_q_norm_g=m_l1_q_norm_g, m_l1_k_norm_g=m_l1_k_norm_g, m_l1_out_w=m_l1_out_w, m_l1_mlp_w1=m_l1_mlp_w1, m_l1_mlp_w2=m_l1_mlp_w2, m_final_g=m_final_g, v_c_ctx=v_c_ctx, v_l0_ada_w=v_l0_ada_w, v_l0_ada_b=v_l0_ada_b, v_l0_in_w=v_l0_in_w, v_l0_conv_w=v_l0_conv_w, v_l0_conv_b=v_l0_conv_b, v_l0_conv_ln_g=v_l0_conv_ln_g, v_l0_conv_ln_b=v_l0_conv_ln_b, v_l0_pool_w=v_l0_pool_w, v_l0_pool_scale=v_l0_pool_scale, v_l0_out_w=v_l0_out_w, v_l0_mlp_w1=v_l0_mlp_w1, v_l0_mlp_w2=v_l0_mlp_w2, v_l1_ada_w=v_l1_ada_w, v_l1_ada_b=v_l1_ada_b, v_l1_qkv_w=v_l1_qkv_w, v_l1_q_norm_g=v_l1_q_norm_g, v_l1_k_norm_g=v_l1_k_norm_g, v_l1_out_w=v_l1_out_w, v_l1_mlp_w1=v_l1_mlp_w1, v_l1_mlp_w2=v_l1_mlp_w2, v_final_g=v_final_g)
    weights = {n: given[n] for n in TWIN_WEIGHTS}
    shared = {n: given[n] for n in SHARED_INPUTS}
    per_example = {n: given[n] for n in ['x', 'c', 'ctx']}
    grad_fn = _jax.value_and_grad(_loss, argnums=(0, 1))

    def one_microbatch(ex, loss_target):
        ex = dict(ex)
        diff = ex.pop(TWIN_DIFF_INPUT)
        return grad_fn(weights, diff, {**shared, **ex}, loss_target)

    if N_MICROBATCH == 1:
        loss, (grad_w, grad_x) = one_microbatch(per_example, given["loss_target"])
    else:
        def body(carry, xs):
            loss_sum, grad_sum = carry
            l_k, (gw_k, gx_k) = one_microbatch(xs[0], xs[1])
            with _jax.named_scope("update"):
                return (loss_sum + l_k, _jax.tree.map(_jnp.add, grad_sum, gw_k)), gx_k

        init = (_jnp.zeros((), _jnp.float32), _jax.tree.map(_jnp.zeros_like, weights))
        (loss, grad_w), grad_x = _jax.lax.scan(body, init, (per_example, given["loss_target"]))
    with _jax.named_scope("update"):
        delta_w, new_m, new_v = {}, {}, {}
        for n in TWIN_WEIGHTS:
            delta_w[n], new_m[n], new_v[n] = _adamw(weights[n], grad_w[n], given["m_" + n], given["v_" + n])
    return (loss, grad_x, *[grad_w[n] for n in TWIN_WEIGHTS], *[delta_w[n] for n in TWIN_WEIGHTS],
            *[new_m[n] for n in TWIN_WEIGHTS], *[new_v[n] for n in TWIN_WEIGHTS])
```

```python
import functools

import jax
import jax.numpy as jnp
from jax import lax
from jax.experimental import pallas as pl
from jax.experimental.pallas import tpu as pltpu

F32 = jnp.float32
BF16 = jnp.bfloat16
N_DEV = 8
MESH_AXES = ("x", "y", "c")
EPS = 1e-6
HEAD_DIM = 128
POOL_WINDOWS = (2, 4, 8, 16)
GRID_W = 64
ROPE_THETA = 10000.0
HALO = 16
ADAM_LR, ADAM_B1, ADAM_B2, ADAM_EPS, ADAM_WD, ADAM_STEP = 0.001, 0.9, 0.999, 1e-08, 0.01, 10
VMEM_CAP_MB = 60
ADAMW_BLOCK_ELEMS = 1 << 18
MESH_ID = pl.DeviceIdType.MESH


def _div(n, prefs):
    for p in prefs:
        if n % p == 0:
            return p
    raise ValueError(f"no tile for {n} in {prefs}")


def _params(sem, vmem_mb):
    return pltpu.CompilerParams(dimension_semantics=sem, vmem_limit_bytes=min(vmem_mb, VMEM_CAP_MB) << 20)


def _sigmoid(x):
    return 1.0 / (1.0 + jnp.exp(-x))


def _silu(x):
    return x * _sigmoid(x)


def _rms(x):
    return x * lax.rsqrt(jnp.mean(x * x, axis=-1, keepdims=True) + EPS)


def _rms_mod(x, shift, scale):
    return _rms(x) * (1.0 + scale) + shift


def _layernorm(x, g, b):
    mu = jnp.mean(x, axis=-1, keepdims=True)
    var = jnp.mean(jnp.square(x - mu), axis=-1, keepdims=True)
    return (x - mu) * lax.rsqrt(var + EPS) * g + b


def _ln_silu(x, g, b):
    return _silu(_layernorm(x, g, b))


def _matmul(a, b, *, mode, tm, tn, tk, out_dtypes, name, extras=(), epilogue=None):
    if mode == "tn":
        kdim, m = a.shape
        n = b.shape[1]
    else:
        m, kdim = a.shape
        n = b.shape[0] if mode == "nt" else b.shape[1]
    assert m % tm == 0 and n % tn == 0 and kdim % tk == 0, (name, m, n, kdim, tm, tn, tk)
    nk = kdim // tk
    n_ex = len(extras)
    n_out = len(out_dtypes)

    def body(a_ref, b_ref, *rest):
        ex = rest[:n_ex]
        outs = rest[n_ex:n_ex + n_out]
        acc_ref = rest[n_ex + n_out] if nk > 1 else None
        k = pl.program_id(2)
        av = a_ref[...].astype(BF16)
        bv = b_ref[...].astype(BF16)
        dims = {"nn": ((1,), (0,)), "nt": ((1,), (1,)), "tn": ((0,), (0,))}[mode]
        part = lax.dot_general(av, bv, (dims, ((), ())), preferred_element_type=F32)

        rows = pl.program_id(0) * tm + lax.broadcasted_iota(jnp.int32, (tm, 1), 0)

        def finish(acc):
            res = (acc,) if epilogue is None else epilogue(acc, rows, *[e[...] for e in ex])
            for o, r in zip(outs, res):
                o[...] = r.astype(o.dtype)

        if nk == 1:
            finish(part)
        else:
            @pl.when(k == 0)
            def _():
                acc_ref[...] = part

            @pl.when(k > 0)
            def _():
                acc_ref[...] += part

            @pl.when(k == nk - 1)
            def _():
                finish(acc_ref[...])

    if mode == "tn":
        a_spec = pl.BlockSpec((tk, tm), lambda i, j, k: (k, i))
        b_spec = pl.BlockSpec((tk, tn), lambda i, j, k: (k, j))
    else:
        a_spec = pl.BlockSpec((tm, tk), lambda i, j, k: (i, k))
        b_spec = (pl.BlockSpec((tn, tk), lambda i, j, k: (j, k)) if mode == "nt"
                  else pl.BlockSpec((tk, tn), lambda i, j, k: (k, j)))
    ex_specs, ex_arrays, ex_bytes = [], [], 0
    for kind, arr in extras:
        ex_arrays.append(arr)
        if kind == "tile":
            ex_specs.append(pl.BlockSpec((tm, tn), lambda i, j, k: (i, j)))
            ex_bytes += tm * tn * arr.dtype.itemsize
        else:
            ex_specs.append(pl.BlockSpec((arr.shape[0], tn), lambda i, j, k: (0, j)))
            ex_bytes += arr.shape[0] * tn * 4
    blocks = tm * tk * a.dtype.itemsize + tk * tn * b.dtype.itemsize + ex_bytes
    blocks += sum(tm * tn * jnp.dtype(d).itemsize for d in out_dtypes)
    vmem = (2 * blocks + 4 * tm * tn * 4 + (tm * tk + tk * tn) * 2) // (1 << 20) + 8
    return pl.pallas_call(
        body,
        grid=(m // tm, n // tn, nk),
        in_specs=[a_spec, b_spec] + ex_specs,
        out_specs=[pl.BlockSpec((tm, tn), lambda i, j, k: (i, j)) for _ in out_dtypes],
        out_shape=[jax.ShapeDtypeStruct((m, n), d) for d in out_dtypes],
        scratch_shapes=[pltpu.VMEM((tm, tn), F32)] if nk > 1 else [],
        compiler_params=_params(("parallel", "parallel", "arbitrary"), vmem),
        name=name,
    )(a, b, *ex_arrays)


def _seg_pick(vec, k, rows, n_x):
    return jnp.where(rows < n_x, vec[2 * k:2 * k + 1], vec[2 * k + 1:2 * k + 2])


def _zero_accs(i, accs):
    @pl.when(i == 0)
    def _():
        for a in accs:
            a[...] = jnp.zeros_like(a)


def _rms_mod_fwd(xs, mods, *, k_shift, n_x, name):
    t, d = xs.shape
    tm = _div(t, (256, 128))

    def body(x_ref, mods_ref, h_ref):
        seg = (pl.program_id(0) * tm >= n_x).astype(jnp.int32)
        shift = mods_ref[pl.ds(2 * k_shift + seg, 1), :]
        scale = mods_ref[pl.ds(2 * k_shift + 2 + seg, 1), :]
        h_ref[...] = _rms_mod(x_ref[...], shift, scale).astype(BF16)

    return pl.pallas_call(
        body, grid=(t // tm,),
        in_specs=[pl.BlockSpec((tm, d), lambda i: (i, 0)), pl.BlockSpec((16, d), lambda i: (0, 0))],
        out_specs=pl.BlockSpec((tm, d), lambda i: (i, 0)),
        out_shape=jax.ShapeDtypeStruct((t, d), BF16),
        compiler_params=_params(("parallel",), 32), name=name,
    )(xs, mods)


def _rms_mod_bwd(xs, mods, dh, dres, *, k_shift, n_x, name):
    t, d = xs.shape
    tm = _div(t, (256, 128))

    def body(x_ref, mods_ref, dh_ref, dres_ref, dx_ref, dmods_ref):
        i = pl.program_id(0)
        _zero_accs(i, [dmods_ref])
        seg = (i * tm >= n_x).astype(jnp.int32)
        r_shift = 2 * k_shift + seg
        r_scale = 2 * k_shift + 2 + seg
        shift = mods_ref[pl.ds(r_shift, 1), :]
        scale = mods_ref[pl.ds(r_scale, 1), :]
        _, vjp = jax.vjp(_rms_mod, x_ref[...], shift, scale)
        dx, dshift, dscale = vjp(dh_ref[...].astype(F32))
        dx_ref[...] = dres_ref[...] + dx
        dmods_ref[pl.ds(r_shift, 1), :] += dshift
        dmods_ref[pl.ds(r_scale, 1), :] += dscale

    row = pl.BlockSpec((tm, d), lambda i: (i, 0))
    vec = pl.BlockSpec((16, d), lambda i: (0, 0))
    return pl.pallas_call(
        body, grid=(t // tm,),
        in_specs=[row, vec, row, row], out_specs=[row, vec],
        out_shape=[jax.ShapeDtypeStruct((t, d), F32), jax.ShapeDtypeStruct((16, d), F32)],
        compiler_params=_params(("arbitrary",), 40), name=name,
    )(xs, mods, dh, dres)


def _gate_bwd(dx, branch, mods, *, k_gate, n_x, name):
    t, d = dx.shape
    tm = _div(t, (256, 128))

    def body(dx_ref, br_ref, mods_ref, db_ref, dmods_ref):
        i = pl.program_id(0)
        _zero_accs(i, [dmods_ref])
        seg = (i * tm >= n_x).astype(jnp.int32)
        r_gate = 2 * k_gate + seg
        dxv = dx_ref[...]
        db_ref[...] = (dxv * mods_ref[pl.ds(r_gate, 1), :]).astype(BF16)
        dmods_ref[pl.ds(r_gate, 1), :] += jnp.sum(dxv * br_ref[...].astype(F32), axis=0, keepdims=True)

    row = pl.BlockSpec((tm, d), lambda i: (i, 0))
    vec = pl.BlockSpec((16, d), lambda i: (0, 0))
    return pl.pallas_call(
        body, grid=(t // tm,),
        in_specs=[row, row, vec], out_specs=[row, vec],
        out_shape=[jax.ShapeDtypeStruct((t, d), BF16), jax.ShapeDtypeStruct((16, d), F32)],
        compiler_params=_params(("arbitrary",), 32), name=name,
    )(dx, branch, mods)


def _final_loss(xs, g, target, *, name):
    t, d = xs.shape
    tm = _div(t, (256, 128))

    def loss_fn(x, gv, tgt):
        err = _rms(x) * gv - tgt
        return 0.5 * jnp.sum(jnp.mean(jnp.square(err), axis=-1))

    def body(x_ref, g_ref, t_ref, dx_ref, loss_ref, dg_ref):
        i = pl.program_id(0)
        _zero_accs(i, [loss_ref, dg_ref])
        val, vjp = jax.vjp(loss_fn, x_ref[...], g_ref[0:1, :], t_ref[...])
        dx, dg, _ = vjp(jnp.ones((), F32))
        dx_ref[...] = dx
        loss_ref[...] += val
        dg_ref[0:1, :] += dg

    row = pl.BlockSpec((tm, d), lambda i: (i, 0))
    return pl.pallas_call(
        body, grid=(t // tm,),
        in_specs=[row, pl.BlockSpec((8, d), lambda i: (0, 0)), row],
        out_specs=[row, pl.BlockSpec((8, 128), lambda i: (0, 0)), pl.BlockSpec((8, d), lambda i: (0, 0))],
        out_shape=[jax.ShapeDtypeStruct((t, d), F32), jax.ShapeDtypeStruct((8, 128), F32),
                   jax.ShapeDtypeStruct((8, d), F32)],
        compiler_params=_params(("arbitrary",), 40), name=name,
    )(xs, g, target)


def _halo_specs(r, width, col, t):
    h_per = r // HALO

    def prev(i):
        return (jnp.maximum(i * h_per - 1, 0), col)

    def nxt(i):
        return (jnp.minimum((i + 1) * h_per, t // HALO - 1), col)

    return (pl.BlockSpec((HALO, width), prev), pl.BlockSpec((r, width), lambda i: (i, col)),
            pl.BlockSpec((HALO, width), nxt))


def _seg_geometry(i, r, n_x, t):
    row0 = i * r
    in_ctx = row0 >= n_x
    first = jnp.logical_or(row0 == 0, row0 == n_x)
    last = jnp.logical_or(row0 + r == n_x, row0 + r == t)
    seg_start = jnp.where(in_ctx, n_x, 0)
    seg_len = jnp.where(in_ctx, t - n_x, n_x)
    return row0, first, last, seg_start, seg_len


def _pool_count(tpos, w, seg_len):
    return (jnp.minimum(tpos + w // 2, seg_len) - jnp.maximum(tpos - w // 2, 0)).astype(F32)


def _mixer0_fwd(z, conv_w, conv_b, ln_g, ln_b, pool_w, pool_scale, *, n_tap, n_x, name):
    t, dc = z.shape[0], z.shape[1] // 3
    n_grp, pg = pool_w.shape[0], pool_w.shape[1]
    r = _div(t - n_x, (256, 128))
    assert n_x % r == 0 and pg * n_grp == dc
    half = n_tap // 2
    assert half < HALO and max(POOL_WINDOWS) // 2 <= HALO

    def body(ap, ac, an, gp, gc, gn, pp, pc, pn, w_ref, cb_ref, lg_ref, lb_ref, pw_ref, ps_ref,
             y_ref, cv_ref, d_ref, uwin, pwin):
        i = pl.program_id(0)
        row0, first, last, seg_start, seg_len = _seg_geometry(i, r, n_x, t)
        keep_prev = jnp.where(first, 0.0, 1.0)
        keep_next = jnp.where(last, 0.0, 1.0)
        uwin[0:HALO, :] = ap[...] * _sigmoid(gp[...]) * keep_prev
        uwin[HALO:HALO + r, :] = ac[...] * _sigmoid(gc[...])
        uwin[HALO + r:, :] = an[...] * _sigmoid(gn[...]) * keep_next
        pwin[0:HALO, :] = pp[...] * keep_prev
        pwin[HALO:HALO + r, :] = pc[...]
        pwin[HALO + r:, :] = pn[...] * keep_next
        acc = jnp.zeros((r, dc), F32) + cb_ref[0:1, :]
        for k in range(n_tap):
            off = HALO - half + k
            acc = acc + w_ref[k:k + 1, :] * uwin[off:off + r, :]
        cv_ref[...] = acc
        y_ref[:, 0:dc] = _ln_silu(acc, lg_ref[0:1, :], lb_ref[0:1, :]).astype(BF16)
        tpos = row0 - seg_start + lax.broadcasted_iota(jnp.int32, (r, 1), 0)
        for g, w in enumerate(POOL_WINDOWS):
            cols = slice(g * pg, (g + 1) * pg)
            s = jnp.zeros((r, pg), F32)
            for j in range(-(w // 2), w // 2):
                s = s + pwin[HALO + j:HALO + j + r, cols]
            diff = (s / _pool_count(tpos, w, seg_len) - pwin[HALO:HALO + r, cols]).astype(BF16)
            d_ref[:, cols] = diff
            pm = jnp.dot(diff, pw_ref[g].astype(BF16), preferred_element_type=F32)
            y_ref[:, dc + g * pg:dc + (g + 1) * pg] = (pm * ps_ref[0:1, cols]).astype(BF16)

    vec = lambda rows, width: pl.BlockSpec((rows, width), lambda i: (0, 0))
    in_specs = [*_halo_specs(r, dc, 0, t), *_halo_specs(r, dc, 1, t), *_halo_specs(r, dc, 2, t),
                vec(conv_w.shape[0], dc), vec(8, dc), vec(8, dc), vec(8, dc),
                pl.BlockSpec((n_grp, pg, pg), lambda i: (0, 0, 0)), vec(8, dc)]
    return pl.pallas_call(
        body, grid=(t // r,), in_specs=in_specs,
        out_specs=[pl.BlockSpec((r, 2 * dc), lambda i: (i, 0)), pl.BlockSpec((r, dc), lambda i: (i, 0)),
                   pl.BlockSpec((r, dc), lambda i: (i, 0))],
        out_shape=[jax.ShapeDtypeStruct((t, 2 * dc), BF16), jax.ShapeDtypeStruct((t, dc), F32),
                   jax.ShapeDtypeStruct((t, dc), BF16)],
        scratch_shapes=[pltpu.VMEM((r + 2 * HALO, dc), F32), pltpu.VMEM((r + 2 * HALO, dc), F32)],
        compiler_params=_params(("parallel",), 40), name=name,
    )(*([z] * 9), conv_w, conv_b, ln_g, ln_b, pool_w, pool_scale)


def _mixer0_bwd(dy, z, cv, dsave, conv_w, ln_g, ln_b, pool_w, pool_scale, *, n_tap, n_x, name):
    t, dc = cv.shape
    n_grp, pg = pool_w.shape[0], pool_w.shape[1]
    r = _div(t - n_x, (256, 128))
    half = n_tap // 2
    dyp_, zp_, cvp_ = dy, z, cv
    rw = r + 2 * HALO

    def body(dcp, dcc, dcn, dpp, dpc, dpn, cvp, cvc, cvn, ap, ac, an, gp, gc, gn, d_ref,
             w_ref, lg_ref, lb_ref, pw_ref, ps_ref,
             dz_ref, dw_ref, dcb_ref, dlg_ref, dlb_ref, dpw_ref, dps_ref, uwin, dcvwin, ewin):
        i = pl.program_id(0)
        _zero_accs(i, [dw_ref, dcb_ref, dlg_ref, dlb_ref, dpw_ref, dps_ref])
        row0, first, last, seg_start, seg_len = _seg_geometry(i, r, n_x, t)
        keep_prev = jnp.where(first, 0.0, 1.0)
        keep_next = jnp.where(last, 0.0, 1.0)
        lg, lb = lg_ref[0:1, :], lb_ref[0:1, :]
        _, vjp = jax.vjp(_ln_silu, cvc[...], lg, lb)
        dcv, dlg, dlb = vjp(dcc[...])
        dlg_ref[0:1, :] += dlg
        dlb_ref[0:1, :] += dlb
        dcb_ref[0:1, :] += jnp.sum(dcv, axis=0, keepdims=True)
        dcvwin[HALO:HALO + r, :] = dcv
        for halo_cv, halo_dy, keep, lo in ((cvp, dcp, keep_prev, 0), (cvn, dcn, keep_next, HALO + r)):
            _, vjp_h = jax.vjp(lambda v: _ln_silu(v, lg, lb), halo_cv[...])
            dcvwin[lo:lo + HALO, :] = vjp_h(halo_dy[...])[0] * keep
        uwin[0:HALO, :] = ap[...] * _sigmoid(gp[...]) * keep_prev
        sig = _sigmoid(gc[...])
        uwin[HALO:HALO + r, :] = ac[...] * sig
        uwin[HALO + r:, :] = an[...] * _sigmoid(gn[...]) * keep_next
        du = jnp.zeros((r, dc), F32)
        for k in range(n_tap):
            du = du + w_ref[k:k + 1, :] * dcvwin[HALO + half - k:HALO + half - k + r, :]
            off = HALO - half + k
            dw_ref[k:k + 1, :] += jnp.sum(dcv * uwin[off:off + r, :], axis=0, keepdims=True)
        dz_ref[:, 0:dc] = (du * sig).astype(BF16)
        dz_ref[:, dc:2 * dc] = (du * ac[...] * sig * (1.0 - sig)).astype(BF16)
        twin = row0 - seg_start - HALO + lax.broadcasted_iota(jnp.int32, (rw, 1), 0)
        for g, w in enumerate(POOL_WINDOWS):
            cols = slice(g * pg, (g + 1) * pg)
            wg = pw_ref[g].astype(BF16)
            scale = ps_ref[0:1, cols]
            dyp_c = dpc[:, cols]
            dpm_win = jnp.concatenate([dpp[:, cols] * keep_prev, dyp_c, dpn[:, cols] * keep_next], axis=0) * scale
            dd_win = lax.dot_general(dpm_win.astype(BF16), wg, (((1,), (1,)), ((), ())), preferred_element_type=F32)
            cnt = jnp.maximum(_pool_count(twin, w, seg_len), 1.0)
            ewin[:, cols] = dd_win / cnt
            dup = -dd_win[HALO:HALO + r, :]
            for j in range(-(w // 2) + 1, w // 2 + 1):
                dup = dup + ewin[HALO + j:HALO + j + r, cols]
            dz_ref[:, 2 * dc + g * pg:2 * dc + (g + 1) * pg] = dup.astype(BF16)
            dsv = d_ref[:, cols]
            pm = jnp.dot(dsv, wg, preferred_element_type=F32)
            dps_ref[0:1, cols] += jnp.sum(dyp_c * pm, axis=0, keepdims=True)
            dpw_ref[g] += lax.dot_general(dsv, (dyp_c * scale).astype(BF16), (((0,), (0,)), ((), ())),
                                          preferred_element_type=F32)

    vec = lambda rows, width: pl.BlockSpec((rows, width), lambda i: (0, 0))
    grp = pl.BlockSpec((n_grp, pg, pg), lambda i: (0, 0, 0))
    in_specs = [*_halo_specs(r, dc, 0, t), *_halo_specs(r, dc, 1, t), *_halo_specs(r, dc, 0, t),
                *_halo_specs(r, dc, 0, t), *_halo_specs(r, dc, 1, t), pl.BlockSpec((r, dc), lambda i: (i, 0)),
                vec(conv_w.shape[0], dc), vec(8, dc), vec(8, dc), grp, vec(8, dc)]
    outs = pl.pallas_call(
        body, grid=(t // r,), in_specs=in_specs,
        out_specs=[pl.BlockSpec((r, 3 * dc), lambda i: (i, 0)), vec(conv_w.shape[0], dc), vec(8, dc), vec(8, dc),
                   vec(8, dc), grp, vec(8, dc)],
        out_shape=[jax.ShapeDtypeStruct((t, 3 * dc), BF16), jax.ShapeDtypeStruct(conv_w.shape, F32),
                   jax.ShapeDtypeStruct((8, dc), F32), jax.ShapeDtypeStruct((8, dc), F32),
                   jax.ShapeDtypeStruct((8, dc), F32), jax.ShapeDtypeStruct(pool_w.shape, F32),
                   jax.ShapeDtypeStruct((8, dc), F32)],
        scratch_shapes=[pltpu.VMEM((rw, dc), F32), pltpu.VMEM((rw, dc), F32), pltpu.VMEM((rw, dc), F32)],
        compiler_params=_params(("arbitrary",), 48), name=name,
    )(dyp_, dyp_, dyp_, dyp_, dyp_, dyp_, cvp_, cvp_, cvp_, zp_, zp_, zp_, zp_, zp_, zp_, dsave,
      conv_w, ln_g, ln_b, pool_w, pool_scale)
    return outs


def _swap_halves(x):
    lane = lax.broadcasted_iota(jnp.int32, x.shape, 1)
    quarter = HEAD_DIM // 4
    return jnp.where(lane % (2 * quarter) < quarter,
                     pltpu.roll(x, HEAD_DIM - quarter, 1), pltpu.roll(x, quarter, 1))


def _rope_tables(n_x, n_ctx):
    half = HEAD_DIM // 4
    freqs = ROPE_THETA ** (-jnp.arange(half, dtype=F32) / half)
    tok = jnp.arange(n_x)
    row = (tok // GRID_W).astype(F32)[:, None] * freqs[None, :]
    col = (tok % GRID_W).astype(F32)[:, None] * freqs[None, :]
    cos = jnp.concatenate([jnp.cos(row), jnp.cos(row), jnp.cos(col), jnp.cos(col)], axis=1)
    sin = jnp.concatenate([-jnp.sin(row), jnp.sin(row), -jnp.sin(col), jnp.sin(col)], axis=1)
    cos = jnp.concatenate([cos, jnp.ones((n_ctx, HEAD_DIM), F32)], axis=0)
    sin = jnp.concatenate([sin, jnp.zeros((n_ctx, HEAD_DIM), F32)], axis=0)
    return cos, sin


def _norm_g(x, g):
    return _rms(x) * g


def _qk_prep_fwd(qkv, gq, gk, cos, sin, *, d_q, name):
    t, width = qkv.shape
    d_kv = (width - d_q) // 2
    tm = _div(t, (256, 128))

    def body(qkv_ref, gq_ref, gk_ref, cos_ref, sin_ref, q_ref, k_ref, v_ref):
        cs, sn = cos_ref[...], sin_ref[...]
        for h in range((d_q + d_kv) // HEAD_DIM):
            g = gq_ref[0:1, :] if h * HEAD_DIM < d_q else gk_ref[0:1, :]
            xn = _norm_g(qkv_ref[:, h * HEAD_DIM:(h + 1) * HEAD_DIM], g)
            rot = (xn * cs + _swap_halves(xn) * sn).astype(BF16)
            if h * HEAD_DIM < d_q:
                q_ref[:, h * HEAD_DIM:(h + 1) * HEAD_DIM] = rot
            else:
                k_ref[:, h * HEAD_DIM - d_q:(h + 1) * HEAD_DIM - d_q] = rot
        v_ref[...] = qkv_ref[:, d_q + d_kv:].astype(BF16)

    row = lambda w: pl.BlockSpec((tm, w), lambda i: (i, 0))
    vec = pl.BlockSpec((8, HEAD_DIM), lambda i: (0, 0))
    return pl.pallas_call(
        body, grid=(t // tm,),
        in_specs=[row(width), vec, vec, row(HEAD_DIM), row(HEAD_DIM)],
        out_specs=[row(d_q), row(d_kv), row(d_kv)],
        out_shape=[jax.ShapeDtypeStruct((t, d_q), BF16), jax.ShapeDtypeStruct((t, d_kv), BF16),
                   jax.ShapeDtypeStruct((t, d_kv), BF16)],
        compiler_params=_params(("parallel",), 32), name=name,
    )(qkv, gq, gk, cos, sin)


def _qk_prep_bwd(qkv, dq, dk, dv, gq, gk, cos, sin, *, n_x, name):
    t, width = qkv.shape
    d_q, d_kv = dq.shape[1], dk.shape[1]
    tm = _div(t, (256, 128))
    last_q = n_x // tm - 1

    def body(qkv_ref, dq_ref, dk_ref, dv_ref, gq_ref, gk_ref, cos_ref, sin_ref, out_ref, dgq_ref, dgk_ref):
        i = pl.program_id(0)
        _zero_accs(i, [dgq_ref, dgk_ref])
        is_x = jnp.where(i * tm < n_x, 1.0, 0.0)
        cs, sn = cos_ref[...], sin_ref[...]
        for h in range((d_q + d_kv) // HEAD_DIM):
            sl = slice(h * HEAD_DIM, (h + 1) * HEAD_DIM)
            if h * HEAD_DIM < d_q:
                g, dg_ref, dr = gq_ref[0:1, :], dgq_ref, dq_ref[:, sl] * is_x
            else:
                g, dg_ref = gk_ref[0:1, :], dgk_ref
                dr = dk_ref[:, h * HEAD_DIM - d_q:(h + 1) * HEAD_DIM - d_q]
            dxn = dr * cs + _swap_halves(dr * sn)
            _, vjp = jax.vjp(_norm_g, qkv_ref[:, sl], g)
            dx, dg = vjp(dxn)
            out_ref[:, sl] = dx.astype(BF16)
            dg_ref[0:1, :] += dg
        out_ref[:, d_q + d_kv:] = dv_ref[...].astype(BF16)

    row = lambda w: pl.BlockSpec((tm, w), lambda i: (i, 0))
    vec = pl.BlockSpec((8, HEAD_DIM), lambda i: (0, 0))
    return pl.pallas_call(
        body, grid=(t // tm,),
        in_specs=[row(width), pl.BlockSpec((tm, d_q), lambda i: (jnp.minimum(i, last_q), 0)), row(d_kv), row(d_kv),
                  vec, vec, row(HEAD_DIM), row(HEAD_DIM)],
        out_specs=[row(width), vec, vec],
        out_shape=[jax.ShapeDtypeStruct((t, width), BF16), jax.ShapeDtypeStruct((8, HEAD_DIM), F32),
                   jax.ShapeDtypeStruct((8, HEAD_DIM), F32)],
        compiler_params=_params(("arbitrary",), 40), name=name,
    )(qkv, dq, dk, dv, gq, gk, cos, sin)


def _stack_heads(ref, grp):
    return jnp.concatenate([ref[:, g * HEAD_DIM:(g + 1) * HEAD_DIM] for g in range(grp)], axis=0)


def _attention_fwd(q, k, v, *, n_x, name):
    t, d_kv = k.shape
    d_q = q.shape[1]
    kvh = d_kv // HEAD_DIM
    grp = d_q // d_kv
    tq = _div(n_x, (128,))
    tk = _div(t, (768, 640, 512, 384, 256, 128))
    scale = HEAD_DIM ** -0.5
    gw = grp * HEAD_DIM

    def body(q_ref, k_ref, v_ref, o_ref, lse_ref, m_ref, l_ref, acc_ref):
        qs = _stack_heads(q_ref, grp)
        m_ref[...] = jnp.full_like(m_ref, -jnp.inf)
        l_ref[...] = jnp.zeros_like(l_ref)
        acc_ref[...] = jnp.zeros_like(acc_ref)

        def step(j, carry):
            rows = pl.ds(pl.multiple_of(j * tk, tk), tk)
            s = lax.dot_general(qs, k_ref[rows, :], (((1,), (1,)), ((), ())), preferred_element_type=F32) * scale
            m_old = m_ref[...]
            m_new = jnp.maximum(m_old, jnp.max(s, axis=-1, keepdims=True))
            alpha = jnp.exp(m_old - m_new)
            p = jnp.exp(s - m_new)
            l_ref[...] = alpha * l_ref[...] + jnp.sum(p, axis=-1, keepdims=True)
            acc_ref[...] = alpha * acc_ref[...] + jnp.dot(p.astype(BF16), v_ref[rows, :], preferred_element_type=F32)
            m_ref[...] = m_new
            return carry

        lax.fori_loop(0, t // tk, step, 0)
        out = acc_ref[...] / l_ref[...]
        lse = m_ref[...] + jnp.log(l_ref[...])
        for g in range(grp):
            o_ref[:, g * HEAD_DIM:(g + 1) * HEAD_DIM] = out[g * tq:(g + 1) * tq].astype(BF16)
            lse_ref[:, g:g + 1] = lse[g * tq:(g + 1) * tq]

    return pl.pallas_call(
        body, grid=(kvh, n_x // tq),
        in_specs=[pl.BlockSpec((tq, gw), lambda h, i: (i, h)),
                  pl.BlockSpec((t, HEAD_DIM), lambda h, i: (0, h)),
                  pl.BlockSpec((t, HEAD_DIM), lambda h, i: (0, h))],
        out_specs=[pl.BlockSpec((tq, gw), lambda h, i: (i, h)),
                   pl.BlockSpec((None, tq, grp), lambda h, i: (h, i, 0))],
        out_shape=[jax.ShapeDtypeStruct((n_x, d_q), BF16), jax.ShapeDtypeStruct((kvh, n_x, grp), F32)],
        scratch_shapes=[pltpu.VMEM((grp * tq, 1), F32), pltpu.VMEM((grp * tq, 1), F32),
                        pltpu.VMEM((grp * tq, HEAD_DIM), F32)],
        compiler_params=_params(("parallel", "arbitrary"), 48), name=name,
    )(q, k, v)


def _attention_bwd(q, k, v, o, lse, do, *, n_x, name):
    t, d_kv = k.shape
    d_q = q.shape[1]
    kvh = d_kv // HEAD_DIM
    grp = d_q // d_kv
    tq = _div(n_x, (128,))
    tk = _div(t, (768, 640, 512, 384, 256, 128))
    scale = HEAD_DIM ** -0.5
    gw = grp * HEAD_DIM

    def body(q_ref, k_ref, v_ref, o_ref, lse_ref, do_ref, dq_ref, dk_ref, dv_ref, dq_acc):
        i = pl.program_id(1)
        _zero_accs(i, [dk_ref, dv_ref])
        qs = _stack_heads(q_ref, grp)
        dos = _stack_heads(do_ref, grp)
        delta = jnp.sum(dos.astype(F32) * _stack_heads(o_ref, grp).astype(F32), axis=-1, keepdims=True)
        lse = jnp.concatenate([lse_ref[:, g:g + 1] for g in range(grp)], axis=0)
        dq_acc[...] = jnp.zeros_like(dq_acc)

        def step(j, carry):
            rows = pl.ds(pl.multiple_of(j * tk, tk), tk)
            kc, vc = k_ref[rows, :], v_ref[rows, :]
            s = lax.dot_general(qs, kc, (((1,), (1,)), ((), ())), preferred_element_type=F32) * scale
            p = jnp.exp(s - lse)
            dv_ref[rows, :] += lax.dot_general(p.astype(BF16), dos, (((0,), (0,)), ((), ())),
                                               preferred_element_type=F32)
            dp = lax.dot_general(dos, vc, (((1,), (1,)), ((), ())), preferred_element_type=F32)
            ds = (p * (dp - delta) * scale).astype(BF16)
            dq_acc[...] += jnp.dot(ds, kc, preferred_element_type=F32)
            dk_ref[rows, :] += lax.dot_general(ds, qs, (((0,), (0,)), ((), ())), preferred_element_type=F32)
            return carry

        lax.fori_loop(0, t // tk, step, 0)
        for g in range(grp):
            dq_ref[:, g * HEAD_DIM:(g + 1) * HEAD_DIM] = dq_acc[g * tq:(g + 1) * tq, :]

    qspec = pl.BlockSpec((tq, gw), lambda h, i: (i, h))
    kspec = pl.BlockSpec((t, HEAD_DIM), lambda h, i: (0, h))
    return pl.pallas_call(
        body, grid=(kvh, n_x // tq),
        in_specs=[qspec, kspec, kspec, qspec, pl.BlockSpec((None, tq, grp), lambda h, i: (h, i, 0)), qspec],
        out_specs=[qspec, kspec, kspec],
        out_shape=[jax.ShapeDtypeStruct((n_x, d_q), F32), jax.ShapeDtypeStruct((t, d_kv), F32),
                   jax.ShapeDtypeStruct((t, d_kv), F32)],
        scratch_shapes=[pltpu.VMEM((grp * tq, HEAD_DIM), F32)],
        compiler_params=_params(("parallel", "arbitrary"), 56), name=name,
    )(q, k, v, o, lse, do)


def _whole(body, ins, out_shapes, name):
    return pl.pallas_call(
        body, out_shape=[jax.ShapeDtypeStruct(s, d) for s, d in out_shapes],
        compiler_params=pltpu.CompilerParams(vmem_limit_bytes=40 << 20), name=name)(*ins)


def _silu_rows(x, *, name):
    def body(x_ref, o_ref):
        o_ref[...] = _silu(x_ref[...])
    return _whole(body, [x], [(x.shape, F32)], name)[0]


def _assemble_dmods(gathered, *, name):
    width = gathered.shape[1]

    def body(g_ref, dm0, dm1, db0, db1):
        for l, (dm, db) in enumerate(((dm0, db0), (dm1, db1))):
            ctx = jnp.zeros((1, width), F32)
            tot = jnp.zeros((1, width), F32)
            for q in range(N_DEV):
                row = g_ref[16 * q + 8 * l:16 * q + 8 * l + 1, :]
                dm[q:q + 1, :] = row
                tot = tot + row
                ctx = ctx + g_ref[16 * q + 8 * l + 1:16 * q + 8 * l + 2, :]
            dm[N_DEV:N_DEV + 1, :] = ctx
            dm[N_DEV + 1:, :] = jnp.zeros((16 - N_DEV - 1, width), F32)
            db[...] = jnp.zeros_like(db)
            db[0:1, :] = tot + ctx

    return _whole(body, [gathered], [((16, width), F32), ((16, width), F32), ((8, width), F32), ((8, width), F32)],
                  name)


def _sum_slots(gathered, rows, *, name):
    def body(g_ref, o_ref):
        acc = g_ref[0:rows, :]
        for q in range(1, N_DEV):
            acc = acc + g_ref[q * rows:(q + 1) * rows, :]
        o_ref[...] = acc
    return _whole(body, [gathered], [((rows, gathered.shape[1]), F32)], name)[0]


def _silu_grad(x, dy, *, name):
    def body(x_ref, dy_ref, o_ref):
        _, vjp = jax.vjp(_silu, x_ref[...])
        o_ref[...] = vjp(dy_ref[...])[0]
    return _whole(body, [x, dy], [(x.shape, F32)], name)[0]


def _adamw(w, m, v, *, name, recv=None, grad=None):
    rows, cols = w.shape
    budget = max(8, ADAMW_BLOCK_ELEMS // cols)
    tr = _div(rows, [c for c in (512, 256, 128, 64, 32, 16, 8) if c <= budget] + [rows])
    c1 = 1.0 - ADAM_B1 ** ADAM_STEP
    c2 = 1.0 - ADAM_B2 ** ADAM_STEP

    def body(w_ref, m_ref, v_ref, g_in, g_ref, d_ref, nm_ref, nv_ref):
        if recv is not None:
            g = g_in[0].astype(F32)
            for q in range(1, N_DEV):
                g = g + g_in[q].astype(F32)
        else:
            g = g_in[...]
        nm = ADAM_B1 * m_ref[...] + (1.0 - ADAM_B1) * g
        nv = ADAM_B2 * v_ref[...] + (1.0 - ADAM_B2) * jnp.square(g)
        g_ref[...] = g
        nm_ref[...] = nm
        nv_ref[...] = nv
        d_ref[...] = -ADAM_LR * ((nm / c1) / (jnp.sqrt(nv / c2) + ADAM_EPS) + ADAM_WD * w_ref[...])

    blk = pl.BlockSpec((tr, cols), lambda i: (i, 0))
    g_spec = pl.BlockSpec((N_DEV, tr, cols), lambda i: (0, i, 0)) if recv is not None else blk
    return pl.pallas_call(
        body, grid=(rows // tr,), in_specs=[blk, blk, blk, g_spec], out_specs=[blk] * 4,
        out_shape=[jax.ShapeDtypeStruct((rows, cols), F32)] * 4,
        compiler_params=_params(("parallel",), 48), name=name,
    )(w, m, v, recv if recv is not None else grad)


def _position():
    return tuple(lax.axis_index(a) for a in MESH_AXES)


def _linear(pos):
    return 4 * pos[0] + 2 * pos[1] + pos[2]


def _window(ref, axis, dev, size):
    start = pl.multiple_of(dev * size, size)
    return ref.at[pl.ds(start, size), :] if axis == 0 else ref.at[:, pl.ds(start, size)]


def _all_gather(shards, axes, *, name):
    n = len(shards)
    sizes = [s.shape[ax] for s, ax in zip(shards, axes)]

    def body(*refs):
        src, dst = refs[:n], refs[n:2 * n]
        send_sems, recv_sems, local_sems = refs[2 * n:]
        x, y, c = _position()
        me, sibling = (x, y, c), (x, y, 1 - c)
        chips = [(1 - x, y), (x, 1 - y), (1 - x, 1 - y)]

        def win(k, pos):
            return _window(dst[k], axes[k], _linear(pos), sizes[k])

        def copy(k, sem, block, to, from_src=False):
            return pltpu.make_async_remote_copy(
                src_ref=src[k] if from_src else win(k, block), dst_ref=win(k, block),
                send_sem=send_sems.at[k, sem], recv_sem=recv_sems.at[k, sem],
                device_id=to, device_id_type=MESH_ID)

        mine = [pltpu.make_async_copy(src[k], win(k, me), local_sems.at[k]) for k in range(n)]
        for cp in mine:
            cp.start()
        first = []
        for k in range(n):
            first.append(copy(k, 0, me, sibling, from_src=True))
            first += [copy(k, 1 + j, me, (*chip, c), from_src=True) for j, chip in enumerate(chips)]
        for cp in first:
            cp.start()
        passed = []
        for j, chip in enumerate(chips):
            for k in range(n):
                copy(k, 1 + j, (*chip, c), me).wait_recv()
                fwd = copy(k, 4 + j, (*chip, c), sibling)
                fwd.start()
                passed.append(fwd)
        for k in range(n):
            copy(k, 0, sibling, me).wait_recv()
            for j, chip in enumerate(chips):
                copy(k, 4 + j, (*chip, 1 - c), me).wait_recv()
        for cp in first + passed:
            cp.wait_send()
        for cp in mine:
            cp.wait()

    out_shape = []
    for s, ax in zip(shards, axes):
        full = (s.shape[0] * N_DEV, s.shape[1]) if ax == 0 else (s.shape[0], s.shape[1] * N_DEV)
        out_shape.append(jax.ShapeDtypeStruct(full, s.dtype))
    any_spec = pl.BlockSpec(memory_space=pl.ANY)
    return pl.pallas_call(
        body, in_specs=[any_spec] * n, out_specs=[any_spec] * n, out_shape=out_shape,
        scratch_shapes=[pltpu.SemaphoreType.DMA((n, 7)), pltpu.SemaphoreType.DMA((n, 7)),
                        pltpu.SemaphoreType.DMA((n,))],
        name=name,
    )(*shards)


def _all_to_all(parts, axes, *, name):
    n = len(parts)
    sizes = [p.shape[ax] // N_DEV for p, ax in zip(parts, axes)]

    def body(*refs):
        src, dst = refs[:n], refs[n:2 * n]
        send_sems, recv_sems, local_sems = refs[2 * n:]
        me = _position()
        my_slot = _linear(me)

        def peer(mask):
            return tuple(1 - p if (mask >> (2 - b)) & 1 else p for b, p in enumerate(me))

        def copy(k, mask):
            to = peer(mask)
            return pltpu.make_async_remote_copy(
                src_ref=_window(src[k], axes[k], _linear(to), sizes[k]), dst_ref=dst[k].at[my_slot],
                send_sem=send_sems.at[k, mask - 1], recv_sem=recv_sems.at[k, mask - 1],
                device_id=to, device_id_type=MESH_ID)

        def arrival(k, mask):
            frm = peer(mask)
            return pltpu.make_async_remote_copy(
                src_ref=_window(src[k], axes[k], my_slot, sizes[k]), dst_ref=dst[k].at[_linear(frm)],
                send_sem=send_sems.at[k, mask - 1], recv_sem=recv_sems.at[k, mask - 1],
                device_id=frm, device_id_type=MESH_ID)

        mine = [pltpu.make_async_copy(_window(src[k], axes[k], my_slot, sizes[k]), dst[k].at[my_slot],
                                      local_sems.at[k]) for k in range(n)]
        for cp in mine:
            cp.start()
        sent = [copy(k, mask) for mask in range(1, N_DEV) for k in range(n)]
        for cp in sent:
            cp.start()
        for mask in range(1, N_DEV):
            for k in range(n):
                arrival(k, mask).wait_recv()
        for cp in sent:
            cp.wait_send()
        for cp in mine:
            cp.wait()

    out_shape = []
    for p, ax, sz in zip(parts, axes, sizes):
        shard = (sz, p.shape[1]) if ax == 0 else (p.shape[0], sz)
        out_shape.append(jax.ShapeDtypeStruct((N_DEV, *shard), p.dtype))
    any_spec = pl.BlockSpec(memory_space=pl.ANY)
    return pl.pallas_call(
        body, in_specs=[any_spec] * n, out_specs=[any_spec] * n, out_shape=out_shape,
        scratch_shapes=[pltpu.SemaphoreType.DMA((n, 7)), pltpu.SemaphoreType.DMA((n, 7)),
                        pltpu.SemaphoreType.DMA((n,))],
        name=name,
    )(*parts)


WEIGHTS = ['c_ctx', 'l0_ada_w', 'l0_ada_b', 'l0_in_w', 'l0_conv_w', 'l0_conv_b', 'l0_conv_ln_g', 'l0_conv_ln_b',
           'l0_pool_w', 'l0_pool_scale', 'l0_out_w', 'l0_mlp_w1', 'l0_mlp_w2', 'l1_ada_w', 'l1_ada_b', 'l1_qkv_w',
           'l1_q_norm_g', 'l1_k_norm_g', 'l1_out_w', 'l1_mlp_w1', 'l1_mlp_w2', 'final_g']
SHARDED = {'l0_in_w': 1, 'l0_out_w': 0, 'l0_mlp_w1': 1, 'l0_mlp_w2': 0,
           'l1_qkv_w': 1, 'l1_out_w': 0, 'l1_mlp_w1': 1, 'l1_mlp_w2': 0}
REPLICATED_SMALL = ['l0_conv_b', 'l0_conv_ln_g', 'l0_conv_ln_b', 'l0_pool_scale', 'l1_q_norm_g', 'l1_k_norm_g',
                    'final_g']


def _row8(v):
    v = v.reshape(1, -1)
    return jnp.pad(v, ((0, 7), (0, 0)))


def _mods16(full, me):
    d = full.shape[1] // 6
    mine = lax.dynamic_slice_in_dim(full, me, 1, axis=0).reshape(6, d)
    ctx = full[N_DEV].reshape(6, d)
    return jnp.pad(jnp.stack([mine, ctx], axis=1).reshape(12, d), ((0, 4), (0, 0)))


def _mlp_fwd(xs, mods, w1, w2, *, n_x, tm, tag):
    t, d = xs.shape
    dff = w1.shape[1]
    h = _rms_mod_fwd(xs, mods, k_shift=3, n_x=n_x, name=f"{tag}_norm2")
    pre, act = _matmul(h, w1, mode="nn", tm=tm, tn=_div(dff, (1024, 512)), tk=d, out_dtypes=[BF16, BF16],
                       name=f"{tag}_mlp1", epilogue=lambda acc, rows: (acc, jnp.square(jnp.maximum(acc, 0.0))))
    xo, branch = _matmul(
        act, w2, mode="nn", tm=tm, tn=_div(d, (1024, 512)), tk=_div(dff, (2048, 1024)), out_dtypes=[F32, BF16],
        name=f"{tag}_mlp2", extras=[("tile", xs), ("vec", mods)],
        epilogue=lambda acc, rows, res, mv: (res + _seg_pick(mv, 5, rows, n_x) * acc, acc))
    return xo, dict(h=h, pre=pre, act=act, branch=branch, x_in=xs)


def _mlp_bwd(dxo, saved, mods, w1, w2, *, n_x, tm, tag):
    t, d = dxo.shape
    dff = w1.shape[1]
    tkt = _div(t, (768, 1024, 512, 640, 128))
    dbranch, dm_gate = _gate_bwd(dxo, saved["branch"], mods, k_gate=5, n_x=n_x, name=f"{tag}_gate2_bwd")
    dpre, = _matmul(dbranch, w2, mode="nt", tm=tm, tn=_div(dff, (1024, 512)), tk=d, out_dtypes=[BF16],
                    name=f"{tag}_mlp2_dx", extras=[("tile", saved["pre"])],
                    epilogue=lambda acc, rows, pre: (acc * 2.0 * jnp.maximum(pre.astype(F32), 0.0),))
    dw2, = _matmul(saved["act"], dbranch, mode="tn", tm=_div(dff, (1024, 512)), tn=_div(d, (1024, 512)), tk=tkt,
                   out_dtypes=[BF16], name=f"{tag}_mlp2_dw")
    dh, = _matmul(dpre, w1, mode="nt", tm=tm, tn=_div(d, (1024, 512)), tk=_div(dff, (2048, 1024)),
                  out_dtypes=[F32], name=f"{tag}_mlp1_dx")
    dw1, = _matmul(saved["h"], dpre, mode="tn", tm=_div(d, (1024, 512)), tn=_div(dff, (1024, 512)), tk=tkt,
                   out_dtypes=[BF16], name=f"{tag}_mlp1_dw")
    dx, dm_norm = _rms_mod_bwd(saved["x_in"], mods, dh, dxo, k_shift=3, n_x=n_x, name=f"{tag}_norm2_bwd")
    return dx, dw1, dw2, dm_gate + dm_norm


def kernel(x, c, ctx, c_ctx, l0_ada_w, l0_ada_b, l0_in_w, l0_conv_w, l0_conv_b, l0_conv_ln_g, l0_conv_ln_b, l0_pool_w, l0_pool_scale, l0_out_w, l0_mlp_w1, l0_mlp_w2, l1_ada_w, l1_ada_b, l1_qkv_w, l1_q_norm_g, l1_k_norm_g, l1_out_w, l1_mlp_w1, l1_mlp_w2, final_g, loss_target, m_c_ctx, m_l0_ada_w, m_l0_ada_b, m_l0_in_w, m_l0_conv_w, m_l0_conv_b, m_l0_conv_ln_g, m_l0_conv_ln_b, m_l0_pool_w, m_l0_pool_scale, m_l0_out_w, m_l0_mlp_w1, m_l0_mlp_w2, m_l1_ada_w, m_l1_ada_b, m_l1_qkv_w, m_l1_q_norm_g, m_l1_k_norm_g, m_l1_out_w, m_l1_mlp_w1, m_l1_mlp_w2, m_final_g, v_c_ctx, v_l0_ada_w, v_l0_ada_b, v_l0_in_w, v_l0_conv_w, v_l0_conv_b, v_l0_conv_ln_g, v_l0_conv_ln_b, v_l0_pool_w, v_l0_pool_scale, v_l0_out_w, v_l0_mlp_w1, v_l0_mlp_w2, v_l1_ada_w, v_l1_ada_b, v_l1_qkv_w, v_l1_q_norm_g, v_l1_k_norm_g, v_l1_out_w, v_l1_mlp_w1, v_l1_mlp_w2, v_final_g):
    p = dict(locals())
    me = _linear(_position())
    n_x, d = x.shape[1], x.shape[2]
    n_ctx = ctx.shape[1]
    t = n_x + n_ctx
    dc = l0_conv_b.shape[0]
    n_tap = l0_conv_w.shape[0]
    d_q = d
    n_mod = l0_ada_b.shape[0] // d
    ada_cols = l0_ada_w.shape[1]
    tm_t = _div(t, (768, 640, 512, 128))
    tm_x = _div(n_x, (1024, 512))

    names = list(SHARDED)
    full = _all_gather([p[nm].astype(BF16) for nm in names], [SHARDED[nm] for nm in names], name="gather_weights")
    wfull = dict(zip(names, full))
    conv_w_full, pool_w_full = _all_gather(
        [jnp.pad(l0_conv_w, ((0, 1), (0, 0))), l0_pool_w.reshape(-1, l0_pool_w.shape[2])], [1, 0],
        name="gather_small_weights")
    n_grp, pg = l0_pool_w.shape[0], l0_pool_w.shape[2]
    pool_w_full = pool_w_full.reshape(N_DEV, n_grp, pg // N_DEV, pg).transpose(1, 0, 2, 3).reshape(n_grp, pg, pg)

    c_all = _all_gather([_row8(c)], [0], name="gather_cond")[0].reshape(N_DEV, 8, d)[:, 0]
    cond = jnp.concatenate([c_all, c_ctx.reshape(1, d), jnp.zeros((16 - N_DEV - 1, d), F32)], axis=0)
    s16 = _silu_rows(cond, name="cond_silu")
    mod_shards = []
    for li, (lw, lb) in enumerate(((l0_ada_w, l0_ada_b), (l1_ada_w, l1_ada_b))):
        bias = _row8(lax.dynamic_slice_in_dim(lb, me * ada_cols, ada_cols))
        mod_shards.append(_matmul(s16, lw, mode="nn", tm=16, tn=_div(ada_cols, (512, 384, 256, 128)), tk=d,
                                  out_dtypes=[F32], name=f"l{li}_ada_fwd", extras=[("vec", bias)],
                                  epilogue=lambda acc, rows, b: (acc + b[0:1],))[0])
    mods_full = _all_gather([jnp.concatenate(mod_shards, axis=0)], [1], name="gather_mods")[0]
    mods0, mods1 = _mods16(mods_full[:16], me), _mods16(mods_full[16:], me)

    xs0 = jnp.concatenate([x[0], ctx[0]], axis=0)
    h0 = _rms_mod_fwd(xs0, mods0, k_shift=0, n_x=n_x, name="l0_norm1")
    z, = _matmul(h0, wfull['l0_in_w'], mode="nn", tm=tm_t, tn=_div(3 * dc, (1024, 768, 512, 384)), tk=d,
                 out_dtypes=[F32], name="l0_in_proj")
    y0, cv, dsave = _mixer0_fwd(z, conv_w_full, _row8(l0_conv_b), _row8(l0_conv_ln_g), _row8(l0_conv_ln_b),
                                pool_w_full, _row8(l0_pool_scale), n_tap=n_tap, n_x=n_x, name="l0_mixer")
    xs1, mix0 = _matmul(y0, wfull['l0_out_w'], mode="nn", tm=tm_t, tn=_div(d, (1024, 512)), tk=2 * dc,
                        out_dtypes=[F32, BF16], name="l0_out_proj", extras=[("tile", xs0), ("vec", mods0)],
                        epilogue=lambda acc, rows, res, mv: (res + _seg_pick(mv, 2, rows, n_x) * acc, acc))
    xs2, mlp0 = _mlp_fwd(xs1, mods0, wfull['l0_mlp_w1'], wfull['l0_mlp_w2'], n_x=n_x, tm=tm_t, tag="l0")

    h2 = _rms_mod_fwd(xs2, mods1, k_shift=0, n_x=n_x, name="l1_norm1")
    qkv, = _matmul(h2, wfull['l1_qkv_w'], mode="nn", tm=tm_t, tn=_div(l1_qkv_w.shape[1] * N_DEV, (1024, 768, 512)),
                   tk=d, out_dtypes=[F32], name="l1_qkv_proj")
    cos, sin = _rope_tables(n_x, n_ctx)
    gq, gk = _row8(l1_q_norm_g), _row8(l1_k_norm_g)
    q, k, v = _qk_prep_fwd(qkv, gq, gk, cos, sin, d_q=d_q, name="l1_qk_prep")
    o, lse = _attention_fwd(q, k, v, n_x=n_x, name="l1_attention")
    x2 = xs2[:n_x]
    x3, mix1 = _matmul(o, wfull['l1_out_w'], mode="nn", tm=tm_x, tn=_div(d, (1024, 512)), tk=d_q,
                       out_dtypes=[F32, BF16], name="l1_out_proj", extras=[("tile", x2), ("vec", mods1)],
                       epilogue=lambda acc, rows, res, mv: (res + mv[4:5] * acc, acc))
    x4, mlp1 = _mlp_fwd(x3, mods1, wfull['l1_mlp_w1'], wfull['l1_mlp_w2'], n_x=n_x, tm=tm_x, tag="l1")

    dx4, loss_part, dfinal_g = _final_loss(x4, _row8(final_g), loss_target[0], name="loss_head")
    loss = lax.psum(loss_part[0, 0], MESH_AXES)

    grads = {}
    dx3, grads['l1_mlp_w1'], grads['l1_mlp_w2'], dmods1 = _mlp_bwd(
        dx4, mlp1, mods1, wfull['l1_mlp_w1'], wfull['l1_mlp_w2'], n_x=n_x, tm=tm_x, tag="l1")
    dmix1, dm = _gate_bwd(dx3, mix1, mods1, k_gate=2, n_x=n_x, name="l1_gate1_bwd")
    dmods1 = dmods1 + dm
    do, = _matmul(dmix1, wfull['l1_out_w'], mode="nt", tm=tm_x, tn=_div(d_q, (1024, 512)), tk=d,
                  out_dtypes=[BF16], name="l1_out_dx")
    grads['l1_out_w'], = _matmul(o, dmix1, mode="tn", tm=_div(d_q, (1024, 512)), tn=_div(d, (1024, 512)),
                                 tk=_div(n_x, (1024, 512)), out_dtypes=[BF16], name="l1_out_dw")
    dq, dk, dv = _attention_bwd(q, k, v, o, lse, do, n_x=n_x, name="l1_attention_bwd")
    dqkv, dgq, dgk = _qk_prep_bwd(qkv, dq, dk, dv, gq, gk, cos, sin, n_x=n_x, name="l1_qk_prep_bwd")
    tkt = _div(t, (768, 640, 512, 128))
    dh2, = _matmul(dqkv, wfull['l1_qkv_w'], mode="nt", tm=tm_t, tn=_div(d, (1024, 512)), tk=dqkv.shape[1],
                   out_dtypes=[F32], name="l1_qkv_dx")
    grads['l1_qkv_w'], = _matmul(h2, dqkv, mode="tn", tm=_div(d, (1024, 512)),
                                 tn=_div(dqkv.shape[1], (1024, 768, 512)), tk=tkt, out_dtypes=[BF16],
                                 name="l1_qkv_dw")
    dres2 = jnp.concatenate([dx3, jnp.zeros((n_ctx, d), F32)], axis=0)
    dxs2, dm = _rms_mod_bwd(xs2, mods1, dh2, dres2, k_shift=0, n_x=n_x, name="l1_norm1_bwd")
    dmods1 = dmods1 + dm

    dxs1, grads['l0_mlp_w1'], grads['l0_mlp_w2'], dmods0 = _mlp_bwd(
        dxs2, mlp0, mods0, wfull['l0_mlp_w1'], wfull['l0_mlp_w2'], n_x=n_x, tm=tm_t, tag="l0")
    dmix0, dm = _gate_bwd(dxs1, mix0, mods0, k_gate=2, n_x=n_x, name="l0_gate1_bwd")
    dmods0 = dmods0 + dm
    dy0, = _matmul(dmix0, wfull['l0_out_w'], mode="nt", tm=tm_t, tn=_div(2 * dc, (1024, 512)), tk=d,
                   out_dtypes=[F32], name="l0_out_dx")
    grads['l0_out_w'], = _matmul(y0, dmix0, mode="tn", tm=_div(2 * dc, (1024, 512)), tn=_div(d, (1024, 512)),
                                 tk=tkt, out_dtypes=[BF16], name="l0_out_dw")
    dz, dconv_w, dconv_b, dln_g, dln_b, dpool_w, dpool_scale = _mixer0_bwd(
        dy0, z, cv, dsave, conv_w_full, _row8(l0_conv_ln_g), _row8(l0_conv_ln_b), pool_w_full,
        _row8(l0_pool_scale), n_tap=n_tap, n_x=n_x, name="l0_mixer_bwd")
    dh0, = _matmul(dz, wfull['l0_in_w'], mode="nt", tm=tm_t, tn=_div(d, (1024, 512)), tk=3 * dc,
                   out_dtypes=[F32], name="l0_in_dx")
    grads['l0_in_w'], = _matmul(h0, dz, mode="tn", tm=_div(d, (1024, 512)), tn=_div(3 * dc, (1024, 768, 512, 384)),
                                tk=tkt, out_dtypes=[BF16], name="l0_in_dw")
    dxs0, dm = _rms_mod_bwd(xs0, mods0, dh0, dxs1, k_shift=0, n_x=n_x, name="l0_norm1_bwd")
    dmods0 = dmods0 + dm
    grad_x = dxs0[:n_x][None]

    def dmod_rows(dm16):
        rows = dm16[:2 * n_mod].reshape(n_mod, 2, d).transpose(1, 0, 2).reshape(2, n_mod * d)
        return jnp.pad(rows, ((0, 6), (0, 0)))
    dm_gathered = _all_gather([jnp.concatenate([dmod_rows(dmods0), dmod_rows(dmods1)], axis=0)], [0],
                              name="gather_dmods")[0]
    dm0, dm1, db0, db1 = _assemble_dmods(dm_gathered, name="assemble_dmods")
    out_g = {'l0_ada_b': db0[0], 'l1_ada_b': db1[0]}
    ds_part = jnp.zeros((16, d), F32)
    for nm, lw, dmf in (('l0_ada_w', l0_ada_w, dm0), ('l1_ada_w', l1_ada_w, dm1)):
        dm_cols = lax.dynamic_slice_in_dim(dmf, me * ada_cols, ada_cols, axis=1)
        out_g[nm], = _matmul(s16, dm_cols, mode="tn", tm=_div(d, (1024, 512)),
                             tn=_div(ada_cols, (512, 384, 256, 128)), tk=16, out_dtypes=[F32], name=f"{nm}_dw")
        ds_part = ds_part + _matmul(dm_cols, lw, mode="nt", tm=16, tn=_div(d, (1024, 512)),
                                    tk=_div(ada_cols, (512, 384, 256, 128)), out_dtypes=[F32], name=f"{nm}_dx")[0]

    small = {'l0_conv_b': dconv_b[0], 'l0_conv_ln_g': dln_g[0], 'l0_conv_ln_b': dln_b[0],
             'l0_pool_scale': dpool_scale[0], 'l1_q_norm_g': dgq[0], 'l1_k_norm_g': dgk[0],
             'final_g': dfinal_g[0], 'dsilu_ctx': ds_part[N_DEV], 'l0_conv_w': dconv_w[:-1].reshape(-1),
             'l0_pool_w': dpool_w.reshape(-1)}
    flat = jnp.concatenate([small[nm] for nm in small])
    rows = -(-flat.shape[0] // 1024) * 8
    packed = jnp.pad(flat, (0, rows * 128 - flat.shape[0])).reshape(rows, 128)
    summed = _sum_slots(_all_gather([packed], [0], name="gather_small_grads")[0], rows,
                        name="sum_small_grads").reshape(-1)
    off = 0
    for nm in small:
        size = small[nm].shape[0]
        small[nm] = summed[off:off + size]
        off += size
    out_g['c_ctx'] = _silu_grad(_row8(c_ctx), _row8(small['dsilu_ctx']), name="c_ctx_grad")[0]
    for nm in REPLICATED_SMALL:
        out_g[nm] = small[nm]
    conv_cols = l0_conv_w.shape[1]
    out_g['l0_conv_w'] = lax.dynamic_slice_in_dim(small['l0_conv_w'].reshape(n_tap, dc), me * conv_cols, conv_cols,
                                                  axis=1)
    out_g['l0_pool_w'] = lax.dynamic_slice_in_dim(small['l0_pool_w'].reshape(n_grp, pg, pg), me * (pg // N_DEV),
                                                  pg // N_DEV, axis=1)

    recv = dict(zip(names, _all_to_all([grads[nm] for nm in names], [SHARDED[nm] for nm in names],
                                       name="exchange_weight_grads")))
    delta, new_m, new_v = {}, {}, {}
    for nm in names:
        out_g[nm], delta[nm], new_m[nm], new_v[nm] = _adamw(p[nm], p['m_' + nm], p['v_' + nm], recv=recv[nm],
                                                            name=f"adamw_{nm}")
    for nm in ('l0_ada_w', 'l1_ada_w'):
        out_g[nm], delta[nm], new_m[nm], new_v[nm] = _adamw(p[nm], p['m_' + nm], p['v_' + nm], grad=out_g[nm],
                                                            name=f"adamw_{nm}")
    for nm in WEIGHTS:
        if nm in delta:
            continue
        shape = p[nm].shape
        as2d = lambda a: a.reshape(1, -1) if a.ndim == 1 else a.reshape(-1, a.shape[-1])
        res = _adamw(as2d(p[nm]), as2d(p['m_' + nm]), as2d(p['v_' + nm]), grad=as2d(out_g[nm]), name=f"adamw_{nm}")
        out_g[nm], delta[nm], new_m[nm], new_v[nm] = [r.reshape(shape) for r in res]

    return (loss, grad_x, *[out_g[nm] for nm in WEIGHTS], *[delta[nm] for nm in WEIGHTS],
            *[new_m[nm] for nm in WEIGHTS], *[new_v[nm] for nm in WEIGHTS])
```

```python
import functools

import jax
import jax.numpy as jnp
from jax import lax
from jax.experimental import pallas as pl
from jax.experimental.pallas import tpu as pltpu

F32 = jnp.float32
BF16 = jnp.bfloat16
N_DEV = 8
MESH_AXES = ("x", "y", "c")
EPS = 1e-6
HEAD_DIM = 128
POOL_WINDOWS = (2, 4, 8, 16)
GRID_W = 64
ROPE_THETA = 10000.0
ATTN_SCALE = HEAD_DIM ** -0.5
LOG2_E = 1.4426950408889634
Q_SCALE_LOG2 = ATTN_SCALE * LOG2_E
HALO = 16
ADAM_LR, ADAM_B1, ADAM_B2, ADAM_EPS, ADAM_WD, ADAM_STEP = 0.001, 0.9, 0.999, 1e-08, 0.01, 10
VMEM_CAP_MB = 60
ADAMW_BLOCK_ELEMS = 1 << 18
MESH_ID = pl.DeviceIdType.MESH


def _div(n, prefs):
    for p in prefs:
        if n % p == 0:
            return p
    raise ValueError(f"no tile for {n} in {prefs}")


def _params(sem, vmem_mb):
    return pltpu.CompilerParams(dimension_semantics=sem, vmem_limit_bytes=min(vmem_mb, VMEM_CAP_MB) << 20)


def _sigmoid(x):
    return 1.0 / (1.0 + jnp.exp(-x))


def _silu(x):
    return x * _sigmoid(x)


def _rms(x):
    return x * lax.rsqrt(jnp.mean(x * x, axis=-1, keepdims=True) + EPS)


def _rms_mod(x, shift, scale):
    return _rms(x) * (1.0 + scale) + shift


def _layernorm(x, g, b):
    mu = jnp.mean(x, axis=-1, keepdims=True)
    var = jnp.mean(jnp.square(x - mu), axis=-1, keepdims=True)
    return (x - mu) * lax.rsqrt(var + EPS) * g + b


def _ln_silu(x, g, b):
    return _silu(_layernorm(x, g, b))


def _matmul(a, b, *, mode, tm, tn, tk, out_dtypes, name, extras=(), epilogue=None, rider=None):
    if mode == "tn":
        kdim, m = a.shape
        n = b.shape[1]
    else:
        m, kdim = a.shape
        n = b.shape[0] if mode == "nt" else b.shape[1]
    assert m % tm == 0 and n % tn == 0 and kdim % tk == 0, (name, m, n, kdim, tm, tn, tk)
    nk = kdim // tk
    n_ex = len(extras)
    n_out = len(out_dtypes)

    def body(a_ref, b_ref, *rest):
        ex = rest[:n_ex]
        outs = rest[n_ex:n_ex + n_out]
        acc_ref = rest[n_ex + n_out] if nk > 1 else None
        k = pl.program_id(2)
        av = a_ref[...].astype(BF16)
        bv = b_ref[...].astype(BF16)
        dims = {"nn": ((1,), (0,)), "nt": ((1,), (1,)), "tn": ((0,), (0,))}[mode]
        part = lax.dot_general(av, bv, (dims, ((), ())), preferred_element_type=F32)

        rows = pl.program_id(0) * tm + lax.broadcasted_iota(jnp.int32, (tm, 1), 0)

        def finish(acc):
            res = (acc,) if epilogue is None else epilogue(acc, rows, *[e[...] for e in ex])
            for o, r in zip(outs, res):
                o[...] = r.astype(o.dtype)

        if nk == 1:
            finish(part)
        else:
            @pl.when(k == 0)
            def _():
                acc_ref[...] = part

            @pl.when(k > 0)
            def _():
                acc_ref[...] += part

            @pl.when(k == nk - 1)
            def _():
                finish(acc_ref[...])

    if mode == "tn":
        a_spec = pl.BlockSpec((tk, tm), lambda i, j, k: (k, i))
        b_spec = pl.BlockSpec((tk, tn), lambda i, j, k: (k, j))
    else:
        a_spec = pl.BlockSpec((tm, tk), lambda i, j, k: (i, k))
        b_spec = (pl.BlockSpec((tn, tk), lambda i, j, k: (j, k)) if mode == "nt"
                  else pl.BlockSpec((tk, tn), lambda i, j, k: (k, j)))
    ex_specs, ex_arrays, ex_bytes = [], [], 0
    for kind, arr in extras:
        ex_arrays.append(arr)
        if kind == "tile":
            ex_specs.append(pl.BlockSpec((tm, tn), lambda i, j, k: (i, j)))
            ex_bytes += tm * tn * arr.dtype.itemsize
        else:
            ex_specs.append(pl.BlockSpec((arr.shape[0], tn), lambda i, j, k: (0, j)))
            ex_bytes += arr.shape[0] * tn * 4
    blocks = tm * tk * a.dtype.itemsize + tk * tn * b.dtype.itemsize + ex_bytes
    blocks += sum(tm * tn * jnp.dtype(d).itemsize for d in out_dtypes)
    vmem = (2 * blocks + 4 * tm * tn * 4 + (tm * tk + tk * tn) * 2) // (1 << 20) + 8
    return _pcall(
        body,
        grid=(m // tm, n // tn, nk),
        in_specs=[a_spec, b_spec] + ex_specs,
        out_specs=[pl.BlockSpec((tm, tn), lambda i, j, k: (i, j)) for _ in out_dtypes],
        out_shape=[jax.ShapeDtypeStruct((m, n), d) for d in out_dtypes],
        scratch_shapes=[pltpu.VMEM((tm, tn), F32)] if nk > 1 else [],
        semantics=("parallel", "parallel", "arbitrary"), vmem_mb=vmem, name=name,
        operands=[a, b, *ex_arrays], rider=rider)


def _seg_pick(vec, k, rows, n_x):
    return jnp.where(rows < n_x, vec[2 * k:2 * k + 1], vec[2 * k + 1:2 * k + 2])


def _zero_accs(i, accs):
    @pl.when(i == 0)
    def _():
        for a in accs:
            a[...] = jnp.zeros_like(a)


def _rms_mod_fwd(xs, mods, *, k_shift, n_x, name):
    t, d = xs.shape
    tm = _div(t, (256, 128))

    def body(x_ref, mods_ref, h_ref):
        seg = (pl.program_id(0) * tm >= n_x).astype(jnp.int32)
        shift = mods_ref[pl.ds(2 * k_shift + seg, 1), :]
        scale = mods_ref[pl.ds(2 * k_shift + 2 + seg, 1), :]
        h_ref[...] = _rms_mod(x_ref[...], shift, scale).astype(BF16)

    return pl.pallas_call(
        body, grid=(t // tm,),
        in_specs=[pl.BlockSpec((tm, d), lambda i: (i, 0)), pl.BlockSpec((16, d), lambda i: (0, 0))],
        out_specs=pl.BlockSpec((tm, d), lambda i: (i, 0)),
        out_shape=jax.ShapeDtypeStruct((t, d), BF16),
        compiler_params=_params(("parallel",), 32), name=name,
    )(xs, mods)


def _rms_mod_bwd(xs, mods, dh, dres, *, k_shift, n_x, name, rider=None):
    t, d = xs.shape
    tm = _div(t, (256, 128))

    def body(x_ref, mods_ref, dh_ref, dres_ref, dx_ref, dmods_ref):
        i = pl.program_id(0)
        _zero_accs(i, [dmods_ref])
        seg = (i * tm >= n_x).astype(jnp.int32)
        r_shift = 2 * k_shift + seg
        r_scale = 2 * k_shift + 2 + seg
        shift = mods_ref[pl.ds(r_shift, 1), :]
        scale = mods_ref[pl.ds(r_scale, 1), :]
        _, vjp = jax.vjp(_rms_mod, x_ref[...], shift, scale)
        dx, dshift, dscale = vjp(dh_ref[...].astype(F32))
        dx_ref[...] = dres_ref[...] + dx
        dmods_ref[pl.ds(r_shift, 1), :] += dshift
        dmods_ref[pl.ds(r_scale, 1), :] += dscale

    row = pl.BlockSpec((tm, d), lambda i: (i, 0))
    vec = pl.BlockSpec((16, d), lambda i: (0, 0))
    return _pcall(
        body, grid=(t // tm,),
        in_specs=[row, vec, row, row], out_specs=[row, vec],
        out_shape=[jax.ShapeDtypeStruct((t, d), F32), jax.ShapeDtypeStruct((16, d), F32)],
        scratch_shapes=[], semantics=("arbitrary",), vmem_mb=40, name=name,
        operands=[xs, mods, dh, dres], rider=rider)


def _gate_bwd(dx, branch, mods, *, k_gate, n_x, name):
    t, d = dx.shape
    tm = _div(t, (256, 128))

    def body(dx_ref, br_ref, mods_ref, db_ref, dmods_ref):
        i = pl.program_id(0)
        _zero_accs(i, [dmods_ref])
        seg = (i * tm >= n_x).astype(jnp.int32)
        r_gate = 2 * k_gate + seg
        dxv = dx_ref[...]
        db_ref[...] = (dxv * mods_ref[pl.ds(r_gate, 1), :]).astype(BF16)
        dmods_ref[pl.ds(r_gate, 1), :] += jnp.sum(dxv * br_ref[...].astype(F32), axis=0, keepdims=True)

    row = pl.BlockSpec((tm, d), lambda i: (i, 0))
    vec = pl.BlockSpec((16, d), lambda i: (0, 0))
    return pl.pallas_call(
        body, grid=(t // tm,),
        in_specs=[row, row, vec], out_specs=[row, vec],
        out_shape=[jax.ShapeDtypeStruct((t, d), BF16), jax.ShapeDtypeStruct((16, d), F32)],
        compiler_params=_params(("arbitrary",), 32), name=name,
    )(dx, branch, mods)


def _final_loss(xs, g, target, *, name):
    t, d = xs.shape
    tm = _div(t, (256, 128))

    def loss_fn(x, gv, tgt):
        err = _rms(x) * gv - tgt
        return 0.5 * jnp.sum(jnp.mean(jnp.square(err), axis=-1))

    def body(x_ref, g_ref, t_ref, dx_ref, loss_ref, dg_ref):
        i = pl.program_id(0)
        _zero_accs(i, [loss_ref, dg_ref])
        val, vjp = jax.vjp(loss_fn, x_ref[...], g_ref[0:1, :], t_ref[...])
        dx, dg, _ = vjp(jnp.ones((), F32))
        dx_ref[...] = dx
        loss_ref[...] += val
        dg_ref[0:1, :] += dg

    row = pl.BlockSpec((tm, d), lambda i: (i, 0))
    return pl.pallas_call(
        body, grid=(t // tm,),
        in_specs=[row, pl.BlockSpec((8, d), lambda i: (0, 0)), row],
        out_specs=[row, pl.BlockSpec((8, 128), lambda i: (0, 0)), pl.BlockSpec((8, d), lambda i: (0, 0))],
        out_shape=[jax.ShapeDtypeStruct((t, d), F32), jax.ShapeDtypeStruct((8, 128), F32),
                   jax.ShapeDtypeStruct((8, d), F32)],
        compiler_params=_params(("arbitrary",), 40), name=name,
    )(xs, g, target)


def _halo_specs(r, width, col, t):
    h_per = r // HALO

    def prev(i):
        return (jnp.maximum(i * h_per - 1, 0), col)

    def nxt(i):
        return (jnp.minimum((i + 1) * h_per, t // HALO - 1), col)

    return (pl.BlockSpec((HALO, width), prev), pl.BlockSpec((r, width), lambda i: (i, col)),
            pl.BlockSpec((HALO, width), nxt))


def _seg_geometry(i, r, n_x, t):
    row0 = i * r
    in_ctx = row0 >= n_x
    first = jnp.logical_or(row0 == 0, row0 == n_x)
    last = jnp.logical_or(row0 + r == n_x, row0 + r == t)
    seg_start = jnp.where(in_ctx, n_x, 0)
    seg_len = jnp.where(in_ctx, t - n_x, n_x)
    return row0, first, last, seg_start, seg_len


def _pool_count(tpos, w, seg_len):
    return (jnp.minimum(tpos + w // 2, seg_len) - jnp.maximum(tpos - w // 2, 0)).astype(F32)


def _mixer0_fwd(z, conv_w, conv_b, ln_g, ln_b, pool_w, pool_scale, *, n_tap, n_x, name, rider=None):
    t, dc = z.shape[0], z.shape[1] // 3
    n_grp, pg = pool_w.shape[0], pool_w.shape[1]
    r = _div(t - n_x, (256, 128))
    assert n_x % r == 0 and pg * n_grp == dc
    half = n_tap // 2
    assert half < HALO and max(POOL_WINDOWS) // 2 <= HALO

    def body(ap, ac, an, gp, gc, gn, pp, pc, pn, w_ref, cb_ref, lg_ref, lb_ref, pw_ref, ps_ref,
             y_ref, cv_ref, d_ref, uwin, pwin):
        i = pl.program_id(0)
        row0, first, last, seg_start, seg_len = _seg_geometry(i, r, n_x, t)
        keep_prev = jnp.where(first, 0.0, 1.0)
        keep_next = jnp.where(last, 0.0, 1.0)
        uwin[0:HALO, :] = ap[...] * _sigmoid(gp[...]) * keep_prev
        uwin[HALO:HALO + r, :] = ac[...] * _sigmoid(gc[...])
        uwin[HALO + r:, :] = an[...] * _sigmoid(gn[...]) * keep_next
        pwin[0:HALO, :] = pp[...] * keep_prev
        pwin[HALO:HALO + r, :] = pc[...]
        pwin[HALO + r:, :] = pn[...] * keep_next
        acc = jnp.zeros((r, dc), F32) + cb_ref[0:1, :]
        for k in range(n_tap):
            off = HALO - half + k
            acc = acc + w_ref[k:k + 1, :] * uwin[off:off + r, :]
        cv_ref[...] = acc
        y_ref[:, 0:dc] = _ln_silu(acc, lg_ref[0:1, :], lb_ref[0:1, :]).astype(BF16)
        tpos = row0 - seg_start + lax.broadcasted_iota(jnp.int32, (r, 1), 0)
        for g, w in enumerate(POOL_WINDOWS):
            cols = slice(g * pg, (g + 1) * pg)
            s = jnp.zeros((r, pg), F32)
            for j in range(-(w // 2), w // 2):
                s = s + pwin[HALO + j:HALO + j + r, cols]
            diff = (s / _pool_count(tpos, w, seg_len) - pwin[HALO:HALO + r, cols]).astype(BF16)
            d_ref[:, cols] = diff
            pm = jnp.dot(diff, pw_ref[g].astype(BF16), preferred_element_type=F32)
            y_ref[:, dc + g * pg:dc + (g + 1) * pg] = (pm * ps_ref[0:1, cols]).astype(BF16)

    vec = lambda rows, width: pl.BlockSpec((rows, width), lambda i: (0, 0))
    in_specs = [*_halo_specs(r, dc, 0, t), *_halo_specs(r, dc, 1, t), *_halo_specs(r, dc, 2, t),
                vec(conv_w.shape[0], dc), vec(8, dc), vec(8, dc), vec(8, dc),
                pl.BlockSpec((n_grp, pg, pg), lambda i: (0, 0, 0)), vec(8, dc)]
    return _pcall(
        body, grid=(t // r,), in_specs=in_specs,
        out_specs=[pl.BlockSpec((r, 2 * dc), lambda i: (i, 0)), pl.BlockSpec((r, dc), lambda i: (i, 0)),
                   pl.BlockSpec((r, dc), lambda i: (i, 0))],
        out_shape=[jax.ShapeDtypeStruct((t, 2 * dc), BF16), jax.ShapeDtypeStruct((t, dc), F32),
                   jax.ShapeDtypeStruct((t, dc), BF16)],
        scratch_shapes=[pltpu.VMEM((r + 2 * HALO, dc), F32), pltpu.VMEM((r + 2 * HALO, dc), F32)],
        semantics=("parallel",), vmem_mb=40, name=name,
        operands=[*([z] * 9), conv_w, conv_b, ln_g, ln_b, pool_w, pool_scale], rider=rider)


def _mixer0_bwd(dy, z, cv, dsave, conv_w, ln_g, ln_b, pool_w, pool_scale, *, n_tap, n_x, name, rider=None):
    t, dc = cv.shape
    n_grp, pg = pool_w.shape[0], pool_w.shape[1]
    r = _div(t - n_x, (256, 128))
    half = n_tap // 2
    dyp_, zp_, cvp_ = dy, z, cv
    rw = r + 2 * HALO

    def body(dcp, dcc, dcn, dpp, dpc, dpn, cvp, cvc, cvn, ap, ac, an, gp, gc, gn, d_ref,
             w_ref, lg_ref, lb_ref, pw_ref, ps_ref,
             dz_ref, dw_ref, dcb_ref, dlg_ref, dlb_ref, dpw_ref, dps_ref, uwin, dcvwin, ewin):
        i = pl.program_id(0)
        _zero_accs(i, [dw_ref, dcb_ref, dlg_ref, dlb_ref, dpw_ref, dps_ref])
        row0, first, last, seg_start, seg_len = _seg_geometry(i, r, n_x, t)
        keep_prev = jnp.where(first, 0.0, 1.0)
        keep_next = jnp.where(last, 0.0, 1.0)
        lg, lb = lg_ref[0:1, :], lb_ref[0:1, :]
        _, vjp = jax.vjp(_ln_silu, cvc[...], lg, lb)
        dcv, dlg, dlb = vjp(dcc[...])
        dlg_ref[0:1, :] += dlg
        dlb_ref[0:1, :] += dlb
        dcb_ref[0:1, :] += jnp.sum(dcv, axis=0, keepdims=True)
        dcvwin[HALO:HALO + r, :] = dcv
        for halo_cv, halo_dy, keep, lo in ((cvp, dcp, keep_prev, 0), (cvn, dcn, keep_next, HALO + r)):
            _, vjp_h = jax.vjp(lambda v: _ln_silu(v, lg, lb), halo_cv[...])
            dcvwin[lo:lo + HALO, :] = vjp_h(halo_dy[...])[0] * keep
        uwin[0:HALO, :] = ap[...] * _sigmoid(gp[...]) * keep_prev
        sig = _sigmoid(gc[...])
        uwin[HALO:HALO + r, :] = ac[...] * sig
        uwin[HALO + r:, :] = an[...] * _sigmoid(gn[...]) * keep_next
        du = jnp.zeros((r, dc), F32)
        for k in range(n_tap):
            du = du + w_ref[k:k + 1, :] * dcvwin[HALO + half - k:HALO + half - k + r, :]
            off = HALO - half + k
            dw_ref[k:k + 1, :] += jnp.sum(dcv * uwin[off:off + r, :], axis=0, keepdims=True)
        dz_ref[:, 0:dc] = (du * sig).astype(BF16)
        dz_ref[:, dc:2 * dc] = (du * ac[...] * sig * (1.0 - sig)).astype(BF16)
        twin = row0 - seg_start - HALO + lax.broadcasted_iota(jnp.int32, (rw, 1), 0)
        for g, w in enumerate(POOL_WINDOWS):
            cols = slice(g * pg, (g + 1) * pg)
            wg = pw_ref[g].astype(BF16)
            scale = ps_ref[0:1, cols]
            dyp_c = dpc[:, cols]
            dpm_win = jnp.concatenate([dpp[:, cols] * keep_prev, dyp_c, dpn[:, cols] * keep_next], axis=0) * scale
            dd_win = lax.dot_general(dpm_win.astype(BF16), wg, (((1,), (1,)), ((), ())), preferred_element_type=F32)
            cnt = jnp.maximum(_pool_count(twin, w, seg_len), 1.0)
            ewin[:, cols] = dd_win / cnt
            dup = -dd_win[HALO:HALO + r, :]
            for j in range(-(w // 2) + 1, w // 2 + 1):
                dup = dup + ewin[HALO + j:HALO + j + r, cols]
            dz_ref[:, 2 * dc + g * pg:2 * dc + (g + 1) * pg] = dup.astype(BF16)
            dsv = d_ref[:, cols]
            pm = jnp.dot(dsv, wg, preferred_element_type=F32)
            dps_ref[0:1, cols] += jnp.sum(dyp_c * pm, axis=0, keepdims=True)
            dpw_ref[g] += lax.dot_general(dsv, (dyp_c * scale).astype(BF16), (((0,), (0,)), ((), ())),
                                          preferred_element_type=F32)

    vec = lambda rows, width: pl.BlockSpec((rows, width), lambda i: (0, 0))
    grp = pl.BlockSpec((n_grp, pg, pg), lambda i: (0, 0, 0))
    in_specs = [*_halo_specs(r, dc, 0, t), *_halo_specs(r, dc, 1, t), *_halo_specs(r, dc, 0, t),
                *_halo_specs(r, dc, 0, t), *_halo_specs(r, dc, 1, t), pl.BlockSpec((r, dc), lambda i: (i, 0)),
                vec(conv_w.shape[0], dc), vec(8, dc), vec(8, dc), grp, vec(8, dc)]
    return _pcall(
        body, grid=(t // r,), in_specs=in_specs,
        out_specs=[pl.BlockSpec((r, 3 * dc), lambda i: (i, 0)), vec(conv_w.shape[0], dc), vec(8, dc), vec(8, dc),
                   vec(8, dc), grp, vec(8, dc)],
        out_shape=[jax.ShapeDtypeStruct((t, 3 * dc), BF16), jax.ShapeDtypeStruct(conv_w.shape, F32),
                   jax.ShapeDtypeStruct((8, dc), F32), jax.ShapeDtypeStruct((8, dc), F32),
                   jax.ShapeDtypeStruct((8, dc), F32), jax.ShapeDtypeStruct(pool_w.shape, F32),
                   jax.ShapeDtypeStruct((8, dc), F32)],
        scratch_shapes=[pltpu.VMEM((rw, dc), F32), pltpu.VMEM((rw, dc), F32), pltpu.VMEM((rw, dc), F32)],
        semantics=("arbitrary",), vmem_mb=48, name=name,
        operands=[dyp_, dyp_, dyp_, dyp_, dyp_, dyp_, cvp_, cvp_, cvp_, zp_, zp_, zp_, zp_, zp_, zp_, dsave,
                  conv_w, ln_g, ln_b, pool_w, pool_scale], rider=rider)


def _swap_halves(x):
    lane = lax.broadcasted_iota(jnp.int32, x.shape, 1)
    quarter = HEAD_DIM // 4
    return jnp.where(lane % (2 * quarter) < quarter,
                     pltpu.roll(x, HEAD_DIM - quarter, 1), pltpu.roll(x, quarter, 1))


def _rope_tables(n_x, n_ctx):
    half = HEAD_DIM // 4
    freqs = ROPE_THETA ** (-jnp.arange(half, dtype=F32) / half)
    tok = jnp.arange(n_x)
    row = (tok // GRID_W).astype(F32)[:, None] * freqs[None, :]
    col = (tok % GRID_W).astype(F32)[:, None] * freqs[None, :]
    cos = jnp.concatenate([jnp.cos(row), jnp.cos(row), jnp.cos(col), jnp.cos(col)], axis=1)
    sin = jnp.concatenate([-jnp.sin(row), jnp.sin(row), -jnp.sin(col), jnp.sin(col)], axis=1)
    cos = jnp.concatenate([cos, jnp.ones((n_ctx, HEAD_DIM), F32)], axis=0)
    sin = jnp.concatenate([sin, jnp.zeros((n_ctx, HEAD_DIM), F32)], axis=0)
    return cos, sin


def _norm_g(x, g):
    return _rms(x) * g


def _qk_prep_fwd(qkv, gq, gk, cos, sin, *, d_q, name):
    t, width = qkv.shape
    d_kv = (width - d_q) // 2
    tm = _div(t, (256, 128))

    def body(qkv_ref, gq_ref, gk_ref, cos_ref, sin_ref, q_ref, k_ref, v_ref):
        cs, sn = cos_ref[...], sin_ref[...]
        for h in range((d_q + d_kv) // HEAD_DIM):
            g = gq_ref[0:1, :] if h * HEAD_DIM < d_q else gk_ref[0:1, :]
            xn = _norm_g(qkv_ref[:, h * HEAD_DIM:(h + 1) * HEAD_DIM], g)
            rot = xn * cs + _swap_halves(xn) * sn
            if h * HEAD_DIM < d_q:
                q_ref[:, h * HEAD_DIM:(h + 1) * HEAD_DIM] = (rot * Q_SCALE_LOG2).astype(BF16)
            else:
                k_ref[:, h * HEAD_DIM - d_q:(h + 1) * HEAD_DIM - d_q] = rot.astype(BF16)
        v_ref[...] = qkv_ref[:, d_q + d_kv:].astype(BF16)

    row = lambda w: pl.BlockSpec((tm, w), lambda i: (i, 0))
    vec = pl.BlockSpec((8, HEAD_DIM), lambda i: (0, 0))
    return pl.pallas_call(
        body, grid=(t // tm,),
        in_specs=[row(width), vec, vec, row(HEAD_DIM), row(HEAD_DIM)],
        out_specs=[row(d_q), row(d_kv), row(d_kv)],
        out_shape=[jax.ShapeDtypeStruct((t, d_q), BF16), jax.ShapeDtypeStruct((t, d_kv), BF16),
                   jax.ShapeDtypeStruct((t, d_kv), BF16)],
        compiler_params=_params(("parallel",), 32), name=name,
    )(qkv, gq, gk, cos, sin)


def _qk_prep_bwd(qkv, dq, dk, dv, gq, gk, cos, sin, *, n_x, name):
    t, width = qkv.shape
    d_q, d_kv = dq.shape[1], dk.shape[1]
    tm = _div(t, (256, 128))
    last_q = n_x // tm - 1

    def body(qkv_ref, dq_ref, dk_ref, dv_ref, gq_ref, gk_ref, cos_ref, sin_ref, out_ref, dgq_ref, dgk_ref):
        i = pl.program_id(0)
        _zero_accs(i, [dgq_ref, dgk_ref])
        is_x = jnp.where(i * tm < n_x, 1.0, 0.0)
        cs, sn = cos_ref[...], sin_ref[...]
        for h in range((d_q + d_kv) // HEAD_DIM):
            sl = slice(h * HEAD_DIM, (h + 1) * HEAD_DIM)
            if h * HEAD_DIM < d_q:
                g, dg_ref, dr = gq_ref[0:1, :], dgq_ref, dq_ref[:, sl] * is_x
            else:
                g, dg_ref = gk_ref[0:1, :], dgk_ref
                dr = dk_ref[:, h * HEAD_DIM - d_q:(h + 1) * HEAD_DIM - d_q]
            dxn = dr * cs + _swap_halves(dr * sn)
            _, vjp = jax.vjp(_norm_g, qkv_ref[:, sl], g)
            dx, dg = vjp(dxn)
            out_ref[:, sl] = dx.astype(BF16)
            dg_ref[0:1, :] += dg
        out_ref[:, d_q + d_kv:] = dv_ref[...].astype(BF16)

    row = lambda w: pl.BlockSpec((tm, w), lambda i: (i, 0))
    vec = pl.BlockSpec((8, HEAD_DIM), lambda i: (0, 0))
    return pl.pallas_call(
        body, grid=(t // tm,),
        in_specs=[row(width), pl.BlockSpec((tm, d_q), lambda i: (jnp.minimum(i, last_q), 0)), row(d_kv), row(d_kv),
                  vec, vec, row(HEAD_DIM), row(HEAD_DIM)],
        out_specs=[row(width), vec, vec],
        out_shape=[jax.ShapeDtypeStruct((t, width), BF16), jax.ShapeDtypeStruct((8, HEAD_DIM), F32),
                   jax.ShapeDtypeStruct((8, HEAD_DIM), F32)],
        compiler_params=_params(("arbitrary",), 40), name=name,
    )(qkv, dq, dk, dv, gq, gk, cos, sin)


ATTN_TQ = (256, 128)
ATTN_TK = (768, 640, 512, 384, 256, 128)


def _attention_fwd(q, k, v, *, n_x, name):
    t, d_kv = k.shape
    d_q = q.shape[1]
    kvh = d_kv // HEAD_DIM
    grp = d_q // d_kv
    tq = _div(n_x, ATTN_TQ)
    tk = _div(t, ATTN_TK)
    gw = grp * HEAD_DIM

    n_kv = t // tk
    n_pair = (n_kv - 1) // 2

    def body(q_ref, k_ref, v_ref, o_ref, lse_ref, m_ref, l_ref, acc_ref, s_even, s_odd):
        m_ref[...] = jnp.full_like(m_ref, -jnp.inf)
        l_ref[...] = jnp.zeros_like(l_ref)
        acc_ref[...] = jnp.zeros_like(acc_ref)

        def key_rows(j):
            return pl.ds(pl.multiple_of(j * tk, tk), tk)

        def scores(g, kc):
            return lax.dot_general(q_ref[:, g * HEAD_DIM:(g + 1) * HEAD_DIM], kc, (((1,), (1,)), ((), ())),
                                   preferred_element_type=F32)

        def chunk(j, s_cur, s_next):
            vc = v_ref[key_rows(j), :]
            kn = k_ref[key_rows(j + 1), :] if s_next is not None else None
            for g in range(grp):
                s = s_cur[g]
                m_old = m_ref[g]
                m_new = jnp.maximum(m_old, jnp.max(s, axis=-1, keepdims=True))
                alpha = jnp.exp2(m_old - m_new)
                p = jnp.exp2(s - m_new)
                if s_next is not None:
                    s_next[g] = scores(g, kn)
                l_ref[g] = alpha * l_ref[g] + jnp.sum(p, axis=-1, keepdims=True)
                acc_ref[g] = alpha * acc_ref[g] + jnp.dot(p.astype(BF16), vc, preferred_element_type=F32)
                m_ref[g] = m_new

        k0 = k_ref[key_rows(0), :]
        for g in range(grp):
            s_even[g] = scores(g, k0)

        def pair(i, carry):
            chunk(2 * i, s_even, s_odd)
            chunk(2 * i + 1, s_odd, s_even)
            return carry

        lax.fori_loop(0, n_pair, pair, 0)
        if n_kv - 2 * n_pair == 2:
            chunk(n_kv - 2, s_even, s_odd)
            chunk(n_kv - 1, s_odd, None)
        else:
            chunk(n_kv - 1, s_even, None)
        for g in range(grp):
            o_ref[:, g * HEAD_DIM:(g + 1) * HEAD_DIM] = (acc_ref[g] / l_ref[g]).astype(BF16)
            lse_ref[:, g:g + 1] = m_ref[g] + jnp.log(l_ref[g]) * LOG2_E

    return pl.pallas_call(
        body, grid=(kvh, n_x // tq),
        in_specs=[pl.BlockSpec((tq, gw), lambda h, i: (i, h)),
                  pl.BlockSpec((t, HEAD_DIM), lambda h, i: (0, h)),
                  pl.BlockSpec((t, HEAD_DIM), lambda h, i: (0, h))],
        out_specs=[pl.BlockSpec((tq, gw), lambda h, i: (i, h)),
                   pl.BlockSpec((None, tq, grp), lambda h, i: (h, i, 0))],
        out_shape=[jax.ShapeDtypeStruct((n_x, d_q), BF16), jax.ShapeDtypeStruct((kvh, n_x, grp), F32)],
        scratch_shapes=[pltpu.VMEM((grp, tq, 1), F32), pltpu.VMEM((grp, tq, 1), F32),
                        pltpu.VMEM((grp, tq, HEAD_DIM), F32),
                        pltpu.VMEM((grp, tq, tk), F32), pltpu.VMEM((grp, tq, tk), F32)],
        compiler_params=_params(("parallel", "arbitrary"), 48), name=name,
    )(q, k, v)


def _attention_bwd(q, k, v, o, lse, do, *, n_x, name):
    t, d_kv = k.shape
    d_q = q.shape[1]
    kvh = d_kv // HEAD_DIM
    grp = d_q // d_kv
    tq = _div(n_x, ATTN_TQ)
    tk = _div(t, ATTN_TK)
    gw = grp * HEAD_DIM
    n_q = n_x // tq

    def body(q_ref, k_ref, v_ref, o_ref, lse_ref, do_ref, dq_ref, dk_ref, dv_ref, dq_acc, lse_s, delta_s):
        i = pl.program_id(1)
        _zero_accs(i, [dk_ref, dv_ref])
        dq_acc[...] = jnp.zeros_like(dq_acc)
        for g in range(grp):
            sl = slice(g * HEAD_DIM, (g + 1) * HEAD_DIM)
            lse_s[g] = lse_ref[:, g:g + 1]
            delta_s[g] = jnp.sum(do_ref[:, sl].astype(F32) * o_ref[:, sl].astype(F32), axis=-1, keepdims=True)

        def step(j, carry):
            rows = pl.ds(pl.multiple_of(j * tk, tk), tk)
            kc, vc = k_ref[rows, :], v_ref[rows, :]
            dk_part = jnp.zeros((tk, HEAD_DIM), F32)
            dv_part = jnp.zeros((tk, HEAD_DIM), F32)
            for g in range(grp):
                sl = slice(g * HEAD_DIM, (g + 1) * HEAD_DIM)
                qg, dog = q_ref[:, sl], do_ref[:, sl]
                s = lax.dot_general(qg, kc, (((1,), (1,)), ((), ())), preferred_element_type=F32)
                p = jnp.exp2(s - lse_s[g])
                dp = lax.dot_general(dog, vc, (((1,), (1,)), ((), ())), preferred_element_type=F32)
                ds = (p * (dp - delta_s[g])).astype(BF16)
                dq_acc[g] += jnp.dot(ds, kc, preferred_element_type=F32)
                dv_part = dv_part + lax.dot_general(p.astype(BF16), dog, (((0,), (0,)), ((), ())),
                                                    preferred_element_type=F32)
                dk_part = dk_part + lax.dot_general(ds, qg, (((0,), (0,)), ((), ())), preferred_element_type=F32)
            dv_ref[rows, :] += dv_part
            dk_ref[rows, :] += dk_part
            return carry

        lax.fori_loop(0, t // tk, step, 0)
        for g in range(grp):
            dq_ref[:, g * HEAD_DIM:(g + 1) * HEAD_DIM] = dq_acc[g] * ATTN_SCALE

        @pl.when(i == n_q - 1)
        def _():
            dk_ref[...] = dk_ref[...] * (1.0 / LOG2_E)

    qspec = pl.BlockSpec((tq, gw), lambda h, i: (i, h))
    kspec = pl.BlockSpec((t, HEAD_DIM), lambda h, i: (0, h))
    return pl.pallas_call(
        body, grid=(kvh, n_q),
        in_specs=[qspec, kspec, kspec, qspec, pl.BlockSpec((None, tq, grp), lambda h, i: (h, i, 0)), qspec],
        out_specs=[qspec, kspec, kspec],
        out_shape=[jax.ShapeDtypeStruct((n_x, d_q), F32), jax.ShapeDtypeStruct((t, d_kv), F32),
                   jax.ShapeDtypeStruct((t, d_kv), F32)],
        scratch_shapes=[pltpu.VMEM((grp, tq, HEAD_DIM), F32), pltpu.VMEM((grp, tq, 1), F32),
                        pltpu.VMEM((grp, tq, 1), F32)],
        compiler_params=_params(("parallel", "arbitrary"), 56), name=name,
    )(q, k, v, o, lse, do)


def _whole(body, ins, out_shapes, name):
    return pl.pallas_call(
        body, out_shape=[jax.ShapeDtypeStruct(s, d) for s, d in out_shapes],
        compiler_params=pltpu.CompilerParams(vmem_limit_bytes=40 << 20), name=name)(*ins)


def _silu_rows(x, *, name):
    def body(x_ref, o_ref):
        o_ref[...] = _silu(x_ref[...])
    return _whole(body, [x], [(x.shape, F32)], name)[0]


def _assemble_dmods(gathered, *, name):
    width = gathered.shape[1]

    def body(g_ref, dm0, dm1, db0, db1):
        for l, (dm, db) in enumerate(((dm0, db0), (dm1, db1))):
            ctx = jnp.zeros((1, width), F32)
            tot = jnp.zeros((1, width), F32)
            for q in range(N_DEV):
                row = g_ref[16 * q + 8 * l:16 * q + 8 * l + 1, :]
                dm[q:q + 1, :] = row
                tot = tot + row
                ctx = ctx + g_ref[16 * q + 8 * l + 1:16 * q + 8 * l + 2, :]
            dm[N_DEV:N_DEV + 1, :] = ctx
            dm[N_DEV + 1:, :] = jnp.zeros((16 - N_DEV - 1, width), F32)
            db[...] = jnp.zeros_like(db)
            db[0:1, :] = tot + ctx

    return _whole(body, [gathered], [((16, width), F32), ((16, width), F32), ((8, width), F32), ((8, width), F32)],
                  name)


def _sum_slots(gathered, rows, *, name):
    def body(g_ref, o_ref):
        acc = g_ref[0:rows, :]
        for q in range(1, N_DEV):
            acc = acc + g_ref[q * rows:(q + 1) * rows, :]
        o_ref[...] = acc
    return _whole(body, [gathered], [((rows, gathered.shape[1]), F32)], name)[0]


def _silu_grad(x, dy, *, name):
    def body(x_ref, dy_ref, o_ref):
        _, vjp = jax.vjp(_silu, x_ref[...])
        o_ref[...] = vjp(dy_ref[...])[0]
    return _whole(body, [x, dy], [(x.shape, F32)], name)[0]


def _adamw(w, m, v, *, name, recv=None, grad=None):
    rows, cols = w.shape
    budget = max(8, ADAMW_BLOCK_ELEMS // cols)
    tr = _div(rows, [c for c in (512, 256, 128, 64, 32, 16, 8) if c <= budget] + [rows])
    c1 = 1.0 - ADAM_B1 ** ADAM_STEP
    c2 = 1.0 - ADAM_B2 ** ADAM_STEP

    def body(w_ref, m_ref, v_ref, g_in, g_ref, d_ref, nm_ref, nv_ref):
        if recv is not None:
            g = g_in[0].astype(F32)
            for q in range(1, N_DEV):
                g = g + g_in[q].astype(F32)
        else:
            g = g_in[...]
        nm = ADAM_B1 * m_ref[...] + (1.0 - ADAM_B1) * g
        nv = ADAM_B2 * v_ref[...] + (1.0 - ADAM_B2) * jnp.square(g)
        g_ref[...] = g
        nm_ref[...] = nm
        nv_ref[...] = nv
        d_ref[...] = -ADAM_LR * ((nm / c1) / (jnp.sqrt(nv / c2) + ADAM_EPS) + ADAM_WD * w_ref[...])

    blk = pl.BlockSpec((tr, cols), lambda i: (i, 0))
    g_spec = pl.BlockSpec((N_DEV, tr, cols), lambda i: (0, i, 0)) if recv is not None else blk
    return pl.pallas_call(
        body, grid=(rows // tr,), in_specs=[blk, blk, blk, g_spec], out_specs=[blk] * 4,
        out_shape=[jax.ShapeDtypeStruct((rows, cols), F32)] * 4,
        compiler_params=_params(("parallel",), 48), name=name,
    )(w, m, v, recv if recv is not None else grad)


def _position():
    return tuple(lax.axis_index(a) for a in MESH_AXES)


def _linear(pos):
    return 4 * pos[0] + 2 * pos[1] + pos[2]


def _window(ref, axis, dev, size):
    start = pl.multiple_of(dev * size, size)
    return ref.at[pl.ds(start, size), :] if axis == 0 else ref.at[:, pl.ds(start, size)]


def _all_gather(shards, axes, *, name):
    n = len(shards)
    sizes = [s.shape[ax] for s, ax in zip(shards, axes)]

    def body(*refs):
        src, dst = refs[:n], refs[n:2 * n]
        send_sems, recv_sems, local_sems = refs[2 * n:]
        x, y, c = _position()
        me, sibling = (x, y, c), (x, y, 1 - c)
        chips = [(1 - x, y), (x, 1 - y), (1 - x, 1 - y)]

        def win(k, pos):
            return _window(dst[k], axes[k], _linear(pos), sizes[k])

        def copy(k, sem, block, to, from_src=False):
            return pltpu.make_async_remote_copy(
                src_ref=src[k] if from_src else win(k, block), dst_ref=win(k, block),
                send_sem=send_sems.at[k, sem], recv_sem=recv_sems.at[k, sem],
                device_id=to, device_id_type=MESH_ID)

        mine = [pltpu.make_async_copy(src[k], win(k, me), local_sems.at[k]) for k in range(n)]
        for cp in mine:
            cp.start()
        first = []
        for k in range(n):
            first.append(copy(k, 0, me, sibling, from_src=True))
            first += [copy(k, 1 + j, me, (*chip, c), from_src=True) for j, chip in enumerate(chips)]
        for cp in first:
            cp.start()
        passed = []
        for j, chip in enumerate(chips):
            for k in range(n):
                copy(k, 1 + j, (*chip, c), me).wait_recv()
                fwd = copy(k, 4 + j, (*chip, c), sibling)
                fwd.start()
                passed.append(fwd)
        for k in range(n):
            copy(k, 0, sibling, me).wait_recv()
            for j, chip in enumerate(chips):
                copy(k, 4 + j, (*chip, 1 - c), me).wait_recv()
        for cp in first + passed:
            cp.wait_send()
        for cp in mine:
            cp.wait()

    out_shape = []
    for s, ax in zip(shards, axes):
        full = (s.shape[0] * N_DEV, s.shape[1]) if ax == 0 else (s.shape[0], s.shape[1] * N_DEV)
        out_shape.append(jax.ShapeDtypeStruct(full, s.dtype))
    any_spec = pl.BlockSpec(memory_space=pl.ANY)
    return pl.pallas_call(
        body, in_specs=[any_spec] * n, out_specs=[any_spec] * n, out_shape=out_shape,
        scratch_shapes=[pltpu.SemaphoreType.DMA((n, 7)), pltpu.SemaphoreType.DMA((n, 7)),
                        pltpu.SemaphoreType.DMA((n,))],
        name=name,
    )(*shards)


class Rider:
    def __init__(self, kind, arrays, axes):
        self.kind, self.arrays, self.axes = kind, list(arrays), list(axes)
        self.n = len(self.arrays)
        if kind == "gather":
            self.sizes = [a.shape[ax] for a, ax in zip(self.arrays, self.axes)]
        else:
            self.sizes = [a.shape[ax] // N_DEV for a, ax in zip(self.arrays, self.axes)]

    def out_shape(self):
        shapes = []
        for a, ax, sz in zip(self.arrays, self.axes, self.sizes):
            if self.kind == "gather":
                full = (sz * N_DEV, a.shape[1]) if ax == 0 else (a.shape[0], sz * N_DEV)
                shapes.append(jax.ShapeDtypeStruct(full, a.dtype))
            else:
                shard = (sz, a.shape[1]) if ax == 0 else (a.shape[0], sz)
                shapes.append(jax.ShapeDtypeStruct((N_DEV, *shard), a.dtype))
        return shapes

    def scratch(self):
        return [pltpu.SemaphoreType.DMA((self.n, N_DEV - 1)), pltpu.SemaphoreType.DMA((self.n, N_DEV - 1)),
                pltpu.SemaphoreType.DMA((self.n,))]

    def plan(self, src, dst, send_sems, recv_sems, local_sems):
        me = _position()
        mine = _linear(me)

        def peer(mask):
            return tuple(1 - p if (mask >> (2 - b)) & 1 else p for b, p in enumerate(me))

        def remote(k, mask, src_ref, dst_ref):
            return pltpu.make_async_remote_copy(
                src_ref=src_ref, dst_ref=dst_ref, send_sem=send_sems.at[k, mask - 1],
                recv_sem=recv_sems.at[k, mask - 1], device_id=peer(mask), device_id_type=MESH_ID)

        local, sends, arrivals = [], [], []
        for k in range(self.n):
            ax, sz = self.axes[k], self.sizes[k]
            if self.kind == "gather":
                own_src, own_dst = src[k], _window(dst[k], ax, mine, sz)
            else:
                own_src, own_dst = _window(src[k], ax, mine, sz), dst[k].at[mine]
            local.append(pltpu.make_async_copy(own_src, own_dst, local_sems.at[k]))
            for mask in range(1, N_DEV):
                other = _linear(peer(mask))
                if self.kind == "gather":
                    sends.append(remote(k, mask, src[k], own_dst))
                    arrivals.append(remote(k, mask, src[k], _window(dst[k], ax, other, sz)))
                else:
                    sends.append(remote(k, mask, _window(src[k], ax, other, sz), own_dst))
                    arrivals.append(remote(k, mask, own_src, dst[k].at[other]))
        return local, sends, arrivals


def _pcall(body, *, grid, in_specs, out_specs, out_shape, scratch_shapes, semantics, vmem_mb, name, operands,
           rider=None):
    if rider is None:
        return pl.pallas_call(body, grid=grid, in_specs=in_specs, out_specs=out_specs, out_shape=out_shape,
                              scratch_shapes=scratch_shapes, compiler_params=_params(semantics, vmem_mb),
                              name=name)(*operands)
    n_in, n_out, n_scr, n = len(in_specs), len(out_specs), len(scratch_shapes), rider.n

    def wrapped(*refs):
        ins, src = refs[:n_in], refs[n_in:n_in + n]
        outs = refs[n_in + n:n_in + n + n_out]
        dst = refs[n_in + n + n_out:n_in + 2 * n + n_out]
        rest = refs[n_in + 2 * n + n_out:]
        scratch, sems = rest[:n_scr], rest[n_scr:]
        ids = [pl.program_id(dim) for dim in range(len(grid))]
        first = functools.reduce(jnp.logical_and, [i == 0 for i in ids])
        last = functools.reduce(jnp.logical_and, [i == g - 1 for i, g in zip(ids, grid)])

        @pl.when(first)
        def _():
            local, sends, _ = rider.plan(src, dst, *sems)
            for cp in local + sends:
                cp.start()

        body(*ins, *outs, *scratch)

        @pl.when(last)
        def _():
            local, sends, arrivals = rider.plan(src, dst, *sems)
            for cp in arrivals:
                cp.wait_recv()
            for cp in sends:
                cp.wait_send()
            for cp in local:
                cp.wait()

    any_spec = pl.BlockSpec(memory_space=pl.ANY)
    return pl.pallas_call(
        wrapped, grid=grid, in_specs=list(in_specs) + [any_spec] * n, out_specs=list(out_specs) + [any_spec] * n,
        out_shape=list(out_shape) + rider.out_shape(), scratch_shapes=list(scratch_shapes) + rider.scratch(),
        compiler_params=_params(("arbitrary",) * len(grid), vmem_mb), name=name,
    )(*operands, *rider.arrays)


WEIGHTS = ['c_ctx', 'l0_ada_w', 'l0_ada_b', 'l0_in_w', 'l0_conv_w', 'l0_conv_b', 'l0_conv_ln_g', 'l0_conv_ln_b',
           'l0_pool_w', 'l0_pool_scale', 'l0_out_w', 'l0_mlp_w1', 'l0_mlp_w2', 'l1_ada_w', 'l1_ada_b', 'l1_qkv_w',
           'l1_q_norm_g', 'l1_k_norm_g', 'l1_out_w', 'l1_mlp_w1', 'l1_mlp_w2', 'final_g']
SHARDED = {'l0_in_w': 1, 'l0_out_w': 0, 'l0_mlp_w1': 1, 'l0_mlp_w2': 0,
           'l1_qkv_w': 1, 'l1_out_w': 0, 'l1_mlp_w1': 1, 'l1_mlp_w2': 0}
REPLICATED_SMALL = ['l0_conv_b', 'l0_conv_ln_g', 'l0_conv_ln_b', 'l0_pool_scale', 'l1_q_norm_g', 'l1_k_norm_g',
                    'final_g']


def _row8(v):
    v = v.reshape(1, -1)
    return jnp.pad(v, ((0, 7), (0, 0)))


def _mods16(full, me):
    d = full.shape[1] // 6
    mine = lax.dynamic_slice_in_dim(full, me, 1, axis=0).reshape(6, d)
    ctx = full[N_DEV].reshape(6, d)
    return jnp.pad(jnp.stack([mine, ctx], axis=1).reshape(12, d), ((0, 4), (0, 0)))


def _mlp_fwd(xs, mods, w1, w2, *, n_x, tm, tag, rider1=None, rider2=None):
    t, d = xs.shape
    dff = w1.shape[1]
    h = _rms_mod_fwd(xs, mods, k_shift=3, n_x=n_x, name=f"{tag}_norm2")
    pre, act, *ride1 = _matmul(h, w1, mode="nn", tm=tm, tn=_div(dff, (1024, 512)), tk=d, out_dtypes=[BF16, BF16],
                               name=f"{tag}_mlp1", rider=rider1,
                               epilogue=lambda acc, rows: (acc, jnp.square(jnp.maximum(acc, 0.0))))
    w2 = w2 if w2 is not None else ride1[-1]
    xo, branch, *ride2 = _matmul(
        act, w2, mode="nn", tm=tm, tn=_div(d, (1024, 512)), tk=_div(dff, (2048, 1024)), out_dtypes=[F32, BF16],
        name=f"{tag}_mlp2", extras=[("tile", xs), ("vec", mods)], rider=rider2,
        epilogue=lambda acc, rows, res, mv: (res + _seg_pick(mv, 5, rows, n_x) * acc, acc))
    return xo, dict(h=h, pre=pre, act=act, branch=branch, x_in=xs), ride1, ride2


def _mlp_bwd(dxo, saved, mods, w1, w2, *, n_x, tm, tag, ride_dx2=(), ride_dx1=()):
    t, d = dxo.shape
    dff = w1.shape[1]
    tkt = _div(t, (768, 1024, 512, 640, 128))

    def exchange(items):
        return Rider("exchange", [a for a, _ in items], [ax for _, ax in items]) if items else None

    dbranch, dm_gate = _gate_bwd(dxo, saved["branch"], mods, k_gate=5, n_x=n_x, name=f"{tag}_gate2_bwd")
    dpre, *recv2 = _matmul(dbranch, w2, mode="nt", tm=tm, tn=_div(dff, (1024, 512)), tk=d, out_dtypes=[BF16],
                           name=f"{tag}_mlp2_dx", extras=[("tile", saved["pre"])], rider=exchange(list(ride_dx2)),
                           epilogue=lambda acc, rows, pre: (acc * 2.0 * jnp.maximum(pre.astype(F32), 0.0),))
    dw2, = _matmul(saved["act"], dbranch, mode="tn", tm=_div(dff, (1024, 512)), tn=_div(d, (1024, 512)), tk=tkt,
                   out_dtypes=[BF16], name=f"{tag}_mlp2_dw")
    dh, *recv1 = _matmul(dpre, w1, mode="nt", tm=tm, tn=_div(d, (1024, 512)), tk=_div(dff, (2048, 1024)),
                         out_dtypes=[F32], name=f"{tag}_mlp1_dx", rider=exchange(list(ride_dx1) + [(dw2, 0)]))
    dw1, = _matmul(saved["h"], dpre, mode="tn", tm=_div(d, (1024, 512)), tn=_div(dff, (1024, 512)), tk=tkt,
                   out_dtypes=[BF16], name=f"{tag}_mlp1_dw")
    dx, dm_norm = _rms_mod_bwd(saved["x_in"], mods, dh, dxo, k_shift=3, n_x=n_x, name=f"{tag}_norm2_bwd")
    return dx, dw1, dm_gate + dm_norm, recv2, recv1


def kernel(x, c, ctx, c_ctx, l0_ada_w, l0_ada_b, l0_in_w, l0_conv_w, l0_conv_b, l0_conv_ln_g, l0_conv_ln_b, l0_pool_w, l0_pool_scale, l0_out_w, l0_mlp_w1, l0_mlp_w2, l1_ada_w, l1_ada_b, l1_qkv_w, l1_q_norm_g, l1_k_norm_g, l1_out_w, l1_mlp_w1, l1_mlp_w2, final_g, loss_target, m_c_ctx, m_l0_ada_w, m_l0_ada_b, m_l0_in_w, m_l0_conv_w, m_l0_conv_b, m_l0_conv_ln_g, m_l0_conv_ln_b, m_l0_pool_w, m_l0_pool_scale, m_l0_out_w, m_l0_mlp_w1, m_l0_mlp_w2, m_l1_ada_w, m_l1_ada_b, m_l1_qkv_w, m_l1_q_norm_g, m_l1_k_norm_g, m_l1_out_w, m_l1_mlp_w1, m_l1_mlp_w2, m_final_g, v_c_ctx, v_l0_ada_w, v_l0_ada_b, v_l0_in_w, v_l0_conv_w, v_l0_conv_b, v_l0_conv_ln_g, v_l0_conv_ln_b, v_l0_pool_w, v_l0_pool_scale, v_l0_out_w, v_l0_mlp_w1, v_l0_mlp_w2, v_l1_ada_w, v_l1_ada_b, v_l1_qkv_w, v_l1_q_norm_g, v_l1_k_norm_g, v_l1_out_w, v_l1_mlp_w1, v_l1_mlp_w2, v_final_g):
    p = dict(locals())
    me = _linear(_position())
    n_x, d = x.shape[1], x.shape[2]
    n_ctx = ctx.shape[1]
    t = n_x + n_ctx
    dc = l0_conv_b.shape[0]
    n_tap = l0_conv_w.shape[0]
    d_q = d
    n_mod = l0_ada_b.shape[0] // d
    ada_cols = l0_ada_w.shape[1]
    tm_t = _div(t, (768, 640, 512, 128))
    tm_x = _div(n_x, (1024, 512))

    names = list(SHARDED)
    shard16 = {nm: p[nm].astype(BF16) for nm in names}

    def gather_of(*nms):
        return Rider("gather", [shard16[nm] for nm in nms], [SHARDED[nm] for nm in nms])

    wfull = {}
    first = ['l0_in_w', 'l0_out_w', 'l0_mlp_w1']
    *full, conv_w_full, pool_w_full = _all_gather(
        [shard16[nm] for nm in first] + [jnp.pad(l0_conv_w, ((0, 1), (0, 0))),
                                         l0_pool_w.reshape(-1, l0_pool_w.shape[2])],
        [SHARDED[nm] for nm in first] + [1, 0], name="gather_first_weights")
    wfull.update(zip(first, full))
    n_grp, pg = l0_pool_w.shape[0], l0_pool_w.shape[2]
    pool_w_full = pool_w_full.reshape(N_DEV, n_grp, pg // N_DEV, pg).transpose(1, 0, 2, 3).reshape(n_grp, pg, pg)

    c_all = _all_gather([_row8(c)], [0], name="gather_cond")[0].reshape(N_DEV, 8, d)[:, 0]
    cond = jnp.concatenate([c_all, c_ctx.reshape(1, d), jnp.zeros((16 - N_DEV - 1, d), F32)], axis=0)
    s16 = _silu_rows(cond, name="cond_silu")
    mod_shards = []
    for li, (lw, lb) in enumerate(((l0_ada_w, l0_ada_b), (l1_ada_w, l1_ada_b))):
        bias = _row8(lax.dynamic_slice_in_dim(lb, me * ada_cols, ada_cols))
        mod_shards.append(_matmul(s16, lw, mode="nn", tm=16, tn=_div(ada_cols, (512, 384, 256, 128)), tk=d,
                                  out_dtypes=[F32], name=f"l{li}_ada_fwd", extras=[("vec", bias)],
                                  epilogue=lambda acc, rows, b: (acc + b[0:1],))[0])
    mods_full = _all_gather([jnp.concatenate(mod_shards, axis=0)], [1], name="gather_mods")[0]
    mods0, mods1 = _mods16(mods_full[:16], me), _mods16(mods_full[16:], me)

    xs0 = jnp.concatenate([x[0], ctx[0]], axis=0)
    h0 = _rms_mod_fwd(xs0, mods0, k_shift=0, n_x=n_x, name="l0_norm1")
    z, = _matmul(h0, wfull['l0_in_w'], mode="nn", tm=tm_t, tn=_div(3 * dc, (1024, 768, 512, 384)), tk=d,
                 out_dtypes=[F32], name="l0_in_proj")
    y0, cv, dsave, wfull['l0_mlp_w2'] = _mixer0_fwd(
        z, conv_w_full, _row8(l0_conv_b), _row8(l0_conv_ln_g), _row8(l0_conv_ln_b), pool_w_full,
        _row8(l0_pool_scale), n_tap=n_tap, n_x=n_x, name="l0_mixer", rider=gather_of('l0_mlp_w2'))
    xs1, mix0 = _matmul(y0, wfull['l0_out_w'], mode="nn", tm=tm_t, tn=_div(d, (1024, 512)), tk=2 * dc,
                        out_dtypes=[F32, BF16], name="l0_out_proj", extras=[("tile", xs0), ("vec", mods0)],
                        epilogue=lambda acc, rows, res, mv: (res + _seg_pick(mv, 2, rows, n_x) * acc, acc))
    xs2, mlp0, ride1, ride2 = _mlp_fwd(
        xs1, mods0, wfull['l0_mlp_w1'], wfull['l0_mlp_w2'], n_x=n_x, tm=tm_t, tag="l0",
        rider1=gather_of('l1_qkv_w', 'l1_out_w'), rider2=gather_of('l1_mlp_w1'))
    wfull['l1_qkv_w'], wfull['l1_out_w'] = ride1
    wfull['l1_mlp_w1'], = ride2

    h2 = _rms_mod_fwd(xs2, mods1, k_shift=0, n_x=n_x, name="l1_norm1")
    qkv, = _matmul(h2, wfull['l1_qkv_w'], mode="nn", tm=tm_t, tn=_div(l1_qkv_w.shape[1] * N_DEV, (1024, 768, 512)),
                   tk=d, out_dtypes=[F32], name="l1_qkv_proj")
    cos, sin = _rope_tables(n_x, n_ctx)
    gq, gk = _row8(l1_q_norm_g), _row8(l1_k_norm_g)
    q, k, v = _qk_prep_fwd(qkv, gq, gk, cos, sin, d_q=d_q, name="l1_qk_prep")
    o, lse = _attention_fwd(q, k, v, n_x=n_x, name="l1_attention")
    x2 = xs2[:n_x]
    x3, mix1 = _matmul(o, wfull['l1_out_w'], mode="nn", tm=tm_x, tn=_div(d, (1024, 512)), tk=d_q,
                       out_dtypes=[F32, BF16], name="l1_out_proj", extras=[("tile", x2), ("vec", mods1)],
                       epilogue=lambda acc, rows, res, mv: (res + mv[4:5] * acc, acc))
    x4, mlp1, ride1, _ = _mlp_fwd(x3, mods1, wfull['l1_mlp_w1'], None, n_x=n_x, tm=tm_x, tag="l1",
                                  rider1=gather_of('l1_mlp_w2'))
    wfull['l1_mlp_w2'], = ride1

    dx4, loss_part, dfinal_g = _final_loss(x4, _row8(final_g), loss_target[0], name="loss_head")
    loss = lax.psum(loss_part[0, 0], MESH_AXES)

    recv = {}
    dx3, dw, dmods1, _, (recv['l1_mlp_w2'],) = _mlp_bwd(
        dx4, mlp1, mods1, wfull['l1_mlp_w1'], wfull['l1_mlp_w2'], n_x=n_x, tm=tm_x, tag="l1")
    pending = [(dw, SHARDED['l1_mlp_w1'])]
    dmix1, dm = _gate_bwd(dx3, mix1, mods1, k_gate=2, n_x=n_x, name="l1_gate1_bwd")
    dmods1 = dmods1 + dm
    do, = _matmul(dmix1, wfull['l1_out_w'], mode="nt", tm=tm_x, tn=_div(d_q, (1024, 512)), tk=d,
                  out_dtypes=[BF16], name="l1_out_dx")
    dw, = _matmul(o, dmix1, mode="tn", tm=_div(d_q, (1024, 512)), tn=_div(d, (1024, 512)),
                  tk=_div(n_x, (1024, 512)), out_dtypes=[BF16], name="l1_out_dw")
    pending.append((dw, SHARDED['l1_out_w']))
    dq, dk, dv = _attention_bwd(q, k, v, o, lse, do, n_x=n_x, name="l1_attention_bwd")
    dqkv, dgq, dgk = _qk_prep_bwd(qkv, dq, dk, dv, gq, gk, cos, sin, n_x=n_x, name="l1_qk_prep_bwd")
    tkt = _div(t, (768, 640, 512, 128))
    dh2, = _matmul(dqkv, wfull['l1_qkv_w'], mode="nt", tm=tm_t, tn=_div(d, (1024, 512)), tk=dqkv.shape[1],
                   out_dtypes=[F32], name="l1_qkv_dx")
    dw_qkv, = _matmul(h2, dqkv, mode="tn", tm=_div(d, (1024, 512)), tn=_div(dqkv.shape[1], (1024, 768, 512)),
                      tk=tkt, out_dtypes=[BF16], name="l1_qkv_dw")
    dres2 = jnp.concatenate([dx3, jnp.zeros((n_ctx, d), F32)], axis=0)
    dxs2, dm = _rms_mod_bwd(xs2, mods1, dh2, dres2, k_shift=0, n_x=n_x, name="l1_norm1_bwd")
    dmods1 = dmods1 + dm

    dxs1, dw1_0, dmods0, (recv['l1_mlp_w1'], recv['l1_out_w']), (recv['l1_qkv_w'], recv['l0_mlp_w2']) = _mlp_bwd(
        dxs2, mlp0, mods0, wfull['l0_mlp_w1'], wfull['l0_mlp_w2'], n_x=n_x, tm=tm_t, tag="l0",
        ride_dx2=pending, ride_dx1=[(dw_qkv, SHARDED['l1_qkv_w'])])
    dmix0, dm = _gate_bwd(dxs1, mix0, mods0, k_gate=2, n_x=n_x, name="l0_gate1_bwd")
    dmods0 = dmods0 + dm
    dy0, = _matmul(dmix0, wfull['l0_out_w'], mode="nt", tm=tm_t, tn=_div(2 * dc, (1024, 512)), tk=d,
                   out_dtypes=[F32], name="l0_out_dx")
    dw_out0, = _matmul(y0, dmix0, mode="tn", tm=_div(2 * dc, (1024, 512)), tn=_div(d, (1024, 512)),
                       tk=tkt, out_dtypes=[BF16], name="l0_out_dw")
    dz, dconv_w, dconv_b, dln_g, dln_b, dpool_w, dpool_scale, recv['l0_mlp_w1'] = _mixer0_bwd(
        dy0, z, cv, dsave, conv_w_full, _row8(l0_conv_ln_g), _row8(l0_conv_ln_b), pool_w_full,
        _row8(l0_pool_scale), n_tap=n_tap, n_x=n_x, name="l0_mixer_bwd",
        rider=Rider("exchange", [dw1_0], [SHARDED['l0_mlp_w1']]))
    dw_in0, = _matmul(h0, dz, mode="tn", tm=_div(d, (1024, 512)), tn=_div(3 * dc, (1024, 768, 512, 384)),
                      tk=tkt, out_dtypes=[BF16], name="l0_in_dw")
    dh0, recv['l0_out_w'] = _matmul(dz, wfull['l0_in_w'], mode="nt", tm=tm_t, tn=_div(d, (1024, 512)), tk=3 * dc,
                                    out_dtypes=[F32], name="l0_in_dx",
                                    rider=Rider("exchange", [dw_out0], [SHARDED['l0_out_w']]))
    dxs0, dm, recv['l0_in_w'] = _rms_mod_bwd(xs0, mods0, dh0, dxs1, k_shift=0, n_x=n_x, name="l0_norm1_bwd",
                                             rider=Rider("exchange", [dw_in0], [SHARDED['l0_in_w']]))
    dmods0 = dmods0 + dm
    grad_x = dxs0[:n_x][None]

    def dmod_rows(dm16):
        rows = dm16[:2 * n_mod].reshape(n_mod, 2, d).transpose(1, 0, 2).reshape(2, n_mod * d)
        return jnp.pad(rows, ((0, 6), (0, 0)))
    dm_gathered = _all_gather([jnp.concatenate([dmod_rows(dmods0), dmod_rows(dmods1)], axis=0)], [0],
                              name="gather_dmods")[0]
    dm0, dm1, db0, db1 = _assemble_dmods(dm_gathered, name="assemble_dmods")
    out_g = {'l0_ada_b': db0[0], 'l1_ada_b': db1[0]}
    ds_part = jnp.zeros((16, d), F32)
    for nm, lw, dmf in (('l0_ada_w', l0_ada_w, dm0), ('l1_ada_w', l1_ada_w, dm1)):
        dm_cols = lax.dynamic_slice_in_dim(dmf, me * ada_cols, ada_cols, axis=1)
        out_g[nm], = _matmul(s16, dm_cols, mode="tn", tm=_div(d, (1024, 512)),
                             tn=_div(ada_cols, (512, 384, 256, 128)), tk=16, out_dtypes=[F32], name=f"{nm}_dw")
        ds_part = ds_part + _matmul(dm_cols, lw, mode="nt", tm=16, tn=_div(d, (1024, 512)),
                                    tk=_div(ada_cols, (512, 384, 256, 128)), out_dtypes=[F32], name=f"{nm}_dx")[0]

    small = {'l0_conv_b': dconv_b[0], 'l0_conv_ln_g': dln_g[0], 'l0_conv_ln_b': dln_b[0],
             'l0_pool_scale': dpool_scale[0], 'l1_q_norm_g': dgq[0], 'l1_k_norm_g': dgk[0],
             'final_g': dfinal_g[0], 'dsilu_ctx': ds_part[N_DEV], 'l0_conv_w': dconv_w[:-1].reshape(-1),
             'l0_pool_w': dpool_w.reshape(-1)}
    flat = jnp.concatenate([small[nm] for nm in small])
    rows = -(-flat.shape[0] // 1024) * 8
    packed = jnp.pad(flat, (0, rows * 128 - flat.shape[0])).reshape(rows, 128)
    summed = _sum_slots(_all_gather([packed], [0], name="gather_small_grads")[0], rows,
                        name="sum_small_grads").reshape(-1)
    off = 0
    for nm in small:
        size = small[nm].shape[0]
        small[nm] = summed[off:off + size]
        off += size
    out_g['c_ctx'] = _silu_grad(_row8(c_ctx), _row8(small['dsilu_ctx']), name="c_ctx_grad")[0]
    for nm in REPLICATED_SMALL:
        out_g[nm] = small[nm]
    conv_cols = l0_conv_w.shape[1]
    out_g['l0_conv_w'] = lax.dynamic_slice_in_dim(small['l0_conv_w'].reshape(n_tap, dc), me * conv_cols, conv_cols,
                                                  axis=1)
    out_g['l0_pool_w'] = lax.dynamic_slice_in_dim(small['l0_pool_w'].reshape(n_grp, pg, pg), me * (pg // N_DEV),
                                                  pg // N_DEV, axis=1)

    delta, new_m, new_v = {}, {}, {}
    for nm in names:
        out_g[nm], delta[nm], new_m[nm], new_v[nm] = _adamw(p[nm], p['m_' + nm], p['v_' + nm], recv=recv[nm],
                                                            name=f"adamw_{nm}")
    for nm in ('l0_ada_w', 'l1_ada_w'):
        out_g[nm], delta[nm], new_m[nm], new_v[nm] = _adamw(p[nm], p['m_' + nm], p['v_' + nm], grad=out_g[nm],
                                                            name=f"adamw_{nm}")
    for nm in WEIGHTS:
        if nm in delta:
            continue
        shape = p[nm].shape
        as2d = lambda a: a.reshape(1, -1) if a.ndim == 1 else a.reshape(-1, a.shape[-1])
        res = _adamw(as2d(p[nm]), as2d(p['m_' + nm]), as2d(p['v_' + nm]), grad=as2d(out_g[nm]), name=f"adamw_{nm}")
        out_g[nm], delta[nm], new_m[nm], new_v[nm] = [r.reshape(shape) for r in res]

    return (loss, grad_x, *[out_g[nm] for nm in WEIGHTS], *[delta[nm] for nm in WEIGHTS],
            *[new_m[nm] for nm in WEIGHTS], *[new_v[nm] for nm in WEIGHTS])
```

```python
import functools

import jax
import jax.numpy as jnp
from jax import lax
from jax.experimental import pallas as pl
from jax.experimental.pallas import tpu as pltpu

F32 = jnp.float32
BF16 = jnp.bfloat16
N_DEV = 8
MESH_AXES = ("x", "y", "c")
EPS = 1e-6
HEAD_DIM = 128
POOL_WINDOWS = (2, 4, 8, 16)
GRID_W = 64
ROPE_THETA = 10000.0
ATTN_SCALE = HEAD_DIM ** -0.5
LOG2_E = 1.4426950408889634
Q_SCALE_LOG2 = ATTN_SCALE * LOG2_E
HALO = 16
ADAM_LR, ADAM_B1, ADAM_B2, ADAM_EPS, ADAM_WD, ADAM_STEP = 0.001, 0.9, 0.999, 1e-08, 0.01, 10
VMEM_CAP_MB = 60
ADAMW_BLOCK_ELEMS = 1 << 18
MESH_ID = pl.DeviceIdType.MESH


def _div(n, prefs):
    for p in prefs:
        if n % p == 0:
            return p
    raise ValueError(f"no tile for {n} in {prefs}")


def _params(sem, vmem_mb):
    return pltpu.CompilerParams(dimension_semantics=sem, vmem_limit_bytes=min(vmem_mb, VMEM_CAP_MB) << 20)


def _sigmoid(x):
    return 1.0 / (1.0 + jnp.exp(-x))


def _silu(x):
    return x * _sigmoid(x)


def _rms(x):
    return x * lax.rsqrt(jnp.mean(x * x, axis=-1, keepdims=True) + EPS)


def _rms_mod(x, shift, scale):
    return _rms(x) * (1.0 + scale) + shift


def _layernorm(x, g, b):
    mu = jnp.mean(x, axis=-1, keepdims=True)
    var = jnp.mean(jnp.square(x - mu), axis=-1, keepdims=True)
    return (x - mu) * lax.rsqrt(var + EPS) * g + b


def _ln_silu(x, g, b):
    return _silu(_layernorm(x, g, b))


def _matmul(a, b, *, mode, tm, tn, tk, out_dtypes, name, extras=(), epilogue=None, rider=None):
    if mode == "tn":
        kdim, m = a.shape
        n = b.shape[1]
    else:
        m, kdim = a.shape
        n = b.shape[0] if mode == "nt" else b.shape[1]
    assert m % tm == 0 and n % tn == 0 and kdim % tk == 0, (name, m, n, kdim, tm, tn, tk)
    nk = kdim // tk
    n_ex = len(extras)
    n_out = len(out_dtypes)

    def body(a_ref, b_ref, *rest):
        ex = rest[:n_ex]
        outs = rest[n_ex:n_ex + n_out]
        acc_ref = rest[n_ex + n_out] if nk > 1 else None
        k = pl.program_id(2)
        av = a_ref[...].astype(BF16)
        bv = b_ref[...].astype(BF16)
        dims = {"nn": ((1,), (0,)), "nt": ((1,), (1,)), "tn": ((0,), (0,))}[mode]
        part = lax.dot_general(av, bv, (dims, ((), ())), preferred_element_type=F32)

        rows = pl.program_id(0) * tm + lax.broadcasted_iota(jnp.int32, (tm, 1), 0)

        def finish(acc):
            res = (acc,) if epilogue is None else epilogue(acc, rows, *[e[...] for e in ex])
            for o, r in zip(outs, res):
                o[...] = r.astype(o.dtype)

        if nk == 1:
            finish(part)
        else:
            @pl.when(k == 0)
            def _():
                acc_ref[...] = part

            @pl.when(k > 0)
            def _():
                acc_ref[...] += part

            @pl.when(k == nk - 1)
            def _():
                finish(acc_ref[...])

    if mode == "tn":
        a_spec = pl.BlockSpec((tk, tm), lambda i, j, k: (k, i))
        b_spec = pl.BlockSpec((tk, tn), lambda i, j, k: (k, j))
    else:
        a_spec = pl.BlockSpec((tm, tk), lambda i, j, k: (i, k))
        b_spec = (pl.BlockSpec((tn, tk), lambda i, j, k: (j, k)) if mode == "nt"
                  else pl.BlockSpec((tk, tn), lambda i, j, k: (k, j)))
    ex_specs, ex_arrays, ex_bytes = [], [], 0
    for kind, arr in extras:
        ex_arrays.append(arr)
        if kind == "tile":
            ex_specs.append(pl.BlockSpec((tm, tn), lambda i, j, k: (i, j)))
            ex_bytes += tm * tn * arr.dtype.itemsize
        else:
            ex_specs.append(pl.BlockSpec((arr.shape[0], tn), lambda i, j, k: (0, j)))
            ex_bytes += arr.shape[0] * tn * 4
    blocks = tm * tk * a.dtype.itemsize + tk * tn * b.dtype.itemsize + ex_bytes
    blocks += sum(tm * tn * jnp.dtype(d).itemsize for d in out_dtypes)
    vmem = (2 * blocks + 4 * tm * tn * 4 + (tm * tk + tk * tn) * 2) // (1 << 20) + 8
    return _pcall(
        body,
        grid=(m // tm, n // tn, nk),
        in_specs=[a_spec, b_spec] + ex_specs,
        out_specs=[pl.BlockSpec((tm, tn), lambda i, j, k: (i, j)) for _ in out_dtypes],
        out_shape=[jax.ShapeDtypeStruct((m, n), d) for d in out_dtypes],
        scratch_shapes=[pltpu.VMEM((tm, tn), F32)] if nk > 1 else [],
        semantics=("parallel", "parallel", "arbitrary"), vmem_mb=vmem, name=name,
        operands=[a, b, *ex_arrays], rider=rider)


def _seg_pick(vec, k, rows, n_x):
    return jnp.where(rows < n_x, vec[2 * k:2 * k + 1], vec[2 * k + 1:2 * k + 2])


def _zero_accs(i, accs):
    @pl.when(i == 0)
    def _():
        for a in accs:
            a[...] = jnp.zeros_like(a)


def _rms_mod_fwd(xs, mods, *, k_shift, n_x, name):
    t, d = xs.shape
    tm = _div(t, (256, 128))

    def body(x_ref, mods_ref, h_ref):
        seg = (pl.program_id(0) * tm >= n_x).astype(jnp.int32)
        shift = mods_ref[pl.ds(2 * k_shift + seg, 1), :]
        scale = mods_ref[pl.ds(2 * k_shift + 2 + seg, 1), :]
        h_ref[...] = _rms_mod(x_ref[...], shift, scale).astype(BF16)

    return pl.pallas_call(
        body, grid=(t // tm,),
        in_specs=[pl.BlockSpec((tm, d), lambda i: (i, 0)), pl.BlockSpec((16, d), lambda i: (0, 0))],
        out_specs=pl.BlockSpec((tm, d), lambda i: (i, 0)),
        out_shape=jax.ShapeDtypeStruct((t, d), BF16),
        compiler_params=_params(("parallel",), 32), name=name,
    )(xs, mods)


def _gate_part(dxv, seg, k_gate, br_ref, gmods_ref, db_ref, dgm_ref):
    r_gate = 2 * k_gate + seg
    db_ref[...] = (dxv * gmods_ref[pl.ds(r_gate, 1), :]).astype(BF16)
    dgm_ref[pl.ds(r_gate, 1), :] += jnp.sum(dxv * br_ref[...].astype(F32), axis=0, keepdims=True)


def _rms_mod_bwd(xs, mods, dh, dres, *, k_shift, n_x, name, rider=None, gate=None, dres_rows=None, out_rows=None):
    t, d = xs.shape
    tm = _div(t, (256, 128))
    n_gate = 2 if gate is not None else 0

    def body(x_ref, mods_ref, dh_ref, dres_ref, *rest):
        gate_in, (dx_ref, dmods_ref), gate_out = rest[:n_gate], rest[n_gate:n_gate + 2], rest[n_gate + 2:]
        i = pl.program_id(0)
        _zero_accs(i, [dmods_ref, *gate_out[1:]])
        seg = (i * tm >= n_x).astype(jnp.int32)
        r_shift = 2 * k_shift + seg
        r_scale = 2 * k_shift + 2 + seg
        shift = mods_ref[pl.ds(r_shift, 1), :]
        scale = mods_ref[pl.ds(r_scale, 1), :]
        _, vjp = jax.vjp(_rms_mod, x_ref[...], shift, scale)
        dx, dshift, dscale = vjp(dh_ref[...].astype(F32))
        dres_v = dres_ref[...]
        if dres_rows is not None:
            dres_v = dres_v * jnp.where(i * tm < dres_rows, 1.0, 0.0)
        dx = dres_v + dx
        if out_rows is None:
            dx_ref[...] = dx
        else:
            @pl.when(i * tm < out_rows)
            def _():
                dx_ref[...] = dx
        dmods_ref[pl.ds(r_shift, 1), :] += dshift
        dmods_ref[pl.ds(r_scale, 1), :] += dscale
        if gate is not None:
            _gate_part(dx, seg, gate[2], *gate_in, *gate_out)

    def clamped(rows):
        return pl.BlockSpec((tm, d), lambda i: (jnp.minimum(i, rows // tm - 1), 0))

    row = pl.BlockSpec((tm, d), lambda i: (i, 0))
    vec = pl.BlockSpec((16, d), lambda i: (0, 0))
    in_specs = [row, vec, row, row if dres_rows is None else clamped(dres_rows)]
    out_specs = [row if out_rows is None else clamped(out_rows), vec]
    out_shape = [jax.ShapeDtypeStruct((t if out_rows is None else out_rows, d), F32),
                 jax.ShapeDtypeStruct((16, d), F32)]
    operands = [xs, mods, dh, dres]
    if gate is not None:
        in_specs += [row, vec]
        out_specs += [row, vec]
        out_shape += [jax.ShapeDtypeStruct((t, d), BF16), jax.ShapeDtypeStruct((16, d), F32)]
        operands += [gate[0], gate[1]]
    return _pcall(body, grid=(t // tm,), in_specs=in_specs, out_specs=out_specs, out_shape=out_shape,
                  scratch_shapes=[], semantics=("arbitrary",), vmem_mb=48, name=name, operands=operands, rider=rider)


def _final_loss(xs, g, target, branch, gmods, *, k_gate, name):
    t, d = xs.shape
    tm = _div(t, (256, 128))

    def loss_fn(x, gv, tgt):
        err = _rms(x) * gv - tgt
        return 0.5 * jnp.sum(jnp.mean(jnp.square(err), axis=-1))

    def body(x_ref, g_ref, t_ref, br_ref, gmods_ref, dx_ref, loss_ref, dg_ref, db_ref, dgm_ref):
        i = pl.program_id(0)
        _zero_accs(i, [loss_ref, dg_ref, dgm_ref])
        val, vjp = jax.vjp(loss_fn, x_ref[...], g_ref[0:1, :], t_ref[...])
        dx, dg, _ = vjp(jnp.ones((), F32))
        dx_ref[...] = dx
        loss_ref[...] += val
        dg_ref[0:1, :] += dg
        _gate_part(dx, 0, k_gate, br_ref, gmods_ref, db_ref, dgm_ref)

    row = pl.BlockSpec((tm, d), lambda i: (i, 0))
    vec16 = pl.BlockSpec((16, d), lambda i: (0, 0))
    return pl.pallas_call(
        body, grid=(t // tm,),
        in_specs=[row, pl.BlockSpec((8, d), lambda i: (0, 0)), row, row, vec16],
        out_specs=[row, pl.BlockSpec((8, 128), lambda i: (0, 0)), pl.BlockSpec((8, d), lambda i: (0, 0)), row, vec16],
        out_shape=[jax.ShapeDtypeStruct((t, d), F32), jax.ShapeDtypeStruct((8, 128), F32),
                   jax.ShapeDtypeStruct((8, d), F32), jax.ShapeDtypeStruct((t, d), BF16),
                   jax.ShapeDtypeStruct((16, d), F32)],
        compiler_params=_params(("arbitrary",), 48), name=name,
    )(xs, g, target, branch, gmods)


def _halo_specs(r, width, col, t):
    h_per = r // HALO

    def prev(i):
        return (jnp.maximum(i * h_per - 1, 0), col)

    def nxt(i):
        return (jnp.minimum((i + 1) * h_per, t // HALO - 1), col)

    return (pl.BlockSpec((HALO, width), prev), pl.BlockSpec((r, width), lambda i: (i, col)),
            pl.BlockSpec((HALO, width), nxt))


def _seg_geometry(i, r, n_x, t):
    row0 = i * r
    in_ctx = row0 >= n_x
    first = jnp.logical_or(row0 == 0, row0 == n_x)
    last = jnp.logical_or(row0 + r == n_x, row0 + r == t)
    seg_start = jnp.where(in_ctx, n_x, 0)
    seg_len = jnp.where(in_ctx, t - n_x, n_x)
    return row0, first, last, seg_start, seg_len


ROW_CHUNK = 64
LANE_CHUNK = 128


def _chunks(width, rows, col0=0):
    lanes = min(LANE_CHUNK, width)
    return [(slice(col0 + c, col0 + c + lanes), r0) for c in range(0, width, lanes) for r0 in range(0, rows, ROW_CHUNK)]


def _pool_count(tpos, w, seg_len):
    return (jnp.minimum(tpos + w // 2, seg_len) - jnp.maximum(tpos - w // 2, 0)).astype(F32)


def _mixer0_fwd(z, conv_w, conv_b, ln_g, ln_b, pool_w, pool_scale, *, n_tap, n_x, name, rider=None):
    t, dc = z.shape[0], z.shape[1] // 3
    n_grp, pg = pool_w.shape[0], pool_w.shape[1]
    r = _div(t - n_x, (256, 128))
    assert n_x % r == 0 and pg * n_grp == dc
    half = n_tap // 2
    assert half < HALO and max(POOL_WINDOWS) // 2 <= HALO

    def body(ap, ac, an, gp, gc, gn, pp, pc, pn, w_ref, cb_ref, lg_ref, lb_ref, pw_ref, ps_ref,
             y_ref, cv_ref, d_ref, uwin, pwin):
        i = pl.program_id(0)
        row0, first, last, seg_start, seg_len = _seg_geometry(i, r, n_x, t)
        keep_prev = jnp.where(first, 0.0, 1.0)
        keep_next = jnp.where(last, 0.0, 1.0)
        uwin[0:HALO, :] = ap[...] * _sigmoid(gp[...]) * keep_prev
        uwin[HALO:HALO + r, :] = ac[...] * _sigmoid(gc[...])
        uwin[HALO + r:, :] = an[...] * _sigmoid(gn[...]) * keep_next
        pwin[0:HALO, :] = pp[...] * keep_prev
        pwin[HALO:HALO + r, :] = pc[...]
        pwin[HALO + r:, :] = pn[...] * keep_next
        for cols, r0 in _chunks(dc, r):
            acc = jnp.zeros((ROW_CHUNK, cols.stop - cols.start), F32) + cb_ref[0:1, cols]
            for k in range(n_tap):
                off = HALO - half + k + r0
                acc = acc + w_ref[k:k + 1, cols] * uwin[off:off + ROW_CHUNK, cols]
            cv_ref[r0:r0 + ROW_CHUNK, cols] = acc
        y_ref[:, 0:dc] = _ln_silu(cv_ref[...], lg_ref[0:1, :], lb_ref[0:1, :]).astype(BF16)
        tpos = row0 - seg_start + lax.broadcasted_iota(jnp.int32, (r, 1), 0)
        for g, w in enumerate(POOL_WINDOWS):
            cnt = _pool_count(tpos, w, seg_len)
            for cols, r0 in _chunks(pg, r, g * pg):
                s = jnp.zeros((ROW_CHUNK, cols.stop - cols.start), F32)
                for j in range(-(w // 2), w // 2):
                    s = s + pwin[HALO + j + r0:HALO + j + r0 + ROW_CHUNK, cols]
                diff = s / cnt[r0:r0 + ROW_CHUNK] - pwin[HALO + r0:HALO + r0 + ROW_CHUNK, cols]
                d_ref[r0:r0 + ROW_CHUNK, cols] = diff.astype(BF16)
            cols = slice(g * pg, (g + 1) * pg)
            pm = jnp.dot(d_ref[:, cols], pw_ref[g].astype(BF16), preferred_element_type=F32)
            y_ref[:, dc + g * pg:dc + (g + 1) * pg] = (pm * ps_ref[0:1, cols]).astype(BF16)

    vec = lambda rows, width: pl.BlockSpec((rows, width), lambda i: (0, 0))
    in_specs = [*_halo_specs(r, dc, 0, t), *_halo_specs(r, dc, 1, t), *_halo_specs(r, dc, 2, t),
                vec(conv_w.shape[0], dc), vec(8, dc), vec(8, dc), vec(8, dc),
                pl.BlockSpec((n_grp, pg, pg), lambda i: (0, 0, 0)), vec(8, dc)]
    return _pcall(
        body, grid=(t // r,), in_specs=in_specs,
        out_specs=[pl.BlockSpec((r, 2 * dc), lambda i: (i, 0)), pl.BlockSpec((r, dc), lambda i: (i, 0)),
                   pl.BlockSpec((r, dc), lambda i: (i, 0))],
        out_shape=[jax.ShapeDtypeStruct((t, 2 * dc), BF16), jax.ShapeDtypeStruct((t, dc), F32),
                   jax.ShapeDtypeStruct((t, dc), BF16)],
        scratch_shapes=[pltpu.VMEM((r + 2 * HALO, dc), F32), pltpu.VMEM((r + 2 * HALO, dc), F32)],
        semantics=("parallel",), vmem_mb=40, name=name,
        operands=[*([z] * 9), conv_w, conv_b, ln_g, ln_b, pool_w, pool_scale], rider=rider)


def _mixer0_bwd(dy, z, cv, dsave, conv_w, ln_g, ln_b, pool_w, pool_scale, *, n_tap, n_x, name, rider=None):
    t, dc = cv.shape
    n_grp, pg = pool_w.shape[0], pool_w.shape[1]
    r = _div(t - n_x, (256, 128))
    half = n_tap // 2
    dyp_, zp_, cvp_ = dy, z, cv
    rw = r + 2 * HALO

    def body(dcp, dcc, dcn, dpp, dpc, dpn, cvp, cvc, cvn, ap, ac, an, gp, gc, gn, d_ref,
             w_ref, lg_ref, lb_ref, pw_ref, ps_ref,
             dz_ref, dw_ref, dcb_ref, dlg_ref, dlb_ref, dpw_ref, dps_ref, uwin, dcvwin, ewin, ddwin):
        i = pl.program_id(0)
        _zero_accs(i, [dw_ref, dcb_ref, dlg_ref, dlb_ref, dpw_ref, dps_ref])
        row0, first, last, seg_start, seg_len = _seg_geometry(i, r, n_x, t)
        keep_prev = jnp.where(first, 0.0, 1.0)
        keep_next = jnp.where(last, 0.0, 1.0)
        lg, lb = lg_ref[0:1, :], lb_ref[0:1, :]
        _, vjp = jax.vjp(_ln_silu, cvc[...], lg, lb)
        dcv, dlg, dlb = vjp(dcc[...])
        dlg_ref[0:1, :] += dlg
        dlb_ref[0:1, :] += dlb
        dcb_ref[0:1, :] += jnp.sum(dcv, axis=0, keepdims=True)
        dcvwin[HALO:HALO + r, :] = dcv
        for halo_cv, halo_dy, keep, lo in ((cvp, dcp, keep_prev, 0), (cvn, dcn, keep_next, HALO + r)):
            _, vjp_h = jax.vjp(lambda v: _ln_silu(v, lg, lb), halo_cv[...])
            dcvwin[lo:lo + HALO, :] = vjp_h(halo_dy[...])[0] * keep
        uwin[0:HALO, :] = ap[...] * _sigmoid(gp[...]) * keep_prev
        uwin[HALO:HALO + r, :] = ac[...] * _sigmoid(gc[...])
        uwin[HALO + r:, :] = an[...] * _sigmoid(gn[...]) * keep_next
        for cols, r0 in _chunks(dc, r):
            du = jnp.zeros((ROW_CHUNK, cols.stop - cols.start), F32)
            for k in range(n_tap):
                off = HALO + half - k + r0
                du = du + w_ref[k:k + 1, cols] * dcvwin[off:off + ROW_CHUNK, cols]
            rows = slice(r0, r0 + ROW_CHUNK)
            sig = _sigmoid(gc[rows, cols])
            dz_ref[rows, cols] = (du * sig).astype(BF16)
            dz_ref[rows, dc + cols.start:dc + cols.stop] = (du * ac[rows, cols] * sig * (1.0 - sig)).astype(BF16)
        for c0 in range(0, dc, LANE_CHUNK):
            cols = slice(c0, c0 + LANE_CHUNK)
            taps = [jnp.zeros((8, LANE_CHUNK), F32) for _ in range(n_tap)]
            for r0 in range(0, r, ROW_CHUNK):
                dcv_c = dcvwin[HALO + r0:HALO + r0 + ROW_CHUNK, cols]
                for k in range(n_tap):
                    off = HALO - half + k + r0
                    prod = dcv_c * uwin[off:off + ROW_CHUNK, cols]
                    taps[k] = taps[k] + functools.reduce(
                        jnp.add, [prod[8 * s:8 * s + 8] for s in range(ROW_CHUNK // 8)])
            for k in range(n_tap):
                dw_ref[k:k + 1, cols] += jnp.sum(taps[k], axis=0, keepdims=True)
        twin = row0 - seg_start - HALO + lax.broadcasted_iota(jnp.int32, (rw, 1), 0)
        for g, w in enumerate(POOL_WINDOWS):
            cols = slice(g * pg, (g + 1) * pg)
            wg = pw_ref[g].astype(BF16)
            scale = ps_ref[0:1, cols]
            dyp_c = dpc[:, cols]
            dpm_win = jnp.concatenate([dpp[:, cols] * keep_prev, dyp_c, dpn[:, cols] * keep_next], axis=0) * scale
            dd_win = lax.dot_general(dpm_win.astype(BF16), wg, (((1,), (1,)), ((), ())), preferred_element_type=F32)
            cnt = jnp.maximum(_pool_count(twin, w, seg_len), 1.0)
            ddwin[:, cols] = dd_win
            ewin[:, cols] = dd_win / cnt
            for ccols, r0 in _chunks(pg, r, g * pg):
                dup = -ddwin[HALO + r0:HALO + r0 + ROW_CHUNK, ccols]
                for j in range(-(w // 2) + 1, w // 2 + 1):
                    dup = dup + ewin[HALO + j + r0:HALO + j + r0 + ROW_CHUNK, ccols]
                dz_ref[r0:r0 + ROW_CHUNK, 2 * dc + ccols.start:2 * dc + ccols.stop] = dup.astype(BF16)
            dsv = d_ref[:, cols]
            pm = jnp.dot(dsv, wg, preferred_element_type=F32)
            dps_ref[0:1, cols] += jnp.sum(dyp_c * pm, axis=0, keepdims=True)
            dpw_ref[g] += lax.dot_general(dsv, (dyp_c * scale).astype(BF16), (((0,), (0,)), ((), ())),
                                          preferred_element_type=F32)

    vec = lambda rows, width: pl.BlockSpec((rows, width), lambda i: (0, 0))
    grp = pl.BlockSpec((n_grp, pg, pg), lambda i: (0, 0, 0))
    in_specs = [*_halo_specs(r, dc, 0, t), *_halo_specs(r, dc, 1, t), *_halo_specs(r, dc, 0, t),
                *_halo_specs(r, dc, 0, t), *_halo_specs(r, dc, 1, t), pl.BlockSpec((r, dc), lambda i: (i, 0)),
                vec(conv_w.shape[0], dc), vec(8, dc), vec(8, dc), grp, vec(8, dc)]
    return _pcall(
        body, grid=(t // r,), in_specs=in_specs,
        out_specs=[pl.BlockSpec((r, 3 * dc), lambda i: (i, 0)), vec(conv_w.shape[0], dc), vec(8, dc), vec(8, dc),
                   vec(8, dc), grp, vec(8, dc)],
        out_shape=[jax.ShapeDtypeStruct((t, 3 * dc), BF16), jax.ShapeDtypeStruct(conv_w.shape, F32),
                   jax.ShapeDtypeStruct((8, dc), F32), jax.ShapeDtypeStruct((8, dc), F32),
                   jax.ShapeDtypeStruct((8, dc), F32), jax.ShapeDtypeStruct(pool_w.shape, F32),
                   jax.ShapeDtypeStruct((8, dc), F32)],
        scratch_shapes=[pltpu.VMEM((rw, dc), F32)] * 4,
        semantics=("arbitrary",), vmem_mb=VMEM_CAP_MB, name=name,
        operands=[dyp_, dyp_, dyp_, dyp_, dyp_, dyp_, cvp_, cvp_, cvp_, zp_, zp_, zp_, zp_, zp_, zp_, dsave,
                  conv_w, ln_g, ln_b, pool_w, pool_scale], rider=rider)


def _swap_halves(x):
    lane = lax.broadcasted_iota(jnp.int32, x.shape, 1)
    quarter = HEAD_DIM // 4
    return jnp.where(lane % (2 * quarter) < quarter,
                     pltpu.roll(x, HEAD_DIM - quarter, 1), pltpu.roll(x, quarter, 1))


def _rope_tables(n_x, n_ctx):
    half = HEAD_DIM // 4
    freqs = ROPE_THETA ** (-jnp.arange(half, dtype=F32) / half)
    tok = jnp.arange(n_x)
    row = (tok // GRID_W).astype(F32)[:, None] * freqs[None, :]
    col = (tok % GRID_W).astype(F32)[:, None] * freqs[None, :]
    cos = jnp.concatenate([jnp.cos(row), jnp.cos(row), jnp.cos(col), jnp.cos(col)], axis=1)
    sin = jnp.concatenate([-jnp.sin(row), jnp.sin(row), -jnp.sin(col), jnp.sin(col)], axis=1)
    cos = jnp.concatenate([cos, jnp.ones((n_ctx, HEAD_DIM), F32)], axis=0)
    sin = jnp.concatenate([sin, jnp.zeros((n_ctx, HEAD_DIM), F32)], axis=0)
    return cos, sin


def _norm_g(x, g):
    return _rms(x) * g


def _qk_prep_fwd(qkv, gq, gk, cos, sin, *, d_q, name):
    t, width = qkv.shape
    d_kv = (width - d_q) // 2
    tm = _div(t, (256, 128))

    def body(qkv_ref, gq_ref, gk_ref, cos_ref, sin_ref, q_ref, k_ref, v_ref):
        cs, sn = cos_ref[...], sin_ref[...]
        for h in range((d_q + d_kv) // HEAD_DIM):
            g = gq_ref[0:1, :] if h * HEAD_DIM < d_q else gk_ref[0:1, :]
            xn = _norm_g(qkv_ref[:, h * HEAD_DIM:(h + 1) * HEAD_DIM], g)
            rot = xn * cs + _swap_halves(xn) * sn
            if h * HEAD_DIM < d_q:
                q_ref[:, h * HEAD_DIM:(h + 1) * HEAD_DIM] = (rot * Q_SCALE_LOG2).astype(BF16)
            else:
                k_ref[:, h * HEAD_DIM - d_q:(h + 1) * HEAD_DIM - d_q] = rot.astype(BF16)
        v_ref[...] = qkv_ref[:, d_q + d_kv:].astype(BF16)

    row = lambda w: pl.BlockSpec((tm, w), lambda i: (i, 0))
    vec = pl.BlockSpec((8, HEAD_DIM), lambda i: (0, 0))
    return pl.pallas_call(
        body, grid=(t // tm,),
        in_specs=[row(width), vec, vec, row(HEAD_DIM), row(HEAD_DIM)],
        out_specs=[row(d_q), row(d_kv), row(d_kv)],
        out_shape=[jax.ShapeDtypeStruct((t, d_q), BF16), jax.ShapeDtypeStruct((t, d_kv), BF16),
                   jax.ShapeDtypeStruct((t, d_kv), BF16)],
        compiler_params=_params(("parallel",), 32), name=name,
    )(qkv, gq, gk, cos, sin)


def _qk_prep_bwd(qkv, dq, dk, dv, gq, gk, cos, sin, *, n_x, name):
    t, width = qkv.shape
    d_q, d_kv = dq.shape[1], dk.shape[1]
    tm = _div(t, (256, 128))
    last_q = n_x // tm - 1

    def body(qkv_ref, dq_ref, dk_ref, dv_ref, gq_ref, gk_ref, cos_ref, sin_ref, out_ref, dgq_ref, dgk_ref):
        i = pl.program_id(0)
        _zero_accs(i, [dgq_ref, dgk_ref])
        is_x = jnp.where(i * tm < n_x, 1.0, 0.0)
        cs, sn = cos_ref[...], sin_ref[...]
        for h in range((d_q + d_kv) // HEAD_DIM):
            sl = slice(h * HEAD_DIM, (h + 1) * HEAD_DIM)
            if h * HEAD_DIM < d_q:
                g, dg_ref, dr = gq_ref[0:1, :], dgq_ref, dq_ref[:, sl] * is_x
            else:
                g, dg_ref = gk_ref[0:1, :], dgk_ref
                dr = dk_ref[:, h * HEAD_DIM - d_q:(h + 1) * HEAD_DIM - d_q]
            dxn = dr * cs + _swap_halves(dr * sn)
            _, vjp = jax.vjp(_norm_g, qkv_ref[:, sl], g)
            dx, dg = vjp(dxn)
            out_ref[:, sl] = dx.astype(BF16)
            dg_ref[0:1, :] += dg
        out_ref[:, d_q + d_kv:] = dv_ref[...].astype(BF16)

    row = lambda w: pl.BlockSpec((tm, w), lambda i: (i, 0))
    vec = pl.BlockSpec((8, HEAD_DIM), lambda i: (0, 0))
    return pl.pallas_call(
        body, grid=(t // tm,),
        in_specs=[row(width), pl.BlockSpec((tm, d_q), lambda i: (jnp.minimum(i, last_q), 0)), row(d_kv), row(d_kv),
                  vec, vec, row(HEAD_DIM), row(HEAD_DIM)],
        out_specs=[row(width), vec, vec],
        out_shape=[jax.ShapeDtypeStruct((t, width), BF16), jax.ShapeDtypeStruct((8, HEAD_DIM), F32),
                   jax.ShapeDtypeStruct((8, HEAD_DIM), F32)],
        compiler_params=_params(("arbitrary",), 40), name=name,
    )(qkv, dq, dk, dv, gq, gk, cos, sin)


ATTN_TQ = (256, 128)
ATTN_TK = (768, 640, 512, 384, 256, 128)


def _attention_fwd(q, k, v, *, n_x, name):
    t, d_kv = k.shape
    d_q = q.shape[1]
    kvh = d_kv // HEAD_DIM
    grp = d_q // d_kv
    tq = _div(n_x, ATTN_TQ)
    tk = _div(t, ATTN_TK)
    gw = grp * HEAD_DIM

    n_kv = t // tk
    n_pair = (n_kv - 1) // 2

    def body(q_ref, k_ref, v_ref, o_ref, lse_ref, m_ref, l_ref, acc_ref, s_even, s_odd):
        m_ref[...] = jnp.full_like(m_ref, -jnp.inf)
        l_ref[...] = jnp.zeros_like(l_ref)
        acc_ref[...] = jnp.zeros_like(acc_ref)

        def key_rows(j):
            return pl.ds(pl.multiple_of(j * tk, tk), tk)

        def scores(g, kc):
            return lax.dot_general(q_ref[:, g * HEAD_DIM:(g + 1) * HEAD_DIM], kc, (((1,), (1,)), ((), ())),
                                   preferred_element_type=F32)

        def chunk(j, s_cur, s_next):
            vc = v_ref[key_rows(j), :]
            kn = k_ref[key_rows(j + 1), :] if s_next is not None else None
            for g in range(grp):
                s = s_cur[g]
                m_old = m_ref[g]
                m_new = jnp.maximum(m_old, jnp.max(s, axis=-1, keepdims=True))
                alpha = jnp.exp2(m_old - m_new)
                p = jnp.exp2(s - m_new)
                if s_next is not None:
                    s_next[g] = scores(g, kn)
                l_ref[g] = alpha * l_ref[g] + jnp.sum(p, axis=-1, keepdims=True)
                acc_ref[g] = alpha * acc_ref[g] + jnp.dot(p.astype(BF16), vc, preferred_element_type=F32)
                m_ref[g] = m_new

        k0 = k_ref[key_rows(0), :]
        for g in range(grp):
            s_even[g] = scores(g, k0)

        def pair(i, carry):
            chunk(2 * i, s_even, s_odd)
            chunk(2 * i + 1, s_odd, s_even)
            return carry

        lax.fori_loop(0, n_pair, pair, 0)
        if n_kv - 2 * n_pair == 2:
            chunk(n_kv - 2, s_even, s_odd)
            chunk(n_kv - 1, s_odd, None)
        else:
            chunk(n_kv - 1, s_even, None)
        for g in range(grp):
            o_ref[:, g * HEAD_DIM:(g + 1) * HEAD_DIM] = (acc_ref[g] / l_ref[g]).astype(BF16)
            lse_ref[:, g:g + 1] = m_ref[g] + jnp.log(l_ref[g]) * LOG2_E

    return pl.pallas_call(
        body, grid=(kvh, n_x // tq),
        in_specs=[pl.BlockSpec((tq, gw), lambda h, i: (i, h)),
                  pl.BlockSpec((t, HEAD_DIM), lambda h, i: (0, h)),
                  pl.BlockSpec((t, HEAD_DIM), lambda h, i: (0, h))],
        out_specs=[pl.BlockSpec((tq, gw), lambda h, i: (i, h)),
                   pl.BlockSpec((None, tq, grp), lambda h, i: (h, i, 0))],
        out_shape=[jax.ShapeDtypeStruct((n_x, d_q), BF16), jax.ShapeDtypeStruct((kvh, n_x, grp), F32)],
        scratch_shapes=[pltpu.VMEM((grp, tq, 1), F32), pltpu.VMEM((grp, tq, 1), F32),
                        pltpu.VMEM((grp, tq, HEAD_DIM), F32),
                        pltpu.VMEM((grp, tq, tk), F32), pltpu.VMEM((grp, tq, tk), F32)],
        compiler_params=_params(("parallel", "arbitrary"), 48), name=name,
    )(q, k, v)


def _attention_bwd(q, k, v, o, lse, do, *, n_x, name):
    t, d_kv = k.shape
    d_q = q.shape[1]
    kvh = d_kv // HEAD_DIM
    grp = d_q // d_kv
    tq = _div(n_x, ATTN_TQ)
    tk = _div(t, ATTN_TK)
    gw = grp * HEAD_DIM
    n_q = n_x // tq

    def body(q_ref, k_ref, v_ref, o_ref, lse_ref, do_ref, dq_ref, dk_ref, dv_ref, dq_acc, lse_s, delta_s):
        i = pl.program_id(1)
        _zero_accs(i, [dk_ref, dv_ref])
        dq_acc[...] = jnp.zeros_like(dq_acc)
        for g in range(grp):
            sl = slice(g * HEAD_DIM, (g + 1) * HEAD_DIM)
            lse_s[g] = lse_ref[:, g:g + 1]
            delta_s[g] = jnp.sum(do_ref[:, sl].astype(F32) * o_ref[:, sl].astype(F32), axis=-1, keepdims=True)

        def step(j, carry):
            rows = pl.ds(pl.multiple_of(j * tk, tk), tk)
            kc, vc = k_ref[rows, :], v_ref[rows, :]
            dk_part = jnp.zeros((tk, HEAD_DIM), F32)
            dv_part = jnp.zeros((tk, HEAD_DIM), F32)
            for g in range(grp):
                sl = slice(g * HEAD_DIM, (g + 1) * HEAD_DIM)
                qg, dog = q_ref[:, sl], do_ref[:, sl]
                s = lax.dot_general(qg, kc, (((1,), (1,)), ((), ())), preferred_element_type=F32)
                p = jnp.exp2(s - lse_s[g])
                dp = lax.dot_general(dog, vc, (((1,), (1,)), ((), ())), preferred_element_type=F32)
                ds = (p * (dp - delta_s[g])).astype(BF16)
                dq_acc[g] += jnp.dot(ds, kc, preferred_element_type=F32)
                dv_part = dv_part + lax.dot_general(p.astype(BF16), dog, (((0,), (0,)), ((), ())),
                                                    preferred_element_type=F32)
                dk_part = dk_part + lax.dot_general(ds, qg, (((0,), (0,)), ((), ())), preferred_element_type=F32)
            dv_ref[rows, :] += dv_part
            dk_ref[rows, :] += dk_part
            return carry

        lax.fori_loop(0, t // tk, step, 0)
        for g in range(grp):
            dq_ref[:, g * HEAD_DIM:(g + 1) * HEAD_DIM] = dq_acc[g] * ATTN_SCALE

        @pl.when(i == n_q - 1)
        def _():
            dk_ref[...] = dk_ref[...] * (1.0 / LOG2_E)

    qspec = pl.BlockSpec((tq, gw), lambda h, i: (i, h))
    kspec = pl.BlockSpec((t, HEAD_DIM), lambda h, i: (0, h))
    return pl.pallas_call(
        body, grid=(kvh, n_q),
        in_specs=[qspec, kspec, kspec, qspec, pl.BlockSpec((None, tq, grp), lambda h, i: (h, i, 0)), qspec],
        out_specs=[qspec, kspec, kspec],
        out_shape=[jax.ShapeDtypeStruct((n_x, d_q), F32), jax.ShapeDtypeStruct((t, d_kv), F32),
                   jax.ShapeDtypeStruct((t, d_kv), F32)],
        scratch_shapes=[pltpu.VMEM((grp, tq, HEAD_DIM), F32), pltpu.VMEM((grp, tq, 1), F32),
                        pltpu.VMEM((grp, tq, 1), F32)],
        compiler_params=_params(("parallel", "arbitrary"), 56), name=name,
    )(q, k, v, o, lse, do)


def _whole(body, ins, out_shapes, name):
    return pl.pallas_call(
        body, out_shape=[jax.ShapeDtypeStruct(s, d) for s, d in out_shapes],
        compiler_params=pltpu.CompilerParams(vmem_limit_bytes=40 << 20), name=name)(*ins)


def _silu_rows(x, *, name):
    def body(x_ref, o_ref):
        o_ref[...] = _silu(x_ref[...])
    return _whole(body, [x], [(x.shape, F32)], name)[0]


def _assemble_dmods(gathered, *, name):
    width = gathered.shape[1]

    def body(g_ref, dm0, dm1, db0, db1):
        for l, (dm, db) in enumerate(((dm0, db0), (dm1, db1))):
            ctx = jnp.zeros((1, width), F32)
            tot = jnp.zeros((1, width), F32)
            for q in range(N_DEV):
                row = g_ref[16 * q + 8 * l:16 * q + 8 * l + 1, :]
                dm[q:q + 1, :] = row
                tot = tot + row
                ctx = ctx + g_ref[16 * q + 8 * l + 1:16 * q + 8 * l + 2, :]
            dm[N_DEV:N_DEV + 1, :] = ctx
            dm[N_DEV + 1:, :] = jnp.zeros((16 - N_DEV - 1, width), F32)
            db[...] = jnp.zeros_like(db)
            db[0:1, :] = tot + ctx

    return _whole(body, [gathered], [((16, width), F32), ((16, width), F32), ((8, width), F32), ((8, width), F32)],
                  name)


def _sum_slots(gathered, rows, *, name):
    def body(g_ref, o_ref):
        acc = g_ref[0:rows, :]
        for q in range(1, N_DEV):
            acc = acc + g_ref[q * rows:(q + 1) * rows, :]
        o_ref[...] = acc
    return _whole(body, [gathered], [((rows, gathered.shape[1]), F32)], name)[0]


def _silu_grad(x, dy, *, name):
    def body(x_ref, dy_ref, o_ref):
        _, vjp = jax.vjp(_silu, x_ref[...])
        o_ref[...] = vjp(dy_ref[...])[0]
    return _whole(body, [x, dy], [(x.shape, F32)], name)[0]


def _adamw(w, m, v, *, name, recv=None, grad=None):
    rows, cols = w.shape
    budget = max(8, ADAMW_BLOCK_ELEMS // cols)
    tr = _div(rows, [c for c in (512, 256, 128, 64, 32, 16, 8) if c <= budget] + [rows])
    c1 = 1.0 - ADAM_B1 ** ADAM_STEP
    c2 = 1.0 - ADAM_B2 ** ADAM_STEP

    def body(w_ref, m_ref, v_ref, g_in, g_ref, d_ref, nm_ref, nv_ref):
        if recv is not None:
            g = g_in[0].astype(F32)
            for q in range(1, N_DEV):
                g = g + g_in[q].astype(F32)
        else:
            g = g_in[...]
        nm = ADAM_B1 * m_ref[...] + (1.0 - ADAM_B1) * g
        nv = ADAM_B2 * v_ref[...] + (1.0 - ADAM_B2) * jnp.square(g)
        g_ref[...] = g
        nm_ref[...] = nm
        nv_ref[...] = nv
        d_ref[...] = -ADAM_LR * ((nm / c1) / (jnp.sqrt(nv / c2) + ADAM_EPS) + ADAM_WD * w_ref[...])

    blk = pl.BlockSpec((tr, cols), lambda i: (i, 0))
    g_spec = pl.BlockSpec((N_DEV, tr, cols), lambda i: (0, i, 0)) if recv is not None else blk
    return pl.pallas_call(
        body, grid=(rows // tr,), in_specs=[blk, blk, blk, g_spec], out_specs=[blk] * 4,
        out_shape=[jax.ShapeDtypeStruct((rows, cols), F32)] * 4,
        compiler_params=_params(("parallel",), 48), name=name,
    )(w, m, v, recv if recv is not None else grad)


def _position():
    return tuple(lax.axis_index(a) for a in MESH_AXES)


def _linear(pos):
    return 4 * pos[0] + 2 * pos[1] + pos[2]


def _window(ref, axis, dev, size):
    start = pl.multiple_of(dev * size, size)
    return ref.at[pl.ds(start, size), :] if axis == 0 else ref.at[:, pl.ds(start, size)]


def _all_gather(shards, axes, *, name):
    n = len(shards)
    sizes = [s.shape[ax] for s, ax in zip(shards, axes)]

    def body(*refs):
        src, dst = refs[:n], refs[n:2 * n]
        send_sems, recv_sems, local_sems = refs[2 * n:]
        x, y, c = _position()
        me, sibling = (x, y, c), (x, y, 1 - c)
        chips = [(1 - x, y), (x, 1 - y), (1 - x, 1 - y)]

        def win(k, pos):
            return _window(dst[k], axes[k], _linear(pos), sizes[k])

        def copy(k, sem, block, to, from_src=False):
            return pltpu.make_async_remote_copy(
                src_ref=src[k] if from_src else win(k, block), dst_ref=win(k, block),
                send_sem=send_sems.at[k, sem], recv_sem=recv_sems.at[k, sem],
                device_id=to, device_id_type=MESH_ID)

        mine = [pltpu.make_async_copy(src[k], win(k, me), local_sems.at[k]) for k in range(n)]
        for cp in mine:
            cp.start()
        first = []
        for k in range(n):
            first.append(copy(k, 0, me, sibling, from_src=True))
            first += [copy(k, 1 + j, me, (*chip, c), from_src=True) for j, chip in enumerate(chips)]
        for cp in first:
            cp.start()
        passed = []
        for j, chip in enumerate(chips):
            for k in range(n):
                copy(k, 1 + j, (*chip, c), me).wait_recv()
                fwd = copy(k, 4 + j, (*chip, c), sibling)
                fwd.start()
                passed.append(fwd)
        for k in range(n):
            copy(k, 0, sibling, me).wait_recv()
            for j, chip in enumerate(chips):
                copy(k, 4 + j, (*chip, 1 - c), me).wait_recv()
        for cp in first + passed:
            cp.wait_send()
        for cp in mine:
            cp.wait()

    out_shape = []
    for s, ax in zip(shards, axes):
        full = (s.shape[0] * N_DEV, s.shape[1]) if ax == 0 else (s.shape[0], s.shape[1] * N_DEV)
        out_shape.append(jax.ShapeDtypeStruct(full, s.dtype))
    any_spec = pl.BlockSpec(memory_space=pl.ANY)
    return pl.pallas_call(
        body, in_specs=[any_spec] * n, out_specs=[any_spec] * n, out_shape=out_shape,
        scratch_shapes=[pltpu.SemaphoreType.DMA((n, 7)), pltpu.SemaphoreType.DMA((n, 7)),
                        pltpu.SemaphoreType.DMA((n,))],
        name=name,
    )(*shards)


class Rider:
    def __init__(self, kind, arrays, axes):
        self.kind, self.arrays, self.axes = kind, list(arrays), list(axes)
        self.n = len(self.arrays)
        if kind == "gather":
            self.sizes = [a.shape[ax] for a, ax in zip(self.arrays, self.axes)]
        else:
            self.sizes = [a.shape[ax] // N_DEV for a, ax in zip(self.arrays, self.axes)]

    def out_shape(self):
        shapes = []
        for a, ax, sz in zip(self.arrays, self.axes, self.sizes):
            if self.kind == "gather":
                full = (sz * N_DEV, a.shape[1]) if ax == 0 else (a.shape[0], sz * N_DEV)
                shapes.append(jax.ShapeDtypeStruct(full, a.dtype))
            else:
                shard = (sz, a.shape[1]) if ax == 0 else (a.shape[0], sz)
                shapes.append(jax.ShapeDtypeStruct((N_DEV, *shard), a.dtype))
        return shapes

    def scratch(self):
        return [pltpu.SemaphoreType.DMA((self.n, N_DEV - 1)), pltpu.SemaphoreType.DMA((self.n, N_DEV - 1)),
                pltpu.SemaphoreType.DMA((self.n,))]

    def plan(self, src, dst, send_sems, recv_sems, local_sems):
        x, y, c = me = _position()
        mine = _linear(me)

        def remote(k, sem, src_ref, dst_ref, to):
            return pltpu.make_async_remote_copy(
                src_ref=src_ref, dst_ref=dst_ref, send_sem=send_sems.at[k, sem], recv_sem=recv_sems.at[k, sem],
                device_id=to, device_id_type=MESH_ID)

        ph = dict(local=[], start=[], mid_wait=[], mid_start=[], end_wait=[])
        for k in range(self.n):
            ax, sz = self.axes[k], self.sizes[k]
            if self.kind == "exchange":
                own_src, own_dst = _window(src[k], ax, mine, sz), dst[k].at[mine]
                ph["local"].append(pltpu.make_async_copy(own_src, own_dst, local_sems.at[k]))
                for mask in range(1, N_DEV):
                    to = tuple(1 - p if (mask >> (2 - b)) & 1 else p for b, p in enumerate(me))
                    ph["start"].append(remote(k, mask - 1, _window(src[k], ax, _linear(to), sz), own_dst, to))
                    ph["end_wait"].append(remote(k, mask - 1, own_src, dst[k].at[_linear(to)], to))
            else:
                def win(pos, k=k, ax=ax, sz=sz):
                    return _window(dst[k], ax, _linear(pos), sz)
                sibling = (x, y, 1 - c)
                chips = [(1 - x, y), (x, 1 - y), (1 - x, 1 - y)]
                ph["local"].append(pltpu.make_async_copy(src[k], win(me), local_sems.at[k]))
                ph["start"].append(remote(k, 0, src[k], win(me), sibling))
                ph["end_wait"].append(remote(k, 0, src[k], win(sibling), sibling))
                for j, chip in enumerate(chips):
                    ph["start"].append(remote(k, 1 + j, src[k], win(me), (*chip, c)))
                    ph["mid_wait"].append(remote(k, 1 + j, src[k], win((*chip, c)), (*chip, c)))
                    ph["mid_start"].append(remote(k, 4 + j, win((*chip, c)), win((*chip, c)), sibling))
                    ph["end_wait"].append(remote(k, 4 + j, src[k], win((*chip, 1 - c)), sibling))
        return ph


def _pcall(body, *, grid, in_specs, out_specs, out_shape, scratch_shapes, semantics, vmem_mb, name, operands,
           rider=None):
    if rider is None:
        return pl.pallas_call(body, grid=grid, in_specs=in_specs, out_specs=out_specs, out_shape=out_shape,
                              scratch_shapes=scratch_shapes, compiler_params=_params(semantics, vmem_mb),
                              name=name)(*operands)
    n_in, n_out, n_scr, n = len(in_specs), len(out_specs), len(scratch_shapes), rider.n

    def wrapped(*refs):
        ins, src = refs[:n_in], refs[n_in:n_in + n]
        outs = refs[n_in + n:n_in + n + n_out]
        dst = refs[n_in + n + n_out:n_in + 2 * n + n_out]
        rest = refs[n_in + 2 * n + n_out:]
        scratch, sems = rest[:n_scr], rest[n_scr:]
        step = functools.reduce(lambda acc, ig: acc * ig[1] + ig[0],
                                [(pl.program_id(dim), g) for dim, g in enumerate(grid)], 0)
        n_steps = functools.reduce(lambda a, b: a * b, grid)

        @pl.when(step == 0)
        def _():
            ph = rider.plan(src, dst, *sems)
            for cp in ph["local"] + ph["start"]:
                cp.start()

        body(*ins, *outs, *scratch)

        @pl.when(step == n_steps // 2)
        def _():
            ph = rider.plan(src, dst, *sems)
            for cp in ph["mid_wait"]:
                cp.wait_recv()
            for cp in ph["mid_start"]:
                cp.start()

        @pl.when(step == n_steps - 1)
        def _():
            ph = rider.plan(src, dst, *sems)
            for cp in ph["end_wait"]:
                cp.wait_recv()
            for cp in ph["start"] + ph["mid_start"]:
                cp.wait_send()
            for cp in ph["local"]:
                cp.wait()

    any_spec = pl.BlockSpec(memory_space=pl.ANY)
    return pl.pallas_call(
        wrapped, grid=grid, in_specs=list(in_specs) + [any_spec] * n, out_specs=list(out_specs) + [any_spec] * n,
        out_shape=list(out_shape) + rider.out_shape(), scratch_shapes=list(scratch_shapes) + rider.scratch(),
        compiler_params=_params(("arbitrary",) * len(grid), vmem_mb), name=name,
    )(*operands, *rider.arrays)


WEIGHTS = ['c_ctx', 'l0_ada_w', 'l0_ada_b', 'l0_in_w', 'l0_conv_w', 'l0_conv_b', 'l0_conv_ln_g', 'l0_conv_ln_b',
           'l0_pool_w', 'l0_pool_scale', 'l0_out_w', 'l0_mlp_w1', 'l0_mlp_w2', 'l1_ada_w', 'l1_ada_b', 'l1_qkv_w',
           'l1_q_norm_g', 'l1_k_norm_g', 'l1_out_w', 'l1_mlp_w1', 'l1_mlp_w2', 'final_g']
SHARDED = {'l0_in_w': 1, 'l0_out_w': 0, 'l0_mlp_w1': 1, 'l0_mlp_w2': 0,
           'l1_qkv_w': 1, 'l1_out_w': 0, 'l1_mlp_w1': 1, 'l1_mlp_w2': 0}
REPLICATED_SMALL = ['l0_conv_b', 'l0_conv_ln_g', 'l0_conv_ln_b', 'l0_pool_scale', 'l1_q_norm_g', 'l1_k_norm_g',
                    'final_g']


def _row8(v):
    v = v.reshape(1, -1)
    return jnp.pad(v, ((0, 7), (0, 0)))


def _mods16(full, me):
    d = full.shape[1] // 6
    mine = lax.dynamic_slice_in_dim(full, me, 1, axis=0).reshape(6, d)
    ctx = full[N_DEV].reshape(6, d)
    return jnp.pad(jnp.stack([mine, ctx], axis=1).reshape(12, d), ((0, 4), (0, 0)))


def _mlp_fwd(xs, mods, w1, w2, *, n_x, tm, tag, rider1=None, rider2=None):
    t, d = xs.shape
    dff = w1.shape[1]
    h = _rms_mod_fwd(xs, mods, k_shift=3, n_x=n_x, name=f"{tag}_norm2")
    pre, act, *ride1 = _matmul(h, w1, mode="nn", tm=tm, tn=_div(dff, (1024, 512)), tk=d, out_dtypes=[BF16, BF16],
                               name=f"{tag}_mlp1", rider=rider1,
                               epilogue=lambda acc, rows: (acc, jnp.square(jnp.maximum(acc, 0.0))))
    xo, branch, *ride2 = _matmul(
        act, w2, mode="nn", tm=tm, tn=_div(d, (1024, 512)), tk=_div(dff, (2048, 1024)), out_dtypes=[F32, BF16],
        name=f"{tag}_mlp2", extras=[("tile", xs), ("vec", mods)], rider=rider2,
        epilogue=lambda acc, rows, res, mv: (res + _seg_pick(mv, 5, rows, n_x) * acc, acc))
    return xo, dict(h=h, pre=pre, act=act, branch=branch, x_in=xs), ride1, ride2


def _mlp_bwd(dxo, dbranch, saved, mods, w1, w2, mixer_branch, *, n_x, tm, tag, ride_dx2=(), ride_dx1=()):
    t, d = dxo.shape
    dff = w1.shape[1]
    tkt = _div(t, (768, 1024, 512, 640, 128))

    def exchange(items):
        return Rider("exchange", [a for a, _ in items], [ax for _, ax in items]) if items else None

    dpre, *recv2 = _matmul(dbranch, w2, mode="nt", tm=tm, tn=_div(dff, (1024, 512)), tk=d, out_dtypes=[BF16],
                           name=f"{tag}_mlp2_dx", extras=[("tile", saved["pre"])], rider=exchange(list(ride_dx2)),
                           epilogue=lambda acc, rows, pre: (acc * 2.0 * jnp.maximum(pre.astype(F32), 0.0),))
    dw2, = _matmul(saved["act"], dbranch, mode="tn", tm=_div(dff, (1024, 512)), tn=_div(d, (1024, 512)), tk=tkt,
                   out_dtypes=[BF16], name=f"{tag}_mlp2_dw")
    dh, *recv1 = _matmul(dpre, w1, mode="nt", tm=tm, tn=_div(d, (1024, 512)), tk=_div(dff, (2048, 1024)),
                         out_dtypes=[F32], name=f"{tag}_mlp1_dx", rider=exchange(list(ride_dx1) + [(dw2, 0)]))
    dw1, = _matmul(saved["h"], dpre, mode="tn", tm=_div(d, (1024, 512)), tn=_div(dff, (1024, 512)), tk=tkt,
                   out_dtypes=[BF16], name=f"{tag}_mlp1_dw")
    dx, dm_norm, dmix, dm_gate = _rms_mod_bwd(saved["x_in"], mods, dh, dxo, k_shift=3, n_x=n_x,
                                              name=f"{tag}_norm2_bwd", gate=(mixer_branch, mods, 2))
    return dx, dw1, dm_norm + dm_gate, dmix, recv2, recv1


def kernel(x, c, ctx, c_ctx, l0_ada_w, l0_ada_b, l0_in_w, l0_conv_w, l0_conv_b, l0_conv_ln_g, l0_conv_ln_b, l0_pool_w, l0_pool_scale, l0_out_w, l0_mlp_w1, l0_mlp_w2, l1_ada_w, l1_ada_b, l1_qkv_w, l1_q_norm_g, l1_k_norm_g, l1_out_w, l1_mlp_w1, l1_mlp_w2, final_g, loss_target, m_c_ctx, m_l0_ada_w, m_l0_ada_b, m_l0_in_w, m_l0_conv_w, m_l0_conv_b, m_l0_conv_ln_g, m_l0_conv_ln_b, m_l0_pool_w, m_l0_pool_scale, m_l0_out_w, m_l0_mlp_w1, m_l0_mlp_w2, m_l1_ada_w, m_l1_ada_b, m_l1_qkv_w, m_l1_q_norm_g, m_l1_k_norm_g, m_l1_out_w, m_l1_mlp_w1, m_l1_mlp_w2, m_final_g, v_c_ctx, v_l0_ada_w, v_l0_ada_b, v_l0_in_w, v_l0_conv_w, v_l0_conv_b, v_l0_conv_ln_g, v_l0_conv_ln_b, v_l0_pool_w, v_l0_pool_scale, v_l0_out_w, v_l0_mlp_w1, v_l0_mlp_w2, v_l1_ada_w, v_l1_ada_b, v_l1_qkv_w, v_l1_q_norm_g, v_l1_k_norm_g, v_l1_out_w, v_l1_mlp_w1, v_l1_mlp_w2, v_final_g):
    p = dict(locals())
    me = _linear(_position())
    n_x, d = x.shape[1], x.shape[2]
    n_ctx = ctx.shape[1]
    t = n_x + n_ctx
    dc = l0_conv_b.shape[0]
    n_tap = l0_conv_w.shape[0]
    d_q = d
    n_mod = l0_ada_b.shape[0] // d
    ada_cols = l0_ada_w.shape[1]
    tm_t = _div(t, (768, 640, 512, 128))
    tm_x = _div(n_x, (1024, 512))

    names = list(SHARDED)
    shard16 = {nm: p[nm].astype(BF16) for nm in names}

    def gather_of(*nms):
        return Rider("gather", [shard16[nm] for nm in nms], [SHARDED[nm] for nm in nms])

    wfull = {}
    first = ['l0_in_w', 'l0_out_w']
    *full, conv_w_full, pool_w_full = _all_gather(
        [shard16[nm] for nm in first] + [jnp.pad(l0_conv_w, ((0, 1), (0, 0))),
                                         l0_pool_w.reshape(-1, l0_pool_w.shape[2])],
        [SHARDED[nm] for nm in first] + [1, 0], name="gather_first_weights")
    wfull.update(zip(first, full))
    n_grp, pg = l0_pool_w.shape[0], l0_pool_w.shape[2]
    pool_w_full = pool_w_full.reshape(N_DEV, n_grp, pg // N_DEV, pg).transpose(1, 0, 2, 3).reshape(n_grp, pg, pg)

    c_all = _all_gather([_row8(c)], [0], name="gather_cond")[0].reshape(N_DEV, 8, d)[:, 0]
    cond = jnp.concatenate([c_all, c_ctx.reshape(1, d), jnp.zeros((16 - N_DEV - 1, d), F32)], axis=0)
    s16 = _silu_rows(cond, name="cond_silu")
    mod_shards = []
    for li, (lw, lb) in enumerate(((l0_ada_w, l0_ada_b), (l1_ada_w, l1_ada_b))):
        bias = _row8(lax.dynamic_slice_in_dim(lb, me * ada_cols, ada_cols))
        mod_shards.append(_matmul(s16, lw, mode="nn", tm=16, tn=_div(ada_cols, (512, 384, 256, 128)), tk=d,
                                  out_dtypes=[F32], name=f"l{li}_ada_fwd", extras=[("vec", bias)],
                                  epilogue=lambda acc, rows, b: (acc + b[0:1],))[0])
    mods_full = _all_gather([jnp.concatenate(mod_shards, axis=0)], [1], name="gather_mods")[0]
    mods0, mods1 = _mods16(mods_full[:16], me), _mods16(mods_full[16:], me)

    xs0 = jnp.concatenate([x[0], ctx[0]], axis=0)
    h0 = _rms_mod_fwd(xs0, mods0, k_shift=0, n_x=n_x, name="l0_norm1")
    z, wfull['l0_mlp_w1'] = _matmul(h0, wfull['l0_in_w'], mode="nn", tm=tm_t, tn=_div(3 * dc, (1024, 768, 512, 384)),
                                    tk=d, out_dtypes=[F32], name="l0_in_proj", rider=gather_of('l0_mlp_w1'))
    y0, cv, dsave, wfull['l0_mlp_w2'] = _mixer0_fwd(
        z, conv_w_full, _row8(l0_conv_b), _row8(l0_conv_ln_g), _row8(l0_conv_ln_b), pool_w_full,
        _row8(l0_pool_scale), n_tap=n_tap, n_x=n_x, name="l0_mixer", rider=gather_of('l0_mlp_w2'))
    xs1, mix0 = _matmul(y0, wfull['l0_out_w'], mode="nn", tm=tm_t, tn=_div(d, (1024, 512)), tk=2 * dc,
                        out_dtypes=[F32, BF16], name="l0_out_proj", extras=[("tile", xs0), ("vec", mods0)],
                        epilogue=lambda acc, rows, res, mv: (res + _seg_pick(mv, 2, rows, n_x) * acc, acc))
    xs2, mlp0, ride1, ride2 = _mlp_fwd(
        xs1, mods0, wfull['l0_mlp_w1'], wfull['l0_mlp_w2'], n_x=n_x, tm=tm_t, tag="l0",
        rider1=gather_of('l1_qkv_w', 'l1_out_w'), rider2=gather_of('l1_mlp_w1', 'l1_mlp_w2'))
    wfull['l1_qkv_w'], wfull['l1_out_w'] = ride1
    wfull['l1_mlp_w1'], wfull['l1_mlp_w2'] = ride2

    h2 = _rms_mod_fwd(xs2, mods1, k_shift=0, n_x=n_x, name="l1_norm1")
    qkv, = _matmul(h2, wfull['l1_qkv_w'], mode="nn", tm=tm_t, tn=_div(l1_qkv_w.shape[1] * N_DEV, (1024, 768, 512)),
                   tk=d, out_dtypes=[F32], name="l1_qkv_proj")
    cos, sin = _rope_tables(n_x, n_ctx)
    gq, gk = _row8(l1_q_norm_g), _row8(l1_k_norm_g)
    q, k, v = _qk_prep_fwd(qkv, gq, gk, cos, sin, d_q=d_q, name="l1_qk_prep")
    o, lse = _attention_fwd(q, k, v, n_x=n_x, name="l1_attention")
    x3, mix1 = _matmul(o, wfull['l1_out_w'], mode="nn", tm=tm_x, tn=_div(d, (1024, 512)), tk=d_q,
                       out_dtypes=[F32, BF16], name="l1_out_proj", extras=[("tile", xs2), ("vec", mods1)],
                       epilogue=lambda acc, rows, res, mv: (res + mv[4:5] * acc, acc))
    x4, mlp1, _, _ = _mlp_fwd(x3, mods1, wfull['l1_mlp_w1'], wfull['l1_mlp_w2'], n_x=n_x, tm=tm_x, tag="l1")

    dx4, loss_part, dfinal_g, dbranch1, dmods1 = _final_loss(
        x4, _row8(final_g), loss_target[0], mlp1["branch"], mods1, k_gate=5, name="loss_head")
    loss = lax.psum(loss_part[0, 0], MESH_AXES)

    recv = {}
    dx3, dw, dm, dmix1, _, (recv['l1_mlp_w2'],) = _mlp_bwd(
        dx4, dbranch1, mlp1, mods1, wfull['l1_mlp_w1'], wfull['l1_mlp_w2'], mix1, n_x=n_x, tm=tm_x, tag="l1")
    dmods1 = dmods1 + dm
    pending = [(dw, SHARDED['l1_mlp_w1'])]
    do, = _matmul(dmix1, wfull['l1_out_w'], mode="nt", tm=tm_x, tn=_div(d_q, (1024, 512)), tk=d,
                  out_dtypes=[BF16], name="l1_out_dx")
    dw, = _matmul(o, dmix1, mode="tn", tm=_div(d_q, (1024, 512)), tn=_div(d, (1024, 512)),
                  tk=_div(n_x, (1024, 512)), out_dtypes=[BF16], name="l1_out_dw")
    pending.append((dw, SHARDED['l1_out_w']))
    dq, dk, dv = _attention_bwd(q, k, v, o, lse, do, n_x=n_x, name="l1_attention_bwd")
    dqkv, dgq, dgk = _qk_prep_bwd(qkv, dq, dk, dv, gq, gk, cos, sin, n_x=n_x, name="l1_qk_prep_bwd")
    tkt = _div(t, (768, 640, 512, 128))
    dh2, = _matmul(dqkv, wfull['l1_qkv_w'], mode="nt", tm=tm_t, tn=_div(d, (1024, 512)), tk=dqkv.shape[1],
                   out_dtypes=[F32], name="l1_qkv_dx")
    dw_qkv, = _matmul(h2, dqkv, mode="tn", tm=_div(d, (1024, 512)), tn=_div(dqkv.shape[1], (1024, 768, 512)),
                      tk=tkt, out_dtypes=[BF16], name="l1_qkv_dw")
    dxs2, dm, dbranch0, dmods0 = _rms_mod_bwd(xs2, mods1, dh2, dx3, k_shift=0, n_x=n_x, name="l1_norm1_bwd",
                                              dres_rows=n_x, gate=(mlp0["branch"], mods0, 5))
    dmods1 = dmods1 + dm

    (dxs1, dw1_0, dm, dmix0, (recv['l1_mlp_w1'], recv['l1_out_w']),
     (recv['l1_qkv_w'], recv['l0_mlp_w2'])) = _mlp_bwd(
        dxs2, dbranch0, mlp0, mods0, wfull['l0_mlp_w1'], wfull['l0_mlp_w2'], mix0, n_x=n_x, tm=tm_t, tag="l0",
        ride_dx2=pending, ride_dx1=[(dw_qkv, SHARDED['l1_qkv_w'])])
    dmods0 = dmods0 + dm
    dy0, = _matmul(dmix0, wfull['l0_out_w'], mode="nt", tm=tm_t, tn=_div(2 * dc, (1024, 512)), tk=d,
                   out_dtypes=[F32], name="l0_out_dx")
    dw_out0, = _matmul(y0, dmix0, mode="tn", tm=_div(2 * dc, (1024, 512)), tn=_div(d, (1024, 512)),
                       tk=tkt, out_dtypes=[BF16], name="l0_out_dw")
    dz, dconv_w, dconv_b, dln_g, dln_b, dpool_w, dpool_scale, recv['l0_mlp_w1'] = _mixer0_bwd(
        dy0, z, cv, dsave, conv_w_full, _row8(l0_conv_ln_g), _row8(l0_conv_ln_b), pool_w_full,
        _row8(l0_pool_scale), n_tap=n_tap, n_x=n_x, name="l0_mixer_bwd",
        rider=Rider("exchange", [dw1_0], [SHARDED['l0_mlp_w1']]))
    dw_in0, = _matmul(h0, dz, mode="tn", tm=_div(d, (1024, 512)), tn=_div(3 * dc, (1024, 768, 512, 384)),
                      tk=tkt, out_dtypes=[BF16], name="l0_in_dw")
    dh0, recv['l0_out_w'] = _matmul(dz, wfull['l0_in_w'], mode="nt", tm=tm_t, tn=_div(d, (1024, 512)), tk=3 * dc,
                                    out_dtypes=[F32], name="l0_in_dx",
                                    rider=Rider("exchange", [dw_out0], [SHARDED['l0_out_w']]))
    dx0, dm, recv['l0_in_w'] = _rms_mod_bwd(xs0, mods0, dh0, dxs1, k_shift=0, n_x=n_x, name="l0_norm1_bwd",
                                            out_rows=n_x, rider=Rider("exchange", [dw_in0], [SHARDED['l0_in_w']]))
    dmods0 = dmods0 + dm
    grad_x = dx0[None]

    def dmod_rows(dm16):
        rows = dm16[:2 * n_mod].reshape(n_mod, 2, d).transpose(1, 0, 2).reshape(2, n_mod * d)
        return jnp.pad(rows, ((0, 6), (0, 0)))
    dm_gathered = _all_gather([jnp.concatenate([dmod_rows(dmods0), dmod_rows(dmods1)], axis=0)], [0],
                              name="gather_dmods")[0]
    dm0, dm1, db0, db1 = _assemble_dmods(dm_gathered, name="assemble_dmods")
    out_g = {'l0_ada_b': db0[0], 'l1_ada_b': db1[0]}
    ds_part = jnp.zeros((16, d), F32)
    for nm, lw, dmf in (('l0_ada_w', l0_ada_w, dm0), ('l1_ada_w', l1_ada_w, dm1)):
        dm_cols = lax.dynamic_slice_in_dim(dmf, me * ada_cols, ada_cols, axis=1)
        out_g[nm], = _matmul(s16, dm_cols, mode="tn", tm=_div(d, (1024, 512)),
                             tn=_div(ada_cols, (512, 384, 256, 128)), tk=16, out_dtypes=[F32], name=f"{nm}_dw")
        ds_part = ds_part + _matmul(dm_cols, lw, mode="nt", tm=16, tn=_div(d, (1024, 512)),
                                    tk=_div(ada_cols, (512, 384, 256, 128)), out_dtypes=[F32], name=f"{nm}_dx")[0]

    small = {'l0_conv_b': dconv_b[0], 'l0_conv_ln_g': dln_g[0], 'l0_conv_ln_b': dln_b[0],
             'l0_pool_scale': dpool_scale[0], 'l1_q_norm_g': dgq[0], 'l1_k_norm_g': dgk[0],
             'final_g': dfinal_g[0], 'dsilu_ctx': ds_part[N_DEV], 'l0_conv_w': dconv_w[:-1].reshape(-1),
             'l0_pool_w': dpool_w.reshape(-1)}
    flat = jnp.concatenate([small[nm] for nm in small])
    rows = -(-flat.shape[0] // 1024) * 8
    packed = jnp.pad(flat, (0, rows * 128 - flat.shape[0])).reshape(rows, 128)
    summed = _sum_slots(_all_gather([packed], [0], name="gather_small_grads")[0], rows,
                        name="sum_small_grads").reshape(-1)
    off = 0
    for nm in small:
        size = small[nm].shape[0]
        small[nm] = summed[off:off + size]
        off += size
    out_g['c_ctx'] = _silu_grad(_row8(c_ctx), _row8(small['dsilu_ctx']), name="c_ctx_grad")[0]
    for nm in REPLICATED_SMALL:
        out_g[nm] = small[nm]
    conv_cols = l0_conv_w.shape[1]
    out_g['l0_conv_w'] = lax.dynamic_slice_in_dim(small['l0_conv_w'].reshape(n_tap, dc), me * conv_cols, conv_cols,
                                                  axis=1)
    out_g['l0_pool_w'] = lax.dynamic_slice_in_dim(small['l0_pool_w'].reshape(n_grp, pg, pg), me * (pg // N_DEV),
                                                  pg // N_DEV, axis=1)

    delta, new_m, new_v = {}, {}, {}
    for nm in names:
        out_g[nm], delta[nm], new_m[nm], new_v[nm] = _adamw(p[nm], p['m_' + nm], p['v_' + nm], recv=recv[nm],
                                                            name=f"adamw_{nm}")
    for nm in ('l0_ada_w', 'l1_ada_w'):
        out_g[nm], delta[nm], new_m[nm], new_v[nm] = _adamw(p[nm], p['m_' + nm], p['v_' + nm], grad=out_g[nm],
                                                            name=f"adamw_{nm}")
    for nm in WEIGHTS:
        if nm in delta:
            continue
        shape = p[nm].shape
        as2d = lambda a: a.reshape(1, -1) if a.ndim == 1 else a.reshape(-1, a.shape[-1])
        res = _adamw(as2d(p[nm]), as2d(p['m_' + nm]), as2d(p['v_' + nm]), grad=as2d(out_g[nm]), name=f"adamw_{nm}")
        out_g[nm], delta[nm], new_m[nm], new_v[nm] = [r.reshape(shape) for r in res]

    return (loss, grad_x, *[out_g[nm] for nm in WEIGHTS], *[delta[nm] for nm in WEIGHTS],
            *[new_m[nm] for nm in WEIGHTS], *[new_v[nm] for nm in WEIGHTS])
```

```python
import functools

import jax
import jax.numpy as jnp
from jax import lax
from jax.experimental import pallas as pl
from jax.experimental.pallas import tpu as pltpu

F32 = jnp.float32
BF16 = jnp.bfloat16
N_DEV = 8
MESH_AXES = ("x", "y", "c")
EPS = 1e-6
HEAD_DIM = 128
POOL_WINDOWS = (2, 4, 8, 16)
GRID_W = 64
ROPE_THETA = 10000.0
ATTN_SCALE = HEAD_DIM ** -0.5
LOG2_E = 1.4426950408889634
Q_SCALE_LOG2 = ATTN_SCALE * LOG2_E
HALO = 16
ADAM_LR, ADAM_B1, ADAM_B2, ADAM_EPS, ADAM_WD, ADAM_STEP = 0.001, 0.9, 0.999, 1e-08, 0.01, 10
VMEM_CAP_MB = 60
ADAMW_BLOCK_ELEMS = 1 << 18
TOKEN_TK = (2816, 2048, 1408, 1024, 768, 640, 512, 128)
DEEP_TM = (384, 512, 256, 128)
DEEP_TN = (512,)
MESH_ID = pl.DeviceIdType.MESH


def _div(n, prefs):
    for p in prefs:
        if n % p == 0:
            return p
    raise ValueError(f"no tile for {n} in {prefs}")


def _params(sem, vmem_mb):
    return pltpu.CompilerParams(dimension_semantics=sem, vmem_limit_bytes=min(vmem_mb, VMEM_CAP_MB) << 20)


def _sigmoid(x):
    return 1.0 / (1.0 + jnp.exp(-x))


def _silu(x):
    return x * _sigmoid(x)


def _rms(x):
    return x * lax.rsqrt(jnp.mean(x * x, axis=-1, keepdims=True) + EPS)


def _rms_mod(x, shift, scale):
    return _rms(x) * (1.0 + scale) + shift


def _layernorm(x, g, b):
    mu = jnp.mean(x, axis=-1, keepdims=True)
    var = jnp.mean(jnp.square(x - mu), axis=-1, keepdims=True)
    return (x - mu) * lax.rsqrt(var + EPS) * g + b


def _ln_silu(x, g, b):
    return _silu(_layernorm(x, g, b))


def _matmul(a, b, *, mode, tm, tn, tk, out_dtypes, name, extras=(), epilogue=None, rider=None):
    if mode == "tn":
        kdim, m = a.shape
        n = b.shape[1]
    else:
        m, kdim = a.shape
        n = b.shape[0] if mode == "nt" else b.shape[1]
    assert m % tm == 0 and n % tn == 0 and kdim % tk == 0, (name, m, n, kdim, tm, tn, tk)
    nk = kdim // tk
    n_ex = len(extras)
    n_out = len(out_dtypes)

    def body(a_ref, b_ref, *rest):
        ex = rest[:n_ex]
        outs = rest[n_ex:n_ex + n_out]
        acc_ref = rest[n_ex + n_out] if nk > 1 else None
        k = pl.program_id(2)
        av = a_ref[...].astype(BF16)
        bv = b_ref[...].astype(BF16)
        dims = {"nn": ((1,), (0,)), "nt": ((1,), (1,)), "tn": ((0,), (0,))}[mode]
        part = lax.dot_general(av, bv, (dims, ((), ())), preferred_element_type=F32)

        rows = pl.program_id(0) * tm + lax.broadcasted_iota(jnp.int32, (tm, 1), 0)

        def finish(acc):
            res = (acc,) if epilogue is None else epilogue(acc, rows, *[e[...] for e in ex])
            for o, r in zip(outs, res):
                o[...] = r.astype(o.dtype)

        if nk == 1:
            finish(part)
        else:
            @pl.when(k == 0)
            def _():
                acc_ref[...] = part

            @pl.when(k > 0)
            def _():
                acc_ref[...] += part

            @pl.when(k == nk - 1)
            def _():
                finish(acc_ref[...])

    if mode == "tn":
        a_spec = pl.BlockSpec((tk, tm), lambda i, j, k: (k, i))
        b_spec = pl.BlockSpec((tk, tn), lambda i, j, k: (k, j))
    else:
        a_spec = pl.BlockSpec((tm, tk), lambda i, j, k: (i, k))
        b_spec = (pl.BlockSpec((tn, tk), lambda i, j, k: (j, k)) if mode == "nt"
                  else pl.BlockSpec((tk, tn), lambda i, j, k: (k, j)))
    ex_specs, ex_arrays, ex_bytes = [], [], 0
    for kind, arr in extras:
        ex_arrays.append(arr)
        if kind == "tile":
            ex_specs.append(pl.BlockSpec((tm, tn), lambda i, j, k: (i, j)))
            ex_bytes += tm * tn * arr.dtype.itemsize
        else:
            ex_specs.append(pl.BlockSpec((arr.shape[0], tn), lambda i, j, k: (0, j)))
            ex_bytes += arr.shape[0] * tn * 4
    blocks = tm * tk * a.dtype.itemsize + tk * tn * b.dtype.itemsize + ex_bytes
    blocks += sum(tm * tn * jnp.dtype(d).itemsize for d in out_dtypes)
    casts = sum(rows * cols * 2 for arr, rows, cols in ((a, tm, tk), (b, tk, tn)) if arr.dtype != BF16)
    vmem = (2 * blocks + 4 * tm * tn * 4 + casts) // (1 << 20) + 8
    return _pcall(
        body,
        grid=(m // tm, n // tn, nk),
        in_specs=[a_spec, b_spec] + ex_specs,
        out_specs=[pl.BlockSpec((tm, tn), lambda i, j, k: (i, j)) for _ in out_dtypes],
        out_shape=[jax.ShapeDtypeStruct((m, n), d) for d in out_dtypes],
        scratch_shapes=[pltpu.VMEM((tm, tn), F32)] if nk > 1 else [],
        semantics=("parallel", "parallel", "arbitrary"), vmem_mb=vmem, name=name,
        operands=[a, b, *ex_arrays], rider=rider)


def _seg_pick(vec, k, rows, n_x):
    return jnp.where(rows < n_x, vec[2 * k:2 * k + 1], vec[2 * k + 1:2 * k + 2])


def _zero_accs(i, accs):
    @pl.when(i == 0)
    def _():
        for a in accs:
            a[...] = jnp.zeros_like(a)


def _rms_mod_fwd(xs, mods, *, k_shift, n_x, name):
    t, d = xs.shape
    tm = _div(t, (256, 128))

    def body(x_ref, mods_ref, h_ref):
        seg = (pl.program_id(0) * tm >= n_x).astype(jnp.int32)
        shift = mods_ref[pl.ds(2 * k_shift + seg, 1), :]
        scale = mods_ref[pl.ds(2 * k_shift + 2 + seg, 1), :]
        h_ref[...] = _rms_mod(x_ref[...], shift, scale).astype(BF16)

    return pl.pallas_call(
        body, grid=(t // tm,),
        in_specs=[pl.BlockSpec((tm, d), lambda i: (i, 0)), pl.BlockSpec((16, d), lambda i: (0, 0))],
        out_specs=pl.BlockSpec((tm, d), lambda i: (i, 0)),
        out_shape=jax.ShapeDtypeStruct((t, d), BF16),
        compiler_params=_params(("parallel",), 32), name=name,
    )(xs, mods)


def _gate_part(dxv, seg, k_gate, br_ref, gmods_ref, db_ref, dgm_ref):
    r_gate = 2 * k_gate + seg
    db_ref[...] = (dxv * gmods_ref[pl.ds(r_gate, 1), :]).astype(BF16)
    dgm_ref[pl.ds(r_gate, 1), :] += jnp.sum(dxv * br_ref[...].astype(F32), axis=0, keepdims=True)


def _rms_mod_bwd(xs, mods, dh, dres, *, k_shift, n_x, name, rider=None, gate=None, dres_rows=None, out_rows=None):
    t, d = xs.shape
    tm = _div(t, (256, 128))
    n_gate = 2 if gate is not None else 0

    def body(x_ref, mods_ref, dh_ref, dres_ref, *rest):
        gate_in, (dx_ref, dmods_ref), gate_out = rest[:n_gate], rest[n_gate:n_gate + 2], rest[n_gate + 2:]
        i = pl.program_id(0)
        _zero_accs(i, [dmods_ref, *gate_out[1:]])
        seg = (i * tm >= n_x).astype(jnp.int32)
        r_shift = 2 * k_shift + seg
        r_scale = 2 * k_shift + 2 + seg
        shift = mods_ref[pl.ds(r_shift, 1), :]
        scale = mods_ref[pl.ds(r_scale, 1), :]
        _, vjp = jax.vjp(_rms_mod, x_ref[...], shift, scale)
        dx, dshift, dscale = vjp(dh_ref[...].astype(F32))
        dres_v = dres_ref[...]
        if dres_rows is not None:
            dres_v = dres_v * jnp.where(i * tm < dres_rows, 1.0, 0.0)
        dx = dres_v + dx
        if out_rows is None:
            dx_ref[...] = dx
        else:
            @pl.when(i * tm < out_rows)
            def _():
                dx_ref[...] = dx
        dmods_ref[pl.ds(r_shift, 1), :] += dshift
        dmods_ref[pl.ds(r_scale, 1), :] += dscale
        if gate is not None:
            _gate_part(dx, seg, gate[2], *gate_in, *gate_out)

    def clamped(rows):
        return pl.BlockSpec((tm, d), lambda i: (jnp.minimum(i, rows // tm - 1), 0))

    row = pl.BlockSpec((tm, d), lambda i: (i, 0))
    vec = pl.BlockSpec((16, d), lambda i: (0, 0))
    in_specs = [row, vec, row, row if dres_rows is None else clamped(dres_rows)]
    out_specs = [row if out_rows is None else clamped(out_rows), vec]
    out_shape = [jax.ShapeDtypeStruct((t if out_rows is None else out_rows, d), F32),
                 jax.ShapeDtypeStruct((16, d), F32)]
    operands = [xs, mods, dh, dres]
    if gate is not None:
        in_specs += [row, vec]
        out_specs += [row, vec]
        out_shape += [jax.ShapeDtypeStruct((t, d), BF16), jax.ShapeDtypeStruct((16, d), F32)]
        operands += [gate[0], gate[1]]
    return _pcall(body, grid=(t // tm,), in_specs=in_specs, out_specs=out_specs, out_shape=out_shape,
                  scratch_shapes=[], semantics=("arbitrary",), vmem_mb=48, name=name, operands=operands, rider=rider)


def _final_loss(xs, g, target, branch, gmods, *, k_gate, name):
    t, d = xs.shape
    tm = _div(t, (256, 128))

    def loss_fn(x, gv, tgt):
        err = _rms(x) * gv - tgt
        return 0.5 * jnp.sum(jnp.mean(jnp.square(err), axis=-1))

    def body(x_ref, g_ref, t_ref, br_ref, gmods_ref, dx_ref, loss_ref, dg_ref, db_ref, dgm_ref):
        i = pl.program_id(0)
        _zero_accs(i, [loss_ref, dg_ref, dgm_ref])
        val, vjp = jax.vjp(loss_fn, x_ref[...], g_ref[0:1, :], t_ref[...])
        dx, dg, _ = vjp(jnp.ones((), F32))
        dx_ref[...] = dx
        loss_ref[...] += val
        dg_ref[0:1, :] += dg
        _gate_part(dx, 0, k_gate, br_ref, gmods_ref, db_ref, dgm_ref)

    row = pl.BlockSpec((tm, d), lambda i: (i, 0))
    vec16 = pl.BlockSpec((16, d), lambda i: (0, 0))
    return pl.pallas_call(
        body, grid=(t // tm,),
        in_specs=[row, pl.BlockSpec((8, d), lambda i: (0, 0)), row, row, vec16],
        out_specs=[row, pl.BlockSpec((8, 128), lambda i: (0, 0)), pl.BlockSpec((8, d), lambda i: (0, 0)), row, vec16],
        out_shape=[jax.ShapeDtypeStruct((t, d), F32), jax.ShapeDtypeStruct((8, 128), F32),
                   jax.ShapeDtypeStruct((8, d), F32), jax.ShapeDtypeStruct((t, d), BF16),
                   jax.ShapeDtypeStruct((16, d), F32)],
        compiler_params=_params(("arbitrary",), 48), name=name,
    )(xs, g, target, branch, gmods)


def _halo_specs(r, width, col, t):
    h_per = r // HALO

    def prev(i):
        return (jnp.maximum(i * h_per - 1, 0), col)

    def nxt(i):
        return (jnp.minimum((i + 1) * h_per, t // HALO - 1), col)

    return (pl.BlockSpec((HALO, width), prev), pl.BlockSpec((r, width), lambda i: (i, col)),
            pl.BlockSpec((HALO, width), nxt))


def _seg_geometry(i, r, n_x, t):
    row0 = i * r
    in_ctx = row0 >= n_x
    first = jnp.logical_or(row0 == 0, row0 == n_x)
    last = jnp.logical_or(row0 + r == n_x, row0 + r == t)
    seg_start = jnp.where(in_ctx, n_x, 0)
    seg_len = jnp.where(in_ctx, t - n_x, n_x)
    return row0, first, last, seg_start, seg_len


ROW_CHUNK = 64
LANE_CHUNK = 128


def _chunks(width, rows, col0=0):
    lanes = min(LANE_CHUNK, width)
    return [(slice(col0 + c, col0 + c + lanes), r0) for c in range(0, width, lanes) for r0 in range(0, rows, ROW_CHUNK)]


def _pool_count(tpos, w, seg_len):
    return (jnp.minimum(tpos + w // 2, seg_len) - jnp.maximum(tpos - w // 2, 0)).astype(F32)


def _mixer0_fwd(z, conv_w, conv_b, ln_g, ln_b, pool_w, pool_scale, *, n_tap, n_x, name, rider=None):
    t, dc = z.shape[0], z.shape[1] // 3
    n_grp, pg = pool_w.shape[0], pool_w.shape[1]
    r = _div(t - n_x, (256, 128))
    assert n_x % r == 0 and pg * n_grp == dc
    half = n_tap // 2
    assert half < HALO and max(POOL_WINDOWS) // 2 <= HALO

    def body(ap, ac, an, gp, gc, gn, pp, pc, pn, w_ref, cb_ref, lg_ref, lb_ref, pw_ref, ps_ref,
             y_ref, cv_ref, d_ref, uwin, pwin):
        i = pl.program_id(0)
        row0, first, last, seg_start, seg_len = _seg_geometry(i, r, n_x, t)
        keep_prev = jnp.where(first, 0.0, 1.0)
        keep_next = jnp.where(last, 0.0, 1.0)
        uwin[0:HALO, :] = ap[...] * _sigmoid(gp[...]) * keep_prev
        uwin[HALO:HALO + r, :] = ac[...] * _sigmoid(gc[...])
        uwin[HALO + r:, :] = an[...] * _sigmoid(gn[...]) * keep_next
        pwin[0:HALO, :] = pp[...] * keep_prev
        pwin[HALO:HALO + r, :] = pc[...]
        pwin[HALO + r:, :] = pn[...] * keep_next
        for cols, r0 in _chunks(dc, r):
            acc = jnp.zeros((ROW_CHUNK, cols.stop - cols.start), F32) + cb_ref[0:1, cols]
            for k in range(n_tap):
                off = HALO - half + k + r0
                acc = acc + w_ref[k:k + 1, cols] * uwin[off:off + ROW_CHUNK, cols]
            cv_ref[r0:r0 + ROW_CHUNK, cols] = acc
        y_ref[:, 0:dc] = _ln_silu(cv_ref[...], lg_ref[0:1, :], lb_ref[0:1, :]).astype(BF16)
        tpos = row0 - seg_start + lax.broadcasted_iota(jnp.int32, (r, 1), 0)
        for g, w in enumerate(POOL_WINDOWS):
            cnt = _pool_count(tpos, w, seg_len)
            for cols, r0 in _chunks(pg, r, g * pg):
                s = jnp.zeros((ROW_CHUNK, cols.stop - cols.start), F32)
                for j in range(-(w // 2), w // 2):
                    s = s + pwin[HALO + j + r0:HALO + j + r0 + ROW_CHUNK, cols]
                diff = s / cnt[r0:r0 + ROW_CHUNK] - pwin[HALO + r0:HALO + r0 + ROW_CHUNK, cols]
                d_ref[r0:r0 + ROW_CHUNK, cols] = diff.astype(BF16)
            cols = slice(g * pg, (g + 1) * pg)
            pm = jnp.dot(d_ref[:, cols], pw_ref[g].astype(BF16), preferred_element_type=F32)
            y_ref[:, dc + g * pg:dc + (g + 1) * pg] = (pm * ps_ref[0:1, cols]).astype(BF16)

    vec = lambda rows, width: pl.BlockSpec((rows, width), lambda i: (0, 0))
    in_specs = [*_halo_specs(r, dc, 0, t), *_halo_specs(r, dc, 1, t), *_halo_specs(r, dc, 2, t),
                vec(conv_w.shape[0], dc), vec(8, dc), vec(8, dc), vec(8, dc),
                pl.BlockSpec((n_grp, pg, pg), lambda i: (0, 0, 0)), vec(8, dc)]
    return _pcall(
        body, grid=(t // r,), in_specs=in_specs,
        out_specs=[pl.BlockSpec((r, 2 * dc), lambda i: (i, 0)), pl.BlockSpec((r, dc), lambda i: (i, 0)),
                   pl.BlockSpec((r, dc), lambda i: (i, 0))],
        out_shape=[jax.ShapeDtypeStruct((t, 2 * dc), BF16), jax.ShapeDtypeStruct((t, dc), F32),
                   jax.ShapeDtypeStruct((t, dc), BF16)],
        scratch_shapes=[pltpu.VMEM((r + 2 * HALO, dc), F32), pltpu.VMEM((r + 2 * HALO, dc), F32)],
        semantics=("parallel",), vmem_mb=40, name=name,
        operands=[*([z] * 9), conv_w, conv_b, ln_g, ln_b, pool_w, pool_scale], rider=rider)


def _mixer0_bwd(dy, z, cv, dsave, conv_w, ln_g, ln_b, pool_w, pool_scale, *, n_tap, n_x, name, rider=None):
    t, dc = cv.shape
    n_grp, pg = pool_w.shape[0], pool_w.shape[1]
    r = _div(t - n_x, (256, 128))
    half = n_tap // 2
    dyp_, zp_, cvp_ = dy, z, cv
    rw = r + 2 * HALO

    def body(dcp, dcc, dcn, dpp, dpc, dpn, cvp, cvc, cvn, ap, ac, an, gp, gc, gn, d_ref,
             w_ref, lg_ref, lb_ref, pw_ref, ps_ref,
             dz_ref, dw_ref, dcb_ref, dlg_ref, dlb_ref, dpw_ref, dps_ref, uwin, dcvwin, ewin, ddwin):
        i = pl.program_id(0)
        _zero_accs(i, [dw_ref, dcb_ref, dlg_ref, dlb_ref, dpw_ref, dps_ref])
        row0, first, last, seg_start, seg_len = _seg_geometry(i, r, n_x, t)
        keep_prev = jnp.where(first, 0.0, 1.0)
        keep_next = jnp.where(last, 0.0, 1.0)
        lg, lb = lg_ref[0:1, :], lb_ref[0:1, :]
        _, vjp = jax.vjp(_ln_silu, cvc[...], lg, lb)
        dcv, dlg, dlb = vjp(dcc[...])
        dlg_ref[0:1, :] += dlg
        dlb_ref[0:1, :] += dlb
        dcb_ref[0:1, :] += jnp.sum(dcv, axis=0, keepdims=True)
        dcvwin[HALO:HALO + r, :] = dcv
        for halo_cv, halo_dy, keep, lo in ((cvp, dcp, keep_prev, 0), (cvn, dcn, keep_next, HALO + r)):
            _, vjp_h = jax.vjp(lambda v: _ln_silu(v, lg, lb), halo_cv[...])
            dcvwin[lo:lo + HALO, :] = vjp_h(halo_dy[...])[0] * keep
        uwin[0:HALO, :] = ap[...] * _sigmoid(gp[...]) * keep_prev
        uwin[HALO:HALO + r, :] = ac[...] * _sigmoid(gc[...])
        uwin[HALO + r:, :] = an[...] * _sigmoid(gn[...]) * keep_next
        for cols, r0 in _chunks(dc, r):
            du = jnp.zeros((ROW_CHUNK, cols.stop - cols.start), F32)
            for k in range(n_tap):
                off = HALO + half - k + r0
                du = du + w_ref[k:k + 1, cols] * dcvwin[off:off + ROW_CHUNK, cols]
            rows = slice(r0, r0 + ROW_CHUNK)
            sig = _sigmoid(gc[rows, cols])
            dz_ref[rows, cols] = (du * sig).astype(BF16)
            dz_ref[rows, dc + cols.start:dc + cols.stop] = (du * ac[rows, cols] * sig * (1.0 - sig)).astype(BF16)
        for c0 in range(0, dc, LANE_CHUNK):
            cols = slice(c0, c0 + LANE_CHUNK)
            taps = [jnp.zeros((8, LANE_CHUNK), F32) for _ in range(n_tap)]
            for r0 in range(0, r, ROW_CHUNK):
                dcv_c = dcvwin[HALO + r0:HALO + r0 + ROW_CHUNK, cols]
                for k in range(n_tap):
                    off = HALO - half + k + r0
                    prod = dcv_c * uwin[off:off + ROW_CHUNK, cols]
                    taps[k] = taps[k] + functools.reduce(
                        jnp.add, [prod[8 * s:8 * s + 8] for s in range(ROW_CHUNK // 8)])
            for k in range(n_tap):
                dw_ref[k:k + 1, cols] += jnp.sum(taps[k], axis=0, keepdims=True)
        twin = row0 - seg_start - HALO + lax.broadcasted_iota(jnp.int32, (rw, 1), 0)
        for g, w in enumerate(POOL_WINDOWS):
            cols = slice(g * pg, (g + 1) * pg)
            wg = pw_ref[g].astype(BF16)
            scale = ps_ref[0:1, cols]
            dyp_c = dpc[:, cols]
            dpm_win = jnp.concatenate([dpp[:, cols] * keep_prev, dyp_c, dpn[:, cols] * keep_next], axis=0) * scale
            dd_win = lax.dot_general(dpm_win.astype(BF16), wg, (((1,), (1,)), ((), ())), preferred_element_type=F32)
            cnt = jnp.maximum(_pool_count(twin, w, seg_len), 1.0)
            ddwin[:, cols] = dd_win
            ewin[:, cols] = dd_win / cnt
            for ccols, r0 in _chunks(pg, r, g * pg):
                dup = -ddwin[HALO + r0:HALO + r0 + ROW_CHUNK, ccols]
                for j in range(-(w // 2) + 1, w // 2 + 1):
                    dup = dup + ewin[HALO + j + r0:HALO + j + r0 + ROW_CHUNK, ccols]
                dz_ref[r0:r0 + ROW_CHUNK, 2 * dc + ccols.start:2 * dc + ccols.stop] = dup.astype(BF16)
            dsv = d_ref[:, cols]
            pm = jnp.dot(dsv, wg, preferred_element_type=F32)
            dps_ref[0:1, cols] += jnp.sum(dyp_c * pm, axis=0, keepdims=True)
            dpw_ref[g] += lax.dot_general(dsv, (dyp_c * scale).astype(BF16), (((0,), (0,)), ((), ())),
                                          preferred_element_type=F32)

    vec = lambda rows, width: pl.BlockSpec((rows, width), lambda i: (0, 0))
    grp = pl.BlockSpec((n_grp, pg, pg), lambda i: (0, 0, 0))
    in_specs = [*_halo_specs(r, dc, 0, t), *_halo_specs(r, dc, 1, t), *_halo_specs(r, dc, 0, t),
                *_halo_specs(r, dc, 0, t), *_halo_specs(r, dc, 1, t), pl.BlockSpec((r, dc), lambda i: (i, 0)),
                vec(conv_w.shape[0], dc), vec(8, dc), vec(8, dc), grp, vec(8, dc)]
    return _pcall(
        body, grid=(t // r,), in_specs=in_specs,
        out_specs=[pl.BlockSpec((r, 3 * dc), lambda i: (i, 0)), vec(conv_w.shape[0], dc), vec(8, dc), vec(8, dc),
                   vec(8, dc), grp, vec(8, dc)],
        out_shape=[jax.ShapeDtypeStruct((t, 3 * dc), BF16), jax.ShapeDtypeStruct(conv_w.shape, F32),
                   jax.ShapeDtypeStruct((8, dc), F32), jax.ShapeDtypeStruct((8, dc), F32),
                   jax.ShapeDtypeStruct((8, dc), F32), jax.ShapeDtypeStruct(pool_w.shape, F32),
                   jax.ShapeDtypeStruct((8, dc), F32)],
        scratch_shapes=[pltpu.VMEM((rw, dc), F32)] * 4,
        semantics=("arbitrary",), vmem_mb=VMEM_CAP_MB, name=name,
        operands=[dyp_, dyp_, dyp_, dyp_, dyp_, dyp_, cvp_, cvp_, cvp_, zp_, zp_, zp_, zp_, zp_, zp_, dsave,
                  conv_w, ln_g, ln_b, pool_w, pool_scale], rider=rider)


def _swap_halves(x):
    lane = lax.broadcasted_iota(jnp.int32, x.shape, 1)
    quarter = HEAD_DIM // 4
    return jnp.where(lane % (2 * quarter) < quarter,
                     pltpu.roll(x, HEAD_DIM - quarter, 1), pltpu.roll(x, quarter, 1))


def _rope_tables(n_x, n_ctx):
    half = HEAD_DIM // 4
    freqs = ROPE_THETA ** (-jnp.arange(half, dtype=F32) / half)
    tok = jnp.arange(n_x)
    row = (tok // GRID_W).astype(F32)[:, None] * freqs[None, :]
    col = (tok % GRID_W).astype(F32)[:, None] * freqs[None, :]
    cos = jnp.concatenate([jnp.cos(row), jnp.cos(row), jnp.cos(col), jnp.cos(col)], axis=1)
    sin = jnp.concatenate([-jnp.sin(row), jnp.sin(row), -jnp.sin(col), jnp.sin(col)], axis=1)
    cos = jnp.concatenate([cos, jnp.ones((n_ctx, HEAD_DIM), F32)], axis=0)
    sin = jnp.concatenate([sin, jnp.zeros((n_ctx, HEAD_DIM), F32)], axis=0)
    return cos, sin


def _norm_g(x, g):
    return _rms(x) * g


def _qk_prep_fwd(qkv, gq, gk, cos, sin, *, d_q, name):
    t, width = qkv.shape
    d_kv = (width - d_q) // 2
    tm = _div(t, (256, 128))

    def body(qkv_ref, gq_ref, gk_ref, cos_ref, sin_ref, q_ref, k_ref, v_ref):
        cs, sn = cos_ref[...], sin_ref[...]
        for h in range((d_q + d_kv) // HEAD_DIM):
            g = gq_ref[0:1, :] if h * HEAD_DIM < d_q else gk_ref[0:1, :]
            xn = _norm_g(qkv_ref[:, h * HEAD_DIM:(h + 1) * HEAD_DIM], g)
            rot = xn * cs + _swap_halves(xn) * sn
            if h * HEAD_DIM < d_q:
                q_ref[:, h * HEAD_DIM:(h + 1) * HEAD_DIM] = (rot * Q_SCALE_LOG2).astype(BF16)
            else:
                k_ref[:, h * HEAD_DIM - d_q:(h + 1) * HEAD_DIM - d_q] = rot.astype(BF16)
        v_ref[...] = qkv_ref[:, d_q + d_kv:].astype(BF16)

    row = lambda w: pl.BlockSpec((tm, w), lambda i: (i, 0))
    vec = pl.BlockSpec((8, HEAD_DIM), lambda i: (0, 0))
    return pl.pallas_call(
        body, grid=(t // tm,),
        in_specs=[row(width), vec, vec, row(HEAD_DIM), row(HEAD_DIM)],
        out_specs=[row(d_q), row(d_kv), row(d_kv)],
        out_shape=[jax.ShapeDtypeStruct((t, d_q), BF16), jax.ShapeDtypeStruct((t, d_kv), BF16),
                   jax.ShapeDtypeStruct((t, d_kv), BF16)],
        compiler_params=_params(("parallel",), 32), name=name,
    )(qkv, gq, gk, cos, sin)


def _qk_prep_bwd(qkv, dq, dk, dv, gq, gk, cos, sin, *, n_x, name):
    t, width = qkv.shape
    d_q, d_kv = dq.shape[1], dk.shape[1]
    tm = _div(t, (256, 128))
    last_q = n_x // tm - 1

    def body(qkv_ref, dq_ref, dk_ref, dv_ref, gq_ref, gk_ref, cos_ref, sin_ref, out_ref, dgq_ref, dgk_ref):
        i = pl.program_id(0)
        _zero_accs(i, [dgq_ref, dgk_ref])
        is_x = jnp.where(i * tm < n_x, 1.0, 0.0)
        cs, sn = cos_ref[...], sin_ref[...]
        for h in range((d_q + d_kv) // HEAD_DIM):
            sl = slice(h * HEAD_DIM, (h + 1) * HEAD_DIM)
            if h * HEAD_DIM < d_q:
                g, dg_ref, dr = gq_ref[0:1, :], dgq_ref, dq_ref[:, sl] * is_x
            else:
                g, dg_ref = gk_ref[0:1, :], dgk_ref
                dr = dk_ref[:, h * HEAD_DIM - d_q:(h + 1) * HEAD_DIM - d_q]
            dxn = dr * cs + _swap_halves(dr * sn)
            _, vjp = jax.vjp(_norm_g, qkv_ref[:, sl], g)
            dx, dg = vjp(dxn)
            out_ref[:, sl] = dx.astype(BF16)
            dg_ref[0:1, :] += dg
        out_ref[:, d_q + d_kv:] = dv_ref[...].astype(BF16)

    row = lambda w: pl.BlockSpec((tm, w), lambda i: (i, 0))
    vec = pl.BlockSpec((8, HEAD_DIM), lambda i: (0, 0))
    return pl.pallas_call(
        body, grid=(t // tm,),
        in_specs=[row(width), pl.BlockSpec((tm, d_q), lambda i: (jnp.minimum(i, last_q), 0)), row(d_kv), row(d_kv),
                  vec, vec, row(HEAD_DIM), row(HEAD_DIM)],
        out_specs=[row(width), vec, vec],
        out_shape=[jax.ShapeDtypeStruct((t, width), BF16), jax.ShapeDtypeStruct((8, HEAD_DIM), F32),
                   jax.ShapeDtypeStruct((8, HEAD_DIM), F32)],
        compiler_params=_params(("arbitrary",), 40), name=name,
    )(qkv, dq, dk, dv, gq, gk, cos, sin)


ATTN_TQ = (256, 128)
ATTN_TK = (768, 640, 512, 384, 256, 128)


def _attention_fwd(q, k, v, *, n_x, name):
    t, d_kv = k.shape
    d_q = q.shape[1]
    kvh = d_kv // HEAD_DIM
    grp = d_q // d_kv
    tq = _div(n_x, ATTN_TQ)
    tk = _div(t, ATTN_TK)
    gw = grp * HEAD_DIM

    n_kv = t // tk
    n_pair = (n_kv - 1) // 2

    def body(q_ref, k_ref, v_ref, o_ref, lse_ref, m_ref, l_ref, acc_ref, s_even, s_odd):
        m_ref[...] = jnp.full_like(m_ref, -jnp.inf)
        l_ref[...] = jnp.zeros_like(l_ref)
        acc_ref[...] = jnp.zeros_like(acc_ref)

        def key_rows(j):
            return pl.ds(pl.multiple_of(j * tk, tk), tk)

        def scores(g, kc):
            return lax.dot_general(q_ref[:, g * HEAD_DIM:(g + 1) * HEAD_DIM], kc, (((1,), (1,)), ((), ())),
                                   preferred_element_type=F32)

        def chunk(j, s_cur, s_next):
            vc = v_ref[key_rows(j), :]
            kn = k_ref[key_rows(j + 1), :] if s_next is not None else None
            for g in range(grp):
                s = s_cur[g]
                m_old = m_ref[g]
                m_new = jnp.maximum(m_old, jnp.max(s, axis=-1, keepdims=True))
                alpha = jnp.exp2(m_old - m_new)
                p = jnp.exp2(s - m_new)
                if s_next is not None:
                    s_next[g] = scores(g, kn)
                l_ref[g] = alpha * l_ref[g] + jnp.sum(p, axis=-1, keepdims=True)
                acc_ref[g] = alpha * acc_ref[g] + jnp.dot(p.astype(BF16), vc, preferred_element_type=F32)
                m_ref[g] = m_new

        k0 = k_ref[key_rows(0), :]
        for g in range(grp):
            s_even[g] = scores(g, k0)

        def pair(i, carry):
            chunk(2 * i, s_even, s_odd)
            chunk(2 * i + 1, s_odd, s_even)
            return carry

        lax.fori_loop(0, n_pair, pair, 0)
        if n_kv - 2 * n_pair == 2:
            chunk(n_kv - 2, s_even, s_odd)
            chunk(n_kv - 1, s_odd, None)
        else:
            chunk(n_kv - 1, s_even, None)
        for g in range(grp):
            o_ref[:, g * HEAD_DIM:(g + 1) * HEAD_DIM] = (acc_ref[g] / l_ref[g]).astype(BF16)
            lse_ref[:, g:g + 1] = m_ref[g] + jnp.log(l_ref[g]) * LOG2_E

    return pl.pallas_call(
        body, grid=(kvh, n_x // tq),
        in_specs=[pl.BlockSpec((tq, gw), lambda h, i: (i, h)),
                  pl.BlockSpec((t, HEAD_DIM), lambda h, i: (0, h)),
                  pl.BlockSpec((t, HEAD_DIM), lambda h, i: (0, h))],
        out_specs=[pl.BlockSpec((tq, gw), lambda h, i: (i, h)),
                   pl.BlockSpec((None, tq, grp), lambda h, i: (h, i, 0))],
        out_shape=[jax.ShapeDtypeStruct((n_x, d_q), BF16), jax.ShapeDtypeStruct((kvh, n_x, grp), F32)],
        scratch_shapes=[pltpu.VMEM((grp, tq, 1), F32), pltpu.VMEM((grp, tq, 1), F32),
                        pltpu.VMEM((grp, tq, HEAD_DIM), F32),
                        pltpu.VMEM((grp, tq, tk), F32), pltpu.VMEM((grp, tq, tk), F32)],
        compiler_params=_params(("parallel", "arbitrary"), 48), name=name,
    )(q, k, v)


def _attention_bwd(q, k, v, o, lse, do, *, n_x, name):
    t, d_kv = k.shape
    d_q = q.shape[1]
    kvh = d_kv // HEAD_DIM
    grp = d_q // d_kv
    tq = _div(n_x, ATTN_TQ)
    tk = _div(t, ATTN_TK)
    gw = grp * HEAD_DIM
    n_q = n_x // tq

    def body(q_ref, k_ref, v_ref, o_ref, lse_ref, do_ref, dq_ref, dk_ref, dv_ref, dq_acc, lse_s, delta_s):
        i = pl.program_id(1)
        _zero_accs(i, [dk_ref, dv_ref])
        dq_acc[...] = jnp.zeros_like(dq_acc)
        for g in range(grp):
            sl = slice(g * HEAD_DIM, (g + 1) * HEAD_DIM)
            lse_s[g] = lse_ref[:, g:g + 1]
            delta_s[g] = jnp.sum(do_ref[:, sl].astype(F32) * o_ref[:, sl].astype(F32), axis=-1, keepdims=True)

        def step(j, carry):
            rows = pl.ds(pl.multiple_of(j * tk, tk), tk)
            kc, vc = k_ref[rows, :], v_ref[rows, :]
            dk_part = jnp.zeros((tk, HEAD_DIM), F32)
            dv_part = jnp.zeros((tk, HEAD_DIM), F32)
            for g in range(grp):
                sl = slice(g * HEAD_DIM, (g + 1) * HEAD_DIM)
                qg, dog = q_ref[:, sl], do_ref[:, sl]
                s = lax.dot_general(qg, kc, (((1,), (1,)), ((), ())), preferred_element_type=F32)
                p = jnp.exp2(s - lse_s[g])
                dp = lax.dot_general(dog, vc, (((1,), (1,)), ((), ())), preferred_element_type=F32)
                ds = (p * (dp - delta_s[g])).astype(BF16)
                dq_acc[g] += jnp.dot(ds, kc, preferred_element_type=F32)
                dv_part = dv_part + lax.dot_general(p.astype(BF16), dog, (((0,), (0,)), ((), ())),
                                                    preferred_element_type=F32)
                dk_part = dk_part + lax.dot_general(ds, qg, (((0,), (0,)), ((), ())), preferred_element_type=F32)
            dv_ref[rows, :] += dv_part
            dk_ref[rows, :] += dk_part
            return carry

        lax.fori_loop(0, t // tk, step, 0)
        for g in range(grp):
            dq_ref[:, g * HEAD_DIM:(g + 1) * HEAD_DIM] = dq_acc[g] * ATTN_SCALE

        @pl.when(i == n_q - 1)
        def _():
            dk_ref[...] = dk_ref[...] * (1.0 / LOG2_E)

    qspec = pl.BlockSpec((tq, gw), lambda h, i: (i, h))
    kspec = pl.BlockSpec((t, HEAD_DIM), lambda h, i: (0, h))
    return pl.pallas_call(
        body, grid=(kvh, n_q),
        in_specs=[qspec, kspec, kspec, qspec, pl.BlockSpec((None, tq, grp), lambda h, i: (h, i, 0)), qspec],
        out_specs=[qspec, kspec, kspec],
        out_shape=[jax.ShapeDtypeStruct((n_x, d_q), F32), jax.ShapeDtypeStruct((t, d_kv), F32),
                   jax.ShapeDtypeStruct((t, d_kv), F32)],
        scratch_shapes=[pltpu.VMEM((grp, tq, HEAD_DIM), F32), pltpu.VMEM((grp, tq, 1), F32),
                        pltpu.VMEM((grp, tq, 1), F32)],
        compiler_params=_params(("parallel", "arbitrary"), 56), name=name,
    )(q, k, v, o, lse, do)


def _whole(body, ins, out_shapes, name):
    return pl.pallas_call(
        body, out_shape=[jax.ShapeDtypeStruct(s, d) for s, d in out_shapes],
        compiler_params=pltpu.CompilerParams(vmem_limit_bytes=40 << 20), name=name)(*ins)


def _silu_rows(x, *, name):
    def body(x_ref, o_ref):
        o_ref[...] = _silu(x_ref[...])
    return _whole(body, [x], [(x.shape, F32)], name)[0]


def _assemble_dmods(gathered, *, name):
    width = gathered.shape[1]

    def body(g_ref, dm0, dm1, db0, db1):
        for l, (dm, db) in enumerate(((dm0, db0), (dm1, db1))):
            ctx = jnp.zeros((1, width), F32)
            tot = jnp.zeros((1, width), F32)
            for q in range(N_DEV):
                row = g_ref[16 * q + 8 * l:16 * q + 8 * l + 1, :]
                dm[q:q + 1, :] = row
                tot = tot + row
                ctx = ctx + g_ref[16 * q + 8 * l + 1:16 * q + 8 * l + 2, :]
            dm[N_DEV:N_DEV + 1, :] = ctx
            dm[N_DEV + 1:, :] = jnp.zeros((16 - N_DEV - 1, width), F32)
            db[...] = jnp.zeros_like(db)
            db[0:1, :] = tot + ctx

    return _whole(body, [gathered], [((16, width), F32), ((16, width), F32), ((8, width), F32), ((8, width), F32)],
                  name)


def _sum_slots(gathered, rows, *, name):
    def body(g_ref, o_ref):
        acc = g_ref[0:rows, :]
        for q in range(1, N_DEV):
            acc = acc + g_ref[q * rows:(q + 1) * rows, :]
        o_ref[...] = acc
    return _whole(body, [gathered], [((rows, gathered.shape[1]), F32)], name)[0]


def _silu_grad(x, dy, *, name):
    def body(x_ref, dy_ref, o_ref):
        _, vjp = jax.vjp(_silu, x_ref[...])
        o_ref[...] = vjp(dy_ref[...])[0]
    return _whole(body, [x, dy], [(x.shape, F32)], name)[0]


def _adamw(w, m, v, *, name, recv=None, grad=None):
    rows, cols = w.shape
    budget = max(8, ADAMW_BLOCK_ELEMS // cols)
    tr = _div(rows, [c for c in (512, 256, 128, 64, 32, 16, 8) if c <= budget] + [rows])
    c1 = 1.0 - ADAM_B1 ** ADAM_STEP
    c2 = 1.0 - ADAM_B2 ** ADAM_STEP

    def body(w_ref, m_ref, v_ref, g_in, g_ref, d_ref, nm_ref, nv_ref):
        if recv is not None:
            g = g_in[0].astype(F32)
            for q in range(1, N_DEV):
                g = g + g_in[q].astype(F32)
        else:
            g = g_in[...]
        nm = ADAM_B1 * m_ref[...] + (1.0 - ADAM_B1) * g
        nv = ADAM_B2 * v_ref[...] + (1.0 - ADAM_B2) * jnp.square(g)
        g_ref[...] = g
        nm_ref[...] = nm
        nv_ref[...] = nv
        d_ref[...] = -ADAM_LR * ((nm / c1) / (jnp.sqrt(nv / c2) + ADAM_EPS) + ADAM_WD * w_ref[...])

    blk = pl.BlockSpec((tr, cols), lambda i: (i, 0))
    g_spec = pl.BlockSpec((N_DEV, tr, cols), lambda i: (0, i, 0)) if recv is not None else blk
    return pl.pallas_call(
        body, grid=(rows // tr,), in_specs=[blk, blk, blk, g_spec], out_specs=[blk] * 4,
        out_shape=[jax.ShapeDtypeStruct((rows, cols), F32)] * 4,
        compiler_params=_params(("parallel",), 48), name=name,
    )(w, m, v, recv if recv is not None else grad)


def _position():
    return tuple(lax.axis_index(a) for a in MESH_AXES)


def _linear(pos):
    return 4 * pos[0] + 2 * pos[1] + pos[2]


def _window(ref, axis, dev, size):
    start = pl.multiple_of(dev * size, size)
    return ref.at[pl.ds(start, size), :] if axis == 0 else ref.at[:, pl.ds(start, size)]


def _all_gather(shards, axes, *, name):
    n = len(shards)
    sizes = [s.shape[ax] for s, ax in zip(shards, axes)]

    def body(*refs):
        src, dst = refs[:n], refs[n:2 * n]
        send_sems, recv_sems, local_sems = refs[2 * n:]
        x, y, c = _position()
        me, sibling = (x, y, c), (x, y, 1 - c)
        chips = [(1 - x, y), (x, 1 - y), (1 - x, 1 - y)]

        def win(k, pos):
            return _window(dst[k], axes[k], _linear(pos), sizes[k])

        def copy(k, sem, block, to, from_src=False):
            return pltpu.make_async_remote_copy(
                src_ref=src[k] if from_src else win(k, block), dst_ref=win(k, block),
                send_sem=send_sems.at[k, sem], recv_sem=recv_sems.at[k, sem],
                device_id=to, device_id_type=MESH_ID)

        mine = [pltpu.make_async_copy(src[k], win(k, me), local_sems.at[k]) for k in range(n)]
        for cp in mine:
            cp.start()
        first = []
        for k in range(n):
            first.append(copy(k, 0, me, sibling, from_src=True))
            first += [copy(k, 1 + j, me, (*chip, c), from_src=True) for j, chip in enumerate(chips)]
        for cp in first:
            cp.start()
        passed = []
        for j, chip in enumerate(chips):
            for k in range(n):
                copy(k, 1 + j, (*chip, c), me).wait_recv()
                fwd = copy(k, 4 + j, (*chip, c), sibling)
                fwd.start()
                passed.append(fwd)
        for k in range(n):
            copy(k, 0, sibling, me).wait_recv()
            for j, chip in enumerate(chips):
                copy(k, 4 + j, (*chip, 1 - c), me).wait_recv()
        for cp in first + passed:
            cp.wait_send()
        for cp in mine:
            cp.wait()

    out_shape = []
    for s, ax in zip(shards, axes):
        full = (s.shape[0] * N_DEV, s.shape[1]) if ax == 0 else (s.shape[0], s.shape[1] * N_DEV)
        out_shape.append(jax.ShapeDtypeStruct(full, s.dtype))
    any_spec = pl.BlockSpec(memory_space=pl.ANY)
    return pl.pallas_call(
        body, in_specs=[any_spec] * n, out_specs=[any_spec] * n, out_shape=out_shape,
        scratch_shapes=[pltpu.SemaphoreType.DMA((n, 7)), pltpu.SemaphoreType.DMA((n, 7)),
                        pltpu.SemaphoreType.DMA((n,))],
        name=name,
    )(*shards)


class Rider:
    def __init__(self, kind, arrays, axes):
        self.kind, self.arrays, self.axes = kind, list(arrays), list(axes)
        self.n = len(self.arrays)
        if kind == "gather":
            self.sizes = [a.shape[ax] for a, ax in zip(self.arrays, self.axes)]
        else:
            self.sizes = [a.shape[ax] // N_DEV for a, ax in zip(self.arrays, self.axes)]

    def out_shape(self):
        shapes = []
        for a, ax, sz in zip(self.arrays, self.axes, self.sizes):
            if self.kind == "gather":
                full = (sz * N_DEV, a.shape[1]) if ax == 0 else (a.shape[0], sz * N_DEV)
                shapes.append(jax.ShapeDtypeStruct(full, a.dtype))
            else:
                shard = (sz, a.shape[1]) if ax == 0 else (a.shape[0], sz)
                shapes.append(jax.ShapeDtypeStruct((N_DEV, *shard), a.dtype))
        return shapes

    def scratch(self):
        return [pltpu.SemaphoreType.DMA((self.n, N_DEV - 1)), pltpu.SemaphoreType.DMA((self.n, N_DEV - 1)),
                pltpu.SemaphoreType.DMA((self.n,))]

    def plan(self, src, dst, send_sems, recv_sems, local_sems):
        x, y, c = me = _position()
        mine = _linear(me)

        def remote(k, sem, src_ref, dst_ref, to):
            return pltpu.make_async_remote_copy(
                src_ref=src_ref, dst_ref=dst_ref, send_sem=send_sems.at[k, sem], recv_sem=recv_sems.at[k, sem],
                device_id=to, device_id_type=MESH_ID)

        ph = dict(local=[], start=[], mid_wait=[], mid_start=[], end_wait=[])
        for k in range(self.n):
            ax, sz = self.axes[k], self.sizes[k]
            if self.kind == "exchange":
                own_src, own_dst = _window(src[k], ax, mine, sz), dst[k].at[mine]
                ph["local"].append(pltpu.make_async_copy(own_src, own_dst, local_sems.at[k]))
                for mask in range(1, N_DEV):
                    to = tuple(1 - p if (mask >> (2 - b)) & 1 else p for b, p in enumerate(me))
                    ph["start"].append(remote(k, mask - 1, _window(src[k], ax, _linear(to), sz), own_dst, to))
                    ph["end_wait"].append(remote(k, mask - 1, own_src, dst[k].at[_linear(to)], to))
            else:
                def win(pos, k=k, ax=ax, sz=sz):
                    return _window(dst[k], ax, _linear(pos), sz)
                sibling = (x, y, 1 - c)
                chips = [(1 - x, y), (x, 1 - y), (1 - x, 1 - y)]
                ph["local"].append(pltpu.make_async_copy(src[k], win(me), local_sems.at[k]))
                ph["start"].append(remote(k, 0, src[k], win(me), sibling))
                ph["end_wait"].append(remote(k, 0, src[k], win(sibling), sibling))
                for j, chip in enumerate(chips):
                    ph["start"].append(remote(k, 1 + j, src[k], win(me), (*chip, c)))
                    ph["mid_wait"].append(remote(k, 1 + j, src[k], win((*chip, c)), (*chip, c)))
                    ph["mid_start"].append(remote(k, 4 + j, win((*chip, c)), win((*chip, c)), sibling))
                    ph["end_wait"].append(remote(k, 4 + j, src[k], win((*chip, 1 - c)), sibling))
        return ph


def _pcall(body, *, grid, in_specs, out_specs, out_shape, scratch_shapes, semantics, vmem_mb, name, operands,
           rider=None):
    if rider is None:
        return pl.pallas_call(body, grid=grid, in_specs=in_specs, out_specs=out_specs, out_shape=out_shape,
                              scratch_shapes=scratch_shapes, compiler_params=_params(semantics, vmem_mb),
                              name=name)(*operands)
    n_in, n_out, n_scr, n = len(in_specs), len(out_specs), len(scratch_shapes), rider.n

    def wrapped(*refs):
        ins, src = refs[:n_in], refs[n_in:n_in + n]
        outs = refs[n_in + n:n_in + n + n_out]
        dst = refs[n_in + n + n_out:n_in + 2 * n + n_out]
        rest = refs[n_in + 2 * n + n_out:]
        scratch, sems = rest[:n_scr], rest[n_scr:]
        step = functools.reduce(lambda acc, ig: acc * ig[1] + ig[0],
                                [(pl.program_id(dim), g) for dim, g in enumerate(grid)], 0)
        n_steps = functools.reduce(lambda a, b: a * b, grid)

        @pl.when(step == 0)
        def _():
            ph = rider.plan(src, dst, *sems)
            for cp in ph["local"] + ph["start"]:
                cp.start()

        body(*ins, *outs, *scratch)

        @pl.when(step == (n_steps * 5) // 8)
        def _():
            ph = rider.plan(src, dst, *sems)
            for cp in ph["mid_wait"]:
                cp.wait_recv()
            for cp in ph["mid_start"]:
                cp.start()

        @pl.when(step == n_steps - 1)
        def _():
            ph = rider.plan(src, dst, *sems)
            for cp in ph["end_wait"]:
                cp.wait_recv()
            for cp in ph["start"] + ph["mid_start"]:
                cp.wait_send()
            for cp in ph["local"]:
                cp.wait()

    any_spec = pl.BlockSpec(memory_space=pl.ANY)
    return pl.pallas_call(
        wrapped, grid=grid, in_specs=list(in_specs) + [any_spec] * n, out_specs=list(out_specs) + [any_spec] * n,
        out_shape=list(out_shape) + rider.out_shape(), scratch_shapes=list(scratch_shapes) + rider.scratch(),
        compiler_params=_params(("arbitrary",) * len(grid), vmem_mb), name=name,
    )(*operands, *rider.arrays)


WEIGHTS = ['c_ctx', 'l0_ada_w', 'l0_ada_b', 'l0_in_w', 'l0_conv_w', 'l0_conv_b', 'l0_conv_ln_g', 'l0_conv_ln_b',
           'l0_pool_w', 'l0_pool_scale', 'l0_out_w', 'l0_mlp_w1', 'l0_mlp_w2', 'l1_ada_w', 'l1_ada_b', 'l1_qkv_w',
           'l1_q_norm_g', 'l1_k_norm_g', 'l1_out_w', 'l1_mlp_w1', 'l1_mlp_w2', 'final_g']
SHARDED = {'l0_in_w': 1, 'l0_out_w': 0, 'l0_mlp_w1': 1, 'l0_mlp_w2': 0,
           'l1_qkv_w': 1, 'l1_out_w': 0, 'l1_mlp_w1': 1, 'l1_mlp_w2': 0}
REPLICATED_SMALL = ['l0_conv_b', 'l0_conv_ln_g', 'l0_conv_ln_b', 'l0_pool_scale', 'l1_q_norm_g', 'l1_k_norm_g',
                    'final_g']


def _row8(v):
    v = v.reshape(1, -1)
    return jnp.pad(v, ((0, 7), (0, 0)))


def _mods16(full, me):
    d = full.shape[1] // 6
    mine = lax.dynamic_slice_in_dim(full, me, 1, axis=0).reshape(6, d)
    ctx = full[N_DEV].reshape(6, d)
    return jnp.pad(jnp.stack([mine, ctx], axis=1).reshape(12, d), ((0, 4), (0, 0)))


def _mlp_fwd(xs, mods, w1, w2, *, n_x, tm, tag, rider1=None, rider2=None):
    t, d = xs.shape
    dff = w1.shape[1]
    h = _rms_mod_fwd(xs, mods, k_shift=3, n_x=n_x, name=f"{tag}_norm2")
    pre, act, *ride1 = _matmul(h, w1, mode="nn", tm=tm, tn=_div(dff, (1024, 512)), tk=d, out_dtypes=[BF16, BF16],
                               name=f"{tag}_mlp1", rider=rider1,
                               epilogue=lambda acc, rows: (acc, jnp.square(jnp.maximum(acc, 0.0))))
    if w2 is None:
        w2 = ride1[-1]
    xo, branch, *ride2 = _matmul(
        act, w2, mode="nn", tm=_div(t, DEEP_TM), tn=_div(d, DEEP_TN), tk=dff, out_dtypes=[F32, BF16],
        name=f"{tag}_mlp2", extras=[("tile", xs), ("vec", mods)], rider=rider2,
        epilogue=lambda acc, rows, res, mv: (res + _seg_pick(mv, 5, rows, n_x) * acc, acc))
    return xo, dict(h=h, pre=pre, act=act, branch=branch, x_in=xs), ride1, ride2


def _exchange_of(items):
    return Rider("exchange", [a for a, _ in items], [ax for _, ax in items]) if items else None


def _mlp_bwd(dxo, dbranch, saved, mods, w1, w2, mixer_branch, *, n_x, tm, tag, ride_dx2=(), ride_dw2=()):
    t, d = dxo.shape
    dff = w1.shape[1]
    tkt = _div(t, TOKEN_TK)
    dpre, *recv_a = _matmul(dbranch, w2, mode="nt", tm=tm, tn=_div(dff, (1024, 512)), tk=d, out_dtypes=[BF16],
                            name=f"{tag}_mlp2_dx", extras=[("tile", saved["pre"])],
                            rider=_exchange_of(list(ride_dx2)),
                            epilogue=lambda acc, rows, pre: (acc * 2.0 * jnp.maximum(pre.astype(F32), 0.0),))
    dw2, *recv_b = _matmul(saved["act"], dbranch, mode="tn", tm=_div(dff, (1024, 512)), tn=_div(d, (1024, 512)),
                           tk=tkt, out_dtypes=[BF16], name=f"{tag}_mlp2_dw", rider=_exchange_of(list(ride_dw2)))
    dh, = _matmul(dpre, w1, mode="nt", tm=_div(t, DEEP_TM), tn=_div(d, DEEP_TN), tk=dff,
                  out_dtypes=[F32], name=f"{tag}_mlp1_dx")
    dw1, recv_dw2 = _matmul(saved["h"], dpre, mode="tn", tm=_div(d, (1024, 512)), tn=_div(dff, (1024, 512)),
                            tk=tkt, out_dtypes=[BF16], name=f"{tag}_mlp1_dw", rider=_exchange_of([(dw2, 0)]))
    dx, dm_norm, dmix, dm_gate = _rms_mod_bwd(saved["x_in"], mods, dh, dxo, k_shift=3, n_x=n_x,
                                              name=f"{tag}_norm2_bwd", gate=(mixer_branch, mods, 2))
    return dx, dw1, dm_norm + dm_gate, dmix, recv_a, recv_b, recv_dw2


def kernel(x, c, ctx, c_ctx, l0_ada_w, l0_ada_b, l0_in_w, l0_conv_w, l0_conv_b, l0_conv_ln_g, l0_conv_ln_b, l0_pool_w, l0_pool_scale, l0_out_w, l0_mlp_w1, l0_mlp_w2, l1_ada_w, l1_ada_b, l1_qkv_w, l1_q_norm_g, l1_k_norm_g, l1_out_w, l1_mlp_w1, l1_mlp_w2, final_g, loss_target, m_c_ctx, m_l0_ada_w, m_l0_ada_b, m_l0_in_w, m_l0_conv_w, m_l0_conv_b, m_l0_conv_ln_g, m_l0_conv_ln_b, m_l0_pool_w, m_l0_pool_scale, m_l0_out_w, m_l0_mlp_w1, m_l0_mlp_w2, m_l1_ada_w, m_l1_ada_b, m_l1_qkv_w, m_l1_q_norm_g, m_l1_k_norm_g, m_l1_out_w, m_l1_mlp_w1, m_l1_mlp_w2, m_final_g, v_c_ctx, v_l0_ada_w, v_l0_ada_b, v_l0_in_w, v_l0_conv_w, v_l0_conv_b, v_l0_conv_ln_g, v_l0_conv_ln_b, v_l0_pool_w, v_l0_pool_scale, v_l0_out_w, v_l0_mlp_w1, v_l0_mlp_w2, v_l1_ada_w, v_l1_ada_b, v_l1_qkv_w, v_l1_q_norm_g, v_l1_k_norm_g, v_l1_out_w, v_l1_mlp_w1, v_l1_mlp_w2, v_final_g):
    p = dict(locals())
    me = _linear(_position())
    n_x, d = x.shape[1], x.shape[2]
    n_ctx = ctx.shape[1]
    t = n_x + n_ctx
    dc = l0_conv_b.shape[0]
    n_tap = l0_conv_w.shape[0]
    d_q = d
    n_mod = l0_ada_b.shape[0] // d
    ada_cols = l0_ada_w.shape[1]
    tm_t = _div(t, (768, 640, 512, 128))
    tm_x = _div(n_x, (1024, 512))

    names = list(SHARDED)
    shard16 = {nm: p[nm].astype(BF16) for nm in names}

    def gather_of(*nms):
        return Rider("gather", [shard16[nm] for nm in nms], [SHARDED[nm] for nm in nms])

    wfull = {}
    first = ['l0_in_w', 'l0_out_w']
    *full, conv_w_full, pool_w_full = _all_gather(
        [shard16[nm] for nm in first] + [jnp.pad(l0_conv_w, ((0, 1), (0, 0))),
                                         l0_pool_w.reshape(-1, l0_pool_w.shape[2])],
        [SHARDED[nm] for nm in first] + [1, 0], name="gather_first_weights")
    wfull.update(zip(first, full))
    n_grp, pg = l0_pool_w.shape[0], l0_pool_w.shape[2]
    pool_w_full = pool_w_full.reshape(N_DEV, n_grp, pg // N_DEV, pg).transpose(1, 0, 2, 3).reshape(n_grp, pg, pg)

    c_all = _all_gather([_row8(c)], [0], name="gather_cond")[0].reshape(N_DEV, 8, d)[:, 0]
    cond = jnp.concatenate([c_all, c_ctx.reshape(1, d), jnp.zeros((16 - N_DEV - 1, d), F32)], axis=0)
    s16 = _silu_rows(cond, name="cond_silu")
    mod_shards = []
    for li, (lw, lb) in enumerate(((l0_ada_w, l0_ada_b), (l1_ada_w, l1_ada_b))):
        bias = _row8(lax.dynamic_slice_in_dim(lb, me * ada_cols, ada_cols))
        mod_shards.append(_matmul(s16, lw, mode="nn", tm=16, tn=_div(ada_cols, (512, 384, 256, 128)), tk=d,
                                  out_dtypes=[F32], name=f"l{li}_ada_fwd", extras=[("vec", bias)],
                                  epilogue=lambda acc, rows, b: (acc + b[0:1],))[0])
    mods_full = _all_gather([jnp.concatenate(mod_shards, axis=0)], [1], name="gather_mods")[0]
    mods0, mods1 = _mods16(mods_full[:16], me), _mods16(mods_full[16:], me)

    xs0 = jnp.concatenate([x[0], ctx[0]], axis=0)
    h0 = _rms_mod_fwd(xs0, mods0, k_shift=0, n_x=n_x, name="l0_norm1")
    z, wfull['l0_mlp_w1'] = _matmul(h0, wfull['l0_in_w'], mode="nn", tm=tm_t, tn=_div(3 * dc, (1024, 768, 512, 384)),
                                    tk=d, out_dtypes=[F32], name="l0_in_proj", rider=gather_of('l0_mlp_w1'))
    y0, cv, dsave, wfull['l0_mlp_w2'] = _mixer0_fwd(
        z, conv_w_full, _row8(l0_conv_b), _row8(l0_conv_ln_g), _row8(l0_conv_ln_b), pool_w_full,
        _row8(l0_pool_scale), n_tap=n_tap, n_x=n_x, name="l0_mixer", rider=gather_of('l0_mlp_w2'))
    xs1, mix0 = _matmul(y0, wfull['l0_out_w'], mode="nn", tm=tm_t, tn=_div(d, (1024, 512)), tk=2 * dc,
                        out_dtypes=[F32, BF16], name="l0_out_proj", extras=[("tile", xs0), ("vec", mods0)],
                        epilogue=lambda acc, rows, res, mv: (res + _seg_pick(mv, 2, rows, n_x) * acc, acc))
    xs2, mlp0, ride1, ride2 = _mlp_fwd(
        xs1, mods0, wfull['l0_mlp_w1'], wfull['l0_mlp_w2'], n_x=n_x, tm=tm_t, tag="l0",
        rider1=gather_of('l1_qkv_w', 'l1_out_w'), rider2=gather_of('l1_mlp_w1'))
    wfull['l1_qkv_w'], wfull['l1_out_w'] = ride1
    wfull['l1_mlp_w1'], = ride2

    h2 = _rms_mod_fwd(xs2, mods1, k_shift=0, n_x=n_x, name="l1_norm1")
    qkv, = _matmul(h2, wfull['l1_qkv_w'], mode="nn", tm=tm_t, tn=_div(l1_qkv_w.shape[1] * N_DEV, (1024, 768, 512)),
                   tk=d, out_dtypes=[F32], name="l1_qkv_proj")
    cos, sin = _rope_tables(n_x, n_ctx)
    gq, gk = _row8(l1_q_norm_g), _row8(l1_k_norm_g)
    q, k, v = _qk_prep_fwd(qkv, gq, gk, cos, sin, d_q=d_q, name="l1_qk_prep")
    o, lse = _attention_fwd(q, k, v, n_x=n_x, name="l1_attention")
    x3, mix1 = _matmul(o, wfull['l1_out_w'], mode="nn", tm=tm_x, tn=_div(d, (1024, 512)), tk=d_q,
                       out_dtypes=[F32, BF16], name="l1_out_proj", extras=[("tile", xs2), ("vec", mods1)],
                       epilogue=lambda acc, rows, res, mv: (res + mv[4:5] * acc, acc))
    x4, mlp1, (wfull['l1_mlp_w2'],), _ = _mlp_fwd(x3, mods1, wfull['l1_mlp_w1'], None, n_x=n_x, tm=tm_x, tag="l1",
                                                  rider1=gather_of('l1_mlp_w2'))

    dx4, loss_part, dfinal_g, dbranch1, dmods1 = _final_loss(
        x4, _row8(final_g), loss_target[0], mlp1["branch"], mods1, k_gate=5, name="loss_head")
    loss = lax.psum(loss_part[0, 0], MESH_AXES)

    recv = {}
    dx3, dw1_1, dm, dmix1, _, _, recv['l1_mlp_w2'] = _mlp_bwd(
        dx4, dbranch1, mlp1, mods1, wfull['l1_mlp_w1'], wfull['l1_mlp_w2'], mix1, n_x=n_x, tm=tm_x, tag="l1")
    dmods1 = dmods1 + dm
    do, = _matmul(dmix1, wfull['l1_out_w'], mode="nt", tm=tm_x, tn=_div(d_q, (1024, 512)), tk=d,
                  out_dtypes=[BF16], name="l1_out_dx")
    dw_out1, = _matmul(o, dmix1, mode="tn", tm=_div(d_q, (1024, 512)), tn=_div(d, (1024, 512)),
                       tk=_div(n_x, TOKEN_TK), out_dtypes=[BF16], name="l1_out_dw")
    dq, dk, dv = _attention_bwd(q, k, v, o, lse, do, n_x=n_x, name="l1_attention_bwd")
    dqkv, dgq, dgk = _qk_prep_bwd(qkv, dq, dk, dv, gq, gk, cos, sin, n_x=n_x, name="l1_qk_prep_bwd")
    tkt = _div(t, TOKEN_TK)
    dh2, recv['l1_out_w'] = _matmul(dqkv, wfull['l1_qkv_w'], mode="nt", tm=tm_t, tn=_div(d, (1024, 512)),
                                    tk=dqkv.shape[1], out_dtypes=[F32], name="l1_qkv_dx",
                                    rider=_exchange_of([(dw_out1, SHARDED['l1_out_w'])]))
    dw_qkv, = _matmul(h2, dqkv, mode="tn", tm=_div(d, (1024, 512)), tn=_div(dqkv.shape[1], (1024, 768, 512)),
                      tk=tkt, out_dtypes=[BF16], name="l1_qkv_dw")
    dxs2, dm, dbranch0, dmods0 = _rms_mod_bwd(xs2, mods1, dh2, dx3, k_shift=0, n_x=n_x, name="l1_norm1_bwd",
                                              dres_rows=n_x, gate=(mlp0["branch"], mods0, 5))
    dmods1 = dmods1 + dm

    dxs1, dw1_0, dm, dmix0, (recv['l1_qkv_w'],), (recv['l1_mlp_w1'],), recv['l0_mlp_w2'] = _mlp_bwd(
        dxs2, dbranch0, mlp0, mods0, wfull['l0_mlp_w1'], wfull['l0_mlp_w2'], mix0, n_x=n_x, tm=tm_t, tag="l0",
        ride_dx2=[(dw_qkv, SHARDED['l1_qkv_w'])], ride_dw2=[(dw1_1, SHARDED['l1_mlp_w1'])])
    dmods0 = dmods0 + dm
    dy0, = _matmul(dmix0, wfull['l0_out_w'], mode="nt", tm=tm_t, tn=_div(2 * dc, (1024, 512)), tk=d,
                   out_dtypes=[F32], name="l0_out_dx")
    dw_out0, = _matmul(y0, dmix0, mode="tn", tm=_div(2 * dc, (1024, 512)), tn=_div(d, (1024, 512)),
                       tk=tkt, out_dtypes=[BF16], name="l0_out_dw")
    dz, dconv_w, dconv_b, dln_g, dln_b, dpool_w, dpool_scale, recv['l0_mlp_w1'] = _mixer0_bwd(
        dy0, z, cv, dsave, conv_w_full, _row8(l0_conv_ln_g), _row8(l0_conv_ln_b), pool_w_full,
        _row8(l0_pool_scale), n_tap=n_tap, n_x=n_x, name="l0_mixer_bwd",
        rider=Rider("exchange", [dw1_0], [SHARDED['l0_mlp_w1']]))
    dw_in0, recv['l0_out_w'] = _matmul(h0, dz, mode="tn", tm=_div(d, (1024, 512)),
                                       tn=_div(3 * dc, (1024, 768, 512, 384)), tk=tkt, out_dtypes=[BF16],
                                       name="l0_in_dw", rider=_exchange_of([(dw_out0, SHARDED['l0_out_w'])]))
    dh0, recv['l0_in_w'] = _matmul(dz, wfull['l0_in_w'], mode="nt", tm=tm_t, tn=_div(d, (1024, 512)), tk=3 * dc,
                                   out_dtypes=[F32], name="l0_in_dx",
                                   rider=_exchange_of([(dw_in0, SHARDED['l0_in_w'])]))
    dx0, dm = _rms_mod_bwd(xs0, mods0, dh0, dxs1, k_shift=0, n_x=n_x, name="l0_norm1_bwd", out_rows=n_x)
    dmods0 = dmods0 + dm
    grad_x = dx0[None]

    def dmod_rows(dm16):
        rows = dm16[:2 * n_mod].reshape(n_mod, 2, d).transpose(1, 0, 2).reshape(2, n_mod * d)
        return jnp.pad(rows, ((0, 6), (0, 0)))
    dm_gathered = _all_gather([jnp.concatenate([dmod_rows(dmods0), dmod_rows(dmods1)], axis=0)], [0],
                              name="gather_dmods")[0]
    dm0, dm1, db0, db1 = _assemble_dmods(dm_gathered, name="assemble_dmods")
    out_g = {'l0_ada_b': db0[0], 'l1_ada_b': db1[0]}
    ds_part = jnp.zeros((16, d), F32)
    for nm, lw, dmf in (('l0_ada_w', l0_ada_w, dm0), ('l1_ada_w', l1_ada_w, dm1)):
        dm_cols = lax.dynamic_slice_in_dim(dmf, me * ada_cols, ada_cols, axis=1)
        out_g[nm], = _matmul(s16, dm_cols, mode="tn", tm=_div(d, (1024, 512)),
                             tn=_div(ada_cols, (512, 384, 256, 128)), tk=16, out_dtypes=[F32], name=f"{nm}_dw")
        ds_part = ds_part + _matmul(dm_cols, lw, mode="nt", tm=16, tn=_div(d, (1024, 512)),
                                    tk=_div(ada_cols, (512, 384, 256, 128)), out_dtypes=[F32], name=f"{nm}_dx")[0]

    small = {'l0_conv_b': dconv_b[0], 'l0_conv_ln_g': dln_g[0], 'l0_conv_ln_b': dln_b[0],
             'l0_pool_scale': dpool_scale[0], 'l1_q_norm_g': dgq[0], 'l1_k_norm_g': dgk[0],
             'final_g': dfinal_g[0], 'dsilu_ctx': ds_part[N_DEV], 'l0_conv_w': dconv_w[:-1].reshape(-1),
             'l0_pool_w': dpool_w.reshape(-1)}
    flat = jnp.concatenate([small[nm] for nm in small])
    rows = -(-flat.shape[0] // 1024) * 8
    packed = jnp.pad(flat, (0, rows * 128 - flat.shape[0])).reshape(rows, 128)
    summed = _sum_slots(_all_gather([packed], [0], name="gather_small_grads")[0], rows,
                        name="sum_small_grads").reshape(-1)
    off = 0
    for nm in small:
        size = small[nm].shape[0]
        small[nm] = summed[off:off + size]
        off += size
    out_g['c_ctx'] = _silu_grad(_row8(c_ctx), _row8(small['dsilu_ctx']), name="c_ctx_grad")[0]
    for nm in REPLICATED_SMALL:
        out_g[nm] = small[nm]
    conv_cols = l0_conv_w.shape[1]
    out_g['l0_conv_w'] = lax.dynamic_slice_in_dim(small['l0_conv_w'].reshape(n_tap, dc), me * conv_cols, conv_cols,
                                                  axis=1)
    out_g['l0_pool_w'] = lax.dynamic_slice_in_dim(small['l0_pool_w'].reshape(n_grp, pg, pg), me * (pg // N_DEV),
                                                  pg // N_DEV, axis=1)

    delta, new_m, new_v = {}, {}, {}
    for nm in names:
        out_g[nm], delta[nm], new_m[nm], new_v[nm] = _adamw(p[nm], p['m_' + nm], p['v_' + nm], recv=recv[nm],
                                                            name=f"adamw_{nm}")
    for nm in ('l0_ada_w', 'l1_ada_w'):
        out_g[nm], delta[nm], new_m[nm], new_v[nm] = _adamw(p[nm], p['m_' + nm], p['v_' + nm], grad=out_g[nm],
                                                            name=f"adamw_{nm}")
    for nm in WEIGHTS:
        if nm in delta:
            continue
        shape = p[nm].shape
        as2d = lambda a: a.reshape(1, -1) if a.ndim == 1 else a.reshape(-1, a.shape[-1])
        res = _adamw(as2d(p[nm]), as2d(p['m_' + nm]), as2d(p['v_' + nm]), grad=as2d(out_g[nm]), name=f"adamw_{nm}")
        out_g[nm], delta[nm], new_m[nm], new_v[nm] = [r.reshape(shape) for r in res]

    return (loss, grad_x, *[out_g[nm] for nm in WEIGHTS], *[delta[nm] for nm in WEIGHTS],
            *[new_m[nm] for nm in WEIGHTS], *[new_v[nm] for nm in WEIGHTS])
```

```python
import functools

import jax
import jax.numpy as jnp
from jax import lax
from jax.experimental import pallas as pl
from jax.experimental.pallas import tpu as pltpu

F32 = jnp.float32
BF16 = jnp.bfloat16
N_DEV = 8
MESH_AXES = ("x", "y", "c")
EPS = 1e-6
HEAD_DIM = 128
POOL_WINDOWS = (2, 4, 8, 16)
GRID_W = 64
ROPE_THETA = 10000.0
ATTN_SCALE = HEAD_DIM ** -0.5
LOG2_E = 1.4426950408889634
Q_SCALE_LOG2 = ATTN_SCALE * LOG2_E
HALO = 16
ADAM_LR, ADAM_B1, ADAM_B2, ADAM_EPS, ADAM_WD, ADAM_STEP = 0.001, 0.9, 0.999, 1e-08, 0.01, 10
VMEM_CAP_MB = 60
ADAMW_BLOCK_ELEMS = 1 << 18
TOKEN_TK = (2816, 2048, 1408, 1024, 768, 640, 512, 128)
DEEP_TM = (384, 512, 256, 128)
DEEP_TN = (512,)
MESH_ID = pl.DeviceIdType.MESH


def _div(n, prefs):
    for p in prefs:
        if n % p == 0:
            return p
    raise ValueError(f"no tile for {n} in {prefs}")


def _params(sem, vmem_mb):
    return pltpu.CompilerParams(dimension_semantics=sem, vmem_limit_bytes=min(vmem_mb, VMEM_CAP_MB) << 20)


def _sigmoid(x):
    return 1.0 / (1.0 + jnp.exp(-x))


def _silu(x):
    return x * _sigmoid(x)


def _rms(x):
    return x * lax.rsqrt(jnp.mean(x * x, axis=-1, keepdims=True) + EPS)


def _rms_mod(x, shift, scale):
    return _rms(x) * (1.0 + scale) + shift


def _layernorm(x, g, b):
    mu = jnp.mean(x, axis=-1, keepdims=True)
    var = jnp.mean(jnp.square(x - mu), axis=-1, keepdims=True)
    return (x - mu) * lax.rsqrt(var + EPS) * g + b


def _ln_silu(x, g, b):
    return _silu(_layernorm(x, g, b))


def _matmul(a, b, *, mode, tm, tn, tk, out_dtypes, name, extras=(), epilogue=None, rider=None):
    if mode == "tn":
        kdim, m = a.shape
        n = b.shape[1]
    else:
        m, kdim = a.shape
        n = b.shape[0] if mode == "nt" else b.shape[1]
    assert m % tm == 0 and n % tn == 0 and kdim % tk == 0, (name, m, n, kdim, tm, tn, tk)
    nk = kdim // tk
    n_ex = len(extras)
    n_out = len(out_dtypes)

    def body(a_ref, b_ref, *rest):
        ex = rest[:n_ex]
        outs = rest[n_ex:n_ex + n_out]
        acc_ref = rest[n_ex + n_out] if nk > 1 else None
        k = pl.program_id(2)
        av = a_ref[...].astype(BF16)
        bv = b_ref[...].astype(BF16)
        dims = {"nn": ((1,), (0,)), "nt": ((1,), (1,)), "tn": ((0,), (0,))}[mode]
        part = lax.dot_general(av, bv, (dims, ((), ())), preferred_element_type=F32)

        rows = pl.program_id(0) * tm + lax.broadcasted_iota(jnp.int32, (tm, 1), 0)

        def finish(acc):
            res = (acc,) if epilogue is None else epilogue(acc, rows, *[e[...] for e in ex])
            for o, r in zip(outs, res):
                o[...] = r.astype(o.dtype)

        if nk == 1:
            finish(part)
        else:
            @pl.when(k == 0)
            def _():
                acc_ref[...] = part

            @pl.when(k > 0)
            def _():
                acc_ref[...] += part

            @pl.when(k == nk - 1)
            def _():
                finish(acc_ref[...])

    if mode == "tn":
        a_spec = pl.BlockSpec((tk, tm), lambda i, j, k: (k, i))
        b_spec = pl.BlockSpec((tk, tn), lambda i, j, k: (k, j))
    else:
        a_spec = pl.BlockSpec((tm, tk), lambda i, j, k: (i, k))
        b_spec = (pl.BlockSpec((tn, tk), lambda i, j, k: (j, k)) if mode == "nt"
                  else pl.BlockSpec((tk, tn), lambda i, j, k: (k, j)))
    ex_specs, ex_arrays, ex_bytes = [], [], 0
    for kind, arr in extras:
        ex_arrays.append(arr)
        if kind == "tile":
            ex_specs.append(pl.BlockSpec((tm, tn), lambda i, j, k: (i, j)))
            ex_bytes += tm * tn * arr.dtype.itemsize
        else:
            ex_specs.append(pl.BlockSpec((arr.shape[0], tn), lambda i, j, k: (0, j)))
            ex_bytes += arr.shape[0] * tn * 4
    blocks = tm * tk * a.dtype.itemsize + tk * tn * b.dtype.itemsize + ex_bytes
    blocks += sum(tm * tn * jnp.dtype(d).itemsize for d in out_dtypes)
    casts = sum(rows * cols * 2 for arr, rows, cols in ((a, tm, tk), (b, tk, tn)) if arr.dtype != BF16)
    vmem = (2 * blocks + 4 * tm * tn * 4 + casts) // (1 << 20) + 8
    return _pcall(
        body,
        grid=(m // tm, n // tn, nk),
        in_specs=[a_spec, b_spec] + ex_specs,
        out_specs=[pl.BlockSpec((tm, tn), lambda i, j, k: (i, j)) for _ in out_dtypes],
        out_shape=[jax.ShapeDtypeStruct((m, n), d) for d in out_dtypes],
        scratch_shapes=[pltpu.VMEM((tm, tn), F32)] if nk > 1 else [],
        semantics=("parallel", "parallel", "arbitrary"), vmem_mb=vmem, name=name,
        operands=[a, b, *ex_arrays], rider=rider)


def _seg_pick(vec, k, rows, n_x):
    return jnp.where(rows < n_x, vec[2 * k:2 * k + 1], vec[2 * k + 1:2 * k + 2])


def _zero_accs(i, accs):
    @pl.when(i == 0)
    def _():
        for a in accs:
            a[...] = jnp.zeros_like(a)


def _rms_mod_fwd(xs, mods, *, k_shift, n_x, name):
    t, d = xs.shape
    tm = _div(t, (256, 128))

    def body(x_ref, mods_ref, h_ref):
        seg = (pl.program_id(0) * tm >= n_x).astype(jnp.int32)
        shift = mods_ref[pl.ds(2 * k_shift + seg, 1), :]
        scale = mods_ref[pl.ds(2 * k_shift + 2 + seg, 1), :]
        h_ref[...] = _rms_mod(x_ref[...], shift, scale).astype(BF16)

    return pl.pallas_call(
        body, grid=(t // tm,),
        in_specs=[pl.BlockSpec((tm, d), lambda i: (i, 0)), pl.BlockSpec((16, d), lambda i: (0, 0))],
        out_specs=pl.BlockSpec((tm, d), lambda i: (i, 0)),
        out_shape=jax.ShapeDtypeStruct((t, d), BF16),
        compiler_params=_params(("parallel",), 32), name=name,
    )(xs, mods)


def _gate_part(dxv, seg, k_gate, br_ref, gmods_ref, db_ref, dgm_ref):
    r_gate = 2 * k_gate + seg
    db_ref[...] = (dxv * gmods_ref[pl.ds(r_gate, 1), :]).astype(BF16)
    dgm_ref[pl.ds(r_gate, 1), :] += jnp.sum(dxv * br_ref[...].astype(F32), axis=0, keepdims=True)


def _rms_mod_bwd(xs, mods, dh, dres, *, k_shift, n_x, name, rider=None, gate=None, dres_rows=None, out_rows=None):
    t, d = xs.shape
    tm = _div(t, (256, 128))
    n_gate = 2 if gate is not None else 0

    def body(x_ref, mods_ref, dh_ref, dres_ref, *rest):
        gate_in, (dx_ref, dmods_ref), gate_out = rest[:n_gate], rest[n_gate:n_gate + 2], rest[n_gate + 2:]
        i = pl.program_id(0)
        _zero_accs(i, [dmods_ref, *gate_out[1:]])
        seg = (i * tm >= n_x).astype(jnp.int32)
        r_shift = 2 * k_shift + seg
        r_scale = 2 * k_shift + 2 + seg
        shift = mods_ref[pl.ds(r_shift, 1), :]
        scale = mods_ref[pl.ds(r_scale, 1), :]
        _, vjp = jax.vjp(_rms_mod, x_ref[...], shift, scale)
        dx, dshift, dscale = vjp(dh_ref[...].astype(F32))
        dres_v = dres_ref[...]
        if dres_rows is not None:
            dres_v = dres_v * jnp.where(i * tm < dres_rows, 1.0, 0.0)
        dx = dres_v + dx
        if out_rows is None:
            dx_ref[...] = dx
        else:
            @pl.when(i * tm < out_rows)
            def _():
                dx_ref[...] = dx
        dmods_ref[pl.ds(r_shift, 1), :] += dshift
        dmods_ref[pl.ds(r_scale, 1), :] += dscale
        if gate is not None:
            _gate_part(dx, seg, gate[2], *gate_in, *gate_out)

    def clamped(rows):
        return pl.BlockSpec((tm, d), lambda i: (jnp.minimum(i, rows // tm - 1), 0))

    row = pl.BlockSpec((tm, d), lambda i: (i, 0))
    vec = pl.BlockSpec((16, d), lambda i: (0, 0))
    in_specs = [row, vec, row, row if dres_rows is None else clamped(dres_rows)]
    out_specs = [row if out_rows is None else clamped(out_rows), vec]
    out_shape = [jax.ShapeDtypeStruct((t if out_rows is None else out_rows, d), F32),
                 jax.ShapeDtypeStruct((16, d), F32)]
    operands = [xs, mods, dh, dres]
    if gate is not None:
        in_specs += [row, vec]
        out_specs += [row, vec]
        out_shape += [jax.ShapeDtypeStruct((t, d), BF16), jax.ShapeDtypeStruct((16, d), F32)]
        operands += [gate[0], gate[1]]
    return _pcall(body, grid=(t // tm,), in_specs=in_specs, out_specs=out_specs, out_shape=out_shape,
                  scratch_shapes=[], semantics=("arbitrary",), vmem_mb=48, name=name, operands=operands, rider=rider)


def _final_loss(xs, g, target, branch, gmods, *, k_gate, name):
    t, d = xs.shape
    tm = _div(t, (256, 128))

    def loss_fn(x, gv, tgt):
        err = _rms(x) * gv - tgt
        return 0.5 * jnp.sum(jnp.mean(jnp.square(err), axis=-1))

    def body(x_ref, g_ref, t_ref, br_ref, gmods_ref, dx_ref, loss_ref, dg_ref, db_ref, dgm_ref):
        i = pl.program_id(0)
        _zero_accs(i, [loss_ref, dg_ref, dgm_ref])
        val, vjp = jax.vjp(loss_fn, x_ref[...], g_ref[0:1, :], t_ref[...])
        dx, dg, _ = vjp(jnp.ones((), F32))
        dx_ref[...] = dx
        loss_ref[...] += val
        dg_ref[0:1, :] += dg
        _gate_part(dx, 0, k_gate, br_ref, gmods_ref, db_ref, dgm_ref)

    row = pl.BlockSpec((tm, d), lambda i: (i, 0))
    vec16 = pl.BlockSpec((16, d), lambda i: (0, 0))
    return pl.pallas_call(
        body, grid=(t // tm,),
        in_specs=[row, pl.BlockSpec((8, d), lambda i: (0, 0)), row, row, vec16],
        out_specs=[row, pl.BlockSpec((8, 128), lambda i: (0, 0)), pl.BlockSpec((8, d), lambda i: (0, 0)), row, vec16],
        out_shape=[jax.ShapeDtypeStruct((t, d), F32), jax.ShapeDtypeStruct((8, 128), F32),
                   jax.ShapeDtypeStruct((8, d), F32), jax.ShapeDtypeStruct((t, d), BF16),
                   jax.ShapeDtypeStruct((16, d), F32)],
        compiler_params=_params(("arbitrary",), 48), name=name,
    )(xs, g, target, branch, gmods)


def _halo_specs(r, width, col, t):
    h_per = r // HALO

    def prev(i):
        return (jnp.maximum(i * h_per - 1, 0), col)

    def nxt(i):
        return (jnp.minimum((i + 1) * h_per, t // HALO - 1), col)

    return (pl.BlockSpec((HALO, width), prev), pl.BlockSpec((r, width), lambda i: (i, col)),
            pl.BlockSpec((HALO, width), nxt))


def _seg_geometry(i, r, n_x, t):
    row0 = i * r
    in_ctx = row0 >= n_x
    first = jnp.logical_or(row0 == 0, row0 == n_x)
    last = jnp.logical_or(row0 + r == n_x, row0 + r == t)
    seg_start = jnp.where(in_ctx, n_x, 0)
    seg_len = jnp.where(in_ctx, t - n_x, n_x)
    return row0, first, last, seg_start, seg_len


ROW_CHUNK = 64
LANE_CHUNK = 128


def _chunks(width, rows, col0=0):
    lanes = min(LANE_CHUNK, width)
    return [(slice(col0 + c, col0 + c + lanes), r0) for c in range(0, width, lanes) for r0 in range(0, rows, ROW_CHUNK)]


def _pool_count(tpos, w, seg_len):
    return (jnp.minimum(tpos + w // 2, seg_len) - jnp.maximum(tpos - w // 2, 0)).astype(F32)


def _mixer0_fwd(z, conv_w, conv_b, ln_g, ln_b, pool_w, pool_scale, *, n_tap, n_x, name, rider=None):
    t, dc = z.shape[0], z.shape[1] // 3
    n_grp, pg = pool_w.shape[0], pool_w.shape[1]
    r = _div(t - n_x, (256, 128))
    assert n_x % r == 0 and pg * n_grp == dc
    half = n_tap // 2
    assert half < HALO and max(POOL_WINDOWS) // 2 <= HALO

    def body(ap, ac, an, gp, gc, gn, pp, pc, pn, w_ref, cb_ref, lg_ref, lb_ref, pw_ref, ps_ref,
             y_ref, cv_ref, d_ref, uwin, pwin):
        i = pl.program_id(0)
        row0, first, last, seg_start, seg_len = _seg_geometry(i, r, n_x, t)
        keep_prev = jnp.where(first, 0.0, 1.0)
        keep_next = jnp.where(last, 0.0, 1.0)
        uwin[0:HALO, :] = ap[...] * _sigmoid(gp[...]) * keep_prev
        uwin[HALO:HALO + r, :] = ac[...] * _sigmoid(gc[...])
        uwin[HALO + r:, :] = an[...] * _sigmoid(gn[...]) * keep_next
        pwin[0:HALO, :] = pp[...] * keep_prev
        pwin[HALO:HALO + r, :] = pc[...]
        pwin[HALO + r:, :] = pn[...] * keep_next
        for cols, r0 in _chunks(dc, r):
            acc = jnp.zeros((ROW_CHUNK, cols.stop - cols.start), F32) + cb_ref[0:1, cols]
            for k in range(n_tap):
                off = HALO - half + k + r0
                acc = acc + w_ref[k:k + 1, cols] * uwin[off:off + ROW_CHUNK, cols]
            cv_ref[r0:r0 + ROW_CHUNK, cols] = acc
        y_ref[:, 0:dc] = _ln_silu(cv_ref[...], lg_ref[0:1, :], lb_ref[0:1, :]).astype(BF16)
        tpos = row0 - seg_start + lax.broadcasted_iota(jnp.int32, (r, 1), 0)
        for g, w in enumerate(POOL_WINDOWS):
            cnt = _pool_count(tpos, w, seg_len)
            for cols, r0 in _chunks(pg, r, g * pg):
                s = jnp.zeros((ROW_CHUNK, cols.stop - cols.start), F32)
                for j in range(-(w // 2), w // 2):
                    s = s + pwin[HALO + j + r0:HALO + j + r0 + ROW_CHUNK, cols]
                diff = s / cnt[r0:r0 + ROW_CHUNK] - pwin[HALO + r0:HALO + r0 + ROW_CHUNK, cols]
                d_ref[r0:r0 + ROW_CHUNK, cols] = diff.astype(BF16)
            cols = slice(g * pg, (g + 1) * pg)
            pm = jnp.dot(d_ref[:, cols], pw_ref[g].astype(BF16), preferred_element_type=F32)
            y_ref[:, dc + g * pg:dc + (g + 1) * pg] = (pm * ps_ref[0:1, cols]).astype(BF16)

    vec = lambda rows, width: pl.BlockSpec((rows, width), lambda i: (0, 0))
    in_specs = [*_halo_specs(r, dc, 0, t), *_halo_specs(r, dc, 1, t), *_halo_specs(r, dc, 2, t),
                vec(conv_w.shape[0], dc), vec(8, dc), vec(8, dc), vec(8, dc),
                pl.BlockSpec((n_grp, pg, pg), lambda i: (0, 0, 0)), vec(8, dc)]
    return _pcall(
        body, grid=(t // r,), in_specs=in_specs,
        out_specs=[pl.BlockSpec((r, 2 * dc), lambda i: (i, 0)), pl.BlockSpec((r, dc), lambda i: (i, 0)),
                   pl.BlockSpec((r, dc), lambda i: (i, 0))],
        out_shape=[jax.ShapeDtypeStruct((t, 2 * dc), BF16), jax.ShapeDtypeStruct((t, dc), F32),
                   jax.ShapeDtypeStruct((t, dc), BF16)],
        scratch_shapes=[pltpu.VMEM((r + 2 * HALO, dc), F32), pltpu.VMEM((r + 2 * HALO, dc), F32)],
        semantics=("parallel",), vmem_mb=40, name=name,
        operands=[*([z] * 9), conv_w, conv_b, ln_g, ln_b, pool_w, pool_scale], rider=rider)


def _mixer0_bwd(dy, z, cv, dsave, conv_w, ln_g, ln_b, pool_w, pool_scale, *, n_tap, n_x, name, rider=None):
    t, dc = cv.shape
    n_grp, pg = pool_w.shape[0], pool_w.shape[1]
    r = _div(t - n_x, (256, 128))
    half = n_tap // 2
    dyp_, zp_, cvp_ = dy, z, cv
    rw = r + 2 * HALO

    def body(dcp, dcc, dcn, dpp, dpc, dpn, cvp, cvc, cvn, ap, ac, an, gp, gc, gn, d_ref,
             w_ref, lg_ref, lb_ref, pw_ref, ps_ref,
             dz_ref, dw_ref, dcb_ref, dlg_ref, dlb_ref, dpw_ref, dps_ref, uwin, dcvwin, ewin, ddwin):
        i = pl.program_id(0)
        _zero_accs(i, [dw_ref, dcb_ref, dlg_ref, dlb_ref, dpw_ref, dps_ref])
        row0, first, last, seg_start, seg_len = _seg_geometry(i, r, n_x, t)
        keep_prev = jnp.where(first, 0.0, 1.0)
        keep_next = jnp.where(last, 0.0, 1.0)
        lg, lb = lg_ref[0:1, :], lb_ref[0:1, :]
        _, vjp = jax.vjp(_ln_silu, cvc[...], lg, lb)
        dcv, dlg, dlb = vjp(dcc[...])
        dlg_ref[0:1, :] += dlg
        dlb_ref[0:1, :] += dlb
        dcb_ref[0:1, :] += jnp.sum(dcv, axis=0, keepdims=True)
        dcvwin[HALO:HALO + r, :] = dcv
        for halo_cv, halo_dy, keep, lo in ((cvp, dcp, keep_prev, 0), (cvn, dcn, keep_next, HALO + r)):
            _, vjp_h = jax.vjp(lambda v: _ln_silu(v, lg, lb), halo_cv[...])
            dcvwin[lo:lo + HALO, :] = vjp_h(halo_dy[...])[0] * keep
        uwin[0:HALO, :] = ap[...] * _sigmoid(gp[...]) * keep_prev
        uwin[HALO:HALO + r, :] = ac[...] * _sigmoid(gc[...])
        uwin[HALO + r:, :] = an[...] * _sigmoid(gn[...]) * keep_next
        for cols, r0 in _chunks(dc, r):
            du = jnp.zeros((ROW_CHUNK, cols.stop - cols.start), F32)
            for k in range(n_tap):
                off = HALO + half - k + r0
                du = du + w_ref[k:k + 1, cols] * dcvwin[off:off + ROW_CHUNK, cols]
            rows = slice(r0, r0 + ROW_CHUNK)
            sig = _sigmoid(gc[rows, cols])
            dz_ref[rows, cols] = (du * sig).astype(BF16)
            dz_ref[rows, dc + cols.start:dc + cols.stop] = (du * ac[rows, cols] * sig * (1.0 - sig)).astype(BF16)
        for c0 in range(0, dc, LANE_CHUNK):
            cols = slice(c0, c0 + LANE_CHUNK)
            taps = [jnp.zeros((8, LANE_CHUNK), F32) for _ in range(n_tap)]
            for r0 in range(0, r, ROW_CHUNK):
                dcv_c = dcvwin[HALO + r0:HALO + r0 + ROW_CHUNK, cols]
                for k in range(n_tap):
                    off = HALO - half + k + r0
                    prod = dcv_c * uwin[off:off + ROW_CHUNK, cols]
                    taps[k] = taps[k] + functools.reduce(
                        jnp.add, [prod[8 * s:8 * s + 8] for s in range(ROW_CHUNK // 8)])
            for k in range(n_tap):
                dw_ref[k:k + 1, cols] += jnp.sum(taps[k], axis=0, keepdims=True)
        twin = row0 - seg_start - HALO + lax.broadcasted_iota(jnp.int32, (rw, 1), 0)
        for g, w in enumerate(POOL_WINDOWS):
            cols = slice(g * pg, (g + 1) * pg)
            wg = pw_ref[g].astype(BF16)
            scale = ps_ref[0:1, cols]
            dyp_c = dpc[:, cols]
            dpm_win = jnp.concatenate([dpp[:, cols] * keep_prev, dyp_c, dpn[:, cols] * keep_next], axis=0) * scale
            dd_win = lax.dot_general(dpm_win.astype(BF16), wg, (((1,), (1,)), ((), ())), preferred_element_type=F32)
            cnt = jnp.maximum(_pool_count(twin, w, seg_len), 1.0)
            ddwin[:, cols] = dd_win
            ewin[:, cols] = dd_win / cnt
            for ccols, r0 in _chunks(pg, r, g * pg):
                dup = -ddwin[HALO + r0:HALO + r0 + ROW_CHUNK, ccols]
                for j in range(-(w // 2) + 1, w // 2 + 1):
                    dup = dup + ewin[HALO + j + r0:HALO + j + r0 + ROW_CHUNK, ccols]
                dz_ref[r0:r0 + ROW_CHUNK, 2 * dc + ccols.start:2 * dc + ccols.stop] = dup.astype(BF16)
            dsv = d_ref[:, cols]
            pm = jnp.dot(dsv, wg, preferred_element_type=F32)
            dps_ref[0:1, cols] += jnp.sum(dyp_c * pm, axis=0, keepdims=True)
            dpw_ref[g] += lax.dot_general(dsv, (dyp_c * scale).astype(BF16), (((0,), (0,)), ((), ())),
                                          preferred_element_type=F32)

    vec = lambda rows, width: pl.BlockSpec((rows, width), lambda i: (0, 0))
    grp = pl.BlockSpec((n_grp, pg, pg), lambda i: (0, 0, 0))
    in_specs = [*_halo_specs(r, dc, 0, t), *_halo_specs(r, dc, 1, t), *_halo_specs(r, dc, 0, t),
                *_halo_specs(r, dc, 0, t), *_halo_specs(r, dc, 1, t), pl.BlockSpec((r, dc), lambda i: (i, 0)),
                vec(conv_w.shape[0], dc), vec(8, dc), vec(8, dc), grp, vec(8, dc)]
    return _pcall(
        body, grid=(t // r,), in_specs=in_specs,
        out_specs=[pl.BlockSpec((r, 3 * dc), lambda i: (i, 0)), vec(conv_w.shape[0], dc), vec(8, dc), vec(8, dc),
                   vec(8, dc), grp, vec(8, dc)],
        out_shape=[jax.ShapeDtypeStruct((t, 3 * dc), BF16), jax.ShapeDtypeStruct(conv_w.shape, F32),
                   jax.ShapeDtypeStruct((8, dc), F32), jax.ShapeDtypeStruct((8, dc), F32),
                   jax.ShapeDtypeStruct((8, dc), F32), jax.ShapeDtypeStruct(pool_w.shape, F32),
                   jax.ShapeDtypeStruct((8, dc), F32)],
        scratch_shapes=[pltpu.VMEM((rw, dc), F32)] * 4,
        semantics=("arbitrary",), vmem_mb=VMEM_CAP_MB, name=name,
        operands=[dyp_, dyp_, dyp_, dyp_, dyp_, dyp_, cvp_, cvp_, cvp_, zp_, zp_, zp_, zp_, zp_, zp_, dsave,
                  conv_w, ln_g, ln_b, pool_w, pool_scale], rider=rider)


def _swap_halves(x):
    lane = lax.broadcasted_iota(jnp.int32, x.shape, 1)
    quarter = HEAD_DIM // 4
    return jnp.where(lane % (2 * quarter) < quarter,
                     pltpu.roll(x, HEAD_DIM - quarter, 1), pltpu.roll(x, quarter, 1))


def _rope_tables(n_x, n_ctx):
    half = HEAD_DIM // 4
    freqs = ROPE_THETA ** (-jnp.arange(half, dtype=F32) / half)
    tok = jnp.arange(n_x)
    row = (tok // GRID_W).astype(F32)[:, None] * freqs[None, :]
    col = (tok % GRID_W).astype(F32)[:, None] * freqs[None, :]
    cos = jnp.concatenate([jnp.cos(row), jnp.cos(row), jnp.cos(col), jnp.cos(col)], axis=1)
    sin = jnp.concatenate([-jnp.sin(row), jnp.sin(row), -jnp.sin(col), jnp.sin(col)], axis=1)
    cos = jnp.concatenate([cos, jnp.ones((n_ctx, HEAD_DIM), F32)], axis=0)
    sin = jnp.concatenate([sin, jnp.zeros((n_ctx, HEAD_DIM), F32)], axis=0)
    return cos, sin


def _norm_g(x, g):
    return _rms(x) * g


def _qk_prep_fwd(qkv, gq, gk, cos, sin, *, d_q, name):
    t, width = qkv.shape
    d_kv = (width - d_q) // 2
    tm = _div(t, (256, 128))

    def body(qkv_ref, gq_ref, gk_ref, cos_ref, sin_ref, q_ref, k_ref, v_ref):
        cs, sn = cos_ref[...], sin_ref[...]
        for h in range((d_q + d_kv) // HEAD_DIM):
            g = gq_ref[0:1, :] if h * HEAD_DIM < d_q else gk_ref[0:1, :]
            xn = _norm_g(qkv_ref[:, h * HEAD_DIM:(h + 1) * HEAD_DIM], g)
            rot = xn * cs + _swap_halves(xn) * sn
            if h * HEAD_DIM < d_q:
                q_ref[:, h * HEAD_DIM:(h + 1) * HEAD_DIM] = (rot * Q_SCALE_LOG2).astype(BF16)
            else:
                k_ref[:, h * HEAD_DIM - d_q:(h + 1) * HEAD_DIM - d_q] = rot.astype(BF16)
        v_ref[...] = qkv_ref[:, d_q + d_kv:].astype(BF16)

    row = lambda w: pl.BlockSpec((tm, w), lambda i: (i, 0))
    vec = pl.BlockSpec((8, HEAD_DIM), lambda i: (0, 0))
    return pl.pallas_call(
        body, grid=(t // tm,),
        in_specs=[row(width), vec, vec, row(HEAD_DIM), row(HEAD_DIM)],
        out_specs=[row(d_q), row(d_kv), row(d_kv)],
        out_shape=[jax.ShapeDtypeStruct((t, d_q), BF16), jax.ShapeDtypeStruct((t, d_kv), BF16),
                   jax.ShapeDtypeStruct((t, d_kv), BF16)],
        compiler_params=_params(("parallel",), 32), name=name,
    )(qkv, gq, gk, cos, sin)


def _qk_prep_bwd(qkv, dq, dk, dv, gq, gk, cos, sin, *, n_x, name):
    t, width = qkv.shape
    d_q, d_kv = dq.shape[1], dk.shape[1]
    tm = _div(t, (256, 128))
    last_q = n_x // tm - 1

    def body(qkv_ref, dq_ref, dk_ref, dv_ref, gq_ref, gk_ref, cos_ref, sin_ref, out_ref, dgq_ref, dgk_ref):
        i = pl.program_id(0)
        _zero_accs(i, [dgq_ref, dgk_ref])
        is_x = jnp.where(i * tm < n_x, 1.0, 0.0)
        cs, sn = cos_ref[...], sin_ref[...]
        for h in range((d_q + d_kv) // HEAD_DIM):
            sl = slice(h * HEAD_DIM, (h + 1) * HEAD_DIM)
            if h * HEAD_DIM < d_q:
                g, dg_ref, dr = gq_ref[0:1, :], dgq_ref, dq_ref[:, sl] * is_x
            else:
                g, dg_ref = gk_ref[0:1, :], dgk_ref
                dr = dk_ref[:, h * HEAD_DIM - d_q:(h + 1) * HEAD_DIM - d_q]
            dxn = dr * cs + _swap_halves(dr * sn)
            _, vjp = jax.vjp(_norm_g, qkv_ref[:, sl], g)
            dx, dg = vjp(dxn)
            out_ref[:, sl] = dx.astype(BF16)
            dg_ref[0:1, :] += dg
        out_ref[:, d_q + d_kv:] = dv_ref[...].astype(BF16)

    row = lambda w: pl.BlockSpec((tm, w), lambda i: (i, 0))
    vec = pl.BlockSpec((8, HEAD_DIM), lambda i: (0, 0))
    return pl.pallas_call(
        body, grid=(t // tm,),
        in_specs=[row(width), pl.BlockSpec((tm, d_q), lambda i: (jnp.minimum(i, last_q), 0)), row(d_kv), row(d_kv),
                  vec, vec, row(HEAD_DIM), row(HEAD_DIM)],
        out_specs=[row(width), vec, vec],
        out_shape=[jax.ShapeDtypeStruct((t, width), BF16), jax.ShapeDtypeStruct((8, HEAD_DIM), F32),
                   jax.ShapeDtypeStruct((8, HEAD_DIM), F32)],
        compiler_params=_params(("arbitrary",), 40), name=name,
    )(qkv, dq, dk, dv, gq, gk, cos, sin)


ATTN_TQ = (256, 128)
ATTN_TK = (768, 640, 512, 384, 256, 128)


def _attention_fwd(q, k, v, *, n_x, name):
    t, d_kv = k.shape
    d_q = q.shape[1]
    kvh = d_kv // HEAD_DIM
    grp = d_q // d_kv
    tq = _div(n_x, ATTN_TQ)
    tk = _div(t, ATTN_TK)
    gw = grp * HEAD_DIM

    n_kv = t // tk
    n_pair = (n_kv - 1) // 2

    def body(q_ref, k_ref, v_ref, o_ref, lse_ref, m_ref, l_ref, acc_ref, s_even, s_odd):
        m_ref[...] = jnp.full_like(m_ref, -jnp.inf)
        l_ref[...] = jnp.zeros_like(l_ref)
        acc_ref[...] = jnp.zeros_like(acc_ref)

        def key_rows(j):
            return pl.ds(pl.multiple_of(j * tk, tk), tk)

        def scores(g, kc):
            return lax.dot_general(q_ref[:, g * HEAD_DIM:(g + 1) * HEAD_DIM], kc, (((1,), (1,)), ((), ())),
                                   preferred_element_type=F32)

        def chunk(j, s_cur, s_next):
            vc = v_ref[key_rows(j), :]
            kn = k_ref[key_rows(j + 1), :] if s_next is not None else None
            for g in range(grp):
                s = s_cur[g]
                m_old = m_ref[g]
                m_new = jnp.maximum(m_old, jnp.max(s, axis=-1, keepdims=True))
                alpha = jnp.exp2(m_old - m_new)
                p = jnp.exp2(s - m_new)
                if s_next is not None:
                    s_next[g] = scores(g, kn)
                l_ref[g] = alpha * l_ref[g] + functools.reduce(
                    jnp.add, [p[:, c:c + HEAD_DIM] for c in range(0, tk, HEAD_DIM)])
                acc_ref[g] = alpha * acc_ref[g] + jnp.dot(p.astype(BF16), vc, preferred_element_type=F32)
                m_ref[g] = m_new

        k0 = k_ref[key_rows(0), :]
        for g in range(grp):
            s_even[g] = scores(g, k0)

        def pair(i, carry):
            chunk(2 * i, s_even, s_odd)
            chunk(2 * i + 1, s_odd, s_even)
            return carry

        lax.fori_loop(0, n_pair, pair, 0)
        if n_kv - 2 * n_pair == 2:
            chunk(n_kv - 2, s_even, s_odd)
            chunk(n_kv - 1, s_odd, None)
        else:
            chunk(n_kv - 1, s_even, None)
        for g in range(grp):
            l = jnp.sum(l_ref[g], axis=-1, keepdims=True)
            o_ref[:, g * HEAD_DIM:(g + 1) * HEAD_DIM] = (acc_ref[g] / l).astype(BF16)
            lse_ref[:, g:g + 1] = m_ref[g] + jnp.log(l) * LOG2_E

    return pl.pallas_call(
        body, grid=(kvh, n_x // tq),
        in_specs=[pl.BlockSpec((tq, gw), lambda h, i: (i, h)),
                  pl.BlockSpec((t, HEAD_DIM), lambda h, i: (0, h)),
                  pl.BlockSpec((t, HEAD_DIM), lambda h, i: (0, h))],
        out_specs=[pl.BlockSpec((tq, gw), lambda h, i: (i, h)),
                   pl.BlockSpec((None, tq, grp), lambda h, i: (h, i, 0))],
        out_shape=[jax.ShapeDtypeStruct((n_x, d_q), BF16), jax.ShapeDtypeStruct((kvh, n_x, grp), F32)],
        scratch_shapes=[pltpu.VMEM((grp, tq, 1), F32), pltpu.VMEM((grp, tq, HEAD_DIM), F32),
                        pltpu.VMEM((grp, tq, HEAD_DIM), F32),
                        pltpu.VMEM((grp, tq, tk), F32), pltpu.VMEM((grp, tq, tk), F32)],
        compiler_params=_params(("parallel", "arbitrary"), 48), name=name,
    )(q, k, v)


def _attention_bwd(q, k, v, o, lse, do, *, n_x, name):
    t, d_kv = k.shape
    d_q = q.shape[1]
    kvh = d_kv // HEAD_DIM
    grp = d_q // d_kv
    tq = _div(n_x, ATTN_TQ)
    tk = _div(t, ATTN_TK)
    gw = grp * HEAD_DIM
    n_q = n_x // tq

    def body(q_ref, k_ref, v_ref, o_ref, lse_ref, do_ref, dq_ref, dkt_ref, dvt_ref, dq_acc, lse_s, delta_s, qt_s,
             dot_s):
        i = pl.program_id(1)
        _zero_accs(i, [dkt_ref, dvt_ref])
        dq_acc[...] = jnp.zeros_like(dq_acc)
        for g in range(grp):
            sl = slice(g * HEAD_DIM, (g + 1) * HEAD_DIM)
            lse_s[g] = lse_ref[:, g:g + 1]
            delta_s[g] = jnp.sum(do_ref[:, sl].astype(F32) * o_ref[:, sl].astype(F32), axis=-1, keepdims=True)
            qt_s[g] = q_ref[:, sl].T
            dot_s[g] = do_ref[:, sl].T

        def step(j, carry):
            start = pl.multiple_of(j * tk, tk)
            kc, vc = k_ref[pl.ds(start, tk), :], v_ref[pl.ds(start, tk), :]
            dkt_part = jnp.zeros((HEAD_DIM, tk), F32)
            dvt_part = jnp.zeros((HEAD_DIM, tk), F32)
            for g in range(grp):
                sl = slice(g * HEAD_DIM, (g + 1) * HEAD_DIM)
                s = lax.dot_general(q_ref[:, sl], kc, (((1,), (1,)), ((), ())), preferred_element_type=F32)
                p = jnp.exp2(s - lse_s[g])
                dp = lax.dot_general(do_ref[:, sl], vc, (((1,), (1,)), ((), ())), preferred_element_type=F32)
                ds = (p * (dp - delta_s[g])).astype(BF16)
                dq_acc[g] += jnp.dot(ds, kc, preferred_element_type=F32)
                dvt_part = dvt_part + jnp.dot(dot_s[g], p.astype(BF16), preferred_element_type=F32)
                dkt_part = dkt_part + jnp.dot(qt_s[g], ds, preferred_element_type=F32)
            dvt_ref[:, pl.ds(start, tk)] += dvt_part
            dkt_ref[:, pl.ds(start, tk)] += dkt_part
            return carry

        lax.fori_loop(0, t // tk, step, 0)
        for g in range(grp):
            dq_ref[:, g * HEAD_DIM:(g + 1) * HEAD_DIM] = dq_acc[g] * ATTN_SCALE

        @pl.when(i == n_q - 1)
        def _():
            dkt_ref[...] = dkt_ref[...] * (1.0 / LOG2_E)

    qspec = pl.BlockSpec((tq, gw), lambda h, i: (i, h))
    kspec = pl.BlockSpec((t, HEAD_DIM), lambda h, i: (0, h))
    ktspec = pl.BlockSpec((HEAD_DIM, t), lambda h, i: (h, 0))
    return pl.pallas_call(
        body, grid=(kvh, n_q),
        in_specs=[qspec, kspec, kspec, qspec, pl.BlockSpec((None, tq, grp), lambda h, i: (h, i, 0)), qspec],
        out_specs=[qspec, ktspec, ktspec],
        out_shape=[jax.ShapeDtypeStruct((n_x, d_q), F32), jax.ShapeDtypeStruct((d_kv, t), F32),
                   jax.ShapeDtypeStruct((d_kv, t), F32)],
        scratch_shapes=[pltpu.VMEM((grp, tq, HEAD_DIM), F32), pltpu.VMEM((grp, tq, 1), F32),
                        pltpu.VMEM((grp, tq, 1), F32), pltpu.VMEM((grp, HEAD_DIM, tq), BF16),
                        pltpu.VMEM((grp, HEAD_DIM, tq), BF16)],
        compiler_params=_params(("parallel", "arbitrary"), 56), name=name,
    )(q, k, v, o, lse, do)


def _whole(body, ins, out_shapes, name):
    return pl.pallas_call(
        body, out_shape=[jax.ShapeDtypeStruct(s, d) for s, d in out_shapes],
        compiler_params=pltpu.CompilerParams(vmem_limit_bytes=40 << 20), name=name)(*ins)


def _silu_rows(x, *, name):
    def body(x_ref, o_ref):
        o_ref[...] = _silu(x_ref[...])
    return _whole(body, [x], [(x.shape, F32)], name)[0]


def _assemble_dmods(gathered, *, name):
    width = gathered.shape[1]

    def body(g_ref, dm0, dm1, db0, db1):
        for l, (dm, db) in enumerate(((dm0, db0), (dm1, db1))):
            ctx = jnp.zeros((1, width), F32)
            tot = jnp.zeros((1, width), F32)
            for q in range(N_DEV):
                row = g_ref[16 * q + 8 * l:16 * q + 8 * l + 1, :]
                dm[q:q + 1, :] = row
                tot = tot + row
                ctx = ctx + g_ref[16 * q + 8 * l + 1:16 * q + 8 * l + 2, :]
            dm[N_DEV:N_DEV + 1, :] = ctx
            dm[N_DEV + 1:, :] = jnp.zeros((16 - N_DEV - 1, width), F32)
            db[...] = jnp.zeros_like(db)
            db[0:1, :] = tot + ctx

    return _whole(body, [gathered], [((16, width), F32), ((16, width), F32), ((8, width), F32), ((8, width), F32)],
                  name)


def _sum_slots(gathered, rows, *, name):
    def body(g_ref, o_ref):
        acc = g_ref[0:rows, :]
        for q in range(1, N_DEV):
            acc = acc + g_ref[q * rows:(q + 1) * rows, :]
        o_ref[...] = acc
    return _whole(body, [gathered], [((rows, gathered.shape[1]), F32)], name)[0]


def _silu_grad(x, dy, *, name):
    def body(x_ref, dy_ref, o_ref):
        _, vjp = jax.vjp(_silu, x_ref[...])
        o_ref[...] = vjp(dy_ref[...])[0]
    return _whole(body, [x, dy], [(x.shape, F32)], name)[0]


def _adamw(w, m, v, *, name, recv=None, grad=None):
    rows, cols = w.shape
    budget = max(8, ADAMW_BLOCK_ELEMS // cols)
    tr = _div(rows, [c for c in (512, 256, 128, 64, 32, 16, 8) if c <= budget] + [rows])
    c1 = 1.0 - ADAM_B1 ** ADAM_STEP
    c2 = 1.0 - ADAM_B2 ** ADAM_STEP

    def body(w_ref, m_ref, v_ref, g_in, g_ref, d_ref, nm_ref, nv_ref):
        if recv is not None:
            g = g_in[0].astype(F32)
            for q in range(1, N_DEV):
                g = g + g_in[q].astype(F32)
        else:
            g = g_in[...]
        nm = ADAM_B1 * m_ref[...] + (1.0 - ADAM_B1) * g
        nv = ADAM_B2 * v_ref[...] + (1.0 - ADAM_B2) * jnp.square(g)
        g_ref[...] = g
        nm_ref[...] = nm
        nv_ref[...] = nv
        d_ref[...] = -ADAM_LR * ((nm / c1) / (jnp.sqrt(nv / c2) + ADAM_EPS) + ADAM_WD * w_ref[...])

    blk = pl.BlockSpec((tr, cols), lambda i: (i, 0))
    g_spec = pl.BlockSpec((N_DEV, tr, cols), lambda i: (0, i, 0)) if recv is not None else blk
    return pl.pallas_call(
        body, grid=(rows // tr,), in_specs=[blk, blk, blk, g_spec], out_specs=[blk] * 4,
        out_shape=[jax.ShapeDtypeStruct((rows, cols), F32)] * 4,
        compiler_params=_params(("parallel",), 48), name=name,
    )(w, m, v, recv if recv is not None else grad)


def _position():
    return tuple(lax.axis_index(a) for a in MESH_AXES)


def _linear(pos):
    return 4 * pos[0] + 2 * pos[1] + pos[2]


def _window(ref, axis, dev, size):
    start = pl.multiple_of(dev * size, size)
    return ref.at[pl.ds(start, size), :] if axis == 0 else ref.at[:, pl.ds(start, size)]


def _all_gather(shards, axes, *, name):
    n = len(shards)
    sizes = [s.shape[ax] for s, ax in zip(shards, axes)]

    def body(*refs):
        src, dst = refs[:n], refs[n:2 * n]
        send_sems, recv_sems, local_sems = refs[2 * n:]
        x, y, c = _position()
        me, sibling = (x, y, c), (x, y, 1 - c)
        chips = [(1 - x, y), (x, 1 - y), (1 - x, 1 - y)]

        def win(k, pos):
            return _window(dst[k], axes[k], _linear(pos), sizes[k])

        def copy(k, sem, block, to, from_src=False):
            return pltpu.make_async_remote_copy(
                src_ref=src[k] if from_src else win(k, block), dst_ref=win(k, block),
                send_sem=send_sems.at[k, sem], recv_sem=recv_sems.at[k, sem],
                device_id=to, device_id_type=MESH_ID)

        mine = [pltpu.make_async_copy(src[k], win(k, me), local_sems.at[k]) for k in range(n)]
        for cp in mine:
            cp.start()
        first = []
        for k in range(n):
            first.append(copy(k, 0, me, sibling, from_src=True))
            first += [copy(k, 1 + j, me, (*chip, c), from_src=True) for j, chip in enumerate(chips)]
        for cp in first:
            cp.start()
        passed = []
        for j, chip in enumerate(chips):
            for k in range(n):
                copy(k, 1 + j, (*chip, c), me).wait_recv()
                fwd = copy(k, 4 + j, (*chip, c), sibling)
                fwd.start()
                passed.append(fwd)
        for k in range(n):
            copy(k, 0, sibling, me).wait_recv()
            for j, chip in enumerate(chips):
                copy(k, 4 + j, (*chip, 1 - c), me).wait_recv()
        for cp in first + passed:
            cp.wait_send()
        for cp in mine:
            cp.wait()

    out_shape = []
    for s, ax in zip(shards, axes):
        full = (s.shape[0] * N_DEV, s.shape[1]) if ax == 0 else (s.shape[0], s.shape[1] * N_DEV)
        out_shape.append(jax.ShapeDtypeStruct(full, s.dtype))
    any_spec = pl.BlockSpec(memory_space=pl.ANY)
    return pl.pallas_call(
        body, in_specs=[any_spec] * n, out_specs=[any_spec] * n, out_shape=out_shape,
        scratch_shapes=[pltpu.SemaphoreType.DMA((n, 7)), pltpu.SemaphoreType.DMA((n, 7)),
                        pltpu.SemaphoreType.DMA((n,))],
        name=name,
    )(*shards)


class Rider:
    def __init__(self, kind, arrays, axes):
        self.kind, self.arrays, self.axes = kind, list(arrays), list(axes)
        self.n = len(self.arrays)
        if kind == "gather":
            self.sizes = [a.shape[ax] for a, ax in zip(self.arrays, self.axes)]
        else:
            self.sizes = [a.shape[ax] // N_DEV for a, ax in zip(self.arrays, self.axes)]

    def out_shape(self):
        shapes = []
        for a, ax, sz in zip(self.arrays, self.axes, self.sizes):
            if self.kind == "gather":
                full = (sz * N_DEV, a.shape[1]) if ax == 0 else (a.shape[0], sz * N_DEV)
                shapes.append(jax.ShapeDtypeStruct(full, a.dtype))
            else:
                shard = (sz, a.shape[1]) if ax == 0 else (a.shape[0], sz)
                shapes.append(jax.ShapeDtypeStruct((N_DEV, *shard), a.dtype))
        return shapes

    def scratch(self):
        return [pltpu.SemaphoreType.DMA((self.n, N_DEV - 1)), pltpu.SemaphoreType.DMA((self.n, N_DEV - 1)),
                pltpu.SemaphoreType.DMA((self.n,))]

    def plan(self, src, dst, send_sems, recv_sems, local_sems):
        x, y, c = me = _position()
        mine = _linear(me)

        def remote(k, sem, src_ref, dst_ref, to):
            return pltpu.make_async_remote_copy(
                src_ref=src_ref, dst_ref=dst_ref, send_sem=send_sems.at[k, sem], recv_sem=recv_sems.at[k, sem],
                device_id=to, device_id_type=MESH_ID)

        ph = dict(local=[], start=[], mid_wait=[], mid_start=[], end_wait=[])
        for k in range(self.n):
            ax, sz = self.axes[k], self.sizes[k]
            if self.kind == "exchange":
                own_src, own_dst = _window(src[k], ax, mine, sz), dst[k].at[mine]
                ph["local"].append(pltpu.make_async_copy(own_src, own_dst, local_sems.at[k]))
                for mask in range(1, N_DEV):
                    to = tuple(1 - p if (mask >> (2 - b)) & 1 else p for b, p in enumerate(me))
                    ph["start"].append(remote(k, mask - 1, _window(src[k], ax, _linear(to), sz), own_dst, to))
                    ph["end_wait"].append(remote(k, mask - 1, own_src, dst[k].at[_linear(to)], to))
            else:
                def win(pos, k=k, ax=ax, sz=sz):
                    return _window(dst[k], ax, _linear(pos), sz)
                sibling = (x, y, 1 - c)
                chips = [(1 - x, y), (x, 1 - y), (1 - x, 1 - y)]
                ph["local"].append(pltpu.make_async_copy(src[k], win(me), local_sems.at[k]))
                ph["start"].append(remote(k, 0, src[k], win(me), sibling))
                ph["end_wait"].append(remote(k, 0, src[k], win(sibling), sibling))
                for j, chip in enumerate(chips):
                    ph["start"].append(remote(k, 1 + j, src[k], win(me), (*chip, c)))
                    ph["mid_wait"].append(remote(k, 1 + j, src[k], win((*chip, c)), (*chip, c)))
                    ph["mid_start"].append(remote(k, 4 + j, win((*chip, c)), win((*chip, c)), sibling))
                    ph["end_wait"].append(remote(k, 4 + j, src[k], win((*chip, 1 - c)), sibling))
        return ph


def _pcall(body, *, grid, in_specs, out_specs, out_shape, scratch_shapes, semantics, vmem_mb, name, operands,
           rider=None):
    if rider is None:
        return pl.pallas_call(body, grid=grid, in_specs=in_specs, out_specs=out_specs, out_shape=out_shape,
                              scratch_shapes=scratch_shapes, compiler_params=_params(semantics, vmem_mb),
                              name=name)(*operands)
    n_in, n_out, n_scr, n = len(in_specs), len(out_specs), len(scratch_shapes), rider.n

    def wrapped(*refs):
        ins, src = refs[:n_in], refs[n_in:n_in + n]
        outs = refs[n_in + n:n_in + n + n_out]
        dst = refs[n_in + n + n_out:n_in + 2 * n + n_out]
        rest = refs[n_in + 2 * n + n_out:]
        scratch, sems = rest[:n_scr], rest[n_scr:]
        step = functools.reduce(lambda acc, ig: acc * ig[1] + ig[0],
                                [(pl.program_id(dim), g) for dim, g in enumerate(grid)], 0)
        n_steps = functools.reduce(lambda a, b: a * b, grid)

        @pl.when(step == 0)
        def _():
            ph = rider.plan(src, dst, *sems)
            for cp in ph["local"] + ph["start"]:
                cp.start()

        body(*ins, *outs, *scratch)

        @pl.when(step == (n_steps * 5) // 8)
        def _():
            ph = rider.plan(src, dst, *sems)
            for cp in ph["mid_wait"]:
                cp.wait_recv()
            for cp in ph["mid_start"]:
                cp.start()

        @pl.when(step == n_steps - 1)
        def _():
            ph = rider.plan(src, dst, *sems)
            for cp in ph["end_wait"]:
                cp.wait_recv()
            for cp in ph["start"] + ph["mid_start"]:
                cp.wait_send()
            for cp in ph["local"]:
                cp.wait()

    any_spec = pl.BlockSpec(memory_space=pl.ANY)
    return pl.pallas_call(
        wrapped, grid=grid, in_specs=list(in_specs) + [any_spec] * n, out_specs=list(out_specs) + [any_spec] * n,
        out_shape=list(out_shape) + rider.out_shape(), scratch_shapes=list(scratch_shapes) + rider.scratch(),
        compiler_params=_params(("arbitrary",) * len(grid), vmem_mb), name=name,
    )(*operands, *rider.arrays)


WEIGHTS = ['c_ctx', 'l0_ada_w', 'l0_ada_b', 'l0_in_w', 'l0_conv_w', 'l0_conv_b', 'l0_conv_ln_g', 'l0_conv_ln_b',
           'l0_pool_w', 'l0_pool_scale', 'l0_out_w', 'l0_mlp_w1', 'l0_mlp_w2', 'l1_ada_w', 'l1_ada_b', 'l1_qkv_w',
           'l1_q_norm_g', 'l1_k_norm_g', 'l1_out_w', 'l1_mlp_w1', 'l1_mlp_w2', 'final_g']
SHARDED = {'l0_in_w': 1, 'l0_out_w': 0, 'l0_mlp_w1': 1, 'l0_mlp_w2': 0,
           'l1_qkv_w': 1, 'l1_out_w': 0, 'l1_mlp_w1': 1, 'l1_mlp_w2': 0}
REPLICATED_SMALL = ['l0_conv_b', 'l0_conv_ln_g', 'l0_conv_ln_b', 'l0_pool_scale', 'l1_q_norm_g', 'l1_k_norm_g',
                    'final_g']


def _row8(v):
    v = v.reshape(1, -1)
    return jnp.pad(v, ((0, 7), (0, 0)))


def _mods16(full, me):
    d = full.shape[1] // 6
    mine = lax.dynamic_slice_in_dim(full, me, 1, axis=0).reshape(6, d)
    ctx = full[N_DEV].reshape(6, d)
    return jnp.pad(jnp.stack([mine, ctx], axis=1).reshape(12, d), ((0, 4), (0, 0)))


def _mlp_fwd(xs, mods, w1, w2, *, n_x, tm, tag, rider1=None, rider2=None):
    t, d = xs.shape
    dff = w1.shape[1]
    h = _rms_mod_fwd(xs, mods, k_shift=3, n_x=n_x, name=f"{tag}_norm2")
    pre, act, *ride1 = _matmul(h, w1, mode="nn", tm=tm, tn=_div(dff, (1024, 512)), tk=d, out_dtypes=[BF16, BF16],
                               name=f"{tag}_mlp1", rider=rider1,
                               epilogue=lambda acc, rows: (acc, jnp.square(jnp.maximum(acc, 0.0))))
    if w2 is None:
        w2 = ride1[-1]
    xo, branch, *ride2 = _matmul(
        act, w2, mode="nn", tm=_div(t, DEEP_TM), tn=_div(d, DEEP_TN), tk=dff, out_dtypes=[F32, BF16],
        name=f"{tag}_mlp2", extras=[("tile", xs), ("vec", mods)], rider=rider2,
        epilogue=lambda acc, rows, res, mv: (res + _seg_pick(mv, 5, rows, n_x) * acc, acc))
    return xo, dict(h=h, pre=pre, act=act, branch=branch, x_in=xs), ride1, ride2


def _exchange_of(items):
    return Rider("exchange", [a for a, _ in items], [ax for _, ax in items]) if items else None


def _mlp_bwd(dxo, dbranch, saved, mods, w1, w2, mixer_branch, *, n_x, tm, tag, ride_dx2=(), ride_dw2=()):
    t, d = dxo.shape
    dff = w1.shape[1]
    tkt = _div(t, TOKEN_TK)
    dpre, *recv_a = _matmul(dbranch, w2, mode="nt", tm=tm, tn=_div(dff, (1024, 512)), tk=d, out_dtypes=[BF16],
                            name=f"{tag}_mlp2_dx", extras=[("tile", saved["pre"])],
                            rider=_exchange_of(list(ride_dx2)),
                            epilogue=lambda acc, rows, pre: (acc * 2.0 * jnp.maximum(pre.astype(F32), 0.0),))
    dw2, *recv_b = _matmul(saved["act"], dbranch, mode="tn", tm=_div(dff, (1024, 512)), tn=_div(d, (1024, 512)),
                           tk=tkt, out_dtypes=[BF16], name=f"{tag}_mlp2_dw", rider=_exchange_of(list(ride_dw2)))
    dh, = _matmul(dpre, w1, mode="nt", tm=_div(t, DEEP_TM), tn=_div(d, DEEP_TN), tk=dff,
                  out_dtypes=[F32], name=f"{tag}_mlp1_dx")
    dw1, recv_dw2 = _matmul(saved["h"], dpre, mode="tn", tm=_div(d, (1024, 512)), tn=_div(dff, (1024, 512)),
                            tk=tkt, out_dtypes=[BF16], name=f"{tag}_mlp1_dw", rider=_exchange_of([(dw2, 0)]))
    dx, dm_norm, dmix, dm_gate = _rms_mod_bwd(saved["x_in"], mods, dh, dxo, k_shift=3, n_x=n_x,
                                              name=f"{tag}_norm2_bwd", gate=(mixer_branch, mods, 2))
    return dx, dw1, dm_norm + dm_gate, dmix, recv_a, recv_b, recv_dw2


def kernel(x, c, ctx, c_ctx, l0_ada_w, l0_ada_b, l0_in_w, l0_conv_w, l0_conv_b, l0_conv_ln_g, l0_conv_ln_b, l0_pool_w, l0_pool_scale, l0_out_w, l0_mlp_w1, l0_mlp_w2, l1_ada_w, l1_ada_b, l1_qkv_w, l1_q_norm_g, l1_k_norm_g, l1_out_w, l1_mlp_w1, l1_mlp_w2, final_g, loss_target, m_c_ctx, m_l0_ada_w, m_l0_ada_b, m_l0_in_w, m_l0_conv_w, m_l0_conv_b, m_l0_conv_ln_g, m_l0_conv_ln_b, m_l0_pool_w, m_l0_pool_scale, m_l0_out_w, m_l0_mlp_w1, m_l0_mlp_w2, m_l1_ada_w, m_l1_ada_b, m_l1_qkv_w, m_l1_q_norm_g, m_l1_k_norm_g, m_l1_out_w, m_l1_mlp_w1, m_l1_mlp_w2, m_final_g, v_c_ctx, v_l0_ada_w, v_l0_ada_b, v_l0_in_w, v_l0_conv_w, v_l0_conv_b, v_l0_conv_ln_g, v_l0_conv_ln_b, v_l0_pool_w, v_l0_pool_scale, v_l0_out_w, v_l0_mlp_w1, v_l0_mlp_w2, v_l1_ada_w, v_l1_ada_b, v_l1_qkv_w, v_l1_q_norm_g, v_l1_k_norm_g, v_l1_out_w, v_l1_mlp_w1, v_l1_mlp_w2, v_final_g):
    p = dict(locals())
    me = _linear(_position())
    n_x, d = x.shape[1], x.shape[2]
    n_ctx = ctx.shape[1]
    t = n_x + n_ctx
    dc = l0_conv_b.shape[0]
    n_tap = l0_conv_w.shape[0]
    d_q = d
    n_mod = l0_ada_b.shape[0] // d
    ada_cols = l0_ada_w.shape[1]
    tm_t = _div(t, (768, 640, 512, 128))
    tm_x = _div(n_x, (1024, 512))

    names = list(SHARDED)
    shard16 = {nm: p[nm].astype(BF16) for nm in names}

    def gather_of(*nms):
        return Rider("gather", [shard16[nm] for nm in nms], [SHARDED[nm] for nm in nms])

    wfull = {}
    first = ['l0_in_w', 'l0_out_w']
    *full, conv_w_full, pool_w_full = _all_gather(
        [shard16[nm] for nm in first] + [jnp.pad(l0_conv_w, ((0, 1), (0, 0))),
                                         l0_pool_w.reshape(-1, l0_pool_w.shape[2])],
        [SHARDED[nm] for nm in first] + [1, 0], name="gather_first_weights")
    wfull.update(zip(first, full))
    n_grp, pg = l0_pool_w.shape[0], l0_pool_w.shape[2]
    pool_w_full = pool_w_full.reshape(N_DEV, n_grp, pg // N_DEV, pg).transpose(1, 0, 2, 3).reshape(n_grp, pg, pg)

    c_all = _all_gather([_row8(c)], [0], name="gather_cond")[0].reshape(N_DEV, 8, d)[:, 0]
    cond = jnp.concatenate([c_all, c_ctx.reshape(1, d), jnp.zeros((16 - N_DEV - 1, d), F32)], axis=0)
    s16 = _silu_rows(cond, name="cond_silu")
    mod_shards = []
    for li, (lw, lb) in enumerate(((l0_ada_w, l0_ada_b), (l1_ada_w, l1_ada_b))):
        bias = _row8(lax.dynamic_slice_in_dim(lb, me * ada_cols, ada_cols))
        mod_shards.append(_matmul(s16, lw, mode="nn", tm=16, tn=_div(ada_cols, (512, 384, 256, 128)), tk=d,
                                  out_dtypes=[F32], name=f"l{li}_ada_fwd", extras=[("vec", bias)],
                                  epilogue=lambda acc, rows, b: (acc + b[0:1],))[0])
    mods_full = _all_gather([jnp.concatenate(mod_shards, axis=0)], [1], name="gather_mods")[0]
    mods0, mods1 = _mods16(mods_full[:16], me), _mods16(mods_full[16:], me)

    xs0 = jnp.concatenate([x[0], ctx[0]], axis=0)
    h0 = _rms_mod_fwd(xs0, mods0, k_shift=0, n_x=n_x, name="l0_norm1")
    z, wfull['l0_mlp_w1'] = _matmul(h0, wfull['l0_in_w'], mode="nn", tm=tm_t, tn=_div(3 * dc, (1024, 768, 512, 384)),
                                    tk=d, out_dtypes=[F32], name="l0_in_proj", rider=gather_of('l0_mlp_w1'))
    y0, cv, dsave, wfull['l0_mlp_w2'] = _mixer0_fwd(
        z, conv_w_full, _row8(l0_conv_b), _row8(l0_conv_ln_g), _row8(l0_conv_ln_b), pool_w_full,
        _row8(l0_pool_scale), n_tap=n_tap, n_x=n_x, name="l0_mixer", rider=gather_of('l0_mlp_w2'))
    xs1, mix0 = _matmul(y0, wfull['l0_out_w'], mode="nn", tm=tm_t, tn=_div(d, (1024, 512)), tk=2 * dc,
                        out_dtypes=[F32, BF16], name="l0_out_proj", extras=[("tile", xs0), ("vec", mods0)],
                        epilogue=lambda acc, rows, res, mv: (res + _seg_pick(mv, 2, rows, n_x) * acc, acc))
    xs2, mlp0, ride1, ride2 = _mlp_fwd(
        xs1, mods0, wfull['l0_mlp_w1'], wfull['l0_mlp_w2'], n_x=n_x, tm=tm_t, tag="l0",
        rider1=gather_of('l1_qkv_w', 'l1_out_w'), rider2=gather_of('l1_mlp_w1'))
    wfull['l1_qkv_w'], wfull['l1_out_w'] = ride1
    wfull['l1_mlp_w1'], = ride2

    h2 = _rms_mod_fwd(xs2, mods1, k_shift=0, n_x=n_x, name="l1_norm1")
    qkv, = _matmul(h2, wfull['l1_qkv_w'], mode="nn", tm=tm_t, tn=_div(l1_qkv_w.shape[1] * N_DEV, (1024, 768, 512)),
                   tk=d, out_dtypes=[F32], name="l1_qkv_proj")
    cos, sin = _rope_tables(n_x, n_ctx)
    gq, gk = _row8(l1_q_norm_g), _row8(l1_k_norm_g)
    q, k, v = _qk_prep_fwd(qkv, gq, gk, cos, sin, d_q=d_q, name="l1_qk_prep")
    o, lse = _attention_fwd(q, k, v, n_x=n_x, name="l1_attention")
    x3, mix1 = _matmul(o, wfull['l1_out_w'], mode="nn", tm=tm_x, tn=_div(d, (1024, 512)), tk=d_q,
                       out_dtypes=[F32, BF16], name="l1_out_proj", extras=[("tile", xs2), ("vec", mods1)],
                       epilogue=lambda acc, rows, res, mv: (res + mv[4:5] * acc, acc))
    x4, mlp1, (wfull['l1_mlp_w2'],), _ = _mlp_fwd(x3, mods1, wfull['l1_mlp_w1'], None, n_x=n_x, tm=tm_x, tag="l1",
                                                  rider1=gather_of('l1_mlp_w2'))

    dx4, loss_part, dfinal_g, dbranch1, dmods1 = _final_loss(
        x4, _row8(final_g), loss_target[0], mlp1["branch"], mods1, k_gate=5, name="loss_head")
    loss = lax.psum(loss_part[0, 0], MESH_AXES)

    recv = {}
    dx3, dw1_1, dm, dmix1, _, _, recv['l1_mlp_w2'] = _mlp_bwd(
        dx4, dbranch1, mlp1, mods1, wfull['l1_mlp_w1'], wfull['l1_mlp_w2'], mix1, n_x=n_x, tm=tm_x, tag="l1")
    dmods1 = dmods1 + dm
    do, = _matmul(dmix1, wfull['l1_out_w'], mode="nt", tm=tm_x, tn=_div(d_q, (1024, 512)), tk=d,
                  out_dtypes=[BF16], name="l1_out_dx")
    dw_out1, = _matmul(o, dmix1, mode="tn", tm=_div(d_q, (1024, 512)), tn=_div(d, (1024, 512)),
                       tk=_div(n_x, TOKEN_TK), out_dtypes=[BF16], name="l1_out_dw")
    dq, dk_t, dv_t = _attention_bwd(q, k, v, o, lse, do, n_x=n_x, name="l1_attention_bwd")
    dk, dv = dk_t.T, dv_t.T
    dqkv, dgq, dgk = _qk_prep_bwd(qkv, dq, dk, dv, gq, gk, cos, sin, n_x=n_x, name="l1_qk_prep_bwd")
    tkt = _div(t, TOKEN_TK)
    dh2, recv['l1_out_w'] = _matmul(dqkv, wfull['l1_qkv_w'], mode="nt", tm=tm_t, tn=_div(d, (1024, 512)),
                                    tk=dqkv.shape[1], out_dtypes=[F32], name="l1_qkv_dx",
                                    rider=_exchange_of([(dw_out1, SHARDED['l1_out_w'])]))
    dw_qkv, = _matmul(h2, dqkv, mode="tn", tm=_div(d, (1024, 512)), tn=_div(dqkv.shape[1], (1024, 768, 512)),
                      tk=tkt, out_dtypes=[BF16], name="l1_qkv_dw")
    dxs2, dm, dbranch0, dmods0 = _rms_mod_bwd(xs2, mods1, dh2, dx3, k_shift=0, n_x=n_x, name="l1_norm1_bwd",
                                              dres_rows=n_x, gate=(mlp0["branch"], mods0, 5))
    dmods1 = dmods1 + dm

    dxs1, dw1_0, dm, dmix0, (recv['l1_qkv_w'],), (recv['l1_mlp_w1'],), recv['l0_mlp_w2'] = _mlp_bwd(
        dxs2, dbranch0, mlp0, mods0, wfull['l0_mlp_w1'], wfull['l0_mlp_w2'], mix0, n_x=n_x, tm=tm_t, tag="l0",
        ride_dx2=[(dw_qkv, SHARDED['l1_qkv_w'])], ride_dw2=[(dw1_1, SHARDED['l1_mlp_w1'])])
    dmods0 = dmods0 + dm
    dy0, = _matmul(dmix0, wfull['l0_out_w'], mode="nt", tm=tm_t, tn=_div(2 * dc, (1024, 512)), tk=d,
                   out_dtypes=[F32], name="l0_out_dx")
    dw_out0, = _matmul(y0, dmix0, mode="tn", tm=_div(2 * dc, (1024, 512)), tn=_div(d, (1024, 512)),
                       tk=tkt, out_dtypes=[BF16], name="l0_out_dw")
    dz, dconv_w, dconv_b, dln_g, dln_b, dpool_w, dpool_scale, recv['l0_mlp_w1'] = _mixer0_bwd(
        dy0, z, cv, dsave, conv_w_full, _row8(l0_conv_ln_g), _row8(l0_conv_ln_b), pool_w_full,
        _row8(l0_pool_scale), n_tap=n_tap, n_x=n_x, name="l0_mixer_bwd",
        rider=Rider("exchange", [dw1_0], [SHARDED['l0_mlp_w1']]))
    dw_in0, recv['l0_out_w'] = _matmul(h0, dz, mode="tn", tm=_div(d, (1024, 512)),
                                       tn=_div(3 * dc, (1024, 768, 512, 384)), tk=tkt, out_dtypes=[BF16],
                                       name="l0_in_dw", rider=_exchange_of([(dw_out0, SHARDED['l0_out_w'])]))
    dh0, recv['l0_in_w'] = _matmul(dz, wfull['l0_in_w'], mode="nt", tm=tm_t, tn=_div(d, (1024, 512)), tk=3 * dc,
                                   out_dtypes=[F32], name="l0_in_dx",
                                   rider=_exchange_of([(dw_in0, SHARDED['l0_in_w'])]))
    dx0, dm = _rms_mod_bwd(xs0, mods0, dh0, dxs1, k_shift=0, n_x=n_x, name="l0_norm1_bwd", out_rows=n_x)
    dmods0 = dmods0 + dm
    grad_x = dx0[None]

    def dmod_rows(dm16):
        rows = dm16[:2 * n_mod].reshape(n_mod, 2, d).transpose(1, 0, 2).reshape(2, n_mod * d)
        return jnp.pad(rows, ((0, 6), (0, 0)))
    dm_gathered = _all_gather([jnp.concatenate([dmod_rows(dmods0), dmod_rows(dmods1)], axis=0)], [0],
                              name="gather_dmods")[0]
    dm0, dm1, db0, db1 = _assemble_dmods(dm_gathered, name="assemble_dmods")
    out_g = {'l0_ada_b': db0[0], 'l1_ada_b': db1[0]}
    ds_part = jnp.zeros((16, d), F32)
    for nm, lw, dmf in (('l0_ada_w', l0_ada_w, dm0), ('l1_ada_w', l1_ada_w, dm1)):
        dm_cols = lax.dynamic_slice_in_dim(dmf, me * ada_cols, ada_cols, axis=1)
        out_g[nm], = _matmul(s16, dm_cols, mode="tn", tm=_div(d, (1024, 512)),
                             tn=_div(ada_cols, (512, 384, 256, 128)), tk=16, out_dtypes=[F32], name=f"{nm}_dw")
        ds_part = ds_part + _matmul(dm_cols, lw, mode="nt", tm=16, tn=_div(d, (1024, 512)),
                                    tk=_div(ada_cols, (512, 384, 256, 128)), out_dtypes=[F32], name=f"{nm}_dx")[0]

    small = {'l0_conv_b': dconv_b[0], 'l0_conv_ln_g': dln_g[0], 'l0_conv_ln_b': dln_b[0],
             'l0_pool_scale': dpool_scale[0], 'l1_q_norm_g': dgq[0], 'l1_k_norm_g': dgk[0],
             'final_g': dfinal_g[0], 'dsilu_ctx': ds_part[N_DEV], 'l0_conv_w': dconv_w[:-1].reshape(-1),
             'l0_pool_w': dpool_w.reshape(-1)}
    flat = jnp.concatenate([small[nm] for nm in small])
    rows = -(-flat.shape[0] // 1024) * 8
    packed = jnp.pad(flat, (0, rows * 128 - flat.shape[0])).reshape(rows, 128)
    summed = _sum_slots(_all_gather([packed], [0], name="gather_small_grads")[0], rows,
                        name="sum_small_grads").reshape(-1)
    off = 0
    for nm in small:
        size = small[nm].shape[0]
        small[nm] = summed[off:off + size]
        off += size
    out_g['c_ctx'] = _silu_grad(_row8(c_ctx), _row8(small['dsilu_ctx']), name="c_ctx_grad")[0]
    for nm in REPLICATED_SMALL:
        out_g[nm] = small[nm]
    conv_cols = l0_conv_w.shape[1]
    out_g['l0_conv_w'] = lax.dynamic_slice_in_dim(small['l0_conv_w'].reshape(n_tap, dc), me * conv_cols, conv_cols,
                                                  axis=1)
    out_g['l0_pool_w'] = lax.dynamic_slice_in_dim(small['l0_pool_w'].reshape(n_grp, pg, pg), me * (pg // N_DEV),
                                                  pg // N_DEV, axis=1)

    delta, new_m, new_v = {}, {}, {}
    for nm in names:
        out_g[nm], delta[nm], new_m[nm], new_v[nm] = _adamw(p[nm], p['m_' + nm], p['v_' + nm], recv=recv[nm],
                                                            name=f"adamw_{nm}")
    for nm in ('l0_ada_w', 'l1_ada_w'):
        out_g[nm], delta[nm], new_m[nm], new_v[nm] = _adamw(p[nm], p['m_' + nm], p['v_' + nm], grad=out_g[nm],
                                                            name=f"adamw_{nm}")
    for nm in WEIGHTS:
        if nm in delta:
            continue
        shape = p[nm].shape
        as2d = lambda a: a.reshape(1, -1) if a.ndim == 1 else a.reshape(-1, a.shape[-1])
        res = _adamw(as2d(p[nm]), as2d(p['m_' + nm]), as2d(p['v_' + nm]), grad=as2d(out_g[nm]), name=f"adamw_{nm}")
        out_g[nm], delta[nm], new_m[nm], new_v[nm] = [r.reshape(shape) for r in res]

    return (loss, grad_x, *[out_g[nm] for nm in WEIGHTS], *[delta[nm] for nm in WEIGHTS],
            *[new_m[nm] for nm in WEIGHTS], *[new_v[nm] for nm in WEIGHTS])
```

```python
import functools

import jax
import jax.numpy as jnp
from jax import lax
from jax.experimental import pallas as pl
from jax.experimental.pallas import tpu as pltpu

F32 = jnp.float32
BF16 = jnp.bfloat16
N_DEV = 8
MESH_AXES = ("x", "y", "c")
EPS = 1e-6
HEAD_DIM = 128
POOL_WINDOWS = (2, 4, 8, 16)
GRID_W = 64
ROPE_THETA = 10000.0
ATTN_SCALE = HEAD_DIM ** -0.5
LOG2_E = 1.4426950408889634
Q_SCALE_LOG2 = ATTN_SCALE * LOG2_E
HALO = 16
ADAM_LR, ADAM_B1, ADAM_B2, ADAM_EPS, ADAM_WD, ADAM_STEP = 0.001, 0.9, 0.999, 1e-08, 0.01, 10
VMEM_CAP_MB = 60
ADAMW_BLOCK_ELEMS = 1 << 18
TOKEN_TK = (2816, 2048, 1408, 1024, 768, 640, 512, 128)
DEEP_TM = (384, 512, 256, 128)
DEEP_TN = (512,)
MESH_ID = pl.DeviceIdType.MESH


def _div(n, prefs):
    for p in prefs:
        if n % p == 0:
            return p
    raise ValueError(f"no tile for {n} in {prefs}")


def _params(sem, vmem_mb):
    return pltpu.CompilerParams(dimension_semantics=sem, vmem_limit_bytes=min(vmem_mb, VMEM_CAP_MB) << 20)


def _sigmoid(x):
    return 1.0 / (1.0 + jnp.exp(-x))


def _silu(x):
    return x * _sigmoid(x)


def _rms(x):
    return x * lax.rsqrt(jnp.mean(x * x, axis=-1, keepdims=True) + EPS)


def _rms_mod(x, shift, scale):
    return _rms(x) * (1.0 + scale) + shift


def _layernorm(x, g, b):
    mu = jnp.mean(x, axis=-1, keepdims=True)
    var = jnp.mean(jnp.square(x - mu), axis=-1, keepdims=True)
    return (x - mu) * lax.rsqrt(var + EPS) * g + b


def _ln_silu(x, g, b):
    return _silu(_layernorm(x, g, b))


def _matmul(a, b, *, mode, tm, tn, tk, out_dtypes, name, extras=(), epilogue=None, rider=None):
    if mode == "tn":
        kdim, m = a.shape
        n = b.shape[1]
    else:
        m, kdim = a.shape
        n = b.shape[0] if mode == "nt" else b.shape[1]
    assert m % tm == 0 and n % tn == 0 and kdim % tk == 0, (name, m, n, kdim, tm, tn, tk)
    nk = kdim // tk
    n_ex = len(extras)
    n_out = len(out_dtypes)

    def body(a_ref, b_ref, *rest):
        ex = rest[:n_ex]
        outs = rest[n_ex:n_ex + n_out]
        acc_ref = rest[n_ex + n_out] if nk > 1 else None
        k = pl.program_id(2)
        av = a_ref[...].astype(BF16)
        bv = b_ref[...].astype(BF16)
        dims = {"nn": ((1,), (0,)), "nt": ((1,), (1,)), "tn": ((0,), (0,))}[mode]
        part = lax.dot_general(av, bv, (dims, ((), ())), preferred_element_type=F32)

        rows = pl.program_id(0) * tm + lax.broadcasted_iota(jnp.int32, (tm, 1), 0)

        def finish(acc):
            res = (acc,) if epilogue is None else epilogue(acc, rows, *[e[...] for e in ex])
            for o, r in zip(outs, res):
                o[...] = r.astype(o.dtype)

        if nk == 1:
            finish(part)
        else:
            @pl.when(k == 0)
            def _():
                acc_ref[...] = part

            @pl.when(k > 0)
            def _():
                acc_ref[...] += part

            @pl.when(k == nk - 1)
            def _():
                finish(acc_ref[...])

    if mode == "tn":
        a_spec = pl.BlockSpec((tk, tm), lambda i, j, k: (k, i))
        b_spec = pl.BlockSpec((tk, tn), lambda i, j, k: (k, j))
    else:
        a_spec = pl.BlockSpec((tm, tk), lambda i, j, k: (i, k))
        b_spec = (pl.BlockSpec((tn, tk), lambda i, j, k: (j, k)) if mode == "nt"
                  else pl.BlockSpec((tk, tn), lambda i, j, k: (k, j)))
    ex_specs, ex_arrays, ex_bytes = [], [], 0
    for kind, arr in extras:
        ex_arrays.append(arr)
        if kind == "tile":
            ex_specs.append(pl.BlockSpec((tm, tn), lambda i, j, k: (i, j)))
            ex_bytes += tm * tn * arr.dtype.itemsize
        else:
            ex_specs.append(pl.BlockSpec((arr.shape[0], tn), lambda i, j, k: (0, j)))
            ex_bytes += arr.shape[0] * tn * 4
    blocks = tm * tk * a.dtype.itemsize + tk * tn * b.dtype.itemsize + ex_bytes
    blocks += sum(tm * tn * jnp.dtype(d).itemsize for d in out_dtypes)
    casts = sum(rows * cols * 2 for arr, rows, cols in ((a, tm, tk), (b, tk, tn)) if arr.dtype != BF16)
    vmem = (2 * blocks + 4 * tm * tn * 4 + casts) // (1 << 20) + 8
    return _pcall(
        body,
        grid=(m // tm, n // tn, nk),
        in_specs=[a_spec, b_spec] + ex_specs,
        out_specs=[pl.BlockSpec((tm, tn), lambda i, j, k: (i, j)) for _ in out_dtypes],
        out_shape=[jax.ShapeDtypeStruct((m, n), d) for d in out_dtypes],
        scratch_shapes=[pltpu.VMEM((tm, tn), F32)] if nk > 1 else [],
        semantics=("parallel", "parallel", "arbitrary"), vmem_mb=vmem, name=name,
        operands=[a, b, *ex_arrays], rider=rider)


def _seg_pick(vec, k, rows, n_x):
    return jnp.where(rows < n_x, vec[2 * k:2 * k + 1], vec[2 * k + 1:2 * k + 2])


def _zero_accs(i, accs):
    @pl.when(i == 0)
    def _():
        for a in accs:
            a[...] = jnp.zeros_like(a)


def _rms_mod_fwd(xs, mods, *, k_shift, n_x, name):
    t, d = xs.shape
    tm = _div(t, (256, 128))

    def body(x_ref, mods_ref, h_ref):
        seg = (pl.program_id(0) * tm >= n_x).astype(jnp.int32)
        shift = mods_ref[pl.ds(2 * k_shift + seg, 1), :]
        scale = mods_ref[pl.ds(2 * k_shift + 2 + seg, 1), :]
        h_ref[...] = _rms_mod(x_ref[...], shift, scale).astype(BF16)

    return pl.pallas_call(
        body, grid=(t // tm,),
        in_specs=[pl.BlockSpec((tm, d), lambda i: (i, 0)), pl.BlockSpec((16, d), lambda i: (0, 0))],
        out_specs=pl.BlockSpec((tm, d), lambda i: (i, 0)),
        out_shape=jax.ShapeDtypeStruct((t, d), BF16),
        compiler_params=_params(("parallel",), 32), name=name,
    )(xs, mods)


def _gate_part(dxv, seg, k_gate, br_ref, gmods_ref, db_ref, dgm_ref):
    r_gate = 2 * k_gate + seg
    db_ref[...] = (dxv * gmods_ref[pl.ds(r_gate, 1), :]).astype(BF16)
    dgm_ref[pl.ds(r_gate, 1), :] += jnp.sum(dxv * br_ref[...].astype(F32), axis=0, keepdims=True)


def _rms_mod_bwd(xs, mods, dh, dres, *, k_shift, n_x, name, rider=None, gate=None, dres_rows=None, out_rows=None):
    t, d = xs.shape
    tm = _div(t, (256, 128))
    n_gate = 2 if gate is not None else 0

    def body(x_ref, mods_ref, dh_ref, dres_ref, *rest):
        gate_in, (dx_ref, dmods_ref), gate_out = rest[:n_gate], rest[n_gate:n_gate + 2], rest[n_gate + 2:]
        i = pl.program_id(0)
        _zero_accs(i, [dmods_ref, *gate_out[1:]])
        seg = (i * tm >= n_x).astype(jnp.int32)
        r_shift = 2 * k_shift + seg
        r_scale = 2 * k_shift + 2 + seg
        shift = mods_ref[pl.ds(r_shift, 1), :]
        scale = mods_ref[pl.ds(r_scale, 1), :]
        _, vjp = jax.vjp(_rms_mod, x_ref[...], shift, scale)
        dx, dshift, dscale = vjp(dh_ref[...].astype(F32))
        dres_v = dres_ref[...]
        if dres_rows is not None:
            dres_v = dres_v * jnp.where(i * tm < dres_rows, 1.0, 0.0)
        dx = dres_v + dx
        if out_rows is None:
            dx_ref[...] = dx
        else:
            @pl.when(i * tm < out_rows)
            def _():
                dx_ref[...] = dx
        dmods_ref[pl.ds(r_shift, 1), :] += dshift
        dmods_ref[pl.ds(r_scale, 1), :] += dscale
        if gate is not None:
            _gate_part(dx, seg, gate[2], *gate_in, *gate_out)

    def clamped(rows):
        return pl.BlockSpec((tm, d), lambda i: (jnp.minimum(i, rows // tm - 1), 0))

    row = pl.BlockSpec((tm, d), lambda i: (i, 0))
    vec = pl.BlockSpec((16, d), lambda i: (0, 0))
    in_specs = [row, vec, row, row if dres_rows is None else clamped(dres_rows)]
    out_specs = [row if out_rows is None else clamped(out_rows), vec]
    out_shape = [jax.ShapeDtypeStruct((t if out_rows is None else out_rows, d), F32),
                 jax.ShapeDtypeStruct((16, d), F32)]
    operands = [xs, mods, dh, dres]
    if gate is not None:
        in_specs += [row, vec]
        out_specs += [row, vec]
        out_shape += [jax.ShapeDtypeStruct((t, d), BF16), jax.ShapeDtypeStruct((16, d), F32)]
        operands += [gate[0], gate[1]]
    return _pcall(body, grid=(t // tm,), in_specs=in_specs, out_specs=out_specs, out_shape=out_shape,
                  scratch_shapes=[], semantics=("arbitrary",), vmem_mb=48, name=name, operands=operands, rider=rider)


def _final_loss(xs, g, target, branch, gmods, *, k_gate, name):
    t, d = xs.shape
    tm = _div(t, (256, 128))

    def loss_fn(x, gv, tgt):
        err = _rms(x) * gv - tgt
        return 0.5 * jnp.sum(jnp.mean(jnp.square(err), axis=-1))

    def body(x_ref, g_ref, t_ref, br_ref, gmods_ref, dx_ref, loss_ref, dg_ref, db_ref, dgm_ref):
        i = pl.program_id(0)
        _zero_accs(i, [loss_ref, dg_ref, dgm_ref])
        val, vjp = jax.vjp(loss_fn, x_ref[...], g_ref[0:1, :], t_ref[...])
        dx, dg, _ = vjp(jnp.ones((), F32))
        dx_ref[...] = dx
        loss_ref[...] += val
        dg_ref[0:1, :] += dg
        _gate_part(dx, 0, k_gate, br_ref, gmods_ref, db_ref, dgm_ref)

    row = pl.BlockSpec((tm, d), lambda i: (i, 0))
    vec16 = pl.BlockSpec((16, d), lambda i: (0, 0))
    return pl.pallas_call(
        body, grid=(t // tm,),
        in_specs=[row, pl.BlockSpec((8, d), lambda i: (0, 0)), row, row, vec16],
        out_specs=[row, pl.BlockSpec((8, 128), lambda i: (0, 0)), pl.BlockSpec((8, d), lambda i: (0, 0)), row, vec16],
        out_shape=[jax.ShapeDtypeStruct((t, d), F32), jax.ShapeDtypeStruct((8, 128), F32),
                   jax.ShapeDtypeStruct((8, d), F32), jax.ShapeDtypeStruct((t, d), BF16),
                   jax.ShapeDtypeStruct((16, d), F32)],
        compiler_params=_params(("arbitrary",), 48), name=name,
    )(xs, g, target, branch, gmods)


def _halo_specs(r, width, col, t):
    h_per = r // HALO

    def prev(i):
        return (jnp.maximum(i * h_per - 1, 0), col)

    def nxt(i):
        return (jnp.minimum((i + 1) * h_per, t // HALO - 1), col)

    return (pl.BlockSpec((HALO, width), prev), pl.BlockSpec((r, width), lambda i: (i, col)),
            pl.BlockSpec((HALO, width), nxt))


def _seg_geometry(i, r, n_x, t):
    row0 = i * r
    in_ctx = row0 >= n_x
    first = jnp.logical_or(row0 == 0, row0 == n_x)
    last = jnp.logical_or(row0 + r == n_x, row0 + r == t)
    seg_start = jnp.where(in_ctx, n_x, 0)
    seg_len = jnp.where(in_ctx, t - n_x, n_x)
    return row0, first, last, seg_start, seg_len


ROW_CHUNK = 64
LANE_CHUNK = 128


def _chunks(width, rows, col0=0):
    lanes = min(LANE_CHUNK, width)
    return [(slice(col0 + c, col0 + c + lanes), r0) for c in range(0, width, lanes) for r0 in range(0, rows, ROW_CHUNK)]


def _pool_count(tpos, w, seg_len):
    return (jnp.minimum(tpos + w // 2, seg_len) - jnp.maximum(tpos - w // 2, 0)).astype(F32)


def _mixer0_fwd(z, conv_w, conv_b, ln_g, ln_b, pool_w, pool_scale, *, n_tap, n_x, name, rider=None):
    t, dc = z.shape[0], z.shape[1] // 3
    n_grp, pg = pool_w.shape[0], pool_w.shape[1]
    r = _div(t - n_x, (256, 128))
    assert n_x % r == 0 and pg * n_grp == dc
    half = n_tap // 2
    assert half < HALO and max(POOL_WINDOWS) // 2 <= HALO

    def body(ap, ac, an, gp, gc, gn, pp, pc, pn, w_ref, cb_ref, lg_ref, lb_ref, pw_ref, ps_ref,
             y_ref, cv_ref, d_ref, uwin, pwin):
        i = pl.program_id(0)
        row0, first, last, seg_start, seg_len = _seg_geometry(i, r, n_x, t)
        keep_prev = jnp.where(first, 0.0, 1.0)
        keep_next = jnp.where(last, 0.0, 1.0)
        uwin[0:HALO, :] = ap[...] * _sigmoid(gp[...]) * keep_prev
        uwin[HALO:HALO + r, :] = ac[...] * _sigmoid(gc[...])
        uwin[HALO + r:, :] = an[...] * _sigmoid(gn[...]) * keep_next
        pwin[0:HALO, :] = pp[...] * keep_prev
        pwin[HALO:HALO + r, :] = pc[...]
        pwin[HALO + r:, :] = pn[...] * keep_next
        for cols, r0 in _chunks(dc, r):
            acc = jnp.zeros((ROW_CHUNK, cols.stop - cols.start), F32) + cb_ref[0:1, cols]
            for k in range(n_tap):
                off = HALO - half + k + r0
                acc = acc + w_ref[k:k + 1, cols] * uwin[off:off + ROW_CHUNK, cols]
            cv_ref[r0:r0 + ROW_CHUNK, cols] = acc
        y_ref[:, 0:dc] = _ln_silu(cv_ref[...], lg_ref[0:1, :], lb_ref[0:1, :]).astype(BF16)
        tpos = row0 - seg_start + lax.broadcasted_iota(jnp.int32, (r, 1), 0)
        for g, w in enumerate(POOL_WINDOWS):
            cnt = _pool_count(tpos, w, seg_len)
            for cols, r0 in _chunks(pg, r, g * pg):
                s = jnp.zeros((ROW_CHUNK, cols.stop - cols.start), F32)
                for j in range(-(w // 2), w // 2):
                    s = s + pwin[HALO + j + r0:HALO + j + r0 + ROW_CHUNK, cols]
                diff = s / cnt[r0:r0 + ROW_CHUNK] - pwin[HALO + r0:HALO + r0 + ROW_CHUNK, cols]
                d_ref[r0:r0 + ROW_CHUNK, cols] = diff.astype(BF16)
            cols = slice(g * pg, (g + 1) * pg)
            pm = jnp.dot(d_ref[:, cols], pw_ref[g].astype(BF16), preferred_element_type=F32)
            y_ref[:, dc + g * pg:dc + (g + 1) * pg] = (pm * ps_ref[0:1, cols]).astype(BF16)

    vec = lambda rows, width: pl.BlockSpec((rows, width), lambda i: (0, 0))
    in_specs = [*_halo_specs(r, dc, 0, t), *_halo_specs(r, dc, 1, t), *_halo_specs(r, dc, 2, t),
                vec(conv_w.shape[0], dc), vec(8, dc), vec(8, dc), vec(8, dc),
                pl.BlockSpec((n_grp, pg, pg), lambda i: (0, 0, 0)), vec(8, dc)]
    return _pcall(
        body, grid=(t // r,), in_specs=in_specs,
        out_specs=[pl.BlockSpec((r, 2 * dc), lambda i: (i, 0)), pl.BlockSpec((r, dc), lambda i: (i, 0)),
                   pl.BlockSpec((r, dc), lambda i: (i, 0))],
        out_shape=[jax.ShapeDtypeStruct((t, 2 * dc), BF16), jax.ShapeDtypeStruct((t, dc), F32),
                   jax.ShapeDtypeStruct((t, dc), BF16)],
        scratch_shapes=[pltpu.VMEM((r + 2 * HALO, dc), F32), pltpu.VMEM((r + 2 * HALO, dc), F32)],
        semantics=("parallel",), vmem_mb=40, name=name,
        operands=[*([z] * 9), conv_w, conv_b, ln_g, ln_b, pool_w, pool_scale], rider=rider)


def _mixer0_bwd(dy, z, cv, dsave, conv_w, ln_g, ln_b, pool_w, pool_scale, *, n_tap, n_x, name, rider=None):
    t, dc = cv.shape
    n_grp, pg = pool_w.shape[0], pool_w.shape[1]
    r = _div(t - n_x, (256, 128))
    half = n_tap // 2
    dyp_, zp_, cvp_ = dy, z, cv
    rw = r + 2 * HALO

    def body(dcp, dcc, dcn, dpp, dpc, dpn, cvp, cvc, cvn, ap, ac, an, gp, gc, gn, d_ref,
             w_ref, lg_ref, lb_ref, pw_ref, ps_ref,
             dz_ref, dw_ref, dcb_ref, dlg_ref, dlb_ref, dpw_ref, dps_ref, uwin, dcvwin, ewin, ddwin):
        i = pl.program_id(0)
        _zero_accs(i, [dw_ref, dcb_ref, dlg_ref, dlb_ref, dpw_ref, dps_ref])
        row0, first, last, seg_start, seg_len = _seg_geometry(i, r, n_x, t)
        keep_prev = jnp.where(first, 0.0, 1.0)
        keep_next = jnp.where(last, 0.0, 1.0)
        lg, lb = lg_ref[0:1, :], lb_ref[0:1, :]
        _, vjp = jax.vjp(_ln_silu, cvc[...], lg, lb)
        dcv, dlg, dlb = vjp(dcc[...])
        dlg_ref[0:1, :] += dlg
        dlb_ref[0:1, :] += dlb
        dcb_ref[0:1, :] += jnp.sum(dcv, axis=0, keepdims=True)
        dcvwin[HALO:HALO + r, :] = dcv
        for halo_cv, halo_dy, keep, lo in ((cvp, dcp, keep_prev, 0), (cvn, dcn, keep_next, HALO + r)):
            _, vjp_h = jax.vjp(lambda v: _ln_silu(v, lg, lb), halo_cv[...])
            dcvwin[lo:lo + HALO, :] = vjp_h(halo_dy[...])[0] * keep
        uwin[0:HALO, :] = ap[...] * _sigmoid(gp[...]) * keep_prev
        uwin[HALO:HALO + r, :] = ac[...] * _sigmoid(gc[...])
        uwin[HALO + r:, :] = an[...] * _sigmoid(gn[...]) * keep_next
        for cols, r0 in _chunks(dc, r):
            du = jnp.zeros((ROW_CHUNK, cols.stop - cols.start), F32)
            for k in range(n_tap):
                off = HALO + half - k + r0
                du = du + w_ref[k:k + 1, cols] * dcvwin[off:off + ROW_CHUNK, cols]
            rows = slice(r0, r0 + ROW_CHUNK)
            sig = _sigmoid(gc[rows, cols])
            dz_ref[rows, cols] = (du * sig).astype(BF16)
            dz_ref[rows, dc + cols.start:dc + cols.stop] = (du * ac[rows, cols] * sig * (1.0 - sig)).astype(BF16)
        for c0 in range(0, dc, LANE_CHUNK):
            cols = slice(c0, c0 + LANE_CHUNK)
            taps = [jnp.zeros((8, LANE_CHUNK), F32) for _ in range(n_tap)]
            for r0 in range(0, r, ROW_CHUNK):
                dcv_c = dcvwin[HALO + r0:HALO + r0 + ROW_CHUNK, cols]
                for k in range(n_tap):
                    off = HALO - half + k + r0
                    prod = dcv_c * uwin[off:off + ROW_CHUNK, cols]
                    taps[k] = taps[k] + functools.reduce(
                        jnp.add, [prod[8 * s:8 * s + 8] for s in range(ROW_CHUNK // 8)])
            for k in range(n_tap):
                dw_ref[k:k + 1, cols] += jnp.sum(taps[k], axis=0, keepdims=True)
        twin = row0 - seg_start - HALO + lax.broadcasted_iota(jnp.int32, (rw, 1), 0)
        for g, w in enumerate(POOL_WINDOWS):
            cols = slice(g * pg, (g + 1) * pg)
            wg = pw_ref[g].astype(BF16)
            scale = ps_ref[0:1, cols]
            dyp_c = dpc[:, cols]
            dpm_win = jnp.concatenate([dpp[:, cols] * keep_prev, dyp_c, dpn[:, cols] * keep_next], axis=0) * scale
            dd_win = lax.dot_general(dpm_win.astype(BF16), wg, (((1,), (1,)), ((), ())), preferred_element_type=F32)
            cnt = jnp.maximum(_pool_count(twin, w, seg_len), 1.0)
            ddwin[:, cols] = dd_win
            ewin[:, cols] = dd_win / cnt
            for ccols, r0 in _chunks(pg, r, g * pg):
                dup = -ddwin[HALO + r0:HALO + r0 + ROW_CHUNK, ccols]
                for j in range(-(w // 2) + 1, w // 2 + 1):
                    dup = dup + ewin[HALO + j + r0:HALO + j + r0 + ROW_CHUNK, ccols]
                dz_ref[r0:r0 + ROW_CHUNK, 2 * dc + ccols.start:2 * dc + ccols.stop] = dup.astype(BF16)
            dsv = d_ref[:, cols]
            pm = jnp.dot(dsv, wg, preferred_element_type=F32)
            dps_ref[0:1, cols] += jnp.sum(dyp_c * pm, axis=0, keepdims=True)
            dpw_ref[g] += lax.dot_general(dsv, (dyp_c * scale).astype(BF16), (((0,), (0,)), ((), ())),
                                          preferred_element_type=F32)

    vec = lambda rows, width: pl.BlockSpec((rows, width), lambda i: (0, 0))
    grp = pl.BlockSpec((n_grp, pg, pg), lambda i: (0, 0, 0))
    in_specs = [*_halo_specs(r, dc, 0, t), *_halo_specs(r, dc, 1, t), *_halo_specs(r, dc, 0, t),
                *_halo_specs(r, dc, 0, t), *_halo_specs(r, dc, 1, t), pl.BlockSpec((r, dc), lambda i: (i, 0)),
                vec(conv_w.shape[0], dc), vec(8, dc), vec(8, dc), grp, vec(8, dc)]
    return _pcall(
        body, grid=(t // r,), in_specs=in_specs,
        out_specs=[pl.BlockSpec((r, 3 * dc), lambda i: (i, 0)), vec(conv_w.shape[0], dc), vec(8, dc), vec(8, dc),
                   vec(8, dc), grp, vec(8, dc)],
        out_shape=[jax.ShapeDtypeStruct((t, 3 * dc), BF16), jax.ShapeDtypeStruct(conv_w.shape, F32),
                   jax.ShapeDtypeStruct((8, dc), F32), jax.ShapeDtypeStruct((8, dc), F32),
                   jax.ShapeDtypeStruct((8, dc), F32), jax.ShapeDtypeStruct(pool_w.shape, F32),
                   jax.ShapeDtypeStruct((8, dc), F32)],
        scratch_shapes=[pltpu.VMEM((rw, dc), F32)] * 4,
        semantics=("arbitrary",), vmem_mb=VMEM_CAP_MB, name=name,
        operands=[dyp_, dyp_, dyp_, dyp_, dyp_, dyp_, cvp_, cvp_, cvp_, zp_, zp_, zp_, zp_, zp_, zp_, dsave,
                  conv_w, ln_g, ln_b, pool_w, pool_scale], rider=rider)


def _swap_halves(x):
    lane = lax.broadcasted_iota(jnp.int32, x.shape, 1)
    quarter = HEAD_DIM // 4
    return jnp.where(lane % (2 * quarter) < quarter,
                     pltpu.roll(x, HEAD_DIM - quarter, 1), pltpu.roll(x, quarter, 1))


def _rope_tables(n_x, n_ctx):
    half = HEAD_DIM // 4
    freqs = ROPE_THETA ** (-jnp.arange(half, dtype=F32) / half)
    tok = jnp.arange(n_x)
    row = (tok // GRID_W).astype(F32)[:, None] * freqs[None, :]
    col = (tok % GRID_W).astype(F32)[:, None] * freqs[None, :]
    cos = jnp.concatenate([jnp.cos(row), jnp.cos(row), jnp.cos(col), jnp.cos(col)], axis=1)
    sin = jnp.concatenate([-jnp.sin(row), jnp.sin(row), -jnp.sin(col), jnp.sin(col)], axis=1)
    cos = jnp.concatenate([cos, jnp.ones((n_ctx, HEAD_DIM), F32)], axis=0)
    sin = jnp.concatenate([sin, jnp.zeros((n_ctx, HEAD_DIM), F32)], axis=0)
    return cos, sin


def _norm_g(x, g):
    return _rms(x) * g


def _qk_prep_fwd(qkv, gq, gk, cos, sin, *, d_q, name):
    t, width = qkv.shape
    d_kv = (width - d_q) // 2
    tm = _div(t, (256, 128))

    def body(qkv_ref, gq_ref, gk_ref, cos_ref, sin_ref, q_ref, k_ref, v_ref):
        cs, sn = cos_ref[...], sin_ref[...]
        for h in range((d_q + d_kv) // HEAD_DIM):
            g = gq_ref[0:1, :] if h * HEAD_DIM < d_q else gk_ref[0:1, :]
            xn = _norm_g(qkv_ref[:, h * HEAD_DIM:(h + 1) * HEAD_DIM], g)
            rot = xn * cs + _swap_halves(xn) * sn
            if h * HEAD_DIM < d_q:
                q_ref[:, h * HEAD_DIM:(h + 1) * HEAD_DIM] = (rot * Q_SCALE_LOG2).astype(BF16)
            else:
                k_ref[:, h * HEAD_DIM - d_q:(h + 1) * HEAD_DIM - d_q] = rot.astype(BF16)
        v_ref[...] = qkv_ref[:, d_q + d_kv:].astype(BF16)

    row = lambda w: pl.BlockSpec((tm, w), lambda i: (i, 0))
    vec = pl.BlockSpec((8, HEAD_DIM), lambda i: (0, 0))
    return pl.pallas_call(
        body, grid=(t // tm,),
        in_specs=[row(width), vec, vec, row(HEAD_DIM), row(HEAD_DIM)],
        out_specs=[row(d_q), row(d_kv), row(d_kv)],
        out_shape=[jax.ShapeDtypeStruct((t, d_q), BF16), jax.ShapeDtypeStruct((t, d_kv), BF16),
                   jax.ShapeDtypeStruct((t, d_kv), BF16)],
        compiler_params=_params(("parallel",), 32), name=name,
    )(qkv, gq, gk, cos, sin)


def _qk_prep_bwd(qkv, dq, dk, dv, gq, gk, cos, sin, *, n_x, name):
    t, width = qkv.shape
    d_q, d_kv = dq.shape[1], dk.shape[1]
    tm = _div(t, (256, 128))
    last_q = n_x // tm - 1

    def body(qkv_ref, dq_ref, dk_ref, dv_ref, gq_ref, gk_ref, cos_ref, sin_ref, out_ref, dgq_ref, dgk_ref):
        i = pl.program_id(0)
        _zero_accs(i, [dgq_ref, dgk_ref])
        is_x = jnp.where(i * tm < n_x, 1.0, 0.0)
        cs, sn = cos_ref[...], sin_ref[...]
        for h in range((d_q + d_kv) // HEAD_DIM):
            sl = slice(h * HEAD_DIM, (h + 1) * HEAD_DIM)
            if h * HEAD_DIM < d_q:
                g, dg_ref, dr = gq_ref[0:1, :], dgq_ref, dq_ref[:, sl] * is_x
            else:
                g, dg_ref = gk_ref[0:1, :], dgk_ref
                dr = dk_ref[:, h * HEAD_DIM - d_q:(h + 1) * HEAD_DIM - d_q]
            dxn = dr * cs + _swap_halves(dr * sn)
            _, vjp = jax.vjp(_norm_g, qkv_ref[:, sl], g)
            dx, dg = vjp(dxn)
            out_ref[:, sl] = dx.astype(BF16)
            dg_ref[0:1, :] += dg
        out_ref[:, d_q + d_kv:] = dv_ref[...].astype(BF16)

    row = lambda w: pl.BlockSpec((tm, w), lambda i: (i, 0))
    vec = pl.BlockSpec((8, HEAD_DIM), lambda i: (0, 0))
    return pl.pallas_call(
        body, grid=(t // tm,),
        in_specs=[row(width), pl.BlockSpec((tm, d_q), lambda i: (jnp.minimum(i, last_q), 0)), row(d_kv), row(d_kv),
                  vec, vec, row(HEAD_DIM), row(HEAD_DIM)],
        out_specs=[row(width), vec, vec],
        out_shape=[jax.ShapeDtypeStruct((t, width), BF16), jax.ShapeDtypeStruct((8, HEAD_DIM), F32),
                   jax.ShapeDtypeStruct((8, HEAD_DIM), F32)],
        compiler_params=_params(("arbitrary",), 40), name=name,
    )(qkv, dq, dk, dv, gq, gk, cos, sin)


ATTN_TQ = (256, 128)
ATTN_TK = (768, 640, 512, 384, 256, 128)


def _attention_fwd(q, k, v_t, *, n_x, name):
    t, d_kv = k.shape
    d_q = q.shape[1]
    kvh = d_kv // HEAD_DIM
    grp = d_q // d_kv
    tq = _div(n_x, ATTN_TQ)
    tk = _div(t, ATTN_TK)
    gw = grp * HEAD_DIM
    n_kv = t // tk
    n_pair = (n_kv - 1) // 2

    def fold8(x):
        return functools.reduce(jnp.add, [x[8 * r:8 * r + 8] for r in range(tk // 8)])

    def body(q_ref, k_ref, vt_ref, o_ref, lse_ref, m_ref, l_ref, acc_ref, qt_ref, s_even, s_odd):
        m_ref[...] = jnp.full_like(m_ref, -jnp.inf)
        l_ref[...] = jnp.zeros_like(l_ref)
        acc_ref[...] = jnp.zeros_like(acc_ref)
        for g in range(grp):
            qt_ref[g] = q_ref[:, g * HEAD_DIM:(g + 1) * HEAD_DIM].T

        def keys(j):
            return pl.ds(pl.multiple_of(j * tk, tk), tk)

        def scores(g, kc):
            return jnp.dot(kc, qt_ref[g], preferred_element_type=F32)

        def chunk(j, s_cur, s_next):
            vt = vt_ref[:, keys(j)]
            kn = k_ref[keys(j + 1), :] if s_next is not None else None
            for g in range(grp):
                s = s_cur[g]
                m_old = m_ref[g]
                m_new = jnp.maximum(m_old, jnp.max(s, axis=0, keepdims=True))
                alpha = jnp.exp2(m_old - m_new)
                p = jnp.exp2(s - m_new)
                if s_next is not None:
                    s_next[g] = scores(g, kn)
                l_ref[g] = alpha * l_ref[g] + fold8(p)
                acc_ref[g] = alpha * acc_ref[g] + jnp.dot(vt, p.astype(BF16), preferred_element_type=F32)
                m_ref[g] = m_new

        k0 = k_ref[keys(0), :]
        for g in range(grp):
            s_even[g] = scores(g, k0)

        def pair(i, carry):
            chunk(2 * i, s_even, s_odd)
            chunk(2 * i + 1, s_odd, s_even)
            return carry

        lax.fori_loop(0, n_pair, pair, 0)
        if n_kv - 2 * n_pair == 2:
            chunk(n_kv - 2, s_even, s_odd)
            chunk(n_kv - 1, s_odd, None)
        else:
            chunk(n_kv - 1, s_even, None)
        for g in range(grp):
            l = jnp.sum(l_ref[g], axis=0, keepdims=True)
            o_ref[:, g * HEAD_DIM:(g + 1) * HEAD_DIM] = (acc_ref[g] / l).T.astype(BF16)
            lse_row = m_ref[g] + jnp.log(l) * LOG2_E
            lse_ref[:, g:g + 1] = jnp.broadcast_to(lse_row, (HEAD_DIM, tq)).T[:, 0:1]

    return pl.pallas_call(
        body, grid=(kvh, n_x // tq),
        in_specs=[pl.BlockSpec((tq, gw), lambda h, i: (i, h)),
                  pl.BlockSpec((t, HEAD_DIM), lambda h, i: (0, h)),
                  pl.BlockSpec((HEAD_DIM, t), lambda h, i: (h, 0))],
        out_specs=[pl.BlockSpec((tq, gw), lambda h, i: (i, h)),
                   pl.BlockSpec((None, tq, grp), lambda h, i: (h, i, 0))],
        out_shape=[jax.ShapeDtypeStruct((n_x, d_q), BF16), jax.ShapeDtypeStruct((kvh, n_x, grp), F32)],
        scratch_shapes=[pltpu.VMEM((grp, 1, tq), F32), pltpu.VMEM((grp, 8, tq), F32),
                        pltpu.VMEM((grp, HEAD_DIM, tq), F32), pltpu.VMEM((grp, HEAD_DIM, tq), BF16),
                        pltpu.VMEM((grp, tk, tq), F32), pltpu.VMEM((grp, tk, tq), F32)],
        compiler_params=_params(("parallel", "arbitrary"), 48), name=name,
    )(q, k, v_t)


def _attention_bwd(q, k, v, o, lse, do, *, n_x, name):
    t, d_kv = k.shape
    d_q = q.shape[1]
    kvh = d_kv // HEAD_DIM
    grp = d_q // d_kv
    tq = _div(n_x, ATTN_TQ)
    tk = _div(t, ATTN_TK)
    gw = grp * HEAD_DIM
    n_q = n_x // tq

    def body(q_ref, k_ref, v_ref, o_ref, lse_ref, do_ref, dq_ref, dkt_ref, dvt_ref, dq_acc, lse_s, delta_s, qt_s,
             dot_s):
        i = pl.program_id(1)
        _zero_accs(i, [dkt_ref, dvt_ref])
        dq_acc[...] = jnp.zeros_like(dq_acc)
        for g in range(grp):
            sl = slice(g * HEAD_DIM, (g + 1) * HEAD_DIM)
            lse_s[g] = lse_ref[:, g:g + 1]
            delta_s[g] = jnp.sum(do_ref[:, sl].astype(F32) * o_ref[:, sl].astype(F32), axis=-1, keepdims=True)
            qt_s[g] = q_ref[:, sl].T
            dot_s[g] = do_ref[:, sl].T

        def step(j, carry):
            start = pl.multiple_of(j * tk, tk)
            kc, vc = k_ref[pl.ds(start, tk), :], v_ref[pl.ds(start, tk), :]
            dkt_part = jnp.zeros((HEAD_DIM, tk), F32)
            dvt_part = jnp.zeros((HEAD_DIM, tk), F32)
            for g in range(grp):
                sl = slice(g * HEAD_DIM, (g + 1) * HEAD_DIM)
                s = lax.dot_general(q_ref[:, sl], kc, (((1,), (1,)), ((), ())), preferred_element_type=F32)
                p = jnp.exp2(s - lse_s[g])
                dp = lax.dot_general(do_ref[:, sl], vc, (((1,), (1,)), ((), ())), preferred_element_type=F32)
                ds = (p * (dp - delta_s[g])).astype(BF16)
                dq_acc[g] += jnp.dot(ds, kc, preferred_element_type=F32)
                dvt_part = dvt_part + jnp.dot(dot_s[g], p.astype(BF16), preferred_element_type=F32)
                dkt_part = dkt_part + jnp.dot(qt_s[g], ds, preferred_element_type=F32)
            dvt_ref[:, pl.ds(start, tk)] += dvt_part
            dkt_ref[:, pl.ds(start, tk)] += dkt_part
            return carry

        lax.fori_loop(0, t // tk, step, 0)
        for g in range(grp):
            dq_ref[:, g * HEAD_DIM:(g + 1) * HEAD_DIM] = dq_acc[g] * ATTN_SCALE

        @pl.when(i == n_q - 1)
        def _():
            dkt_ref[...] = dkt_ref[...] * (1.0 / LOG2_E)

    qspec = pl.BlockSpec((tq, gw), lambda h, i: (i, h))
    kspec = pl.BlockSpec((t, HEAD_DIM), lambda h, i: (0, h))
    ktspec = pl.BlockSpec((HEAD_DIM, t), lambda h, i: (h, 0))
    return pl.pallas_call(
        body, grid=(kvh, n_q),
        in_specs=[qspec, kspec, kspec, qspec, pl.BlockSpec((None, tq, grp), lambda h, i: (h, i, 0)), qspec],
        out_specs=[qspec, ktspec, ktspec],
        out_shape=[jax.ShapeDtypeStruct((n_x, d_q), F32), jax.ShapeDtypeStruct((d_kv, t), F32),
                   jax.ShapeDtypeStruct((d_kv, t), F32)],
        scratch_shapes=[pltpu.VMEM((grp, tq, HEAD_DIM), F32), pltpu.VMEM((grp, tq, 1), F32),
                        pltpu.VMEM((grp, tq, 1), F32), pltpu.VMEM((grp, HEAD_DIM, tq), BF16),
                        pltpu.VMEM((grp, HEAD_DIM, tq), BF16)],
        compiler_params=_params(("parallel", "arbitrary"), 56), name=name,
    )(q, k, v, o, lse, do)


def _whole(body, ins, out_shapes, name):
    return pl.pallas_call(
        body, out_shape=[jax.ShapeDtypeStruct(s, d) for s, d in out_shapes],
        compiler_params=pltpu.CompilerParams(vmem_limit_bytes=40 << 20), name=name)(*ins)


def _silu_rows(x, *, name):
    def body(x_ref, o_ref):
        o_ref[...] = _silu(x_ref[...])
    return _whole(body, [x], [(x.shape, F32)], name)[0]


def _assemble_dmods(gathered, *, name):
    width = gathered.shape[1]

    def body(g_ref, dm0, dm1, db0, db1):
        for l, (dm, db) in enumerate(((dm0, db0), (dm1, db1))):
            ctx = jnp.zeros((1, width), F32)
            tot = jnp.zeros((1, width), F32)
            for q in range(N_DEV):
                row = g_ref[16 * q + 8 * l:16 * q + 8 * l + 1, :]
                dm[q:q + 1, :] = row
                tot = tot + row
                ctx = ctx + g_ref[16 * q + 8 * l + 1:16 * q + 8 * l + 2, :]
            dm[N_DEV:N_DEV + 1, :] = ctx
            dm[N_DEV + 1:, :] = jnp.zeros((16 - N_DEV - 1, width), F32)
            db[...] = jnp.zeros_like(db)
            db[0:1, :] = tot + ctx

    return _whole(body, [gathered], [((16, width), F32), ((16, width), F32), ((8, width), F32), ((8, width), F32)],
                  name)


def _sum_slots(gathered, rows, *, name):
    def body(g_ref, o_ref):
        acc = g_ref[0:rows, :]
        for q in range(1, N_DEV):
            acc = acc + g_ref[q * rows:(q + 1) * rows, :]
        o_ref[...] = acc
    return _whole(body, [gathered], [((rows, gathered.shape[1]), F32)], name)[0]


def _silu_grad(x, dy, *, name):
    def body(x_ref, dy_ref, o_ref):
        _, vjp = jax.vjp(_silu, x_ref[...])
        o_ref[...] = vjp(dy_ref[...])[0]
    return _whole(body, [x, dy], [(x.shape, F32)], name)[0]


def _adamw(w, m, v, *, name, recv=None, grad=None):
    rows, cols = w.shape
    budget = max(8, ADAMW_BLOCK_ELEMS // cols)
    tr = _div(rows, [c for c in (512, 256, 128, 64, 32, 16, 8) if c <= budget] + [rows])
    c1 = 1.0 - ADAM_B1 ** ADAM_STEP
    c2 = 1.0 - ADAM_B2 ** ADAM_STEP

    def body(w_ref, m_ref, v_ref, g_in, g_ref, d_ref, nm_ref, nv_ref):
        if recv is not None:
            g = g_in[0].astype(F32)
            for q in range(1, N_DEV):
                g = g + g_in[q].astype(F32)
        else:
            g = g_in[...]
        nm = ADAM_B1 * m_ref[...] + (1.0 - ADAM_B1) * g
        nv = ADAM_B2 * v_ref[...] + (1.0 - ADAM_B2) * jnp.square(g)
        g_ref[...] = g
        nm_ref[...] = nm
        nv_ref[...] = nv
        d_ref[...] = -ADAM_LR * ((nm / c1) / (jnp.sqrt(nv / c2) + ADAM_EPS) + ADAM_WD * w_ref[...])

    blk = pl.BlockSpec((tr, cols), lambda i: (i, 0))
    g_spec = pl.BlockSpec((N_DEV, tr, cols), lambda i: (0, i, 0)) if recv is not None else blk
    return pl.pallas_call(
        body, grid=(rows // tr,), in_specs=[blk, blk, blk, g_spec], out_specs=[blk] * 4,
        out_shape=[jax.ShapeDtypeStruct((rows, cols), F32)] * 4,
        compiler_params=_params(("parallel",), 48), name=name,
    )(w, m, v, recv if recv is not None else grad)


def _position():
    return tuple(lax.axis_index(a) for a in MESH_AXES)


def _linear(pos):
    return 4 * pos[0] + 2 * pos[1] + pos[2]


def _window(ref, axis, dev, size):
    start = pl.multiple_of(dev * size, size)
    return ref.at[pl.ds(start, size), :] if axis == 0 else ref.at[:, pl.ds(start, size)]


def _all_gather(shards, axes, *, name):
    n = len(shards)
    sizes = [s.shape[ax] for s, ax in zip(shards, axes)]

    def body(*refs):
        src, dst = refs[:n], refs[n:2 * n]
        send_sems, recv_sems, local_sems = refs[2 * n:]
        x, y, c = _position()
        me, sibling = (x, y, c), (x, y, 1 - c)
        chips = [(1 - x, y), (x, 1 - y), (1 - x, 1 - y)]

        def win(k, pos):
            return _window(dst[k], axes[k], _linear(pos), sizes[k])

        def copy(k, sem, block, to, from_src=False):
            return pltpu.make_async_remote_copy(
                src_ref=src[k] if from_src else win(k, block), dst_ref=win(k, block),
                send_sem=send_sems.at[k, sem], recv_sem=recv_sems.at[k, sem],
                device_id=to, device_id_type=MESH_ID)

        mine = [pltpu.make_async_copy(src[k], win(k, me), local_sems.at[k]) for k in range(n)]
        for cp in mine:
            cp.start()
        first = []
        for k in range(n):
            first.append(copy(k, 0, me, sibling, from_src=True))
            first += [copy(k, 1 + j, me, (*chip, c), from_src=True) for j, chip in enumerate(chips)]
        for cp in first:
            cp.start()
        passed = []
        for j, chip in enumerate(chips):
            for k in range(n):
                copy(k, 1 + j, (*chip, c), me).wait_recv()
                fwd = copy(k, 4 + j, (*chip, c), sibling)
                fwd.start()
                passed.append(fwd)
        for k in range(n):
            copy(k, 0, sibling, me).wait_recv()
            for j, chip in enumerate(chips):
                copy(k, 4 + j, (*chip, 1 - c), me).wait_recv()
        for cp in first + passed:
            cp.wait_send()
        for cp in mine:
            cp.wait()

    out_shape = []
    for s, ax in zip(shards, axes):
        full = (s.shape[0] * N_DEV, s.shape[1]) if ax == 0 else (s.shape[0], s.shape[1] * N_DEV)
        out_shape.append(jax.ShapeDtypeStruct(full, s.dtype))
    any_spec = pl.BlockSpec(memory_space=pl.ANY)
    return pl.pallas_call(
        body, in_specs=[any_spec] * n, out_specs=[any_spec] * n, out_shape=out_shape,
        scratch_shapes=[pltpu.SemaphoreType.DMA((n, 7)), pltpu.SemaphoreType.DMA((n, 7)),
                        pltpu.SemaphoreType.DMA((n,))],
        name=name,
    )(*shards)


class Rider:
    def __init__(self, kind, arrays, axes):
        self.kind, self.arrays, self.axes = kind, list(arrays), list(axes)
        self.n = len(self.arrays)
        if kind == "gather":
            self.sizes = [a.shape[ax] for a, ax in zip(self.arrays, self.axes)]
        else:
            self.sizes = [a.shape[ax] // N_DEV for a, ax in zip(self.arrays, self.axes)]

    def out_shape(self):
        shapes = []
        for a, ax, sz in zip(self.arrays, self.axes, self.sizes):
            if self.kind == "gather":
                full = (sz * N_DEV, a.shape[1]) if ax == 0 else (a.shape[0], sz * N_DEV)
                shapes.append(jax.ShapeDtypeStruct(full, a.dtype))
            else:
                shard = (sz, a.shape[1]) if ax == 0 else (a.shape[0], sz)
                shapes.append(jax.ShapeDtypeStruct((N_DEV, *shard), a.dtype))
        return shapes

    def scratch(self):
        return [pltpu.SemaphoreType.DMA((self.n, N_DEV - 1)), pltpu.SemaphoreType.DMA((self.n, N_DEV - 1)),
                pltpu.SemaphoreType.DMA((self.n,))]

    def plan(self, src, dst, send_sems, recv_sems, local_sems):
        x, y, c = me = _position()
        mine = _linear(me)

        def remote(k, sem, src_ref, dst_ref, to):
            return pltpu.make_async_remote_copy(
                src_ref=src_ref, dst_ref=dst_ref, send_sem=send_sems.at[k, sem], recv_sem=recv_sems.at[k, sem],
                device_id=to, device_id_type=MESH_ID)

        ph = dict(local=[], start=[], mid_wait=[], mid_start=[], end_wait=[])
        for k in range(self.n):
            ax, sz = self.axes[k], self.sizes[k]
            if self.kind == "exchange":
                own_src, own_dst = _window(src[k], ax, mine, sz), dst[k].at[mine]
                ph["local"].append(pltpu.make_async_copy(own_src, own_dst, local_sems.at[k]))
                for mask in range(1, N_DEV):
                    to = tuple(1 - p if (mask >> (2 - b)) & 1 else p for b, p in enumerate(me))
                    ph["start"].append(remote(k, mask - 1, _window(src[k], ax, _linear(to), sz), own_dst, to))
                    ph["end_wait"].append(remote(k, mask - 1, own_src, dst[k].at[_linear(to)], to))
            else:
                def win(pos, k=k, ax=ax, sz=sz):
                    return _window(dst[k], ax, _linear(pos), sz)
                sibling = (x, y, 1 - c)
                chips = [(1 - x, y), (x, 1 - y), (1 - x, 1 - y)]
                ph["local"].append(pltpu.make_async_copy(src[k], win(me), local_sems.at[k]))
                ph["start"].append(remote(k, 0, src[k], win(me), sibling))
                ph["end_wait"].append(remote(k, 0, src[k], win(sibling), sibling))
                for j, chip in enumerate(chips):
                    ph["start"].append(remote(k, 1 + j, src[k], win(me), (*chip, c)))
                    ph["mid_wait"].append(remote(k, 1 + j, src[k], win((*chip, c)), (*chip, c)))
                    ph["mid_start"].append(remote(k, 4 + j, win((*chip, c)), win((*chip, c)), sibling))
                    ph["end_wait"].append(remote(k, 4 + j, src[k], win((*chip, 1 - c)), sibling))
        return ph


def _pcall(body, *, grid, in_specs, out_specs, out_shape, scratch_shapes, semantics, vmem_mb, name, operands,
           rider=None):
    if rider is None:
        return pl.pallas_call(body, grid=grid, in_specs=in_specs, out_specs=out_specs, out_shape=out_shape,
                              scratch_shapes=scratch_shapes, compiler_params=_params(semantics, vmem_mb),
                              name=name)(*operands)
    n_in, n_out, n_scr, n = len(in_specs), len(out_specs), len(scratch_shapes), rider.n

    def wrapped(*refs):
        ins, src = refs[:n_in], refs[n_in:n_in + n]
        outs = refs[n_in + n:n_in + n + n_out]
        dst = refs[n_in + n + n_out:n_in + 2 * n + n_out]
        rest = refs[n_in + 2 * n + n_out:]
        scratch, sems = rest[:n_scr], rest[n_scr:]
        step = functools.reduce(lambda acc, ig: acc * ig[1] + ig[0],
                                [(pl.program_id(dim), g) for dim, g in enumerate(grid)], 0)
        n_steps = functools.reduce(lambda a, b: a * b, grid)

        @pl.when(step == 0)
        def _():
            ph = rider.plan(src, dst, *sems)
            for cp in ph["local"] + ph["start"]:
                cp.start()

        body(*ins, *outs, *scratch)

        @pl.when(step == (n_steps * 5) // 8)
        def _():
            ph = rider.plan(src, dst, *sems)
            for cp in ph["mid_wait"]:
                cp.wait_recv()
            for cp in ph["mid_start"]:
                cp.start()

        @pl.when(step == n_steps - 1)
        def _():
            ph = rider.plan(src, dst, *sems)
            for cp in ph["end_wait"]:
                cp.wait_recv()
            for cp in ph["start"] + ph["mid_start"]:
                cp.wait_send()
            for cp in ph["local"]:
                cp.wait()

    any_spec = pl.BlockSpec(memory_space=pl.ANY)
    return pl.pallas_call(
        wrapped, grid=grid, in_specs=list(in_specs) + [any_spec] * n, out_specs=list(out_specs) + [any_spec] * n,
        out_shape=list(out_shape) + rider.out_shape(), scratch_shapes=list(scratch_shapes) + rider.scratch(),
        compiler_params=_params(("arbitrary",) * len(grid), vmem_mb), name=name,
    )(*operands, *rider.arrays)


WEIGHTS = ['c_ctx', 'l0_ada_w', 'l0_ada_b', 'l0_in_w', 'l0_conv_w', 'l0_conv_b', 'l0_conv_ln_g', 'l0_conv_ln_b',
           'l0_pool_w', 'l0_pool_scale', 'l0_out_w', 'l0_mlp_w1', 'l0_mlp_w2', 'l1_ada_w', 'l1_ada_b', 'l1_qkv_w',
           'l1_q_norm_g', 'l1_k_norm_g', 'l1_out_w', 'l1_mlp_w1', 'l1_mlp_w2', 'final_g']
SHARDED = {'l0_in_w': 1, 'l0_out_w': 0, 'l0_mlp_w1': 1, 'l0_mlp_w2': 0,
           'l1_qkv_w': 1, 'l1_out_w': 0, 'l1_mlp_w1': 1, 'l1_mlp_w2': 0}
REPLICATED_SMALL = ['l0_conv_b', 'l0_conv_ln_g', 'l0_conv_ln_b', 'l0_pool_scale', 'l1_q_norm_g', 'l1_k_norm_g',
                    'final_g']


def _row8(v):
    v = v.reshape(1, -1)
    return jnp.pad(v, ((0, 7), (0, 0)))


def _mods16(full, me):
    d = full.shape[1] // 6
    mine = lax.dynamic_slice_in_dim(full, me, 1, axis=0).reshape(6, d)
    ctx = full[N_DEV].reshape(6, d)
    return jnp.pad(jnp.stack([mine, ctx], axis=1).reshape(12, d), ((0, 4), (0, 0)))


def _mlp_fwd(xs, mods, w1, w2, *, n_x, tm, tag, rider1=None, rider2=None):
    t, d = xs.shape
    dff = w1.shape[1]
    h = _rms_mod_fwd(xs, mods, k_shift=3, n_x=n_x, name=f"{tag}_norm2")
    pre, act, *ride1 = _matmul(h, w1, mode="nn", tm=tm, tn=_div(dff, (1024, 512)), tk=d, out_dtypes=[BF16, BF16],
                               name=f"{tag}_mlp1", rider=rider1,
                               epilogue=lambda acc, rows: (acc, jnp.square(jnp.maximum(acc, 0.0))))
    if w2 is None:
        w2 = ride1[-1]
    xo, branch, *ride2 = _matmul(
        act, w2, mode="nn", tm=_div(t, DEEP_TM), tn=_div(d, DEEP_TN), tk=dff, out_dtypes=[F32, BF16],
        name=f"{tag}_mlp2", extras=[("tile", xs), ("vec", mods)], rider=rider2,
        epilogue=lambda acc, rows, res, mv: (res + _seg_pick(mv, 5, rows, n_x) * acc, acc))
    return xo, dict(h=h, pre=pre, act=act, branch=branch, x_in=xs), ride1, ride2


def _exchange_of(items):
    return Rider("exchange", [a for a, _ in items], [ax for _, ax in items]) if items else None


def _mlp_bwd(dxo, dbranch, saved, mods, w1, w2, mixer_branch, *, n_x, tm, tag, ride_dx2=(), ride_dw2=()):
    t, d = dxo.shape
    dff = w1.shape[1]
    tkt = _div(t, TOKEN_TK)
    dpre, *recv_a = _matmul(dbranch, w2, mode="nt", tm=tm, tn=_div(dff, (1024, 512)), tk=d, out_dtypes=[BF16],
                            name=f"{tag}_mlp2_dx", extras=[("tile", saved["pre"])],
                            rider=_exchange_of(list(ride_dx2)),
                            epilogue=lambda acc, rows, pre: (acc * 2.0 * jnp.maximum(pre.astype(F32), 0.0),))
    dw2, *recv_b = _matmul(saved["act"], dbranch, mode="tn", tm=_div(dff, (1024, 512)), tn=_div(d, (1024, 512)),
                           tk=tkt, out_dtypes=[BF16], name=f"{tag}_mlp2_dw", rider=_exchange_of(list(ride_dw2)))
    dh, = _matmul(dpre, w1, mode="nt", tm=_div(t, DEEP_TM), tn=_div(d, DEEP_TN), tk=dff,
                  out_dtypes=[F32], name=f"{tag}_mlp1_dx")
    dw1, recv_dw2 = _matmul(saved["h"], dpre, mode="tn", tm=_div(d, (1024, 512)), tn=_div(dff, (1024, 512)),
                            tk=tkt, out_dtypes=[BF16], name=f"{tag}_mlp1_dw", rider=_exchange_of([(dw2, 0)]))
    dx, dm_norm, dmix, dm_gate = _rms_mod_bwd(saved["x_in"], mods, dh, dxo, k_shift=3, n_x=n_x,
                                              name=f"{tag}_norm2_bwd", gate=(mixer_branch, mods, 2))
    return dx, dw1, dm_norm + dm_gate, dmix, recv_a, recv_b, recv_dw2


def kernel(x, c, ctx, c_ctx, l0_ada_w, l0_ada_b, l0_in_w, l0_conv_w, l0_conv_b, l0_conv_ln_g, l0_conv_ln_b, l0_pool_w, l0_pool_scale, l0_out_w, l0_mlp_w1, l0_mlp_w2, l1_ada_w, l1_ada_b, l1_qkv_w, l1_q_norm_g, l1_k_norm_g, l1_out_w, l1_mlp_w1, l1_mlp_w2, final_g, loss_target, m_c_ctx, m_l0_ada_w, m_l0_ada_b, m_l0_in_w, m_l0_conv_w, m_l0_conv_b, m_l0_conv_ln_g, m_l0_conv_ln_b, m_l0_pool_w, m_l0_pool_scale, m_l0_out_w, m_l0_mlp_w1, m_l0_mlp_w2, m_l1_ada_w, m_l1_ada_b, m_l1_qkv_w, m_l1_q_norm_g, m_l1_k_norm_g, m_l1_out_w, m_l1_mlp_w1, m_l1_mlp_w2, m_final_g, v_c_ctx, v_l0_ada_w, v_l0_ada_b, v_l0_in_w, v_l0_conv_w, v_l0_conv_b, v_l0_conv_ln_g, v_l0_conv_ln_b, v_l0_pool_w, v_l0_pool_scale, v_l0_out_w, v_l0_mlp_w1, v_l0_mlp_w2, v_l1_ada_w, v_l1_ada_b, v_l1_qkv_w, v_l1_q_norm_g, v_l1_k_norm_g, v_l1_out_w, v_l1_mlp_w1, v_l1_mlp_w2, v_final_g):
    p = dict(locals())
    me = _linear(_position())
    n_x, d = x.shape[1], x.shape[2]
    n_ctx = ctx.shape[1]
    t = n_x + n_ctx
    dc = l0_conv_b.shape[0]
    n_tap = l0_conv_w.shape[0]
    d_q = d
    n_mod = l0_ada_b.shape[0] // d
    ada_cols = l0_ada_w.shape[1]
    tm_t = _div(t, (768, 640, 512, 128))
    tm_x = _div(n_x, (1024, 512))

    names = list(SHARDED)
    shard16 = {nm: p[nm].astype(BF16) for nm in names}

    def gather_of(*nms):
        return Rider("gather", [shard16[nm] for nm in nms], [SHARDED[nm] for nm in nms])

    wfull = {}
    first = ['l0_in_w', 'l0_out_w']
    *full, conv_w_full, pool_w_full = _all_gather(
        [shard16[nm] for nm in first] + [jnp.pad(l0_conv_w, ((0, 1), (0, 0))),
                                         l0_pool_w.reshape(-1, l0_pool_w.shape[2])],
        [SHARDED[nm] for nm in first] + [1, 0], name="gather_first_weights")
    wfull.update(zip(first, full))
    n_grp, pg = l0_pool_w.shape[0], l0_pool_w.shape[2]
    pool_w_full = pool_w_full.reshape(N_DEV, n_grp, pg // N_DEV, pg).transpose(1, 0, 2, 3).reshape(n_grp, pg, pg)

    c_all = _all_gather([_row8(c)], [0], name="gather_cond")[0].reshape(N_DEV, 8, d)[:, 0]
    cond = jnp.concatenate([c_all, c_ctx.reshape(1, d), jnp.zeros((16 - N_DEV - 1, d), F32)], axis=0)
    s16 = _silu_rows(cond, name="cond_silu")
    mod_shards = []
    for li, (lw, lb) in enumerate(((l0_ada_w, l0_ada_b), (l1_ada_w, l1_ada_b))):
        bias = _row8(lax.dynamic_slice_in_dim(lb, me * ada_cols, ada_cols))
        mod_shards.append(_matmul(s16, lw, mode="nn", tm=16, tn=_div(ada_cols, (512, 384, 256, 128)), tk=d,
                                  out_dtypes=[F32], name=f"l{li}_ada_fwd", extras=[("vec", bias)],
                                  epilogue=lambda acc, rows, b: (acc + b[0:1],))[0])
    mods_full = _all_gather([jnp.concatenate(mod_shards, axis=0)], [1], name="gather_mods")[0]
    mods0, mods1 = _mods16(mods_full[:16], me), _mods16(mods_full[16:], me)

    xs0 = jnp.concatenate([x[0], ctx[0]], axis=0)
    h0 = _rms_mod_fwd(xs0, mods0, k_shift=0, n_x=n_x, name="l0_norm1")
    z, wfull['l0_mlp_w1'] = _matmul(h0, wfull['l0_in_w'], mode="nn", tm=tm_t, tn=_div(3 * dc, (1024, 768, 512, 384)),
                                    tk=d, out_dtypes=[F32], name="l0_in_proj", rider=gather_of('l0_mlp_w1'))
    y0, cv, dsave, wfull['l0_mlp_w2'] = _mixer0_fwd(
        z, conv_w_full, _row8(l0_conv_b), _row8(l0_conv_ln_g), _row8(l0_conv_ln_b), pool_w_full,
        _row8(l0_pool_scale), n_tap=n_tap, n_x=n_x, name="l0_mixer", rider=gather_of('l0_mlp_w2'))
    xs1, mix0 = _matmul(y0, wfull['l0_out_w'], mode="nn", tm=tm_t, tn=_div(d, (1024, 512)), tk=2 * dc,
                        out_dtypes=[F32, BF16], name="l0_out_proj", extras=[("tile", xs0), ("vec", mods0)],
                        epilogue=lambda acc, rows, res, mv: (res + _seg_pick(mv, 2, rows, n_x) * acc, acc))
    xs2, mlp0, ride1, ride2 = _mlp_fwd(
        xs1, mods0, wfull['l0_mlp_w1'], wfull['l0_mlp_w2'], n_x=n_x, tm=tm_t, tag="l0",
        rider1=gather_of('l1_qkv_w', 'l1_out_w'), rider2=gather_of('l1_mlp_w1'))
    wfull['l1_qkv_w'], wfull['l1_out_w'] = ride1
    wfull['l1_mlp_w1'], = ride2

    h2 = _rms_mod_fwd(xs2, mods1, k_shift=0, n_x=n_x, name="l1_norm1")
    qkv, = _matmul(h2, wfull['l1_qkv_w'], mode="nn", tm=tm_t, tn=_div(l1_qkv_w.shape[1] * N_DEV, (1024, 768, 512)),
                   tk=d, out_dtypes=[F32], name="l1_qkv_proj")
    cos, sin = _rope_tables(n_x, n_ctx)
    gq, gk = _row8(l1_q_norm_g), _row8(l1_k_norm_g)
    q, k, v = _qk_prep_fwd(qkv, gq, gk, cos, sin, d_q=d_q, name="l1_qk_prep")
    o, lse = _attention_fwd(q, k, v.T, n_x=n_x, name="l1_attention")
    x3, mix1 = _matmul(o, wfull['l1_out_w'], mode="nn", tm=tm_x, tn=_div(d, (1024, 512)), tk=d_q,
                       out_dtypes=[F32, BF16], name="l1_out_proj", extras=[("tile", xs2), ("vec", mods1)],
                       epilogue=lambda acc, rows, res, mv: (res + mv[4:5] * acc, acc))
    x4, mlp1, (wfull['l1_mlp_w2'],), _ = _mlp_fwd(x3, mods1, wfull['l1_mlp_w1'], None, n_x=n_x, tm=tm_x, tag="l1",
                                                  rider1=gather_of('l1_mlp_w2'))

    dx4, loss_part, dfinal_g, dbranch1, dmods1 = _final_loss(
        x4, _row8(final_g), loss_target[0], mlp1["branch"], mods1, k_gate=5, name="loss_head")
    loss = lax.psum(loss_part[0, 0], MESH_AXES)

    recv = {}
    dx3, dw1_1, dm, dmix1, _, _, recv['l1_mlp_w2'] = _mlp_bwd(
        dx4, dbranch1, mlp1, mods1, wfull['l1_mlp_w1'], wfull['l1_mlp_w2'], mix1, n_x=n_x, tm=tm_x, tag="l1")
    dmods1 = dmods1 + dm
    do, = _matmul(dmix1, wfull['l1_out_w'], mode="nt", tm=tm_x, tn=_div(d_q, (1024, 512)), tk=d,
                  out_dtypes=[BF16], name="l1_out_dx")
    dw_out1, = _matmul(o, dmix1, mode="tn", tm=_div(d_q, (1024, 512)), tn=_div(d, (1024, 512)),
                       tk=_div(n_x, TOKEN_TK), out_dtypes=[BF16], name="l1_out_dw")
    dq, dk_t, dv_t = _attention_bwd(q, k, v, o, lse, do, n_x=n_x, name="l1_attention_bwd")
    dk, dv = dk_t.T, dv_t.T
    dqkv, dgq, dgk = _qk_prep_bwd(qkv, dq, dk, dv, gq, gk, cos, sin, n_x=n_x, name="l1_qk_prep_bwd")
    tkt = _div(t, TOKEN_TK)
    dh2, recv['l1_out_w'] = _matmul(dqkv, wfull['l1_qkv_w'], mode="nt", tm=tm_t, tn=_div(d, (1024, 512)),
                                    tk=dqkv.shape[1], out_dtypes=[F32], name="l1_qkv_dx",
                                    rider=_exchange_of([(dw_out1, SHARDED['l1_out_w'])]))
    dw_qkv, = _matmul(h2, dqkv, mode="tn", tm=_div(d, (1024, 512)), tn=_div(dqkv.shape[1], (1024, 768, 512)),
                      tk=tkt, out_dtypes=[BF16], name="l1_qkv_dw")
    dxs2, dm, dbranch0, dmods0 = _rms_mod_bwd(xs2, mods1, dh2, dx3, k_shift=0, n_x=n_x, name="l1_norm1_bwd",
                                              dres_rows=n_x, gate=(mlp0["branch"], mods0, 5))
    dmods1 = dmods1 + dm

    dxs1, dw1_0, dm, dmix0, (recv['l1_qkv_w'],), (recv['l1_mlp_w1'],), recv['l0_mlp_w2'] = _mlp_bwd(
        dxs2, dbranch0, mlp0, mods0, wfull['l0_mlp_w1'], wfull['l0_mlp_w2'], mix0, n_x=n_x, tm=tm_t, tag="l0",
        ride_dx2=[(dw_qkv, SHARDED['l1_qkv_w'])], ride_dw2=[(dw1_1, SHARDED['l1_mlp_w1'])])
    dmods0 = dmods0 + dm
    dy0, = _matmul(dmix0, wfull['l0_out_w'], mode="nt", tm=tm_t, tn=_div(2 * dc, (1024, 512)), tk=d,
                   out_dtypes=[F32], name="l0_out_dx")
    dw_out0, = _matmul(y0, dmix0, mode="tn", tm=_div(2 * dc, (1024, 512)), tn=_div(d, (1024, 512)),
                       tk=tkt, out_dtypes=[BF16], name="l0_out_dw")
    dz, dconv_w, dconv_b, dln_g, dln_b, dpool_w, dpool_scale, recv['l0_mlp_w1'] = _mixer0_bwd(
        dy0, z, cv, dsave, conv_w_full, _row8(l0_conv_ln_g), _row8(l0_conv_ln_b), pool_w_full,
        _row8(l0_pool_scale), n_tap=n_tap, n_x=n_x, name="l0_mixer_bwd",
        rider=Rider("exchange", [dw1_0], [SHARDED['l0_mlp_w1']]))
    dw_in0, recv['l0_out_w'] = _matmul(h0, dz, mode="tn", tm=_div(d, (1024, 512)),
                                       tn=_div(3 * dc, (1024, 768, 512, 384)), tk=tkt, out_dtypes=[BF16],
                                       name="l0_in_dw", rider=_exchange_of([(dw_out0, SHARDED['l0_out_w'])]))
    dh0, recv['l0_in_w'] = _matmul(dz, wfull['l0_in_w'], mode="nt", tm=tm_t, tn=_div(d, (1024, 512)), tk=3 * dc,
                                   out_dtypes=[F32], name="l0_in_dx",
                                   rider=_exchange_of([(dw_in0, SHARDED['l0_in_w'])]))
    dx0, dm = _rms_mod_bwd(xs0, mods0, dh0, dxs1, k_shift=0, n_x=n_x, name="l0_norm1_bwd", out_rows=n_x)
    dmods0 = dmods0 + dm
    grad_x = dx0[None]

    def dmod_rows(dm16):
        rows = dm16[:2 * n_mod].reshape(n_mod, 2, d).transpose(1, 0, 2).reshape(2, n_mod * d)
        return jnp.pad(rows, ((0, 6), (0, 0)))
    dm_gathered = _all_gather([jnp.concatenate([dmod_rows(dmods0), dmod_rows(dmods1)], axis=0)], [0],
                              name="gather_dmods")[0]
    dm0, dm1, db0, db1 = _assemble_dmods(dm_gathered, name="assemble_dmods")
    out_g = {'l0_ada_b': db0[0], 'l1_ada_b': db1[0]}
    ds_part = jnp.zeros((16, d), F32)
    for nm, lw, dmf in (('l0_ada_w', l0_ada_w, dm0), ('l1_ada_w', l1_ada_w, dm1)):
        dm_cols = lax.dynamic_slice_in_dim(dmf, me * ada_cols, ada_cols, axis=1)
        out_g[nm], = _matmul(s16, dm_cols, mode="tn", tm=_div(d, (1024, 512)),
                             tn=_div(ada_cols, (512, 384, 256, 128)), tk=16, out_dtypes=[F32], name=f"{nm}_dw")
        ds_part = ds_part + _matmul(dm_cols, lw, mode="nt", tm=16, tn=_div(d, (1024, 512)),
                                    tk=_div(ada_cols, (512, 384, 256, 128)), out_dtypes=[F32], name=f"{nm}_dx")[0]

    small = {'l0_conv_b': dconv_b[0], 'l0_conv_ln_g': dln_g[0], 'l0_conv_ln_b': dln_b[0],
             'l0_pool_scale': dpool_scale[0], 'l1_q_norm_g': dgq[0], 'l1_k_norm_g': dgk[0],
             'final_g': dfinal_g[0], 'dsilu_ctx': ds_part[N_DEV], 'l0_conv_w': dconv_w[:-1].reshape(-1),
             'l0_pool_w': dpool_w.reshape(-1)}
    flat = jnp.concatenate([small[nm] for nm in small])
    rows = -(-flat.shape[0] // 1024) * 8
    packed = jnp.pad(flat, (0, rows * 128 - flat.shape[0])).reshape(rows, 128)
    summed = _sum_slots(_all_gather([packed], [0], name="gather_small_grads")[0], rows,
                        name="sum_small_grads").reshape(-1)
    off = 0
    for nm in small:
        size = small[nm].shape[0]
        small[nm] = summed[off:off + size]
        off += size
    out_g['c_ctx'] = _silu_grad(_row8(c_ctx), _row8(small['dsilu_ctx']), name="c_ctx_grad")[0]
    for nm in REPLICATED_SMALL:
        out_g[nm] = small[nm]
    conv_cols = l0_conv_w.shape[1]
    out_g['l0_conv_w'] = lax.dynamic_slice_in_dim(small['l0_conv_w'].reshape(n_tap, dc), me * conv_cols, conv_cols,
                                                  axis=1)
    out_g['l0_pool_w'] = lax.dynamic_slice_in_dim(small['l0_pool_w'].reshape(n_grp, pg, pg), me * (pg // N_DEV),
                                                  pg // N_DEV, axis=1)

    delta, new_m, new_v = {}, {}, {}
    for nm in names:
        out_g[nm], delta[nm], new_m[nm], new_v[nm] = _adamw(p[nm], p['m_' + nm], p['v_' + nm], recv=recv[nm],
                                                            name=f"adamw_{nm}")
    for nm in ('l0_ada_w', 'l1_ada_w'):
        out_g[nm], delta[nm], new_m[nm], new_v[nm] = _adamw(p[nm], p['m_' + nm], p['v_' + nm], grad=out_g[nm],
                                                            name=f"adamw_{nm}")
    for nm in WEIGHTS:
        if nm in delta:
            continue
        shape = p[nm].shape
        as2d = lambda a: a.reshape(1, -1) if a.ndim == 1 else a.reshape(-1, a.shape[-1])
        res = _adamw(as2d(p[nm]), as2d(p['m_' + nm]), as2d(p['v_' + nm]), grad=as2d(out_g[nm]), name=f"adamw_{nm}")
        out_g[nm], delta[nm], new_m[nm], new_v[nm] = [r.reshape(shape) for r in res]

    return (loss, grad_x, *[out_g[nm] for nm in WEIGHTS], *[delta[nm] for nm in WEIGHTS],
            *[new_m[nm] for nm in WEIGHTS], *[new_v[nm] for nm in WEIGHTS])
```

```python
import functools

import jax
import jax.numpy as jnp
from jax import lax
from jax.experimental import pallas as pl
from jax.experimental.pallas import tpu as pltpu

F32 = jnp.float32
BF16 = jnp.bfloat16
N_DEV = 8
MESH_AXES = ("x", "y", "c")
EPS = 1e-6
HEAD_DIM = 128
POOL_WINDOWS = (2, 4, 8, 16)
GRID_W = 64
ROPE_THETA = 10000.0
ATTN_SCALE = HEAD_DIM ** -0.5
LOG2_E = 1.4426950408889634
Q_SCALE_LOG2 = ATTN_SCALE * LOG2_E
HALO = 16
ADAM_LR, ADAM_B1, ADAM_B2, ADAM_EPS, ADAM_WD, ADAM_STEP = 0.001, 0.9, 0.999, 1e-08, 0.01, 10
VMEM_CAP_MB = 60
ADAMW_BLOCK_ELEMS = 1 << 18
TOKEN_TK = (2816, 2048, 1408, 1024, 768, 640, 512, 128)
DEEP_TM = (384, 512, 256, 128)
DEEP_TN = (512,)
WIDE_TN = (2048, 1024, 512)
WIDE_TILE_ELEMS = 768 * 2048
MESH_ID = pl.DeviceIdType.MESH


def _div(n, prefs):
    for p in prefs:
        if n % p == 0:
            return p
    raise ValueError(f"no tile for {n} in {prefs}")


def _params(sem, vmem_mb):
    return pltpu.CompilerParams(dimension_semantics=sem, vmem_limit_bytes=min(vmem_mb, VMEM_CAP_MB) << 20)


def _sigmoid(x):
    return 1.0 / (1.0 + jnp.exp(-x))


def _silu(x):
    return x * _sigmoid(x)


def _rms(x):
    return x * lax.rsqrt(jnp.mean(x * x, axis=-1, keepdims=True) + EPS)


def _rms_mod(x, shift, scale):
    return _rms(x) * (1.0 + scale) + shift


def _layernorm(x, g, b):
    mu = jnp.mean(x, axis=-1, keepdims=True)
    var = jnp.mean(jnp.square(x - mu), axis=-1, keepdims=True)
    return (x - mu) * lax.rsqrt(var + EPS) * g + b


def _ln_silu(x, g, b):
    return _silu(_layernorm(x, g, b))


def _matmul(a, b, *, mode, tm, tn, tk, out_dtypes, name, extras=(), epilogue=None, rider=None):
    if mode == "tn":
        kdim, m = a.shape
        n = b.shape[1]
    else:
        m, kdim = a.shape
        n = b.shape[0] if mode == "nt" else b.shape[1]
    assert m % tm == 0 and n % tn == 0 and kdim % tk == 0, (name, m, n, kdim, tm, tn, tk)
    nk = kdim // tk
    n_ex = len(extras)
    n_out = len(out_dtypes)

    def body(a_ref, b_ref, *rest):
        ex = rest[:n_ex]
        outs = rest[n_ex:n_ex + n_out]
        acc_ref = rest[n_ex + n_out] if nk > 1 else None
        k = pl.program_id(2)
        av = a_ref[...].astype(BF16)
        bv = b_ref[...].astype(BF16)
        dims = {"nn": ((1,), (0,)), "nt": ((1,), (1,)), "tn": ((0,), (0,))}[mode]
        part = lax.dot_general(av, bv, (dims, ((), ())), preferred_element_type=F32)

        rows = pl.program_id(0) * tm + lax.broadcasted_iota(jnp.int32, (tm, 1), 0)

        def finish(acc):
            res = (acc,) if epilogue is None else epilogue(acc, rows, *[e[...] for e in ex])
            for o, r in zip(outs, res):
                o[...] = r.astype(o.dtype)

        if nk == 1:
            finish(part)
        else:
            @pl.when(k == 0)
            def _():
                acc_ref[...] = part

            @pl.when(k > 0)
            def _():
                acc_ref[...] += part

            @pl.when(k == nk - 1)
            def _():
                finish(acc_ref[...])

    if mode == "tn":
        a_spec = pl.BlockSpec((tk, tm), lambda i, j, k: (k, i))
        b_spec = pl.BlockSpec((tk, tn), lambda i, j, k: (k, j))
    else:
        a_spec = pl.BlockSpec((tm, tk), lambda i, j, k: (i, k))
        b_spec = (pl.BlockSpec((tn, tk), lambda i, j, k: (j, k)) if mode == "nt"
                  else pl.BlockSpec((tk, tn), lambda i, j, k: (k, j)))
    ex_specs, ex_arrays, ex_bytes = [], [], 0
    for kind, arr in extras:
        ex_arrays.append(arr)
        if kind == "tile":
            ex_specs.append(pl.BlockSpec((tm, tn), lambda i, j, k: (i, j)))
            ex_bytes += tm * tn * arr.dtype.itemsize
        else:
            ex_specs.append(pl.BlockSpec((arr.shape[0], tn), lambda i, j, k: (0, j)))
            ex_bytes += arr.shape[0] * tn * 4
    blocks = tm * tk * a.dtype.itemsize + tk * tn * b.dtype.itemsize + ex_bytes
    blocks += sum(tm * tn * jnp.dtype(d).itemsize for d in out_dtypes)
    casts = sum(rows * cols * 2 for arr, rows, cols in ((a, tm, tk), (b, tk, tn)) if arr.dtype != BF16)
    vmem = (2 * blocks + 4 * tm * tn * 4 + casts) // (1 << 20) + 8
    return _pcall(
        body,
        grid=(m // tm, n // tn, nk),
        in_specs=[a_spec, b_spec] + ex_specs,
        out_specs=[pl.BlockSpec((tm, tn), lambda i, j, k: (i, j)) for _ in out_dtypes],
        out_shape=[jax.ShapeDtypeStruct((m, n), d) for d in out_dtypes],
        scratch_shapes=[pltpu.VMEM((tm, tn), F32)] if nk > 1 else [],
        semantics=("parallel", "parallel", "arbitrary"), vmem_mb=vmem, name=name,
        operands=[a, b, *ex_arrays], rider=rider)


def _seg_pick(vec, k, rows, n_x):
    return jnp.where(rows < n_x, vec[2 * k:2 * k + 1], vec[2 * k + 1:2 * k + 2])


def _zero_accs(i, accs):
    @pl.when(i == 0)
    def _():
        for a in accs:
            a[...] = jnp.zeros_like(a)


def _rms_mod_fwd(xs, mods, *, k_shift, n_x, name, rider=None):
    t, d = xs.shape
    tm = _div(t, (256, 128))

    def body(x_ref, mods_ref, h_ref):
        seg = (pl.program_id(0) * tm >= n_x).astype(jnp.int32)
        shift = mods_ref[pl.ds(2 * k_shift + seg, 1), :]
        scale = mods_ref[pl.ds(2 * k_shift + 2 + seg, 1), :]
        h_ref[...] = _rms_mod(x_ref[...], shift, scale).astype(BF16)

    res = _pcall(
        body, grid=(t // tm,),
        in_specs=[pl.BlockSpec((tm, d), lambda i: (i, 0)), pl.BlockSpec((16, d), lambda i: (0, 0))],
        out_specs=[pl.BlockSpec((tm, d), lambda i: (i, 0))],
        out_shape=[jax.ShapeDtypeStruct((t, d), BF16)],
        scratch_shapes=[], semantics=("parallel",), vmem_mb=32, name=name, operands=[xs, mods], rider=rider)
    return res[0] if rider is None else res


def _gate_part(dxv, seg, k_gate, br_ref, gmods_ref, db_ref, dgm_ref):
    r_gate = 2 * k_gate + seg
    db_ref[...] = (dxv * gmods_ref[pl.ds(r_gate, 1), :]).astype(BF16)
    dgm_ref[pl.ds(r_gate, 1), :] += jnp.sum(dxv * br_ref[...].astype(F32), axis=0, keepdims=True)


def _rms_mod_bwd(xs, mods, dh, dres, *, k_shift, n_x, name, rider=None, gate=None, dres_rows=None, out_rows=None):
    t, d = xs.shape
    tm = _div(t, (256, 128))
    n_gate = 2 if gate is not None else 0

    def body(x_ref, mods_ref, dh_ref, dres_ref, *rest):
        gate_in, (dx_ref, dmods_ref), gate_out = rest[:n_gate], rest[n_gate:n_gate + 2], rest[n_gate + 2:]
        i = pl.program_id(0)
        _zero_accs(i, [dmods_ref, *gate_out[1:]])
        seg = (i * tm >= n_x).astype(jnp.int32)
        r_shift = 2 * k_shift + seg
        r_scale = 2 * k_shift + 2 + seg
        shift = mods_ref[pl.ds(r_shift, 1), :]
        scale = mods_ref[pl.ds(r_scale, 1), :]
        _, vjp = jax.vjp(_rms_mod, x_ref[...], shift, scale)
        dx, dshift, dscale = vjp(dh_ref[...].astype(F32))
        dres_v = dres_ref[...]
        if dres_rows is not None:
            dres_v = dres_v * jnp.where(i * tm < dres_rows, 1.0, 0.0)
        dx = dres_v + dx
        if out_rows is None:
            dx_ref[...] = dx
        else:
            @pl.when(i * tm < out_rows)
            def _():
                dx_ref[...] = dx
        dmods_ref[pl.ds(r_shift, 1), :] += dshift
        dmods_ref[pl.ds(r_scale, 1), :] += dscale
        if gate is not None:
            _gate_part(dx, seg, gate[2], *gate_in, *gate_out)

    def clamped(rows):
        return pl.BlockSpec((tm, d), lambda i: (jnp.minimum(i, rows // tm - 1), 0))

    row = pl.BlockSpec((tm, d), lambda i: (i, 0))
    vec = pl.BlockSpec((16, d), lambda i: (0, 0))
    in_specs = [row, vec, row, row if dres_rows is None else clamped(dres_rows)]
    out_specs = [row if out_rows is None else clamped(out_rows), vec]
    out_shape = [jax.ShapeDtypeStruct((t if out_rows is None else out_rows, d), F32),
                 jax.ShapeDtypeStruct((16, d), F32)]
    operands = [xs, mods, dh, dres]
    if gate is not None:
        in_specs += [row, vec]
        out_specs += [row, vec]
        out_shape += [jax.ShapeDtypeStruct((t, d), BF16), jax.ShapeDtypeStruct((16, d), F32)]
        operands += [gate[0], gate[1]]
    return _pcall(body, grid=(t // tm,), in_specs=in_specs, out_specs=out_specs, out_shape=out_shape,
                  scratch_shapes=[], semantics=("arbitrary",), vmem_mb=48, name=name, operands=operands, rider=rider)


def _final_loss(xs, g, target, branch, gmods, *, k_gate, name):
    t, d = xs.shape
    tm = _div(t, (256, 128))

    def loss_fn(x, gv, tgt):
        err = _rms(x) * gv - tgt
        return 0.5 * jnp.sum(jnp.mean(jnp.square(err), axis=-1))

    def body(x_ref, g_ref, t_ref, br_ref, gmods_ref, dx_ref, loss_ref, dg_ref, db_ref, dgm_ref):
        i = pl.program_id(0)
        _zero_accs(i, [loss_ref, dg_ref, dgm_ref])
        val, vjp = jax.vjp(loss_fn, x_ref[...], g_ref[0:1, :], t_ref[...])
        dx, dg, _ = vjp(jnp.ones((), F32))
        dx_ref[...] = dx
        loss_ref[...] += val
        dg_ref[0:1, :] += dg
        _gate_part(dx, 0, k_gate, br_ref, gmods_ref, db_ref, dgm_ref)

    row = pl.BlockSpec((tm, d), lambda i: (i, 0))
    vec16 = pl.BlockSpec((16, d), lambda i: (0, 0))
    return pl.pallas_call(
        body, grid=(t // tm,),
        in_specs=[row, pl.BlockSpec((8, d), lambda i: (0, 0)), row, row, vec16],
        out_specs=[row, pl.BlockSpec((8, 128), lambda i: (0, 0)), pl.BlockSpec((8, d), lambda i: (0, 0)), row, vec16],
        out_shape=[jax.ShapeDtypeStruct((t, d), F32), jax.ShapeDtypeStruct((8, 128), F32),
                   jax.ShapeDtypeStruct((8, d), F32), jax.ShapeDtypeStruct((t, d), BF16),
                   jax.ShapeDtypeStruct((16, d), F32)],
        compiler_params=_params(("arbitrary",), 48), name=name,
    )(xs, g, target, branch, gmods)


def _halo_specs(r, width, col, t):
    h_per = r // HALO

    def prev(i):
        return (jnp.maximum(i * h_per - 1, 0), col)

    def nxt(i):
        return (jnp.minimum((i + 1) * h_per, t // HALO - 1), col)

    return (pl.BlockSpec((HALO, width), prev), pl.BlockSpec((r, width), lambda i: (i, col)),
            pl.BlockSpec((HALO, width), nxt))


def _seg_geometry(i, r, n_x, t):
    row0 = i * r
    in_ctx = row0 >= n_x
    first = jnp.logical_or(row0 == 0, row0 == n_x)
    last = jnp.logical_or(row0 + r == n_x, row0 + r == t)
    seg_start = jnp.where(in_ctx, n_x, 0)
    seg_len = jnp.where(in_ctx, t - n_x, n_x)
    return row0, first, last, seg_start, seg_len


ROW_CHUNK = 64
LANE_CHUNK = 128


def _chunks(width, rows, col0=0):
    lanes = min(LANE_CHUNK, width)
    return [(slice(col0 + c, col0 + c + lanes), r0) for c in range(0, width, lanes) for r0 in range(0, rows, ROW_CHUNK)]


def _pool_count(tpos, w, seg_len):
    return (jnp.minimum(tpos + w // 2, seg_len) - jnp.maximum(tpos - w // 2, 0)).astype(F32)


def _mixer0_fwd(z, conv_w, conv_b, ln_g, ln_b, pool_w, pool_scale, *, n_tap, n_x, name, rider=None):
    t, dc = z.shape[0], z.shape[1] // 3
    n_grp, pg = pool_w.shape[0], pool_w.shape[1]
    r = _div(t - n_x, (256, 128))
    assert n_x % r == 0 and pg * n_grp == dc
    half = n_tap // 2
    assert half < HALO and max(POOL_WINDOWS) // 2 <= HALO

    def body(ap, ac, an, gp, gc, gn, pp, pc, pn, w_ref, cb_ref, lg_ref, lb_ref, pw_ref, ps_ref,
             y_ref, cv_ref, d_ref, uwin, pwin):
        i = pl.program_id(0)
        row0, first, last, seg_start, seg_len = _seg_geometry(i, r, n_x, t)
        keep_prev = jnp.where(first, 0.0, 1.0)
        keep_next = jnp.where(last, 0.0, 1.0)
        uwin[0:HALO, :] = ap[...] * _sigmoid(gp[...]) * keep_prev
        uwin[HALO:HALO + r, :] = ac[...] * _sigmoid(gc[...])
        uwin[HALO + r:, :] = an[...] * _sigmoid(gn[...]) * keep_next
        pwin[0:HALO, :] = pp[...] * keep_prev
        pwin[HALO:HALO + r, :] = pc[...]
        pwin[HALO + r:, :] = pn[...] * keep_next
        for cols, r0 in _chunks(dc, r):
            acc = jnp.zeros((ROW_CHUNK, cols.stop - cols.start), F32) + cb_ref[0:1, cols]
            for k in range(n_tap):
                off = HALO - half + k + r0
                acc = acc + w_ref[k:k + 1, cols] * uwin[off:off + ROW_CHUNK, cols]
            cv_ref[r0:r0 + ROW_CHUNK, cols] = acc
        y_ref[:, 0:dc] = _ln_silu(cv_ref[...], lg_ref[0:1, :], lb_ref[0:1, :]).astype(BF16)
        tpos = row0 - seg_start + lax.broadcasted_iota(jnp.int32, (r, 1), 0)
        for g, w in enumerate(POOL_WINDOWS):
            cnt = _pool_count(tpos, w, seg_len)
            for cols, r0 in _chunks(pg, r, g * pg):
                s = jnp.zeros((ROW_CHUNK, cols.stop - cols.start), F32)
                for j in range(-(w // 2), w // 2):
                    s = s + pwin[HALO + j + r0:HALO + j + r0 + ROW_CHUNK, cols]
                diff = s / cnt[r0:r0 + ROW_CHUNK] - pwin[HALO + r0:HALO + r0 + ROW_CHUNK, cols]
                d_ref[r0:r0 + ROW_CHUNK, cols] = diff.astype(BF16)
            cols = slice(g * pg, (g + 1) * pg)
            pm = jnp.dot(d_ref[:, cols], pw_ref[g].astype(BF16), preferred_element_type=F32)
            y_ref[:, dc + g * pg:dc + (g + 1) * pg] = (pm * ps_ref[0:1, cols]).astype(BF16)

    vec = lambda rows, width: pl.BlockSpec((rows, width), lambda i: (0, 0))
    in_specs = [*_halo_specs(r, dc, 0, t), *_halo_specs(r, dc, 1, t), *_halo_specs(r, dc, 2, t),
                vec(conv_w.shape[0], dc), vec(8, dc), vec(8, dc), vec(8, dc),
                pl.BlockSpec((n_grp, pg, pg), lambda i: (0, 0, 0)), vec(8, dc)]
    return _pcall(
        body, grid=(t // r,), in_specs=in_specs,
        out_specs=[pl.BlockSpec((r, 2 * dc), lambda i: (i, 0)), pl.BlockSpec((r, dc), lambda i: (i, 0)),
                   pl.BlockSpec((r, dc), lambda i: (i, 0))],
        out_shape=[jax.ShapeDtypeStruct((t, 2 * dc), BF16), jax.ShapeDtypeStruct((t, dc), F32),
                   jax.ShapeDtypeStruct((t, dc), BF16)],
        scratch_shapes=[pltpu.VMEM((r + 2 * HALO, dc), F32), pltpu.VMEM((r + 2 * HALO, dc), F32)],
        semantics=("parallel",), vmem_mb=40, name=name,
        operands=[*([z] * 9), conv_w, conv_b, ln_g, ln_b, pool_w, pool_scale], rider=rider)


def _mixer0_bwd(dy, z, cv, dsave, conv_w, ln_g, ln_b, pool_w, pool_scale, *, n_tap, n_x, name, rider=None):
    t, dc = cv.shape
    n_grp, pg = pool_w.shape[0], pool_w.shape[1]
    r = _div(t - n_x, (256, 128))
    half = n_tap // 2
    dyp_, zp_, cvp_ = dy, z, cv
    rw = r + 2 * HALO

    def body(dcp, dcc, dcn, dpp, dpc, dpn, cvp, cvc, cvn, ap, ac, an, gp, gc, gn, d_ref,
             w_ref, lg_ref, lb_ref, pw_ref, ps_ref,
             dz_ref, dw_ref, dcb_ref, dlg_ref, dlb_ref, dpw_ref, dps_ref, uwin, dcvwin, ewin, ddwin):
        i = pl.program_id(0)
        _zero_accs(i, [dw_ref, dcb_ref, dlg_ref, dlb_ref, dpw_ref, dps_ref])
        row0, first, last, seg_start, seg_len = _seg_geometry(i, r, n_x, t)
        keep_prev = jnp.where(first, 0.0, 1.0)
        keep_next = jnp.where(last, 0.0, 1.0)
        lg, lb = lg_ref[0:1, :], lb_ref[0:1, :]
        _, vjp = jax.vjp(_ln_silu, cvc[...], lg, lb)
        dcv, dlg, dlb = vjp(dcc[...])
        dlg_ref[0:1, :] += dlg
        dlb_ref[0:1, :] += dlb
        dcb_ref[0:1, :] += jnp.sum(dcv, axis=0, keepdims=True)
        dcvwin[HALO:HALO + r, :] = dcv
        for halo_cv, halo_dy, keep, lo in ((cvp, dcp, keep_prev, 0), (cvn, dcn, keep_next, HALO + r)):
            _, vjp_h = jax.vjp(lambda v: _ln_silu(v, lg, lb), halo_cv[...])
            dcvwin[lo:lo + HALO, :] = vjp_h(halo_dy[...])[0] * keep
        uwin[0:HALO, :] = ap[...] * _sigmoid(gp[...]) * keep_prev
        uwin[HALO:HALO + r, :] = ac[...] * _sigmoid(gc[...])
        uwin[HALO + r:, :] = an[...] * _sigmoid(gn[...]) * keep_next
        for cols, r0 in _chunks(dc, r):
            du = jnp.zeros((ROW_CHUNK, cols.stop - cols.start), F32)
            for k in range(n_tap):
                off = HALO + half - k + r0
                du = du + w_ref[k:k + 1, cols] * dcvwin[off:off + ROW_CHUNK, cols]
            rows = slice(r0, r0 + ROW_CHUNK)
            sig = _sigmoid(gc[rows, cols])
            dz_ref[rows, cols] = (du * sig).astype(BF16)
            dz_ref[rows, dc + cols.start:dc + cols.stop] = (du * ac[rows, cols] * sig * (1.0 - sig)).astype(BF16)
        for c0 in range(0, dc, LANE_CHUNK):
            cols = slice(c0, c0 + LANE_CHUNK)
            taps = [jnp.zeros((8, LANE_CHUNK), F32) for _ in range(n_tap)]
            for r0 in range(0, r, ROW_CHUNK):
                dcv_c = dcvwin[HALO + r0:HALO + r0 + ROW_CHUNK, cols]
                for k in range(n_tap):
                    off = HALO - half + k + r0
                    prod = dcv_c * uwin[off:off + ROW_CHUNK, cols]
                    taps[k] = taps[k] + functools.reduce(
                        jnp.add, [prod[8 * s:8 * s + 8] for s in range(ROW_CHUNK // 8)])
            for k in range(n_tap):
                dw_ref[k:k + 1, cols] += jnp.sum(taps[k], axis=0, keepdims=True)
        twin = row0 - seg_start - HALO + lax.broadcasted_iota(jnp.int32, (rw, 1), 0)
        for g, w in enumerate(POOL_WINDOWS):
            cols = slice(g * pg, (g + 1) * pg)
            wg = pw_ref[g].astype(BF16)
            scale = ps_ref[0:1, cols]
            dyp_c = dpc[:, cols]
            dpm_win = jnp.concatenate([dpp[:, cols] * keep_prev, dyp_c, dpn[:, cols] * keep_next], axis=0) * scale
            dd_win = lax.dot_general(dpm_win.astype(BF16), wg, (((1,), (1,)), ((), ())), preferred_element_type=F32)
            cnt = jnp.maximum(_pool_count(twin, w, seg_len), 1.0)
            ddwin[:, cols] = dd_win
            ewin[:, cols] = dd_win / cnt
            for ccols, r0 in _chunks(pg, r, g * pg):
                dup = -ddwin[HALO + r0:HALO + r0 + ROW_CHUNK, ccols]
                for j in range(-(w // 2) + 1, w // 2 + 1):
                    dup = dup + ewin[HALO + j + r0:HALO + j + r0 + ROW_CHUNK, ccols]
                dz_ref[r0:r0 + ROW_CHUNK, 2 * dc + ccols.start:2 * dc + ccols.stop] = dup.astype(BF16)
            dsv = d_ref[:, cols]
            pm = jnp.dot(dsv, wg, preferred_element_type=F32)
            dps_ref[0:1, cols] += jnp.sum(dyp_c * pm, axis=0, keepdims=True)
            dpw_ref[g] += lax.dot_general(dsv, (dyp_c * scale).astype(BF16), (((0,), (0,)), ((), ())),
                                          preferred_element_type=F32)

    vec = lambda rows, width: pl.BlockSpec((rows, width), lambda i: (0, 0))
    grp = pl.BlockSpec((n_grp, pg, pg), lambda i: (0, 0, 0))
    in_specs = [*_halo_specs(r, dc, 0, t), *_halo_specs(r, dc, 1, t), *_halo_specs(r, dc, 0, t),
                *_halo_specs(r, dc, 0, t), *_halo_specs(r, dc, 1, t), pl.BlockSpec((r, dc), lambda i: (i, 0)),
                vec(conv_w.shape[0], dc), vec(8, dc), vec(8, dc), grp, vec(8, dc)]
    return _pcall(
        body, grid=(t // r,), in_specs=in_specs,
        out_specs=[pl.BlockSpec((r, 3 * dc), lambda i: (i, 0)), vec(conv_w.shape[0], dc), vec(8, dc), vec(8, dc),
                   vec(8, dc), grp, vec(8, dc)],
        out_shape=[jax.ShapeDtypeStruct((t, 3 * dc), BF16), jax.ShapeDtypeStruct(conv_w.shape, F32),
                   jax.ShapeDtypeStruct((8, dc), F32), jax.ShapeDtypeStruct((8, dc), F32),
                   jax.ShapeDtypeStruct((8, dc), F32), jax.ShapeDtypeStruct(pool_w.shape, F32),
                   jax.ShapeDtypeStruct((8, dc), F32)],
        scratch_shapes=[pltpu.VMEM((rw, dc), F32)] * 4,
        semantics=("arbitrary",), vmem_mb=VMEM_CAP_MB, name=name,
        operands=[dyp_, dyp_, dyp_, dyp_, dyp_, dyp_, cvp_, cvp_, cvp_, zp_, zp_, zp_, zp_, zp_, zp_, dsave,
                  conv_w, ln_g, ln_b, pool_w, pool_scale], rider=rider)


def _swap_halves(x):
    lane = lax.broadcasted_iota(jnp.int32, x.shape, 1)
    quarter = HEAD_DIM // 4
    return jnp.where(lane % (2 * quarter) < quarter,
                     pltpu.roll(x, HEAD_DIM - quarter, 1), pltpu.roll(x, quarter, 1))


def _rope_tables(n_x, n_ctx):
    half = HEAD_DIM // 4
    freqs = ROPE_THETA ** (-jnp.arange(half, dtype=F32) / half)
    tok = jnp.arange(n_x)
    row = (tok // GRID_W).astype(F32)[:, None] * freqs[None, :]
    col = (tok % GRID_W).astype(F32)[:, None] * freqs[None, :]
    cos = jnp.concatenate([jnp.cos(row), jnp.cos(row), jnp.cos(col), jnp.cos(col)], axis=1)
    sin = jnp.concatenate([-jnp.sin(row), jnp.sin(row), -jnp.sin(col), jnp.sin(col)], axis=1)
    cos = jnp.concatenate([cos, jnp.ones((n_ctx, HEAD_DIM), F32)], axis=0)
    sin = jnp.concatenate([sin, jnp.zeros((n_ctx, HEAD_DIM), F32)], axis=0)
    return cos, sin


def _norm_g(x, g):
    return _rms(x) * g


def _qk_prep_fwd(qkv, gq, gk, cos, sin, *, d_q, name):
    t, width = qkv.shape
    d_kv = (width - d_q) // 2
    tm = _div(t, (256, 128))

    def body(qkv_ref, gq_ref, gk_ref, cos_ref, sin_ref, q_ref, k_ref, v_ref):
        cs, sn = cos_ref[...], sin_ref[...]
        for h in range((d_q + d_kv) // HEAD_DIM):
            g = gq_ref[0:1, :] if h * HEAD_DIM < d_q else gk_ref[0:1, :]
            xn = _norm_g(qkv_ref[:, h * HEAD_DIM:(h + 1) * HEAD_DIM], g)
            rot = xn * cs + _swap_halves(xn) * sn
            if h * HEAD_DIM < d_q:
                q_ref[:, h * HEAD_DIM:(h + 1) * HEAD_DIM] = (rot * Q_SCALE_LOG2).astype(BF16)
            else:
                k_ref[:, h * HEAD_DIM - d_q:(h + 1) * HEAD_DIM - d_q] = rot.astype(BF16)
        v_ref[...] = qkv_ref[:, d_q + d_kv:].astype(BF16)

    row = lambda w: pl.BlockSpec((tm, w), lambda i: (i, 0))
    vec = pl.BlockSpec((8, HEAD_DIM), lambda i: (0, 0))
    return pl.pallas_call(
        body, grid=(t // tm,),
        in_specs=[row(width), vec, vec, row(HEAD_DIM), row(HEAD_DIM)],
        out_specs=[row(d_q), row(d_kv), row(d_kv)],
        out_shape=[jax.ShapeDtypeStruct((t, d_q), BF16), jax.ShapeDtypeStruct((t, d_kv), BF16),
                   jax.ShapeDtypeStruct((t, d_kv), BF16)],
        compiler_params=_params(("parallel",), 32), name=name,
    )(qkv, gq, gk, cos, sin)


def _qk_prep_bwd(qkv, dq, dk, dv, gq, gk, cos, sin, *, n_x, name):
    t, width = qkv.shape
    d_q, d_kv = dq.shape[1], dk.shape[1]
    tm = _div(t, (256, 128))
    last_q = n_x // tm - 1

    def body(qkv_ref, dq_ref, dk_ref, dv_ref, gq_ref, gk_ref, cos_ref, sin_ref, out_ref, dgq_ref, dgk_ref):
        i = pl.program_id(0)
        _zero_accs(i, [dgq_ref, dgk_ref])
        is_x = jnp.where(i * tm < n_x, 1.0, 0.0)
        cs, sn = cos_ref[...], sin_ref[...]
        for h in range((d_q + d_kv) // HEAD_DIM):
            sl = slice(h * HEAD_DIM, (h + 1) * HEAD_DIM)
            if h * HEAD_DIM < d_q:
                g, dg_ref, dr = gq_ref[0:1, :], dgq_ref, dq_ref[:, sl] * is_x
            else:
                g, dg_ref = gk_ref[0:1, :], dgk_ref
                dr = dk_ref[:, h * HEAD_DIM - d_q:(h + 1) * HEAD_DIM - d_q]
            dxn = dr * cs + _swap_halves(dr * sn)
            _, vjp = jax.vjp(_norm_g, qkv_ref[:, sl], g)
            dx, dg = vjp(dxn)
            out_ref[:, sl] = dx.astype(BF16)
            dg_ref[0:1, :] += dg
        out_ref[:, d_q + d_kv:] = dv_ref[...].astype(BF16)

    row = lambda w: pl.BlockSpec((tm, w), lambda i: (i, 0))
    vec = pl.BlockSpec((8, HEAD_DIM), lambda i: (0, 0))
    return pl.pallas_call(
        body, grid=(t // tm,),
        in_specs=[row(width), pl.BlockSpec((tm, d_q), lambda i: (jnp.minimum(i, last_q), 0)), row(d_kv), row(d_kv),
                  vec, vec, row(HEAD_DIM), row(HEAD_DIM)],
        out_specs=[row(width), vec, vec],
        out_shape=[jax.ShapeDtypeStruct((t, width), BF16), jax.ShapeDtypeStruct((8, HEAD_DIM), F32),
                   jax.ShapeDtypeStruct((8, HEAD_DIM), F32)],
        compiler_params=_params(("arbitrary",), 40), name=name,
    )(qkv, dq, dk, dv, gq, gk, cos, sin)


ATTN_TQ = (256, 128)
ATTN_TK = (768, 640, 512, 384, 256, 128)


def _attention_fwd(q, k, v_t, *, n_x, name):
    t, d_kv = k.shape
    d_q = q.shape[1]
    kvh = d_kv // HEAD_DIM
    grp = d_q // d_kv
    tq = _div(n_x, ATTN_TQ)
    tk = _div(t, ATTN_TK)
    gw = grp * HEAD_DIM
    n_kv = t // tk
    n_pair = (n_kv - 1) // 2

    def fold8(x):
        return functools.reduce(jnp.add, [x[8 * r:8 * r + 8] for r in range(tk // 8)])

    def body(q_ref, k_ref, vt_ref, o_ref, lse_ref, m_ref, l_ref, acc_ref, qt_ref, s_even, s_odd):
        m_ref[...] = jnp.full_like(m_ref, -jnp.inf)
        l_ref[...] = jnp.zeros_like(l_ref)
        acc_ref[...] = jnp.zeros_like(acc_ref)
        for g in range(grp):
            qt_ref[g] = q_ref[:, g * HEAD_DIM:(g + 1) * HEAD_DIM].T

        def keys(j):
            return pl.ds(pl.multiple_of(j * tk, tk), tk)

        def scores(g, kc):
            return jnp.dot(kc, qt_ref[g], preferred_element_type=F32)

        def chunk(j, s_cur, s_next):
            vt = vt_ref[:, keys(j)]
            kn = k_ref[keys(j + 1), :] if s_next is not None else None
            for g in range(grp):
                s = s_cur[g]
                m_old = m_ref[g]
                m_new = jnp.maximum(m_old, jnp.max(s, axis=0, keepdims=True))
                alpha = jnp.exp2(m_old - m_new)
                p = jnp.exp2(s - m_new)
                if s_next is not None:
                    s_next[g] = scores(g, kn)
                l_ref[g] = alpha * l_ref[g] + fold8(p)
                acc_ref[g] = alpha * acc_ref[g] + jnp.dot(vt, p.astype(BF16), preferred_element_type=F32)
                m_ref[g] = m_new

        k0 = k_ref[keys(0), :]
        for g in range(grp):
            s_even[g] = scores(g, k0)

        def pair(i, carry):
            chunk(2 * i, s_even, s_odd)
            chunk(2 * i + 1, s_odd, s_even)
            return carry

        lax.fori_loop(0, n_pair, pair, 0)
        if n_kv - 2 * n_pair == 2:
            chunk(n_kv - 2, s_even, s_odd)
            chunk(n_kv - 1, s_odd, None)
        else:
            chunk(n_kv - 1, s_even, None)
        for g in range(grp):
            l = jnp.sum(l_ref[g], axis=0, keepdims=True)
            o_ref[:, g * HEAD_DIM:(g + 1) * HEAD_DIM] = (acc_ref[g] / l).T.astype(BF16)
            lse_row = m_ref[g] + jnp.log(l) * LOG2_E
            lse_ref[:, g:g + 1] = jnp.broadcast_to(lse_row, (HEAD_DIM, tq)).T[:, 0:1]

    return pl.pallas_call(
        body, grid=(kvh, n_x // tq),
        in_specs=[pl.BlockSpec((tq, gw), lambda h, i: (i, h)),
                  pl.BlockSpec((t, HEAD_DIM), lambda h, i: (0, h)),
                  pl.BlockSpec((HEAD_DIM, t), lambda h, i: (h, 0))],
        out_specs=[pl.BlockSpec((tq, gw), lambda h, i: (i, h)),
                   pl.BlockSpec((None, tq, grp), lambda h, i: (h, i, 0))],
        out_shape=[jax.ShapeDtypeStruct((n_x, d_q), BF16), jax.ShapeDtypeStruct((kvh, n_x, grp), F32)],
        scratch_shapes=[pltpu.VMEM((grp, 1, tq), F32), pltpu.VMEM((grp, 8, tq), F32),
                        pltpu.VMEM((grp, HEAD_DIM, tq), F32), pltpu.VMEM((grp, HEAD_DIM, tq), BF16),
                        pltpu.VMEM((grp, tk, tq), F32), pltpu.VMEM((grp, tk, tq), F32)],
        compiler_params=_params(("parallel", "arbitrary"), 48), name=name,
    )(q, k, v_t)


def _attention_bwd(q, k, v, o, lse, do, *, n_x, name):
    t, d_kv = k.shape
    d_q = q.shape[1]
    kvh = d_kv // HEAD_DIM
    grp = d_q // d_kv
    tq = _div(n_x, ATTN_TQ)
    tk = _div(t, ATTN_TK)
    gw = grp * HEAD_DIM
    n_q = n_x // tq

    def body(q_ref, k_ref, v_ref, o_ref, lse_ref, do_ref, dq_ref, dkt_ref, dvt_ref, dq_acc, lse_s, delta_s, qt_s,
             dot_s):
        i = pl.program_id(1)
        _zero_accs(i, [dkt_ref, dvt_ref])
        dq_acc[...] = jnp.zeros_like(dq_acc)
        for g in range(grp):
            sl = slice(g * HEAD_DIM, (g + 1) * HEAD_DIM)
            lse_s[g] = lse_ref[:, g:g + 1]
            delta_s[g] = jnp.sum(do_ref[:, sl].astype(F32) * o_ref[:, sl].astype(F32), axis=-1, keepdims=True)
            qt_s[g] = q_ref[:, sl].T
            dot_s[g] = do_ref[:, sl].T

        def step(j, carry):
            start = pl.multiple_of(j * tk, tk)
            kc, vc = k_ref[pl.ds(start, tk), :], v_ref[pl.ds(start, tk), :]
            dkt_part = jnp.zeros((HEAD_DIM, tk), F32)
            dvt_part = jnp.zeros((HEAD_DIM, tk), F32)
            for g in range(grp):
                sl = slice(g * HEAD_DIM, (g + 1) * HEAD_DIM)
                s = lax.dot_general(q_ref[:, sl], kc, (((1,), (1,)), ((), ())), preferred_element_type=F32)
                p = jnp.exp2(s - lse_s[g])
                dp = lax.dot_general(do_ref[:, sl], vc, (((1,), (1,)), ((), ())), preferred_element_type=F32)
                ds = (p * (dp - delta_s[g])).astype(BF16)
                dq_acc[g] += jnp.dot(ds, kc, preferred_element_type=F32)
                dvt_part = dvt_part + jnp.dot(dot_s[g], p.astype(BF16), preferred_element_type=F32)
                dkt_part = dkt_part + jnp.dot(qt_s[g], ds, preferred_element_type=F32)
            dvt_ref[:, pl.ds(start, tk)] += dvt_part
            dkt_ref[:, pl.ds(start, tk)] += dkt_part
            return carry

        lax.fori_loop(0, t // tk, step, 0)
        for g in range(grp):
            dq_ref[:, g * HEAD_DIM:(g + 1) * HEAD_DIM] = dq_acc[g] * ATTN_SCALE

        @pl.when(i == n_q - 1)
        def _():
            dkt_ref[...] = dkt_ref[...] * (1.0 / LOG2_E)

    qspec = pl.BlockSpec((tq, gw), lambda h, i: (i, h))
    kspec = pl.BlockSpec((t, HEAD_DIM), lambda h, i: (0, h))
    ktspec = pl.BlockSpec((HEAD_DIM, t), lambda h, i: (h, 0))
    return pl.pallas_call(
        body, grid=(kvh, n_q),
        in_specs=[qspec, kspec, kspec, qspec, pl.BlockSpec((None, tq, grp), lambda h, i: (h, i, 0)), qspec],
        out_specs=[qspec, ktspec, ktspec],
        out_shape=[jax.ShapeDtypeStruct((n_x, d_q), F32), jax.ShapeDtypeStruct((d_kv, t), F32),
                   jax.ShapeDtypeStruct((d_kv, t), F32)],
        scratch_shapes=[pltpu.VMEM((grp, tq, HEAD_DIM), F32), pltpu.VMEM((grp, tq, 1), F32),
                        pltpu.VMEM((grp, tq, 1), F32), pltpu.VMEM((grp, HEAD_DIM, tq), BF16),
                        pltpu.VMEM((grp, HEAD_DIM, tq), BF16)],
        compiler_params=_params(("parallel", "arbitrary"), 56), name=name,
    )(q, k, v, o, lse, do)


def _whole(body, ins, out_shapes, name):
    return pl.pallas_call(
        body, out_shape=[jax.ShapeDtypeStruct(s, d) for s, d in out_shapes],
        compiler_params=pltpu.CompilerParams(vmem_limit_bytes=40 << 20), name=name)(*ins)


def _silu_rows(x, *, name):
    def body(x_ref, o_ref):
        o_ref[...] = _silu(x_ref[...])
    return _whole(body, [x], [(x.shape, F32)], name)[0]


def _assemble_dmods(gathered, *, name):
    width = gathered.shape[1]

    def body(g_ref, dm0, dm1, db0, db1):
        for l, (dm, db) in enumerate(((dm0, db0), (dm1, db1))):
            ctx = jnp.zeros((1, width), F32)
            tot = jnp.zeros((1, width), F32)
            for q in range(N_DEV):
                row = g_ref[16 * q + 8 * l:16 * q + 8 * l + 1, :]
                dm[q:q + 1, :] = row
                tot = tot + row
                ctx = ctx + g_ref[16 * q + 8 * l + 1:16 * q + 8 * l + 2, :]
            dm[N_DEV:N_DEV + 1, :] = ctx
            dm[N_DEV + 1:, :] = jnp.zeros((16 - N_DEV - 1, width), F32)
            db[...] = jnp.zeros_like(db)
            db[0:1, :] = tot + ctx

    return _whole(body, [gathered], [((16, width), F32), ((16, width), F32), ((8, width), F32), ((8, width), F32)],
                  name)


def _sum_slots(gathered, rows, *, name):
    def body(g_ref, o_ref):
        acc = g_ref[0:rows, :]
        for q in range(1, N_DEV):
            acc = acc + g_ref[q * rows:(q + 1) * rows, :]
        o_ref[...] = acc
    return _whole(body, [gathered], [((rows, gathered.shape[1]), F32)], name)[0]


def _silu_grad(x, dy, *, name):
    def body(x_ref, dy_ref, o_ref):
        _, vjp = jax.vjp(_silu, x_ref[...])
        o_ref[...] = vjp(dy_ref[...])[0]
    return _whole(body, [x, dy], [(x.shape, F32)], name)[0]


def _adamw(w, m, v, *, name, recv=None, grad=None):
    rows, cols = w.shape
    budget = max(8, ADAMW_BLOCK_ELEMS // cols)
    tr = _div(rows, [c for c in (512, 256, 128, 64, 32, 16, 8) if c <= budget] + [rows])
    c1 = 1.0 - ADAM_B1 ** ADAM_STEP
    c2 = 1.0 - ADAM_B2 ** ADAM_STEP

    def body(w_ref, m_ref, v_ref, g_in, g_ref, d_ref, nm_ref, nv_ref):
        if recv is not None:
            g = g_in[0].astype(F32)
            for q in range(1, N_DEV):
                g = g + g_in[q].astype(F32)
        else:
            g = g_in[...]
        nm = ADAM_B1 * m_ref[...] + (1.0 - ADAM_B1) * g
        nv = ADAM_B2 * v_ref[...] + (1.0 - ADAM_B2) * jnp.square(g)
        g_ref[...] = g
        nm_ref[...] = nm
        nv_ref[...] = nv
        d_ref[...] = -ADAM_LR * ((nm / c1) / (jnp.sqrt(nv / c2) + ADAM_EPS) + ADAM_WD * w_ref[...])

    blk = pl.BlockSpec((tr, cols), lambda i: (i, 0))
    g_spec = pl.BlockSpec((N_DEV, tr, cols), lambda i: (0, i, 0)) if recv is not None else blk
    return pl.pallas_call(
        body, grid=(rows // tr,), in_specs=[blk, blk, blk, g_spec], out_specs=[blk] * 4,
        out_shape=[jax.ShapeDtypeStruct((rows, cols), F32)] * 4,
        compiler_params=_params(("parallel",), 48), name=name,
    )(w, m, v, recv if recv is not None else grad)


def _position():
    return tuple(lax.axis_index(a) for a in MESH_AXES)


def _linear(pos):
    return 4 * pos[0] + 2 * pos[1] + pos[2]


def _window(ref, axis, dev, size):
    start = pl.multiple_of(dev * size, size)
    return ref.at[pl.ds(start, size), :] if axis == 0 else ref.at[:, pl.ds(start, size)]


def _all_gather(shards, axes, *, name):
    n = len(shards)
    sizes = [s.shape[ax] for s, ax in zip(shards, axes)]

    def body(*refs):
        src, dst = refs[:n], refs[n:2 * n]
        send_sems, recv_sems, local_sems = refs[2 * n:]
        x, y, c = _position()
        me, sibling = (x, y, c), (x, y, 1 - c)
        chips = [(1 - x, y), (x, 1 - y), (1 - x, 1 - y)]

        def win(k, pos):
            return _window(dst[k], axes[k], _linear(pos), sizes[k])

        def copy(k, sem, block, to, from_src=False):
            return pltpu.make_async_remote_copy(
                src_ref=src[k] if from_src else win(k, block), dst_ref=win(k, block),
                send_sem=send_sems.at[k, sem], recv_sem=recv_sems.at[k, sem],
                device_id=to, device_id_type=MESH_ID)

        mine = [pltpu.make_async_copy(src[k], win(k, me), local_sems.at[k]) for k in range(n)]
        for cp in mine:
            cp.start()
        first = []
        for k in range(n):
            first.append(copy(k, 0, me, sibling, from_src=True))
            first += [copy(k, 1 + j, me, (*chip, c), from_src=True) for j, chip in enumerate(chips)]
        for cp in first:
            cp.start()
        passed = []
        for j, chip in enumerate(chips):
            for k in range(n):
                copy(k, 1 + j, (*chip, c), me).wait_recv()
                fwd = copy(k, 4 + j, (*chip, c), sibling)
                fwd.start()
                passed.append(fwd)
        for k in range(n):
            copy(k, 0, sibling, me).wait_recv()
            for j, chip in enumerate(chips):
                copy(k, 4 + j, (*chip, 1 - c), me).wait_recv()
        for cp in first + passed:
            cp.wait_send()
        for cp in mine:
            cp.wait()

    out_shape = []
    for s, ax in zip(shards, axes):
        full = (s.shape[0] * N_DEV, s.shape[1]) if ax == 0 else (s.shape[0], s.shape[1] * N_DEV)
        out_shape.append(jax.ShapeDtypeStruct(full, s.dtype))
    any_spec = pl.BlockSpec(memory_space=pl.ANY)
    return pl.pallas_call(
        body, in_specs=[any_spec] * n, out_specs=[any_spec] * n, out_shape=out_shape,
        scratch_shapes=[pltpu.SemaphoreType.DMA((n, 7)), pltpu.SemaphoreType.DMA((n, 7)),
                        pltpu.SemaphoreType.DMA((n,))],
        name=name,
    )(*shards)


class Rider:
    def __init__(self, kind, arrays, axes):
        self.kind, self.arrays, self.axes = kind, list(arrays), list(axes)
        self.n = len(self.arrays)
        if kind == "gather":
            self.sizes = [a.shape[ax] for a, ax in zip(self.arrays, self.axes)]
        else:
            self.sizes = [a.shape[ax] // N_DEV for a, ax in zip(self.arrays, self.axes)]

    def out_shape(self):
        shapes = []
        for a, ax, sz in zip(self.arrays, self.axes, self.sizes):
            if self.kind == "gather":
                full = (sz * N_DEV, a.shape[1]) if ax == 0 else (a.shape[0], sz * N_DEV)
                shapes.append(jax.ShapeDtypeStruct(full, a.dtype))
            else:
                shard = (sz, a.shape[1]) if ax == 0 else (a.shape[0], sz)
                shapes.append(jax.ShapeDtypeStruct((N_DEV, *shard), a.dtype))
        return shapes

    def scratch(self):
        return [pltpu.SemaphoreType.DMA((self.n, N_DEV - 1)), pltpu.SemaphoreType.DMA((self.n, N_DEV - 1)),
                pltpu.SemaphoreType.DMA((self.n,))]

    def plan(self, src, dst, send_sems, recv_sems, local_sems):
        x, y, c = me = _position()
        mine = _linear(me)

        def remote(k, sem, src_ref, dst_ref, to):
            return pltpu.make_async_remote_copy(
                src_ref=src_ref, dst_ref=dst_ref, send_sem=send_sems.at[k, sem], recv_sem=recv_sems.at[k, sem],
                device_id=to, device_id_type=MESH_ID)

        ph = dict(local=[], start=[], mid_wait=[], mid_start=[], end_wait=[])
        for k in range(self.n):
            ax, sz = self.axes[k], self.sizes[k]
            if self.kind == "exchange":
                own_src, own_dst = _window(src[k], ax, mine, sz), dst[k].at[mine]
                ph["local"].append(pltpu.make_async_copy(own_src, own_dst, local_sems.at[k]))
                for mask in range(1, N_DEV):
                    to = tuple(1 - p if (mask >> (2 - b)) & 1 else p for b, p in enumerate(me))
                    ph["start"].append(remote(k, mask - 1, _window(src[k], ax, _linear(to), sz), own_dst, to))
                    ph["end_wait"].append(remote(k, mask - 1, own_src, dst[k].at[_linear(to)], to))
            else:
                def win(pos, k=k, ax=ax, sz=sz):
                    return _window(dst[k], ax, _linear(pos), sz)
                sibling = (x, y, 1 - c)
                chips = [(1 - x, y), (x, 1 - y), (1 - x, 1 - y)]
                ph["local"].append(pltpu.make_async_copy(src[k], win(me), local_sems.at[k]))
                ph["start"].append(remote(k, 0, src[k], win(me), sibling))
                ph["end_wait"].append(remote(k, 0, src[k], win(sibling), sibling))
                for j, chip in enumerate(chips):
                    ph["start"].append(remote(k, 1 + j, src[k], win(me), (*chip, c)))
                    ph["mid_wait"].append(remote(k, 1 + j, src[k], win((*chip, c)), (*chip, c)))
                    ph["mid_start"].append(remote(k, 4 + j, win((*chip, c)), win((*chip, c)), sibling))
                    ph["end_wait"].append(remote(k, 4 + j, src[k], win((*chip, 1 - c)), sibling))
        return ph


def _pcall(body, *, grid, in_specs, out_specs, out_shape, scratch_shapes, semantics, vmem_mb, name, operands,
           rider=None):
    if rider is None:
        return pl.pallas_call(body, grid=grid, in_specs=in_specs, out_specs=out_specs, out_shape=out_shape,
                              scratch_shapes=scratch_shapes, compiler_params=_params(semantics, vmem_mb),
                              name=name)(*operands)
    n_in, n_out, n_scr, n = len(in_specs), len(out_specs), len(scratch_shapes), rider.n

    def wrapped(*refs):
        ins, src = refs[:n_in], refs[n_in:n_in + n]
        outs = refs[n_in + n:n_in + n + n_out]
        dst = refs[n_in + n + n_out:n_in + 2 * n + n_out]
        rest = refs[n_in + 2 * n + n_out:]
        scratch, sems = rest[:n_scr], rest[n_scr:]
        step = functools.reduce(lambda acc, ig: acc * ig[1] + ig[0],
                                [(pl.program_id(dim), g) for dim, g in enumerate(grid)], 0)
        n_steps = functools.reduce(lambda a, b: a * b, grid)

        @pl.when(step == 0)
        def _():
            ph = rider.plan(src, dst, *sems)
            for cp in ph["local"] + ph["start"]:
                cp.start()

        body(*ins, *outs, *scratch)

        @pl.when(step == (n_steps * 5) // 8)
        def _():
            ph = rider.plan(src, dst, *sems)
            for cp in ph["mid_wait"]:
                cp.wait_recv()
            for cp in ph["mid_start"]:
                cp.start()

        @pl.when(step == n_steps - 1)
        def _():
            ph = rider.plan(src, dst, *sems)
            for cp in ph["end_wait"]:
                cp.wait_recv()
            for cp in ph["start"] + ph["mid_start"]:
                cp.wait_send()
            for cp in ph["local"]:
                cp.wait()

    any_spec = pl.BlockSpec(memory_space=pl.ANY)
    return pl.pallas_call(
        wrapped, grid=grid, in_specs=list(in_specs) + [any_spec] * n, out_specs=list(out_specs) + [any_spec] * n,
        out_shape=list(out_shape) + rider.out_shape(), scratch_shapes=list(scratch_shapes) + rider.scratch(),
        compiler_params=_params(("arbitrary",) * len(grid), vmem_mb), name=name,
    )(*operands, *rider.arrays)


WEIGHTS = ['c_ctx', 'l0_ada_w', 'l0_ada_b', 'l0_in_w', 'l0_conv_w', 'l0_conv_b', 'l0_conv_ln_g', 'l0_conv_ln_b',
           'l0_pool_w', 'l0_pool_scale', 'l0_out_w', 'l0_mlp_w1', 'l0_mlp_w2', 'l1_ada_w', 'l1_ada_b', 'l1_qkv_w',
           'l1_q_norm_g', 'l1_k_norm_g', 'l1_out_w', 'l1_mlp_w1', 'l1_mlp_w2', 'final_g']
SHARDED = {'l0_in_w': 1, 'l0_out_w': 0, 'l0_mlp_w1': 1, 'l0_mlp_w2': 0,
           'l1_qkv_w': 1, 'l1_out_w': 0, 'l1_mlp_w1': 1, 'l1_mlp_w2': 0}
REPLICATED_SMALL = ['l0_conv_b', 'l0_conv_ln_g', 'l0_conv_ln_b', 'l0_pool_scale', 'l1_q_norm_g', 'l1_k_norm_g',
                    'final_g']


def _row8(v):
    v = v.reshape(1, -1)
    return jnp.pad(v, ((0, 7), (0, 0)))


def _mods16(full, me):
    d = full.shape[1] // 6
    mine = lax.dynamic_slice_in_dim(full, me, 1, axis=0).reshape(6, d)
    ctx = full[N_DEV].reshape(6, d)
    return jnp.pad(jnp.stack([mine, ctx], axis=1).reshape(12, d), ((0, 4), (0, 0)))


def _mlp_fwd(xs, mods, w1, w2, *, n_x, tm, tag, rider1=None, rider2=None):
    t, d = xs.shape
    dff = w1.shape[1]
    h = _rms_mod_fwd(xs, mods, k_shift=3, n_x=n_x, name=f"{tag}_norm2")
    pre, act, *ride1 = _matmul(h, w1, mode="nn", tm=tm, tn=_div(dff, WIDE_TN if tm * WIDE_TN[0] <= WIDE_TILE_ELEMS else WIDE_TN[1:]), tk=d, out_dtypes=[BF16, BF16],
                               name=f"{tag}_mlp1", rider=rider1,
                               epilogue=lambda acc, rows: (acc, jnp.square(jnp.maximum(acc, 0.0))))
    if w2 is None:
        w2 = ride1[-1]
    xo, branch, *ride2 = _matmul(
        act, w2, mode="nn", tm=_div(t, DEEP_TM), tn=_div(d, DEEP_TN), tk=dff, out_dtypes=[F32, BF16],
        name=f"{tag}_mlp2", extras=[("tile", xs), ("vec", mods)], rider=rider2,
        epilogue=lambda acc, rows, res, mv: (res + _seg_pick(mv, 5, rows, n_x) * acc, acc))
    return xo, dict(h=h, pre=pre, act=act, branch=branch, x_in=xs), ride1, ride2


def _exchange_of(items):
    return Rider("exchange", [a for a, _ in items], [ax for _, ax in items]) if items else None


def _mlp_bwd(dxo, dbranch, saved, mods, w1, w2, mixer_branch, *, n_x, tm, tag, ride_dx2=(), ride_dw2=()):
    t, d = dxo.shape
    dff = w1.shape[1]
    tkt = _div(t, TOKEN_TK)
    dpre, *recv_a = _matmul(dbranch, w2, mode="nt", tm=tm, tn=_div(dff, WIDE_TN if tm * WIDE_TN[0] <= WIDE_TILE_ELEMS else WIDE_TN[1:]), tk=d, out_dtypes=[BF16],
                            name=f"{tag}_mlp2_dx", extras=[("tile", saved["pre"])],
                            rider=_exchange_of(list(ride_dx2)),
                            epilogue=lambda acc, rows, pre: (acc * 2.0 * jnp.maximum(pre.astype(F32), 0.0),))
    dw2, *recv_b = _matmul(saved["act"], dbranch, mode="tn", tm=_div(dff, (1024, 512)), tn=_div(d, (1024, 512)),
                           tk=tkt, out_dtypes=[BF16], name=f"{tag}_mlp2_dw", rider=_exchange_of(list(ride_dw2)))
    dh, = _matmul(dpre, w1, mode="nt", tm=_div(t, DEEP_TM), tn=_div(d, DEEP_TN), tk=dff,
                  out_dtypes=[F32], name=f"{tag}_mlp1_dx")
    dw1, recv_dw2 = _matmul(saved["h"], dpre, mode="tn", tm=_div(d, (1024, 512)), tn=_div(dff, (1024, 512)),
                            tk=tkt, out_dtypes=[BF16], name=f"{tag}_mlp1_dw", rider=_exchange_of([(dw2, 0)]))
    dx, dm_norm, dmix, dm_gate = _rms_mod_bwd(saved["x_in"], mods, dh, dxo, k_shift=3, n_x=n_x,
                                              name=f"{tag}_norm2_bwd", gate=(mixer_branch, mods, 2))
    return dx, dw1, dm_norm + dm_gate, dmix, recv_a, recv_b, recv_dw2


def kernel(x, c, ctx, c_ctx, l0_ada_w, l0_ada_b, l0_in_w, l0_conv_w, l0_conv_b, l0_conv_ln_g, l0_conv_ln_b, l0_pool_w, l0_pool_scale, l0_out_w, l0_mlp_w1, l0_mlp_w2, l1_ada_w, l1_ada_b, l1_qkv_w, l1_q_norm_g, l1_k_norm_g, l1_out_w, l1_mlp_w1, l1_mlp_w2, final_g, loss_target, m_c_ctx, m_l0_ada_w, m_l0_ada_b, m_l0_in_w, m_l0_conv_w, m_l0_conv_b, m_l0_conv_ln_g, m_l0_conv_ln_b, m_l0_pool_w, m_l0_pool_scale, m_l0_out_w, m_l0_mlp_w1, m_l0_mlp_w2, m_l1_ada_w, m_l1_ada_b, m_l1_qkv_w, m_l1_q_norm_g, m_l1_k_norm_g, m_l1_out_w, m_l1_mlp_w1, m_l1_mlp_w2, m_final_g, v_c_ctx, v_l0_ada_w, v_l0_ada_b, v_l0_in_w, v_l0_conv_w, v_l0_conv_b, v_l0_conv_ln_g, v_l0_conv_ln_b, v_l0_pool_w, v_l0_pool_scale, v_l0_out_w, v_l0_mlp_w1, v_l0_mlp_w2, v_l1_ada_w, v_l1_ada_b, v_l1_qkv_w, v_l1_q_norm_g, v_l1_k_norm_g, v_l1_out_w, v_l1_mlp_w1, v_l1_mlp_w2, v_final_g):
    p = dict(locals())
    me = _linear(_position())
    n_x, d = x.shape[1], x.shape[2]
    n_ctx = ctx.shape[1]
    t = n_x + n_ctx
    dc = l0_conv_b.shape[0]
    n_tap = l0_conv_w.shape[0]
    d_q = d
    n_mod = l0_ada_b.shape[0] // d
    ada_cols = l0_ada_w.shape[1]
    tm_t = _div(t, (768, 640, 512, 128))
    tm_x = _div(n_x, (1024, 512))

    names = list(SHARDED)
    shard16 = {nm: p[nm].astype(BF16) for nm in names}

    def gather_of(*nms):
        return Rider("gather", [shard16[nm] for nm in nms], [SHARDED[nm] for nm in nms])

    wfull = {}
    n_grp, pg = l0_pool_w.shape[0], l0_pool_w.shape[2]

    c_all = _all_gather([_row8(c)], [0], name="gather_cond")[0].reshape(N_DEV, 8, d)[:, 0]
    cond = jnp.concatenate([c_all, c_ctx.reshape(1, d), jnp.zeros((16 - N_DEV - 1, d), F32)], axis=0)
    s16 = _silu_rows(cond, name="cond_silu")
    mod_shards = []
    for li, (lw, lb) in enumerate(((l0_ada_w, l0_ada_b), (l1_ada_w, l1_ada_b))):
        bias = _row8(lax.dynamic_slice_in_dim(lb, me * ada_cols, ada_cols))
        mod_shards.append(_matmul(s16, lw, mode="nn", tm=16, tn=_div(ada_cols, (512, 384, 256, 128)), tk=d,
                                  out_dtypes=[F32], name=f"l{li}_ada_fwd", extras=[("vec", bias)],
                                  epilogue=lambda acc, rows, b: (acc + b[0:1],))[0])
    mods_full = _all_gather([jnp.concatenate(mod_shards, axis=0)], [1], name="gather_mods")[0]
    mods0, mods1 = _mods16(mods_full[:16], me), _mods16(mods_full[16:], me)

    xs0 = jnp.concatenate([x[0], ctx[0]], axis=0)
    first = Rider("gather", [shard16['l0_in_w'], shard16['l0_out_w'], jnp.pad(l0_conv_w, ((0, 1), (0, 0))),
                             l0_pool_w.reshape(-1, pg)], [SHARDED['l0_in_w'], SHARDED['l0_out_w'], 1, 0])
    h0, wfull['l0_in_w'], wfull['l0_out_w'], conv_w_full, pool_w_full = _rms_mod_fwd(
        xs0, mods0, k_shift=0, n_x=n_x, name="l0_norm1", rider=first)
    pool_w_full = pool_w_full.reshape(N_DEV, n_grp, pg // N_DEV, pg).transpose(1, 0, 2, 3).reshape(n_grp, pg, pg)
    z, wfull['l0_mlp_w1'] = _matmul(h0, wfull['l0_in_w'], mode="nn", tm=tm_t, tn=_div(3 * dc, (1024, 768, 512, 384)),
                                    tk=d, out_dtypes=[F32], name="l0_in_proj", rider=gather_of('l0_mlp_w1'))
    y0, cv, dsave, wfull['l0_mlp_w2'] = _mixer0_fwd(
        z, conv_w_full, _row8(l0_conv_b), _row8(l0_conv_ln_g), _row8(l0_conv_ln_b), pool_w_full,
        _row8(l0_pool_scale), n_tap=n_tap, n_x=n_x, name="l0_mixer", rider=gather_of('l0_mlp_w2'))
    xs1, mix0 = _matmul(y0, wfull['l0_out_w'], mode="nn", tm=tm_t, tn=_div(d, (1024, 512)), tk=2 * dc,
                        out_dtypes=[F32, BF16], name="l0_out_proj", extras=[("tile", xs0), ("vec", mods0)],
                        epilogue=lambda acc, rows, res, mv: (res + _seg_pick(mv, 2, rows, n_x) * acc, acc))
    xs2, mlp0, ride1, ride2 = _mlp_fwd(
        xs1, mods0, wfull['l0_mlp_w1'], wfull['l0_mlp_w2'], n_x=n_x, tm=tm_t, tag="l0",
        rider1=gather_of('l1_qkv_w', 'l1_out_w'), rider2=gather_of('l1_mlp_w1'))
    wfull['l1_qkv_w'], wfull['l1_out_w'] = ride1
    wfull['l1_mlp_w1'], = ride2

    h2 = _rms_mod_fwd(xs2, mods1, k_shift=0, n_x=n_x, name="l1_norm1")
    qkv, = _matmul(h2, wfull['l1_qkv_w'], mode="nn", tm=tm_t, tn=_div(l1_qkv_w.shape[1] * N_DEV, (1024, 768, 512)),
                   tk=d, out_dtypes=[F32], name="l1_qkv_proj")
    cos, sin = _rope_tables(n_x, n_ctx)
    gq, gk = _row8(l1_q_norm_g), _row8(l1_k_norm_g)
    q, k, v = _qk_prep_fwd(qkv, gq, gk, cos, sin, d_q=d_q, name="l1_qk_prep")
    o, lse = _attention_fwd(q, k, v.T, n_x=n_x, name="l1_attention")
    x3, mix1 = _matmul(o, wfull['l1_out_w'], mode="nn", tm=tm_x, tn=_div(d, (1024, 512)), tk=d_q,
                       out_dtypes=[F32, BF16], name="l1_out_proj", extras=[("tile", xs2), ("vec", mods1)],
                       epilogue=lambda acc, rows, res, mv: (res + mv[4:5] * acc, acc))
    x4, mlp1, (wfull['l1_mlp_w2'],), _ = _mlp_fwd(x3, mods1, wfull['l1_mlp_w1'], None, n_x=n_x, tm=tm_x, tag="l1",
                                                  rider1=gather_of('l1_mlp_w2'))

    dx4, loss_part, dfinal_g, dbranch1, dmods1 = _final_loss(
        x4, _row8(final_g), loss_target[0], mlp1["branch"], mods1, k_gate=5, name="loss_head")
    loss = lax.psum(loss_part[0, 0], MESH_AXES)

    recv = {}
    dx3, dw1_1, dm, dmix1, _, _, recv['l1_mlp_w2'] = _mlp_bwd(
        dx4, dbranch1, mlp1, mods1, wfull['l1_mlp_w1'], wfull['l1_mlp_w2'], mix1, n_x=n_x, tm=tm_x, tag="l1")
    dmods1 = dmods1 + dm
    do, = _matmul(dmix1, wfull['l1_out_w'], mode="nt", tm=tm_x, tn=_div(d_q, (1024, 512)), tk=d,
                  out_dtypes=[BF16], name="l1_out_dx")
    dw_out1, = _matmul(o, dmix1, mode="tn", tm=_div(d_q, (1024, 512)), tn=_div(d, (1024, 512)),
                       tk=_div(n_x, TOKEN_TK), out_dtypes=[BF16], name="l1_out_dw")
    dq, dk_t, dv_t = _attention_bwd(q, k, v, o, lse, do, n_x=n_x, name="l1_attention_bwd")
    dk, dv = dk_t.T, dv_t.T
    dqkv, dgq, dgk = _qk_prep_bwd(qkv, dq, dk, dv, gq, gk, cos, sin, n_x=n_x, name="l1_qk_prep_bwd")
    tkt = _div(t, TOKEN_TK)
    dh2, recv['l1_out_w'] = _matmul(dqkv, wfull['l1_qkv_w'], mode="nt", tm=tm_t, tn=_div(d, (1024, 512)),
                                    tk=dqkv.shape[1], out_dtypes=[F32], name="l1_qkv_dx",
                                    rider=_exchange_of([(dw_out1, SHARDED['l1_out_w'])]))
    dw_qkv, = _matmul(h2, dqkv, mode="tn", tm=_div(d, (1024, 512)), tn=_div(dqkv.shape[1], (1024, 768, 512)),
                      tk=tkt, out_dtypes=[BF16], name="l1_qkv_dw")
    dxs2, dm, dbranch0, dmods0 = _rms_mod_bwd(xs2, mods1, dh2, dx3, k_shift=0, n_x=n_x, name="l1_norm1_bwd",
                                              dres_rows=n_x, gate=(mlp0["branch"], mods0, 5))
    dmods1 = dmods1 + dm

    dxs1, dw1_0, dm, dmix0, (recv['l1_qkv_w'],), (recv['l1_mlp_w1'],), recv['l0_mlp_w2'] = _mlp_bwd(
        dxs2, dbranch0, mlp0, mods0, wfull['l0_mlp_w1'], wfull['l0_mlp_w2'], mix0, n_x=n_x, tm=tm_t, tag="l0",
        ride_dx2=[(dw_qkv, SHARDED['l1_qkv_w'])], ride_dw2=[(dw1_1, SHARDED['l1_mlp_w1'])])
    dmods0 = dmods0 + dm
    dy0, = _matmul(dmix0, wfull['l0_out_w'], mode="nt", tm=tm_t, tn=_div(2 * dc, (1024, 512)), tk=d,
                   out_dtypes=[F32], name="l0_out_dx")
    dw_out0, = _matmul(y0, dmix0, mode="tn", tm=_div(2 * dc, (1024, 512)), tn=_div(d, (1024, 512)),
                       tk=tkt, out_dtypes=[BF16], name="l0_out_dw")
    dz, dconv_w, dconv_b, dln_g, dln_b, dpool_w, dpool_scale, recv['l0_mlp_w1'] = _mixer0_bwd(
        dy0, z, cv, dsave, conv_w_full, _row8(l0_conv_ln_g), _row8(l0_conv_ln_b), pool_w_full,
        _row8(l0_pool_scale), n_tap=n_tap, n_x=n_x, name="l0_mixer_bwd",
        rider=Rider("exchange", [dw1_0], [SHARDED['l0_mlp_w1']]))
    dw_in0, recv['l0_out_w'] = _matmul(h0, dz, mode="tn", tm=_div(d, (1024, 512)),
                                       tn=_div(3 * dc, (1024, 768, 512, 384)), tk=tkt, out_dtypes=[BF16],
                                       name="l0_in_dw", rider=_exchange_of([(dw_out0, SHARDED['l0_out_w'])]))
    dh0, recv['l0_in_w'] = _matmul(dz, wfull['l0_in_w'], mode="nt", tm=tm_t, tn=_div(d, (1024, 512)), tk=3 * dc,
                                   out_dtypes=[F32], name="l0_in_dx",
                                   rider=_exchange_of([(dw_in0, SHARDED['l0_in_w'])]))
    dx0, dm = _rms_mod_bwd(xs0, mods0, dh0, dxs1, k_shift=0, n_x=n_x, name="l0_norm1_bwd", out_rows=n_x)
    dmods0 = dmods0 + dm
    grad_x = dx0[None]

    def dmod_rows(dm16):
        rows = dm16[:2 * n_mod].reshape(n_mod, 2, d).transpose(1, 0, 2).reshape(2, n_mod * d)
        return jnp.pad(rows, ((0, 6), (0, 0)))
    dm_gathered = _all_gather([jnp.concatenate([dmod_rows(dmods0), dmod_rows(dmods1)], axis=0)], [0],
                              name="gather_dmods")[0]
    dm0, dm1, db0, db1 = _assemble_dmods(dm_gathered, name="assemble_dmods")
    out_g = {'l0_ada_b': db0[0], 'l1_ada_b': db1[0]}
    ds_part = jnp.zeros((16, d), F32)
    for nm, lw, dmf in (('l0_ada_w', l0_ada_w, dm0), ('l1_ada_w', l1_ada_w, dm1)):
        dm_cols = lax.dynamic_slice_in_dim(dmf, me * ada_cols, ada_cols, axis=1)
        out_g[nm], = _matmul(s16, dm_cols, mode="tn", tm=_div(d, (1024, 512)),
                             tn=_div(ada_cols, (512, 384, 256, 128)), tk=16, out_dtypes=[F32], name=f"{nm}_dw")
        ds_part = ds_part + _matmul(dm_cols, lw, mode="nt", tm=16, tn=_div(d, (1024, 512)),
                                    tk=_div(ada_cols, (512, 384, 256, 128)), out_dtypes=[F32], name=f"{nm}_dx")[0]

    small = {'l0_conv_b': dconv_b[0], 'l0_conv_ln_g': dln_g[0], 'l0_conv_ln_b': dln_b[0],
             'l0_pool_scale': dpool_scale[0], 'l1_q_norm_g': dgq[0], 'l1_k_norm_g': dgk[0],
             'final_g': dfinal_g[0], 'dsilu_ctx': ds_part[N_DEV], 'l0_conv_w': dconv_w[:-1].reshape(-1),
             'l0_pool_w': dpool_w.reshape(-1)}
    flat = jnp.concatenate([small[nm] for nm in small])
    rows = -(-flat.shape[0] // 1024) * 8
    packed = jnp.pad(flat, (0, rows * 128 - flat.shape[0])).reshape(rows, 128)
    summed = _sum_slots(_all_gather([packed], [0], name="gather_small_grads")[0], rows,
                        name="sum_small_grads").reshape(-1)
    off = 0
    for nm in small:
        size = small[nm].shape[0]
        small[nm] = summed[off:off + size]
        off += size
    out_g['c_ctx'] = _silu_grad(_row8(c_ctx), _row8(small['dsilu_ctx']), name="c_ctx_grad")[0]
    for nm in REPLICATED_SMALL:
        out_g[nm] = small[nm]
    conv_cols = l0_conv_w.shape[1]
    out_g['l0_conv_w'] = lax.dynamic_slice_in_dim(small['l0_conv_w'].reshape(n_tap, dc), me * conv_cols, conv_cols,
                                                  axis=1)
    out_g['l0_pool_w'] = lax.dynamic_slice_in_dim(small['l0_pool_w'].reshape(n_grp, pg, pg), me * (pg // N_DEV),
                                                  pg // N_DEV, axis=1)

    delta, new_m, new_v = {}, {}, {}
    for nm in names:
        out_g[nm], delta[nm], new_m[nm], new_v[nm] = _adamw(p[nm], p['m_' + nm], p['v_' + nm], recv=recv[nm],
                                                            name=f"adamw_{nm}")
    for nm in ('l0_ada_w', 'l1_ada_w'):
        out_g[nm], delta[nm], new_m[nm], new_v[nm] = _adamw(p[nm], p['m_' + nm], p['v_' + nm], grad=out_g[nm],
                                                            name=f"adamw_{nm}")
    for nm in WEIGHTS:
        if nm in delta:
            continue
        shape = p[nm].shape
        as2d = lambda a: a.reshape(1, -1) if a.ndim == 1 else a.reshape(-1, a.shape[-1])
        res = _adamw(as2d(p[nm]), as2d(p['m_' + nm]), as2d(p['v_' + nm]), grad=as2d(out_g[nm]), name=f"adamw_{nm}")
        out_g[nm], delta[nm], new_m[nm], new_v[nm] = [r.reshape(shape) for r in res]

    return (loss, grad_x, *[out_g[nm] for nm in WEIGHTS], *[delta[nm] for nm in WEIGHTS],
            *[new_m[nm] for nm in WEIGHTS], *[new_v[nm] for nm in WEIGHTS])
```

```python
import functools

import jax
import jax.numpy as jnp
from jax import lax
from jax.experimental import pallas as pl
from jax.experimental.pallas import tpu as pltpu

F32 = jnp.float32
BF16 = jnp.bfloat16
N_DEV = 8
MESH_AXES = ("x", "y", "c")
EPS = 1e-6
HEAD_DIM = 128
POOL_WINDOWS = (2, 4, 8, 16)
GRID_W = 64
ROPE_THETA = 10000.0
ATTN_SCALE = HEAD_DIM ** -0.5
LOG2_E = 1.4426950408889634
Q_SCALE_LOG2 = ATTN_SCALE * LOG2_E
HALO = 16
ADAM_LR, ADAM_B1, ADAM_B2, ADAM_EPS, ADAM_WD, ADAM_STEP = 0.001, 0.9, 0.999, 1e-08, 0.01, 10
VMEM_CAP_MB = 60
ADAMW_BLOCK_ELEMS = 1 << 18
TOKEN_TK = (2816, 2048, 1408, 1024, 768, 640, 512, 128)
DEEP_TM = (384, 512, 256, 128)
DEEP_TN = (512,)
WIDE_TN = (2048, 1024, 512)
WIDE_TILE_ELEMS = 768 * 2048
MESH_ID = pl.DeviceIdType.MESH


def _div(n, prefs):
    for p in prefs:
        if n % p == 0:
            return p
    raise ValueError(f"no tile for {n} in {prefs}")


def _params(sem, vmem_mb):
    return pltpu.CompilerParams(dimension_semantics=sem, vmem_limit_bytes=min(vmem_mb, VMEM_CAP_MB) << 20)


def _sigmoid(x):
    return 1.0 / (1.0 + jnp.exp(-x))


def _silu(x):
    return x * _sigmoid(x)


def _rms(x):
    return x * lax.rsqrt(jnp.mean(x * x, axis=-1, keepdims=True) + EPS)


def _rms_mod(x, shift, scale):
    return _rms(x) * (1.0 + scale) + shift


def _layernorm(x, g, b):
    mu = jnp.mean(x, axis=-1, keepdims=True)
    var = jnp.mean(jnp.square(x - mu), axis=-1, keepdims=True)
    return (x - mu) * lax.rsqrt(var + EPS) * g + b


def _ln_silu(x, g, b):
    return _silu(_layernorm(x, g, b))


def _matmul(a, b, *, mode, tm, tn, tk, out_dtypes, name, extras=(), epilogue=None, rider=None):
    if mode == "tn":
        kdim, m = a.shape
        n = b.shape[1]
    else:
        m, kdim = a.shape
        n = b.shape[0] if mode == "nt" else b.shape[1]
    assert m % tm == 0 and n % tn == 0 and kdim % tk == 0, (name, m, n, kdim, tm, tn, tk)
    nk = kdim // tk
    n_ex = len(extras)
    n_out = len(out_dtypes)

    def body(a_ref, b_ref, *rest):
        ex = rest[:n_ex]
        outs = rest[n_ex:n_ex + n_out]
        acc_ref = rest[n_ex + n_out] if nk > 1 else None
        k = pl.program_id(2)
        av = a_ref[...].astype(BF16)
        bv = b_ref[...].astype(BF16)
        dims = {"nn": ((1,), (0,)), "nt": ((1,), (1,)), "tn": ((0,), (0,))}[mode]
        part = lax.dot_general(av, bv, (dims, ((), ())), preferred_element_type=F32)

        rows = pl.program_id(0) * tm + lax.broadcasted_iota(jnp.int32, (tm, 1), 0)

        def finish(acc):
            res = (acc,) if epilogue is None else epilogue(acc, rows, *[e[...] for e in ex])
            for o, r in zip(outs, res):
                o[...] = r.astype(o.dtype)

        if nk == 1:
            finish(part)
        else:
            @pl.when(k == 0)
            def _():
                acc_ref[...] = part

            @pl.when(k > 0)
            def _():
                acc_ref[...] += part

            @pl.when(k == nk - 1)
            def _():
                finish(acc_ref[...])

    if mode == "tn":
        a_spec = pl.BlockSpec((tk, tm), lambda i, j, k: (k, i))
        b_spec = pl.BlockSpec((tk, tn), lambda i, j, k: (k, j))
    else:
        a_spec = pl.BlockSpec((tm, tk), lambda i, j, k: (i, k))
        b_spec = (pl.BlockSpec((tn, tk), lambda i, j, k: (j, k)) if mode == "nt"
                  else pl.BlockSpec((tk, tn), lambda i, j, k: (k, j)))
    ex_specs, ex_arrays, ex_bytes = [], [], 0
    for kind, arr in extras:
        ex_arrays.append(arr)
        if kind == "tile":
            ex_specs.append(pl.BlockSpec((tm, tn), lambda i, j, k: (i, j)))
            ex_bytes += tm * tn * arr.dtype.itemsize
        else:
            ex_specs.append(pl.BlockSpec((arr.shape[0], tn), lambda i, j, k: (0, j)))
            ex_bytes += arr.shape[0] * tn * 4
    blocks = tm * tk * a.dtype.itemsize + tk * tn * b.dtype.itemsize + ex_bytes
    blocks += sum(tm * tn * jnp.dtype(d).itemsize for d in out_dtypes)
    casts = sum(rows * cols * 2 for arr, rows, cols in ((a, tm, tk), (b, tk, tn)) if arr.dtype != BF16)
    vmem = (2 * blocks + 4 * tm * tn * 4 + casts) // (1 << 20) + 8
    return _pcall(
        body,
        grid=(m // tm, n // tn, nk),
        in_specs=[a_spec, b_spec] + ex_specs,
        out_specs=[pl.BlockSpec((tm, tn), lambda i, j, k: (i, j)) for _ in out_dtypes],
        out_shape=[jax.ShapeDtypeStruct((m, n), d) for d in out_dtypes],
        scratch_shapes=[pltpu.VMEM((tm, tn), F32)] if nk > 1 else [],
        semantics=("parallel", "parallel", "arbitrary"), vmem_mb=vmem, name=name,
        operands=[a, b, *ex_arrays], rider=rider)


def _seg_pick(vec, k, rows, n_x):
    return jnp.where(rows < n_x, vec[2 * k:2 * k + 1], vec[2 * k + 1:2 * k + 2])


def _zero_accs(i, accs):
    @pl.when(i == 0)
    def _():
        for a in accs:
            a[...] = jnp.zeros_like(a)


def _rms_mod_fwd(xs, mods, *, k_shift, n_x, name, rider=None):
    t, d = xs.shape
    tm = _div(t, (256, 128))

    def body(x_ref, mods_ref, h_ref):
        seg = (pl.program_id(0) * tm >= n_x).astype(jnp.int32)
        shift = mods_ref[pl.ds(2 * k_shift + seg, 1), :]
        scale = mods_ref[pl.ds(2 * k_shift + 2 + seg, 1), :]
        h_ref[...] = _rms_mod(x_ref[...], shift, scale).astype(BF16)

    res = _pcall(
        body, grid=(t // tm,),
        in_specs=[pl.BlockSpec((tm, d), lambda i: (i, 0)), pl.BlockSpec((16, d), lambda i: (0, 0))],
        out_specs=[pl.BlockSpec((tm, d), lambda i: (i, 0))],
        out_shape=[jax.ShapeDtypeStruct((t, d), BF16)],
        scratch_shapes=[], semantics=("parallel",), vmem_mb=32, name=name, operands=[xs, mods], rider=rider)
    return res[0] if rider is None else res


def _gate_part(dxv, seg, k_gate, br_ref, gmods_ref, db_ref, dgm_ref):
    r_gate = 2 * k_gate + seg
    db_ref[...] = (dxv * gmods_ref[pl.ds(r_gate, 1), :]).astype(BF16)
    dgm_ref[pl.ds(r_gate, 1), :] += jnp.sum(dxv * br_ref[...].astype(F32), axis=0, keepdims=True)


def _rms_mod_bwd(xs, mods, dh, dres, *, k_shift, n_x, name, rider=None, gate=None, dres_rows=None, out_rows=None):
    t, d = xs.shape
    tm = _div(t, (256, 128))
    n_gate = 2 if gate is not None else 0

    def body(x_ref, mods_ref, dh_ref, dres_ref, *rest):
        gate_in, (dx_ref, dmods_ref), gate_out = rest[:n_gate], rest[n_gate:n_gate + 2], rest[n_gate + 2:]
        i = pl.program_id(0)
        _zero_accs(i, [dmods_ref, *gate_out[1:]])
        seg = (i * tm >= n_x).astype(jnp.int32)
        r_shift = 2 * k_shift + seg
        r_scale = 2 * k_shift + 2 + seg
        shift = mods_ref[pl.ds(r_shift, 1), :]
        scale = mods_ref[pl.ds(r_scale, 1), :]
        _, vjp = jax.vjp(_rms_mod, x_ref[...], shift, scale)
        dx, dshift, dscale = vjp(dh_ref[...].astype(F32))
        dres_v = dres_ref[...]
        if dres_rows is not None:
            dres_v = dres_v * jnp.where(i * tm < dres_rows, 1.0, 0.0)
        dx = dres_v + dx
        if out_rows is None:
            dx_ref[...] = dx
        else:
            @pl.when(i * tm < out_rows)
            def _():
                dx_ref[...] = dx
        dmods_ref[pl.ds(r_shift, 1), :] += dshift
        dmods_ref[pl.ds(r_scale, 1), :] += dscale
        if gate is not None:
            _gate_part(dx, seg, gate[2], *gate_in, *gate_out)

    def clamped(rows):
        return pl.BlockSpec((tm, d), lambda i: (jnp.minimum(i, rows // tm - 1), 0))

    row = pl.BlockSpec((tm, d), lambda i: (i, 0))
    vec = pl.BlockSpec((16, d), lambda i: (0, 0))
    in_specs = [row, vec, row, row if dres_rows is None else clamped(dres_rows)]
    out_specs = [row if out_rows is None else clamped(out_rows), vec]
    out_shape = [jax.ShapeDtypeStruct((t if out_rows is None else out_rows, d), F32),
                 jax.ShapeDtypeStruct((16, d), F32)]
    operands = [xs, mods, dh, dres]
    if gate is not None:
        in_specs += [row, vec]
        out_specs += [row, vec]
        out_shape += [jax.ShapeDtypeStruct((t, d), BF16), jax.ShapeDtypeStruct((16, d), F32)]
        operands += [gate[0], gate[1]]
    return _pcall(body, grid=(t // tm,), in_specs=in_specs, out_specs=out_specs, out_shape=out_shape,
                  scratch_shapes=[], semantics=("arbitrary",), vmem_mb=48, name=name, operands=operands, rider=rider)


def _final_loss(xs, g, target, branch, gmods, *, k_gate, name):
    t, d = xs.shape
    tm = _div(t, (256, 128))

    def loss_fn(x, gv, tgt):
        err = _rms(x) * gv - tgt
        return 0.5 * jnp.sum(jnp.mean(jnp.square(err), axis=-1))

    def body(x_ref, g_ref, t_ref, br_ref, gmods_ref, dx_ref, loss_ref, dg_ref, db_ref, dgm_ref):
        i = pl.program_id(0)
        _zero_accs(i, [loss_ref, dg_ref, dgm_ref])
        val, vjp = jax.vjp(loss_fn, x_ref[...], g_ref[0:1, :], t_ref[...])
        dx, dg, _ = vjp(jnp.ones((), F32))
        dx_ref[...] = dx
        loss_ref[...] += val
        dg_ref[0:1, :] += dg
        _gate_part(dx, 0, k_gate, br_ref, gmods_ref, db_ref, dgm_ref)

    row = pl.BlockSpec((tm, d), lambda i: (i, 0))
    vec16 = pl.BlockSpec((16, d), lambda i: (0, 0))
    return pl.pallas_call(
        body, grid=(t // tm,),
        in_specs=[row, pl.BlockSpec((8, d), lambda i: (0, 0)), row, row, vec16],
        out_specs=[row, pl.BlockSpec((8, 128), lambda i: (0, 0)), pl.BlockSpec((8, d), lambda i: (0, 0)), row, vec16],
        out_shape=[jax.ShapeDtypeStruct((t, d), F32), jax.ShapeDtypeStruct((8, 128), F32),
                   jax.ShapeDtypeStruct((8, d), F32), jax.ShapeDtypeStruct((t, d), BF16),
                   jax.ShapeDtypeStruct((16, d), F32)],
        compiler_params=_params(("arbitrary",), 48), name=name,
    )(xs, g, target, branch, gmods)


def _halo_specs(r, width, col, t):
    h_per = r // HALO

    def prev(i):
        return (jnp.maximum(i * h_per - 1, 0), col)

    def nxt(i):
        return (jnp.minimum((i + 1) * h_per, t // HALO - 1), col)

    return (pl.BlockSpec((HALO, width), prev), pl.BlockSpec((r, width), lambda i: (i, col)),
            pl.BlockSpec((HALO, width), nxt))


def _seg_geometry(i, r, n_x, t):
    row0 = i * r
    in_ctx = row0 >= n_x
    first = jnp.logical_or(row0 == 0, row0 == n_x)
    last = jnp.logical_or(row0 + r == n_x, row0 + r == t)
    seg_start = jnp.where(in_ctx, n_x, 0)
    seg_len = jnp.where(in_ctx, t - n_x, n_x)
    return row0, first, last, seg_start, seg_len


ROW_CHUNK = 64
LANE_CHUNK = 128


def _chunks(width, rows, col0=0):
    lanes = min(LANE_CHUNK, width)
    return [(slice(col0 + c, col0 + c + lanes), r0) for c in range(0, width, lanes) for r0 in range(0, rows, ROW_CHUNK)]


def _pool_count(tpos, w, seg_len):
    return (jnp.minimum(tpos + w // 2, seg_len) - jnp.maximum(tpos - w // 2, 0)).astype(F32)


def _mixer0_fwd(z, conv_w, conv_b, ln_g, ln_b, pool_w, pool_scale, *, n_tap, n_x, name, rider=None):
    t, dc = z.shape[0], z.shape[1] // 3
    n_grp, pg = pool_w.shape[0], pool_w.shape[1]
    r = _div(t - n_x, (256, 128))
    assert n_x % r == 0 and pg * n_grp == dc
    half = n_tap // 2
    assert half < HALO and max(POOL_WINDOWS) // 2 <= HALO

    def body(ap, ac, an, gp, gc, gn, pp, pc, pn, w_ref, cb_ref, lg_ref, lb_ref, pw_ref, ps_ref,
             y_ref, cv_ref, d_ref, uwin, pwin):
        i = pl.program_id(0)
        row0, first, last, seg_start, seg_len = _seg_geometry(i, r, n_x, t)
        keep_prev = jnp.where(first, 0.0, 1.0)
        keep_next = jnp.where(last, 0.0, 1.0)
        uwin[0:HALO, :] = ap[...] * _sigmoid(gp[...]) * keep_prev
        uwin[HALO:HALO + r, :] = ac[...] * _sigmoid(gc[...])
        uwin[HALO + r:, :] = an[...] * _sigmoid(gn[...]) * keep_next
        pwin[0:HALO, :] = pp[...] * keep_prev
        pwin[HALO:HALO + r, :] = pc[...]
        pwin[HALO + r:, :] = pn[...] * keep_next
        for cols, r0 in _chunks(dc, r):
            acc = jnp.zeros((ROW_CHUNK, cols.stop - cols.start), F32) + cb_ref[0:1, cols]
            for k in range(n_tap):
                off = HALO - half + k + r0
                acc = acc + w_ref[k:k + 1, cols] * uwin[off:off + ROW_CHUNK, cols]
            cv_ref[r0:r0 + ROW_CHUNK, cols] = acc
        y_ref[:, 0:dc] = _ln_silu(cv_ref[...], lg_ref[0:1, :], lb_ref[0:1, :]).astype(BF16)
        tpos = row0 - seg_start + lax.broadcasted_iota(jnp.int32, (r, 1), 0)
        for g, w in enumerate(POOL_WINDOWS):
            cnt = _pool_count(tpos, w, seg_len)
            for cols, r0 in _chunks(pg, r, g * pg):
                s = jnp.zeros((ROW_CHUNK, cols.stop - cols.start), F32)
                for j in range(-(w // 2), w // 2):
                    s = s + pwin[HALO + j + r0:HALO + j + r0 + ROW_CHUNK, cols]
                diff = s / cnt[r0:r0 + ROW_CHUNK] - pwin[HALO + r0:HALO + r0 + ROW_CHUNK, cols]
                d_ref[r0:r0 + ROW_CHUNK, cols] = diff.astype(BF16)
            cols = slice(g * pg, (g + 1) * pg)
            pm = jnp.dot(d_ref[:, cols], pw_ref[g].astype(BF16), preferred_element_type=F32)
            y_ref[:, dc + g * pg:dc + (g + 1) * pg] = (pm * ps_ref[0:1, cols]).astype(BF16)

    vec = lambda rows, width: pl.BlockSpec((rows, width), lambda i: (0, 0))
    in_specs = [*_halo_specs(r, dc, 0, t), *_halo_specs(r, dc, 1, t), *_halo_specs(r, dc, 2, t),
                vec(conv_w.shape[0], dc), vec(8, dc), vec(8, dc), vec(8, dc),
                pl.BlockSpec((n_grp, pg, pg), lambda i: (0, 0, 0)), vec(8, dc)]
    return _pcall(
        body, grid=(t // r,), in_specs=in_specs,
        out_specs=[pl.BlockSpec((r, 2 * dc), lambda i: (i, 0)), pl.BlockSpec((r, dc), lambda i: (i, 0)),
                   pl.BlockSpec((r, dc), lambda i: (i, 0))],
        out_shape=[jax.ShapeDtypeStruct((t, 2 * dc), BF16), jax.ShapeDtypeStruct((t, dc), F32),
                   jax.ShapeDtypeStruct((t, dc), BF16)],
        scratch_shapes=[pltpu.VMEM((r + 2 * HALO, dc), F32), pltpu.VMEM((r + 2 * HALO, dc), F32)],
        semantics=("parallel",), vmem_mb=40, name=name,
        operands=[*([z] * 9), conv_w, conv_b, ln_g, ln_b, pool_w, pool_scale], rider=rider)


def _mixer0_bwd(dy, z, cv, dsave, conv_w, ln_g, ln_b, pool_w, pool_scale, *, n_tap, n_x, name, rider=None):
    t, dc = cv.shape
    n_grp, pg = pool_w.shape[0], pool_w.shape[1]
    r = _div(t - n_x, (256, 128))
    half = n_tap // 2
    dyp_, zp_, cvp_ = dy, z, cv
    rw = r + 2 * HALO

    def body(dcp, dcc, dcn, dpp, dpc, dpn, cvp, cvc, cvn, ap, ac, an, gp, gc, gn, d_ref,
             w_ref, lg_ref, lb_ref, pw_ref, ps_ref,
             dz_ref, dw_ref, dcb_ref, dlg_ref, dlb_ref, dpw_ref, dps_ref, uwin, dcvwin, ewin, ddwin):
        i = pl.program_id(0)
        _zero_accs(i, [dw_ref, dcb_ref, dlg_ref, dlb_ref, dpw_ref, dps_ref])
        row0, first, last, seg_start, seg_len = _seg_geometry(i, r, n_x, t)
        keep_prev = jnp.where(first, 0.0, 1.0)
        keep_next = jnp.where(last, 0.0, 1.0)
        lg, lb = lg_ref[0:1, :], lb_ref[0:1, :]
        _, vjp = jax.vjp(_ln_silu, cvc[...], lg, lb)
        dcv, dlg, dlb = vjp(dcc[...])
        dlg_ref[0:1, :] += dlg
        dlb_ref[0:1, :] += dlb
        dcb_ref[0:1, :] += jnp.sum(dcv, axis=0, keepdims=True)
        dcvwin[HALO:HALO + r, :] = dcv
        for halo_cv, halo_dy, keep, lo in ((cvp, dcp, keep_prev, 0), (cvn, dcn, keep_next, HALO + r)):
            _, vjp_h = jax.vjp(lambda v: _ln_silu(v, lg, lb), halo_cv[...])
            dcvwin[lo:lo + HALO, :] = vjp_h(halo_dy[...])[0] * keep
        uwin[0:HALO, :] = ap[...] * _sigmoid(gp[...]) * keep_prev
        uwin[HALO:HALO + r, :] = ac[...] * _sigmoid(gc[...])
        uwin[HALO + r:, :] = an[...] * _sigmoid(gn[...]) * keep_next
        for cols, r0 in _chunks(dc, r):
            du = jnp.zeros((ROW_CHUNK, cols.stop - cols.start), F32)
            for k in range(n_tap):
                off = HALO + half - k + r0
                du = du + w_ref[k:k + 1, cols] * dcvwin[off:off + ROW_CHUNK, cols]
            rows = slice(r0, r0 + ROW_CHUNK)
            sig = _sigmoid(gc[rows, cols])
            dz_ref[rows, cols] = (du * sig).astype(BF16)
            dz_ref[rows, dc + cols.start:dc + cols.stop] = (du * ac[rows, cols] * sig * (1.0 - sig)).astype(BF16)
        for c0 in range(0, dc, LANE_CHUNK):
            cols = slice(c0, c0 + LANE_CHUNK)
            taps = [jnp.zeros((8, LANE_CHUNK), F32) for _ in range(n_tap)]
            for r0 in range(0, r, ROW_CHUNK):
                dcv_c = dcvwin[HALO + r0:HALO + r0 + ROW_CHUNK, cols]
                for k in range(n_tap):
                    off = HALO - half + k + r0
                    prod = dcv_c * uwin[off:off + ROW_CHUNK, cols]
                    taps[k] = taps[k] + functools.reduce(
                        jnp.add, [prod[8 * s:8 * s + 8] for s in range(ROW_CHUNK // 8)])
            for k in range(n_tap):
                dw_ref[k:k + 1, cols] += jnp.sum(taps[k], axis=0, keepdims=True)
        twin = row0 - seg_start - HALO + lax.broadcasted_iota(jnp.int32, (rw, 1), 0)
        for g, w in enumerate(POOL_WINDOWS):
            cols = slice(g * pg, (g + 1) * pg)
            wg = pw_ref[g].astype(BF16)
            scale = ps_ref[0:1, cols]
            dyp_c = dpc[:, cols]
            dpm_win = jnp.concatenate([dpp[:, cols] * keep_prev, dyp_c, dpn[:, cols] * keep_next], axis=0) * scale
            dd_win = lax.dot_general(dpm_win.astype(BF16), wg, (((1,), (1,)), ((), ())), preferred_element_type=F32)
            cnt = jnp.maximum(_pool_count(twin, w, seg_len), 1.0)
            ddwin[:, cols] = dd_win
            ewin[:, cols] = dd_win / cnt
            for ccols, r0 in _chunks(pg, r, g * pg):
                dup = -ddwin[HALO + r0:HALO + r0 + ROW_CHUNK, ccols]
                for j in range(-(w // 2) + 1, w // 2 + 1):
                    dup = dup + ewin[HALO + j + r0:HALO + j + r0 + ROW_CHUNK, ccols]
                dz_ref[r0:r0 + ROW_CHUNK, 2 * dc + ccols.start:2 * dc + ccols.stop] = dup.astype(BF16)
            dsv = d_ref[:, cols]
            pm = jnp.dot(dsv, wg, preferred_element_type=F32)
            dps_ref[0:1, cols] += jnp.sum(dyp_c * pm, axis=0, keepdims=True)
            dpw_ref[g] += lax.dot_general(dsv, (dyp_c * scale).astype(BF16), (((0,), (0,)), ((), ())),
                                          preferred_element_type=F32)

    vec = lambda rows, width: pl.BlockSpec((rows, width), lambda i: (0, 0))
    grp = pl.BlockSpec((n_grp, pg, pg), lambda i: (0, 0, 0))
    in_specs = [*_halo_specs(r, dc, 0, t), *_halo_specs(r, dc, 1, t), *_halo_specs(r, dc, 0, t),
                *_halo_specs(r, dc, 0, t), *_halo_specs(r, dc, 1, t), pl.BlockSpec((r, dc), lambda i: (i, 0)),
                vec(conv_w.shape[0], dc), vec(8, dc), vec(8, dc), grp, vec(8, dc)]
    return _pcall(
        body, grid=(t // r,), in_specs=in_specs,
        out_specs=[pl.BlockSpec((r, 3 * dc), lambda i: (i, 0)), vec(conv_w.shape[0], dc), vec(8, dc), vec(8, dc),
                   vec(8, dc), grp, vec(8, dc)],
        out_shape=[jax.ShapeDtypeStruct((t, 3 * dc), BF16), jax.ShapeDtypeStruct(conv_w.shape, F32),
                   jax.ShapeDtypeStruct((8, dc), F32), jax.ShapeDtypeStruct((8, dc), F32),
                   jax.ShapeDtypeStruct((8, dc), F32), jax.ShapeDtypeStruct(pool_w.shape, F32),
                   jax.ShapeDtypeStruct((8, dc), F32)],
        scratch_shapes=[pltpu.VMEM((rw, dc), F32)] * 4,
        semantics=("arbitrary",), vmem_mb=VMEM_CAP_MB, name=name,
        operands=[dyp_, dyp_, dyp_, dyp_, dyp_, dyp_, cvp_, cvp_, cvp_, zp_, zp_, zp_, zp_, zp_, zp_, dsave,
                  conv_w, ln_g, ln_b, pool_w, pool_scale], rider=rider)


def _swap_halves(x):
    lane = lax.broadcasted_iota(jnp.int32, x.shape, 1)
    quarter = HEAD_DIM // 4
    return jnp.where(lane % (2 * quarter) < quarter,
                     pltpu.roll(x, HEAD_DIM - quarter, 1), pltpu.roll(x, quarter, 1))


def _rope_tables(n_x, n_ctx):
    half = HEAD_DIM // 4
    freqs = ROPE_THETA ** (-jnp.arange(half, dtype=F32) / half)
    tok = jnp.arange(n_x)
    row = (tok // GRID_W).astype(F32)[:, None] * freqs[None, :]
    col = (tok % GRID_W).astype(F32)[:, None] * freqs[None, :]
    cos = jnp.concatenate([jnp.cos(row), jnp.cos(row), jnp.cos(col), jnp.cos(col)], axis=1)
    sin = jnp.concatenate([-jnp.sin(row), jnp.sin(row), -jnp.sin(col), jnp.sin(col)], axis=1)
    cos = jnp.concatenate([cos, jnp.ones((n_ctx, HEAD_DIM), F32)], axis=0)
    sin = jnp.concatenate([sin, jnp.zeros((n_ctx, HEAD_DIM), F32)], axis=0)
    return cos, sin


def _norm_g(x, g):
    return _rms(x) * g


def _qk_prep_fwd(qkv, gq, gk, cos, sin, *, d_q, name):
    t, width = qkv.shape
    d_kv = (width - d_q) // 2
    tm = _div(t, (256, 128))

    def body(qkv_ref, gq_ref, gk_ref, cos_ref, sin_ref, q_ref, k_ref, v_ref):
        cs, sn = cos_ref[...], sin_ref[...]
        for h in range((d_q + d_kv) // HEAD_DIM):
            g = gq_ref[0:1, :] if h * HEAD_DIM < d_q else gk_ref[0:1, :]
            xn = _norm_g(qkv_ref[:, h * HEAD_DIM:(h + 1) * HEAD_DIM], g)
            rot = xn * cs + _swap_halves(xn) * sn
            if h * HEAD_DIM < d_q:
                q_ref[:, h * HEAD_DIM:(h + 1) * HEAD_DIM] = (rot * Q_SCALE_LOG2).astype(BF16)
            else:
                k_ref[:, h * HEAD_DIM - d_q:(h + 1) * HEAD_DIM - d_q] = rot.astype(BF16)
        v_ref[...] = qkv_ref[:, d_q + d_kv:].astype(BF16)

    row = lambda w: pl.BlockSpec((tm, w), lambda i: (i, 0))
    vec = pl.BlockSpec((8, HEAD_DIM), lambda i: (0, 0))
    return pl.pallas_call(
        body, grid=(t // tm,),
        in_specs=[row(width), vec, vec, row(HEAD_DIM), row(HEAD_DIM)],
        out_specs=[row(d_q), row(d_kv), row(d_kv)],
        out_shape=[jax.ShapeDtypeStruct((t, d_q), BF16), jax.ShapeDtypeStruct((t, d_kv), BF16),
                   jax.ShapeDtypeStruct((t, d_kv), BF16)],
        compiler_params=_params(("parallel",), 32), name=name,
    )(qkv, gq, gk, cos, sin)


def _qk_prep_bwd(qkv, dq, dk, dv, gq, gk, cos, sin, *, n_x, name):
    t, width = qkv.shape
    d_q, d_kv = dq.shape[1], dk.shape[1]
    tm = _div(t, (256, 128))
    last_q = n_x // tm - 1

    def body(qkv_ref, dq_ref, dk_ref, dv_ref, gq_ref, gk_ref, cos_ref, sin_ref, out_ref, dgq_ref, dgk_ref):
        i = pl.program_id(0)
        _zero_accs(i, [dgq_ref, dgk_ref])
        is_x = jnp.where(i * tm < n_x, 1.0, 0.0)
        cs, sn = cos_ref[...], sin_ref[...]
        for h in range((d_q + d_kv) // HEAD_DIM):
            sl = slice(h * HEAD_DIM, (h + 1) * HEAD_DIM)
            if h * HEAD_DIM < d_q:
                g, dg_ref, dr = gq_ref[0:1, :], dgq_ref, dq_ref[:, sl] * is_x
            else:
                g, dg_ref = gk_ref[0:1, :], dgk_ref
                dr = dk_ref[:, h * HEAD_DIM - d_q:(h + 1) * HEAD_DIM - d_q]
            dxn = dr * cs + _swap_halves(dr * sn)
            _, vjp = jax.vjp(_norm_g, qkv_ref[:, sl], g)
            dx, dg = vjp(dxn)
            out_ref[:, sl] = dx.astype(BF16)
            dg_ref[0:1, :] += dg
        out_ref[:, d_q + d_kv:] = dv_ref[...].astype(BF16)

    row = lambda w: pl.BlockSpec((tm, w), lambda i: (i, 0))
    vec = pl.BlockSpec((8, HEAD_DIM), lambda i: (0, 0))
    return pl.pallas_call(
        body, grid=(t // tm,),
        in_specs=[row(width), pl.BlockSpec((tm, d_q), lambda i: (jnp.minimum(i, last_q), 0)), row(d_kv), row(d_kv),
                  vec, vec, row(HEAD_DIM), row(HEAD_DIM)],
        out_specs=[row(width), vec, vec],
        out_shape=[jax.ShapeDtypeStruct((t, width), BF16), jax.ShapeDtypeStruct((8, HEAD_DIM), F32),
                   jax.ShapeDtypeStruct((8, HEAD_DIM), F32)],
        compiler_params=_params(("arbitrary",), 40), name=name,
    )(qkv, dq, dk, dv, gq, gk, cos, sin)


ATTN_TQ = (256, 128)
ATTN_TK = (768, 640, 512, 384, 256, 128)


def _attention_fwd(q, k, v_t, *, n_x, name):
    t, d_kv = k.shape
    d_q = q.shape[1]
    kvh = d_kv // HEAD_DIM
    grp = d_q // d_kv
    tq = _div(n_x, ATTN_TQ)
    tk = _div(t, ATTN_TK)
    gw = grp * HEAD_DIM
    n_kv = t // tk
    n_pair = (n_kv - 1) // 2

    def fold8(x):
        return functools.reduce(jnp.add, [x[8 * r:8 * r + 8] for r in range(tk // 8)])

    def body(q_ref, k_ref, vt_ref, o_ref, lse_ref, m_ref, l_ref, acc_ref, qt_ref, s_even, s_odd):
        m_ref[...] = jnp.full_like(m_ref, -jnp.inf)
        l_ref[...] = jnp.zeros_like(l_ref)
        acc_ref[...] = jnp.zeros_like(acc_ref)
        for g in range(grp):
            qt_ref[g] = q_ref[:, g * HEAD_DIM:(g + 1) * HEAD_DIM].T

        def keys(j):
            return pl.ds(pl.multiple_of(j * tk, tk), tk)

        def scores(g, kc):
            return jnp.dot(kc, qt_ref[g], preferred_element_type=F32)

        def chunk(j, s_cur, s_next):
            vt = vt_ref[:, keys(j)]
            kn = k_ref[keys(j + 1), :] if s_next is not None else None
            for g in range(grp):
                s = s_cur[g]
                m_old = m_ref[g]
                m_new = jnp.maximum(m_old, jnp.max(s, axis=0, keepdims=True))
                alpha = jnp.exp2(m_old - m_new)
                p = jnp.exp2(s - m_new)
                if s_next is not None:
                    s_next[g] = scores(g, kn)
                l_ref[g] = alpha * l_ref[g] + fold8(p)
                acc_ref[g] = alpha * acc_ref[g] + jnp.dot(vt, p.astype(BF16), preferred_element_type=F32)
                m_ref[g] = m_new

        k0 = k_ref[keys(0), :]
        for g in range(grp):
            s_even[g] = scores(g, k0)

        def pair(i, carry):
            chunk(2 * i, s_even, s_odd)
            chunk(2 * i + 1, s_odd, s_even)
            return carry

        lax.fori_loop(0, n_pair, pair, 0)
        if n_kv - 2 * n_pair == 2:
            chunk(n_kv - 2, s_even, s_odd)
            chunk(n_kv - 1, s_odd, None)
        else:
            chunk(n_kv - 1, s_even, None)
        for g in range(grp):
            l = jnp.sum(l_ref[g], axis=0, keepdims=True)
            o_ref[:, g * HEAD_DIM:(g + 1) * HEAD_DIM] = (acc_ref[g] / l).T.astype(BF16)
            lse_row = m_ref[g] + jnp.log(l) * LOG2_E
            lse_ref[:, g:g + 1] = jnp.broadcast_to(lse_row, (HEAD_DIM, tq)).T[:, 0:1]

    return pl.pallas_call(
        body, grid=(kvh, n_x // tq),
        in_specs=[pl.BlockSpec((tq, gw), lambda h, i: (i, h)),
                  pl.BlockSpec((t, HEAD_DIM), lambda h, i: (0, h)),
                  pl.BlockSpec((HEAD_DIM, t), lambda h, i: (h, 0))],
        out_specs=[pl.BlockSpec((tq, gw), lambda h, i: (i, h)),
                   pl.BlockSpec((None, tq, grp), lambda h, i: (h, i, 0))],
        out_shape=[jax.ShapeDtypeStruct((n_x, d_q), BF16), jax.ShapeDtypeStruct((kvh, n_x, grp), F32)],
        scratch_shapes=[pltpu.VMEM((grp, 1, tq), F32), pltpu.VMEM((grp, 8, tq), F32),
                        pltpu.VMEM((grp, HEAD_DIM, tq), F32), pltpu.VMEM((grp, HEAD_DIM, tq), BF16),
                        pltpu.VMEM((grp, tk, tq), F32), pltpu.VMEM((grp, tk, tq), F32)],
        compiler_params=_params(("parallel", "arbitrary"), 48), name=name,
    )(q, k, v_t)


def _attention_bwd(q, k, v, o, lse, do, *, n_x, name):
    t, d_kv = k.shape
    d_q = q.shape[1]
    kvh = d_kv // HEAD_DIM
    grp = d_q // d_kv
    tq = _div(n_x, ATTN_TQ)
    tk = _div(t, ATTN_TK)
    gw = grp * HEAD_DIM
    n_q = n_x // tq

    def body(q_ref, k_ref, v_ref, o_ref, lse_ref, do_ref, dq_ref, dkt_ref, dvt_ref, dq_acc, lse_s, delta_s, qt_s,
             dot_s):
        i = pl.program_id(1)
        _zero_accs(i, [dkt_ref, dvt_ref])
        dq_acc[...] = jnp.zeros_like(dq_acc)
        for g in range(grp):
            sl = slice(g * HEAD_DIM, (g + 1) * HEAD_DIM)
            lse_s[g] = lse_ref[:, g:g + 1]
            delta_s[g] = jnp.sum(do_ref[:, sl].astype(F32) * o_ref[:, sl].astype(F32), axis=-1, keepdims=True)
            qt_s[g] = q_ref[:, sl].T
            dot_s[g] = do_ref[:, sl].T

        def step(j, carry):
            start = pl.multiple_of(j * tk, tk)
            kc, vc = k_ref[pl.ds(start, tk), :], v_ref[pl.ds(start, tk), :]
            dkt_part = jnp.zeros((HEAD_DIM, tk), F32)
            dvt_part = jnp.zeros((HEAD_DIM, tk), F32)
            for g in range(grp):
                sl = slice(g * HEAD_DIM, (g + 1) * HEAD_DIM)
                s = lax.dot_general(q_ref[:, sl], kc, (((1,), (1,)), ((), ())), preferred_element_type=F32)
                p = jnp.exp2(s - lse_s[g])
                dp = lax.dot_general(do_ref[:, sl], vc, (((1,), (1,)), ((), ())), preferred_element_type=F32)
                ds = (p * (dp - delta_s[g])).astype(BF16)
                dq_acc[g] += jnp.dot(ds, kc, preferred_element_type=F32)
                dvt_part = dvt_part + jnp.dot(dot_s[g], p.astype(BF16), preferred_element_type=F32)
                dkt_part = dkt_part + jnp.dot(qt_s[g], ds, preferred_element_type=F32)
            dvt_ref[:, pl.ds(start, tk)] += dvt_part
            dkt_ref[:, pl.ds(start, tk)] += dkt_part
            return carry

        lax.fori_loop(0, t // tk, step, 0)
        for g in range(grp):
            dq_ref[:, g * HEAD_DIM:(g + 1) * HEAD_DIM] = dq_acc[g] * ATTN_SCALE

        @pl.when(i == n_q - 1)
        def _():
            dkt_ref[...] = dkt_ref[...] * (1.0 / LOG2_E)

    qspec = pl.BlockSpec((tq, gw), lambda h, i: (i, h))
    kspec = pl.BlockSpec((t, HEAD_DIM), lambda h, i: (0, h))
    ktspec = pl.BlockSpec((HEAD_DIM, t), lambda h, i: (h, 0))
    return pl.pallas_call(
        body, grid=(kvh, n_q),
        in_specs=[qspec, kspec, kspec, qspec, pl.BlockSpec((None, tq, grp), lambda h, i: (h, i, 0)), qspec],
        out_specs=[qspec, ktspec, ktspec],
        out_shape=[jax.ShapeDtypeStruct((n_x, d_q), F32), jax.ShapeDtypeStruct((d_kv, t), F32),
                   jax.ShapeDtypeStruct((d_kv, t), F32)],
        scratch_shapes=[pltpu.VMEM((grp, tq, HEAD_DIM), F32), pltpu.VMEM((grp, tq, 1), F32),
                        pltpu.VMEM((grp, tq, 1), F32), pltpu.VMEM((grp, HEAD_DIM, tq), BF16),
                        pltpu.VMEM((grp, HEAD_DIM, tq), BF16)],
        compiler_params=_params(("parallel", "arbitrary"), 56), name=name,
    )(q, k, v, o, lse, do)


def _whole(body, ins, out_shapes, name):
    return pl.pallas_call(
        body, out_shape=[jax.ShapeDtypeStruct(s, d) for s, d in out_shapes],
        compiler_params=pltpu.CompilerParams(vmem_limit_bytes=40 << 20), name=name)(*ins)


def _silu_rows(x, *, name):
    def body(x_ref, o_ref):
        o_ref[...] = _silu(x_ref[...])
    return _whole(body, [x], [(x.shape, F32)], name)[0]


def _assemble_dmods(gathered, *, name):
    width = gathered.shape[1]

    def body(g_ref, dm0, dm1, db0, db1):
        for l, (dm, db) in enumerate(((dm0, db0), (dm1, db1))):
            ctx = jnp.zeros((1, width), F32)
            tot = jnp.zeros((1, width), F32)
            for q in range(N_DEV):
                row = g_ref[16 * q + 8 * l:16 * q + 8 * l + 1, :]
                dm[q:q + 1, :] = row
                tot = tot + row
                ctx = ctx + g_ref[16 * q + 8 * l + 1:16 * q + 8 * l + 2, :]
            dm[N_DEV:N_DEV + 1, :] = ctx
            dm[N_DEV + 1:, :] = jnp.zeros((16 - N_DEV - 1, width), F32)
            db[...] = jnp.zeros_like(db)
            db[0:1, :] = tot + ctx

    return _whole(body, [gathered], [((16, width), F32), ((16, width), F32), ((8, width), F32), ((8, width), F32)],
                  name)


def _sum_slots(gathered, rows, *, name):
    def body(g_ref, o_ref):
        acc = g_ref[0:rows, :]
        for q in range(1, N_DEV):
            acc = acc + g_ref[q * rows:(q + 1) * rows, :]
        o_ref[...] = acc
    return _whole(body, [gathered], [((rows, gathered.shape[1]), F32)], name)[0]


def _silu_grad(x, dy, *, name):
    def body(x_ref, dy_ref, o_ref):
        _, vjp = jax.vjp(_silu, x_ref[...])
        o_ref[...] = vjp(dy_ref[...])[0]
    return _whole(body, [x, dy], [(x.shape, F32)], name)[0]


def _adamw(w, m, v, *, name, recv=None, grad=None):
    rows, cols = w.shape
    budget = max(8, ADAMW_BLOCK_ELEMS // cols)
    tr = _div(rows, [c for c in (512, 256, 128, 64, 32, 16, 8) if c <= budget] + [rows])
    c1 = 1.0 - ADAM_B1 ** ADAM_STEP
    c2 = 1.0 - ADAM_B2 ** ADAM_STEP

    def body(w_ref, m_ref, v_ref, g_in, g_ref, d_ref, nm_ref, nv_ref):
        if recv is not None:
            g = g_in[0].astype(F32)
            for q in range(1, N_DEV):
                g = g + g_in[q].astype(F32)
        else:
            g = g_in[...]
        nm = ADAM_B1 * m_ref[...] + (1.0 - ADAM_B1) * g
        nv = ADAM_B2 * v_ref[...] + (1.0 - ADAM_B2) * jnp.square(g)
        g_ref[...] = g
        nm_ref[...] = nm
        nv_ref[...] = nv
        d_ref[...] = -ADAM_LR * ((nm / c1) / (jnp.sqrt(nv / c2) + ADAM_EPS) + ADAM_WD * w_ref[...])

    blk = pl.BlockSpec((tr, cols), lambda i: (i, 0))
    g_spec = pl.BlockSpec((N_DEV, tr, cols), lambda i: (0, i, 0)) if recv is not None else blk
    return pl.pallas_call(
        body, grid=(rows // tr,), in_specs=[blk, blk, blk, g_spec], out_specs=[blk] * 4,
        out_shape=[jax.ShapeDtypeStruct((rows, cols), F32)] * 4,
        compiler_params=_params(("parallel",), 48), name=name,
    )(w, m, v, recv if recv is not None else grad)


def _position():
    return tuple(lax.axis_index(a) for a in MESH_AXES)


def _linear(pos):
    return 4 * pos[0] + 2 * pos[1] + pos[2]


def _window(ref, axis, dev, size):
    start = pl.multiple_of(dev * size, size)
    return ref.at[pl.ds(start, size), :] if axis == 0 else ref.at[:, pl.ds(start, size)]


def _all_gather(shards, axes, *, name):
    n = len(shards)
    sizes = [s.shape[ax] for s, ax in zip(shards, axes)]

    def body(*refs):
        src, dst = refs[:n], refs[n:2 * n]
        send_sems, recv_sems, local_sems = refs[2 * n:]
        x, y, c = _position()
        me, sibling = (x, y, c), (x, y, 1 - c)
        chips = [(1 - x, y), (x, 1 - y), (1 - x, 1 - y)]

        def win(k, pos):
            return _window(dst[k], axes[k], _linear(pos), sizes[k])

        def copy(k, sem, block, to, from_src=False):
            return pltpu.make_async_remote_copy(
                src_ref=src[k] if from_src else win(k, block), dst_ref=win(k, block),
                send_sem=send_sems.at[k, sem], recv_sem=recv_sems.at[k, sem],
                device_id=to, device_id_type=MESH_ID)

        mine = [pltpu.make_async_copy(src[k], win(k, me), local_sems.at[k]) for k in range(n)]
        for cp in mine:
            cp.start()
        first = []
        for k in range(n):
            first.append(copy(k, 0, me, sibling, from_src=True))
            first += [copy(k, 1 + j, me, (*chip, c), from_src=True) for j, chip in enumerate(chips)]
        for cp in first:
            cp.start()
        passed = []
        for j, chip in enumerate(chips):
            for k in range(n):
                copy(k, 1 + j, (*chip, c), me).wait_recv()
                fwd = copy(k, 4 + j, (*chip, c), sibling)
                fwd.start()
                passed.append(fwd)
        for k in range(n):
            copy(k, 0, sibling, me).wait_recv()
            for j, chip in enumerate(chips):
                copy(k, 4 + j, (*chip, 1 - c), me).wait_recv()
        for cp in first + passed:
            cp.wait_send()
        for cp in mine:
            cp.wait()

    out_shape = []
    for s, ax in zip(shards, axes):
        full = (s.shape[0] * N_DEV, s.shape[1]) if ax == 0 else (s.shape[0], s.shape[1] * N_DEV)
        out_shape.append(jax.ShapeDtypeStruct(full, s.dtype))
    any_spec = pl.BlockSpec(memory_space=pl.ANY)
    return pl.pallas_call(
        body, in_specs=[any_spec] * n, out_specs=[any_spec] * n, out_shape=out_shape,
        scratch_shapes=[pltpu.SemaphoreType.DMA((n, 7)), pltpu.SemaphoreType.DMA((n, 7)),
                        pltpu.SemaphoreType.DMA((n,))],
        name=name,
    )(*shards)


class Rider:
    def __init__(self, kind, arrays, axes):
        self.kind, self.arrays, self.axes = kind, list(arrays), list(axes)
        self.n = len(self.arrays)
        if kind == "gather":
            self.sizes = [a.shape[ax] for a, ax in zip(self.arrays, self.axes)]
        else:
            self.sizes = [a.shape[ax] // N_DEV for a, ax in zip(self.arrays, self.axes)]

    def out_shape(self):
        shapes = []
        for a, ax, sz in zip(self.arrays, self.axes, self.sizes):
            if self.kind == "gather":
                full = (sz * N_DEV, a.shape[1]) if ax == 0 else (a.shape[0], sz * N_DEV)
                shapes.append(jax.ShapeDtypeStruct(full, a.dtype))
            else:
                shard = (sz, a.shape[1]) if ax == 0 else (a.shape[0], sz)
                shapes.append(jax.ShapeDtypeStruct((N_DEV, *shard), a.dtype))
        return shapes

    def scratch(self):
        return [pltpu.SemaphoreType.DMA((self.n, N_DEV - 1)), pltpu.SemaphoreType.DMA((self.n, N_DEV - 1)),
                pltpu.SemaphoreType.DMA((self.n,))]

    def plan(self, src, dst, send_sems, recv_sems, local_sems):
        x, y, c = me = _position()
        mine = _linear(me)

        def remote(k, sem, src_ref, dst_ref, to):
            return pltpu.make_async_remote_copy(
                src_ref=src_ref, dst_ref=dst_ref, send_sem=send_sems.at[k, sem], recv_sem=recv_sems.at[k, sem],
                device_id=to, device_id_type=MESH_ID)

        ph = dict(local=[], start=[], mid_wait=[], mid_start=[], end_wait=[])
        for k in range(self.n):
            ax, sz = self.axes[k], self.sizes[k]
            if self.kind == "exchange":
                own_src, own_dst = _window(src[k], ax, mine, sz), dst[k].at[mine]
                ph["local"].append(pltpu.make_async_copy(own_src, own_dst, local_sems.at[k]))
                for mask in range(1, N_DEV):
                    to = tuple(1 - p if (mask >> (2 - b)) & 1 else p for b, p in enumerate(me))
                    ph["start"].append(remote(k, mask - 1, _window(src[k], ax, _linear(to), sz), own_dst, to))
                    ph["end_wait"].append(remote(k, mask - 1, own_src, dst[k].at[_linear(to)], to))
            else:
                def win(pos, k=k, ax=ax, sz=sz):
                    return _window(dst[k], ax, _linear(pos), sz)
                sibling = (x, y, 1 - c)
                chips = [(1 - x, y), (x, 1 - y), (1 - x, 1 - y)]
                ph["local"].append(pltpu.make_async_copy(src[k], win(me), local_sems.at[k]))
                ph["start"].append(remote(k, 0, src[k], win(me), sibling))
                ph["end_wait"].append(remote(k, 0, src[k], win(sibling), sibling))
                for j, chip in enumerate(chips):
                    ph["start"].append(remote(k, 1 + j, src[k], win(me), (*chip, c)))
                    ph["mid_wait"].append(remote(k, 1 + j, src[k], win((*chip, c)), (*chip, c)))
                    ph["mid_start"].append(remote(k, 4 + j, win((*chip, c)), win((*chip, c)), sibling))
                    ph["end_wait"].append(remote(k, 4 + j, src[k], win((*chip, 1 - c)), sibling))
        return ph


def _pcall(body, *, grid, in_specs, out_specs, out_shape, scratch_shapes, semantics, vmem_mb, name, operands,
           rider=None):
    if rider is None:
        return pl.pallas_call(body, grid=grid, in_specs=in_specs, out_specs=out_specs, out_shape=out_shape,
                              scratch_shapes=scratch_shapes, compiler_params=_params(semantics, vmem_mb),
                              name=name)(*operands)
    n_in, n_out, n_scr, n = len(in_specs), len(out_specs), len(scratch_shapes), rider.n

    def wrapped(*refs):
        ins, src = refs[:n_in], refs[n_in:n_in + n]
        outs = refs[n_in + n:n_in + n + n_out]
        dst = refs[n_in + n + n_out:n_in + 2 * n + n_out]
        rest = refs[n_in + 2 * n + n_out:]
        scratch, sems = rest[:n_scr], rest[n_scr:]
        step = functools.reduce(lambda acc, ig: acc * ig[1] + ig[0],
                                [(pl.program_id(dim), g) for dim, g in enumerate(grid)], 0)
        n_steps = functools.reduce(lambda a, b: a * b, grid)

        @pl.when(step == 0)
        def _():
            ph = rider.plan(src, dst, *sems)
            for cp in ph["local"] + ph["start"]:
                cp.start()

        body(*ins, *outs, *scratch)

        @pl.when(step == (n_steps * 5) // 8)
        def _():
            ph = rider.plan(src, dst, *sems)
            for cp in ph["mid_wait"]:
                cp.wait_recv()
            for cp in ph["mid_start"]:
                cp.start()

        @pl.when(step == n_steps - 1)
        def _():
            ph = rider.plan(src, dst, *sems)
            for cp in ph["end_wait"]:
                cp.wait_recv()
            for cp in ph["start"] + ph["mid_start"]:
                cp.wait_send()
            for cp in ph["local"]:
                cp.wait()

    any_spec = pl.BlockSpec(memory_space=pl.ANY)
    return pl.pallas_call(
        wrapped, grid=grid, in_specs=list(in_specs) + [any_spec] * n, out_specs=list(out_specs) + [any_spec] * n,
        out_shape=list(out_shape) + rider.out_shape(), scratch_shapes=list(scratch_shapes) + rider.scratch(),
        compiler_params=_params(("arbitrary",) * len(grid), vmem_mb), name=name,
    )(*operands, *rider.arrays)


WEIGHTS = ['c_ctx', 'l0_ada_w', 'l0_ada_b', 'l0_in_w', 'l0_conv_w', 'l0_conv_b', 'l0_conv_ln_g', 'l0_conv_ln_b',
           'l0_pool_w', 'l0_pool_scale', 'l0_out_w', 'l0_mlp_w1', 'l0_mlp_w2', 'l1_ada_w', 'l1_ada_b', 'l1_qkv_w',
           'l1_q_norm_g', 'l1_k_norm_g', 'l1_out_w', 'l1_mlp_w1', 'l1_mlp_w2', 'final_g']
SHARDED = {'l0_in_w': 1, 'l0_out_w': 0, 'l0_mlp_w1': 1, 'l0_mlp_w2': 0,
           'l1_qkv_w': 1, 'l1_out_w': 0, 'l1_mlp_w1': 1, 'l1_mlp_w2': 0}
REPLICATED_SMALL = ['l0_conv_b', 'l0_conv_ln_g', 'l0_conv_ln_b', 'l0_pool_scale', 'l1_q_norm_g', 'l1_k_norm_g',
                    'final_g']


def _row8(v):
    v = v.reshape(1, -1)
    return jnp.pad(v, ((0, 7), (0, 0)))


def _mods16(full, me):
    d = full.shape[1] // 6
    mine = lax.dynamic_slice_in_dim(full, me, 1, axis=0).reshape(6, d)
    ctx = full[N_DEV].reshape(6, d)
    return jnp.pad(jnp.stack([mine, ctx], axis=1).reshape(12, d), ((0, 4), (0, 0)))


def _mlp_fwd(xs, mods, w1, w2, *, n_x, tm, tag, rider1=None, rider2=None):
    t, d = xs.shape
    dff = w1.shape[1]
    h = _rms_mod_fwd(xs, mods, k_shift=3, n_x=n_x, name=f"{tag}_norm2")
    pre, act, *ride1 = _matmul(h, w1, mode="nn", tm=tm, tn=_div(dff, WIDE_TN if tm * WIDE_TN[0] <= WIDE_TILE_ELEMS else WIDE_TN[1:]), tk=d, out_dtypes=[BF16, BF16],
                               name=f"{tag}_mlp1", rider=rider1,
                               epilogue=lambda acc, rows: (acc, jnp.square(jnp.maximum(acc, 0.0))))
    if w2 is None:
        w2 = ride1[-1]
    xo, branch, *ride2 = _matmul(
        act, w2, mode="nn", tm=_div(t, DEEP_TM), tn=_div(d, DEEP_TN), tk=dff, out_dtypes=[F32, BF16],
        name=f"{tag}_mlp2", extras=[("tile", xs), ("vec", mods)], rider=rider2,
        epilogue=lambda acc, rows, res, mv: (res + _seg_pick(mv, 5, rows, n_x) * acc, acc))
    return xo, dict(h=h, pre=pre, act=act, branch=branch, x_in=xs), ride1, ride2


def _exchange_of(items):
    return Rider("exchange", [a for a, _ in items], [ax for _, ax in items]) if items else None


def _mlp_bwd(dxo, dbranch, saved, mods, w1, w2, mixer_branch, *, n_x, tm, tag, ride_dx2=(), ride_dw2=()):
    t, d = dxo.shape
    dff = w1.shape[1]
    tkt = _div(t, TOKEN_TK)
    dpre, *recv_a = _matmul(dbranch, w2, mode="nt", tm=tm, tn=_div(dff, WIDE_TN if tm * WIDE_TN[0] <= WIDE_TILE_ELEMS else WIDE_TN[1:]), tk=d, out_dtypes=[BF16],
                            name=f"{tag}_mlp2_dx", extras=[("tile", saved["pre"])],
                            rider=_exchange_of(list(ride_dx2)),
                            epilogue=lambda acc, rows, pre: (acc * 2.0 * jnp.maximum(pre.astype(F32), 0.0),))
    dw2, *recv_b = _matmul(saved["act"], dbranch, mode="tn", tm=_div(dff, (1024, 512)), tn=_div(d, (1024, 512)),
                           tk=tkt, out_dtypes=[BF16], name=f"{tag}_mlp2_dw", rider=_exchange_of(list(ride_dw2)))
    dh, = _matmul(dpre, w1, mode="nt", tm=_div(t, DEEP_TM), tn=_div(d, DEEP_TN), tk=dff,
                  out_dtypes=[F32], name=f"{tag}_mlp1_dx")
    dw1, recv_dw2 = _matmul(saved["h"], dpre, mode="tn", tm=_div(d, (1024, 512)), tn=_div(dff, (1024, 512)),
                            tk=tkt, out_dtypes=[BF16], name=f"{tag}_mlp1_dw", rider=_exchange_of([(dw2, 0)]))
    dx, dm_norm, dmix, dm_gate = _rms_mod_bwd(saved["x_in"], mods, dh, dxo, k_shift=3, n_x=n_x,
                                              name=f"{tag}_norm2_bwd", gate=(mixer_branch, mods, 2))
    return dx, dw1, dm_norm + dm_gate, dmix, recv_a, recv_b, recv_dw2


def kernel(x, c, ctx, c_ctx, l0_ada_w, l0_ada_b, l0_in_w, l0_conv_w, l0_conv_b, l0_conv_ln_g, l0_conv_ln_b, l0_pool_w, l0_pool_scale, l0_out_w, l0_mlp_w1, l0_mlp_w2, l1_ada_w, l1_ada_b, l1_qkv_w, l1_q_norm_g, l1_k_norm_g, l1_out_w, l1_mlp_w1, l1_mlp_w2, final_g, loss_target, m_c_ctx, m_l0_ada_w, m_l0_ada_b, m_l0_in_w, m_l0_conv_w, m_l0_conv_b, m_l0_conv_ln_g, m_l0_conv_ln_b, m_l0_pool_w, m_l0_pool_scale, m_l0_out_w, m_l0_mlp_w1, m_l0_mlp_w2, m_l1_ada_w, m_l1_ada_b, m_l1_qkv_w, m_l1_q_norm_g, m_l1_k_norm_g, m_l1_out_w, m_l1_mlp_w1, m_l1_mlp_w2, m_final_g, v_c_ctx, v_l0_ada_w, v_l0_ada_b, v_l0_in_w, v_l0_conv_w, v_l0_conv_b, v_l0_conv_ln_g, v_l0_conv_ln_b, v_l0_pool_w, v_l0_pool_scale, v_l0_out_w, v_l0_mlp_w1, v_l0_mlp_w2, v_l1_ada_w, v_l1_ada_b, v_l1_qkv_w, v_l1_q_norm_g, v_l1_k_norm_g, v_l1_out_w, v_l1_mlp_w1, v_l1_mlp_w2, v_final_g):
    p = dict(locals())
    me = _linear(_position())
    n_x, d = x.shape[1], x.shape[2]
    n_ctx = ctx.shape[1]
    t = n_x + n_ctx
    dc = l0_conv_b.shape[0]
    n_tap = l0_conv_w.shape[0]
    d_q = d
    n_mod = l0_ada_b.shape[0] // d
    ada_cols = l0_ada_w.shape[1]
    tm_t = _div(t, (768, 640, 512, 128))
    tm_x = _div(n_x, (1024, 512))

    names = list(SHARDED)
    shard16 = {nm: p[nm].astype(BF16) for nm in names}

    def gather_of(*nms):
        return Rider("gather", [shard16[nm] for nm in nms], [SHARDED[nm] for nm in nms])

    wfull = {}
    n_grp, pg = l0_pool_w.shape[0], l0_pool_w.shape[2]

    c_all = _all_gather([_row8(c)], [0], name="gather_cond")[0].reshape(N_DEV, 8, d)[:, 0]
    cond = jnp.concatenate([c_all, c_ctx.reshape(1, d), jnp.zeros((16 - N_DEV - 1, d), F32)], axis=0)
    s16 = _silu_rows(cond, name="cond_silu")
    mod_shards = []
    for li, (lw, lb) in enumerate(((l0_ada_w, l0_ada_b), (l1_ada_w, l1_ada_b))):
        bias = _row8(lax.dynamic_slice_in_dim(lb, me * ada_cols, ada_cols))
        mod_shards.append(_matmul(s16, lw, mode="nn", tm=16, tn=_div(ada_cols, (512, 384, 256, 128)), tk=d,
                                  out_dtypes=[F32], name=f"l{li}_ada_fwd", extras=[("vec", bias)],
                                  epilogue=lambda acc, rows, b: (acc + b[0:1],))[0])
    mods_full = _all_gather([jnp.concatenate(mod_shards, axis=0)], [1], name="gather_mods")[0]
    mods0, mods1 = _mods16(mods_full[:16], me), _mods16(mods_full[16:], me)

    xs0 = jnp.concatenate([x[0], ctx[0]], axis=0)
    first = Rider("gather", [shard16['l0_in_w'], jnp.pad(l0_conv_w, ((0, 1), (0, 0))), l0_pool_w.reshape(-1, pg)],
                  [SHARDED['l0_in_w'], 1, 0])
    h0, wfull['l0_in_w'], conv_w_full, pool_w_full = _rms_mod_fwd(
        xs0, mods0, k_shift=0, n_x=n_x, name="l0_norm1", rider=first)
    pool_w_full = pool_w_full.reshape(N_DEV, n_grp, pg // N_DEV, pg).transpose(1, 0, 2, 3).reshape(n_grp, pg, pg)
    z, wfull['l0_out_w'] = _matmul(h0, wfull['l0_in_w'], mode="nn", tm=tm_t, tn=_div(3 * dc, (1024, 768, 512, 384)),
                                   tk=d, out_dtypes=[F32], name="l0_in_proj", rider=gather_of('l0_out_w'))
    y0, cv, dsave, wfull['l0_mlp_w1'] = _mixer0_fwd(
        z, conv_w_full, _row8(l0_conv_b), _row8(l0_conv_ln_g), _row8(l0_conv_ln_b), pool_w_full,
        _row8(l0_pool_scale), n_tap=n_tap, n_x=n_x, name="l0_mixer", rider=gather_of('l0_mlp_w1'))
    xs1, mix0 = _matmul(y0, wfull['l0_out_w'], mode="nn", tm=tm_t, tn=_div(d, (1024, 512)), tk=2 * dc,
                        out_dtypes=[F32, BF16], name="l0_out_proj", extras=[("tile", xs0), ("vec", mods0)],
                        epilogue=lambda acc, rows, res, mv: (res + _seg_pick(mv, 2, rows, n_x) * acc, acc))
    xs2, mlp0, (wfull['l0_mlp_w2'],), (wfull['l1_qkv_w'], wfull['l1_out_w']) = _mlp_fwd(
        xs1, mods0, wfull['l0_mlp_w1'], None, n_x=n_x, tm=tm_t, tag="l0",
        rider1=gather_of('l0_mlp_w2'), rider2=gather_of('l1_qkv_w', 'l1_out_w'))

    h2 = _rms_mod_fwd(xs2, mods1, k_shift=0, n_x=n_x, name="l1_norm1")
    qkv, wfull['l1_mlp_w1'] = _matmul(h2, wfull['l1_qkv_w'], mode="nn", tm=tm_t,
                                      tn=_div(l1_qkv_w.shape[1] * N_DEV, (1024, 768, 512)), tk=d, out_dtypes=[F32],
                                      name="l1_qkv_proj", rider=gather_of('l1_mlp_w1'))
    cos, sin = _rope_tables(n_x, n_ctx)
    gq, gk = _row8(l1_q_norm_g), _row8(l1_k_norm_g)
    q, k, v = _qk_prep_fwd(qkv, gq, gk, cos, sin, d_q=d_q, name="l1_qk_prep")
    o, lse = _attention_fwd(q, k, v.T, n_x=n_x, name="l1_attention")
    x3, mix1 = _matmul(o, wfull['l1_out_w'], mode="nn", tm=tm_x, tn=_div(d, (1024, 512)), tk=d_q,
                       out_dtypes=[F32, BF16], name="l1_out_proj", extras=[("tile", xs2), ("vec", mods1)],
                       epilogue=lambda acc, rows, res, mv: (res + mv[4:5] * acc, acc))
    x4, mlp1, (wfull['l1_mlp_w2'],), _ = _mlp_fwd(x3, mods1, wfull['l1_mlp_w1'], None, n_x=n_x, tm=tm_x, tag="l1",
                                                  rider1=gather_of('l1_mlp_w2'))

    dx4, loss_part, dfinal_g, dbranch1, dmods1 = _final_loss(
        x4, _row8(final_g), loss_target[0], mlp1["branch"], mods1, k_gate=5, name="loss_head")
    loss = lax.psum(loss_part[0, 0], MESH_AXES)

    recv = {}
    dx3, dw1_1, dm, dmix1, _, _, recv['l1_mlp_w2'] = _mlp_bwd(
        dx4, dbranch1, mlp1, mods1, wfull['l1_mlp_w1'], wfull['l1_mlp_w2'], mix1, n_x=n_x, tm=tm_x, tag="l1")
    dmods1 = dmods1 + dm
    do, = _matmul(dmix1, wfull['l1_out_w'], mode="nt", tm=tm_x, tn=_div(d_q, (1024, 512)), tk=d,
                  out_dtypes=[BF16], name="l1_out_dx")
    dw_out1, = _matmul(o, dmix1, mode="tn", tm=_div(d_q, (1024, 512)), tn=_div(d, (1024, 512)),
                       tk=_div(n_x, TOKEN_TK), out_dtypes=[BF16], name="l1_out_dw")
    dq, dk_t, dv_t = _attention_bwd(q, k, v, o, lse, do, n_x=n_x, name="l1_attention_bwd")
    dk, dv = dk_t.T, dv_t.T
    dqkv, dgq, dgk = _qk_prep_bwd(qkv, dq, dk, dv, gq, gk, cos, sin, n_x=n_x, name="l1_qk_prep_bwd")
    tkt = _div(t, TOKEN_TK)
    dh2, recv['l1_out_w'] = _matmul(dqkv, wfull['l1_qkv_w'], mode="nt", tm=tm_t, tn=_div(d, (1024, 512)),
                                    tk=dqkv.shape[1], out_dtypes=[F32], name="l1_qkv_dx",
                                    rider=_exchange_of([(dw_out1, SHARDED['l1_out_w'])]))
    dw_qkv, = _matmul(h2, dqkv, mode="tn", tm=_div(d, (1024, 512)), tn=_div(dqkv.shape[1], (1024, 768, 512)),
                      tk=tkt, out_dtypes=[BF16], name="l1_qkv_dw")
    dxs2, dm, dbranch0, dmods0 = _rms_mod_bwd(xs2, mods1, dh2, dx3, k_shift=0, n_x=n_x, name="l1_norm1_bwd",
                                              dres_rows=n_x, gate=(mlp0["branch"], mods0, 5))
    dmods1 = dmods1 + dm

    dxs1, dw1_0, dm, dmix0, (recv['l1_qkv_w'],), (recv['l1_mlp_w1'],), recv['l0_mlp_w2'] = _mlp_bwd(
        dxs2, dbranch0, mlp0, mods0, wfull['l0_mlp_w1'], wfull['l0_mlp_w2'], mix0, n_x=n_x, tm=tm_t, tag="l0",
        ride_dx2=[(dw_qkv, SHARDED['l1_qkv_w'])], ride_dw2=[(dw1_1, SHARDED['l1_mlp_w1'])])
    dmods0 = dmods0 + dm
    dy0, = _matmul(dmix0, wfull['l0_out_w'], mode="nt", tm=tm_t, tn=_div(2 * dc, (1024, 512)), tk=d,
                   out_dtypes=[F32], name="l0_out_dx")
    dw_out0, = _matmul(y0, dmix0, mode="tn", tm=_div(2 * dc, (1024, 512)), tn=_div(d, (1024, 512)),
                       tk=tkt, out_dtypes=[BF16], name="l0_out_dw")
    dz, dconv_w, dconv_b, dln_g, dln_b, dpool_w, dpool_scale, recv['l0_mlp_w1'] = _mixer0_bwd(
        dy0, z, cv, dsave, conv_w_full, _row8(l0_conv_ln_g), _row8(l0_conv_ln_b), pool_w_full,
        _row8(l0_pool_scale), n_tap=n_tap, n_x=n_x, name="l0_mixer_bwd",
        rider=Rider("exchange", [dw1_0], [SHARDED['l0_mlp_w1']]))
    dw_in0, recv['l0_out_w'] = _matmul(h0, dz, mode="tn", tm=_div(d, (1024, 512)),
                                       tn=_div(3 * dc, (1024, 768, 512, 384)), tk=tkt, out_dtypes=[BF16],
                                       name="l0_in_dw", rider=_exchange_of([(dw_out0, SHARDED['l0_out_w'])]))
    dh0, recv['l0_in_w'] = _matmul(dz, wfull['l0_in_w'], mode="nt", tm=tm_t, tn=_div(d, (1024, 512)), tk=3 * dc,
                                   out_dtypes=[F32], name="l0_in_dx",
                                   rider=_exchange_of([(dw_in0, SHARDED['l0_in_w'])]))
    dx0, dm = _rms_mod_bwd(xs0, mods0, dh0, dxs1, k_shift=0, n_x=n_x, name="l0_norm1_bwd", out_rows=n_x)
    dmods0 = dmods0 + dm
    grad_x = dx0[None]

    def dmod_rows(dm16):
        rows = dm16[:2 * n_mod].reshape(n_mod, 2, d).transpose(1, 0, 2).reshape(2, n_mod * d)
        return jnp.pad(rows, ((0, 6), (0, 0)))
    dm_gathered = _all_gather([jnp.concatenate([dmod_rows(dmods0), dmod_rows(dmods1)], axis=0)], [0],
                              name="gather_dmods")[0]
    dm0, dm1, db0, db1 = _assemble_dmods(dm_gathered, name="assemble_dmods")
    out_g = {'l0_ada_b': db0[0], 'l1_ada_b': db1[0]}
    ds_part = jnp.zeros((16, d), F32)
    for nm, lw, dmf in (('l0_ada_w', l0_ada_w, dm0), ('l1_ada_w', l1_ada_w, dm1)):
        dm_cols = lax.dynamic_slice_in_dim(dmf, me * ada_cols, ada_cols, axis=1)
        out_g[nm], = _matmul(s16, dm_cols, mode="tn", tm=_div(d, (1024, 512)),
                             tn=_div(ada_cols, (512, 384, 256, 128)), tk=16, out_dtypes=[F32], name=f"{nm}_dw")
        ds_part = ds_part + _matmul(dm_cols, lw, mode="nt", tm=16, tn=_div(d, (1024, 512)),
                                    tk=_div(ada_cols, (512, 384, 256, 128)), out_dtypes=[F32], name=f"{nm}_dx")[0]

    small = {'l0_conv_b': dconv_b[0], 'l0_conv_ln_g': dln_g[0], 'l0_conv_ln_b': dln_b[0],
             'l0_pool_scale': dpool_scale[0], 'l1_q_norm_g': dgq[0], 'l1_k_norm_g': dgk[0],
             'final_g': dfinal_g[0], 'dsilu_ctx': ds_part[N_DEV], 'l0_conv_w': dconv_w[:-1].reshape(-1),
             'l0_pool_w': dpool_w.reshape(-1)}
    flat = jnp.concatenate([small[nm] for nm in small])
    rows = -(-flat.shape[0] // 1024) * 8
    packed = jnp.pad(flat, (0, rows * 128 - flat.shape[0])).reshape(rows, 128)
    summed = _sum_slots(_all_gather([packed], [0], name="gather_small_grads")[0], rows,
                        name="sum_small_grads").reshape(-1)
    off = 0
    for nm in small:
        size = small[nm].shape[0]
        small[nm] = summed[off:off + size]
        off += size
    out_g['c_ctx'] = _silu_grad(_row8(c_ctx), _row8(small['dsilu_ctx']), name="c_ctx_grad")[0]
    for nm in REPLICATED_SMALL:
        out_g[nm] = small[nm]
    conv_cols = l0_conv_w.shape[1]
    out_g['l0_conv_w'] = lax.dynamic_slice_in_dim(small['l0_conv_w'].reshape(n_tap, dc), me * conv_cols, conv_cols,
                                                  axis=1)
    out_g['l0_pool_w'] = lax.dynamic_slice_in_dim(small['l0_pool_w'].reshape(n_grp, pg, pg), me * (pg // N_DEV),
                                                  pg // N_DEV, axis=1)

    delta, new_m, new_v = {}, {}, {}
    for nm in names:
        out_g[nm], delta[nm], new_m[nm], new_v[nm] = _adamw(p[nm], p['m_' + nm], p['v_' + nm], recv=recv[nm],
                                                            name=f"adamw_{nm}")
    for nm in ('l0_ada_w', 'l1_ada_w'):
        out_g[nm], delta[nm], new_m[nm], new_v[nm] = _adamw(p[nm], p['m_' + nm], p['v_' + nm], grad=out_g[nm],
                                                            name=f"adamw_{nm}")
    for nm in WEIGHTS:
        if nm in delta:
            continue
        shape = p[nm].shape
        as2d = lambda a: a.reshape(1, -1) if a.ndim == 1 else a.reshape(-1, a.shape[-1])
        res = _adamw(as2d(p[nm]), as2d(p['m_' + nm]), as2d(p['v_' + nm]), grad=as2d(out_g[nm]), name=f"adamw_{nm}")
        out_g[nm], delta[nm], new_m[nm], new_v[nm] = [r.reshape(shape) for r in res]

    return (loss, grad_x, *[out_g[nm] for nm in WEIGHTS], *[delta[nm] for nm in WEIGHTS],
            *[new_m[nm] for nm in WEIGHTS], *[new_v[nm] for nm in WEIGHTS])
```

```python
import functools

import jax
import jax.numpy as jnp
from jax import lax
from jax.experimental import pallas as pl
from jax.experimental.pallas import tpu as pltpu

F32 = jnp.float32
BF16 = jnp.bfloat16
N_DEV = 8
MESH_AXES = ("x", "y", "c")
EPS = 1e-6
HEAD_DIM = 128
POOL_WINDOWS = (2, 4, 8, 16)
GRID_W = 64
ROPE_THETA = 10000.0
ATTN_SCALE = HEAD_DIM ** -0.5
LOG2_E = 1.4426950408889634
Q_SCALE_LOG2 = ATTN_SCALE * LOG2_E
HALO = 16
ADAM_LR, ADAM_B1, ADAM_B2, ADAM_EPS, ADAM_WD, ADAM_STEP = 0.001, 0.9, 0.999, 1e-08, 0.01, 10
VMEM_CAP_MB = 60
ADAMW_BLOCK_ELEMS = 1 << 18
TOKEN_TK = (2816, 2048, 1408, 1024, 768, 640, 512, 128)
DEEP_TM = (384, 512, 256, 128)
DEEP_TN = (512,)
WIDE_TN = (2048, 1024, 512)
WIDE_TILE_ELEMS = 768 * 2048
MESH_ID = pl.DeviceIdType.MESH


def _div(n, prefs):
    for p in prefs:
        if n % p == 0:
            return p
    raise ValueError(f"no tile for {n} in {prefs}")


def _params(sem, vmem_mb):
    return pltpu.CompilerParams(dimension_semantics=sem, vmem_limit_bytes=min(vmem_mb, VMEM_CAP_MB) << 20)


def _sigmoid(x):
    return 1.0 / (1.0 + jnp.exp(-x))


def _silu(x):
    return x * _sigmoid(x)


def _rms(x):
    return x * lax.rsqrt(jnp.mean(x * x, axis=-1, keepdims=True) + EPS)


def _rms_mod(x, shift, scale):
    return _rms(x) * (1.0 + scale) + shift


def _layernorm(x, g, b):
    mu = jnp.mean(x, axis=-1, keepdims=True)
    var = jnp.mean(jnp.square(x - mu), axis=-1, keepdims=True)
    return (x - mu) * lax.rsqrt(var + EPS) * g + b


def _ln_silu(x, g, b):
    return _silu(_layernorm(x, g, b))


def _matmul(a, b, *, mode, tm, tn, tk, out_dtypes, name, extras=(), epilogue=None, rider=None):
    if mode == "tn":
        kdim, m = a.shape
        n = b.shape[1]
    else:
        m, kdim = a.shape
        n = b.shape[0] if mode == "nt" else b.shape[1]
    assert m % tm == 0 and n % tn == 0 and kdim % tk == 0, (name, m, n, kdim, tm, tn, tk)
    nk = kdim // tk
    n_ex = len(extras)
    n_out = len(out_dtypes)

    def body(a_ref, b_ref, *rest):
        ex = rest[:n_ex]
        outs = rest[n_ex:n_ex + n_out]
        acc_ref = rest[n_ex + n_out] if nk > 1 else None
        k = pl.program_id(2)
        av = a_ref[...].astype(BF16)
        bv = b_ref[...].astype(BF16)
        dims = {"nn": ((1,), (0,)), "nt": ((1,), (1,)), "tn": ((0,), (0,))}[mode]
        part = lax.dot_general(av, bv, (dims, ((), ())), preferred_element_type=F32)

        rows = pl.program_id(0) * tm + lax.broadcasted_iota(jnp.int32, (tm, 1), 0)

        def finish(acc):
            res = (acc,) if epilogue is None else epilogue(acc, rows, *[e[...] for e in ex])
            for o, r in zip(outs, res):
                o[...] = r.astype(o.dtype)

        if nk == 1:
            finish(part)
        else:
            @pl.when(k == 0)
            def _():
                acc_ref[...] = part

            @pl.when(k > 0)
            def _():
                acc_ref[...] += part

            @pl.when(k == nk - 1)
            def _():
                finish(acc_ref[...])

    if mode == "tn":
        a_spec = pl.BlockSpec((tk, tm), lambda i, j, k: (k, i))
        b_spec = pl.BlockSpec((tk, tn), lambda i, j, k: (k, j))
    else:
        a_spec = pl.BlockSpec((tm, tk), lambda i, j, k: (i, k))
        b_spec = (pl.BlockSpec((tn, tk), lambda i, j, k: (j, k)) if mode == "nt"
                  else pl.BlockSpec((tk, tn), lambda i, j, k: (k, j)))
    ex_specs, ex_arrays, ex_bytes = [], [], 0
    for kind, arr in extras:
        ex_arrays.append(arr)
        if kind == "tile":
            ex_specs.append(pl.BlockSpec((tm, tn), lambda i, j, k: (i, j)))
            ex_bytes += tm * tn * arr.dtype.itemsize
        else:
            ex_specs.append(pl.BlockSpec((arr.shape[0], tn), lambda i, j, k: (0, j)))
            ex_bytes += arr.shape[0] * tn * 4
    blocks = tm * tk * a.dtype.itemsize + tk * tn * b.dtype.itemsize + ex_bytes
    blocks += sum(tm * tn * jnp.dtype(d).itemsize for d in out_dtypes)
    casts = sum(rows * cols * 2 for arr, rows, cols in ((a, tm, tk), (b, tk, tn)) if arr.dtype != BF16)
    vmem = (2 * blocks + 4 * tm * tn * 4 + casts) // (1 << 20) + 8
    return _pcall(
        body,
        grid=(m // tm, n // tn, nk),
        in_specs=[a_spec, b_spec] + ex_specs,
        out_specs=[pl.BlockSpec((tm, tn), lambda i, j, k: (i, j)) for _ in out_dtypes],
        out_shape=[jax.ShapeDtypeStruct((m, n), d) for d in out_dtypes],
        scratch_shapes=[pltpu.VMEM((tm, tn), F32)] if nk > 1 else [],
        semantics=("parallel", "parallel", "arbitrary"), vmem_mb=vmem, name=name,
        operands=[a, b, *ex_arrays], rider=rider)


def _seg_pick(vec, k, rows, n_x):
    return jnp.where(rows < n_x, vec[2 * k:2 * k + 1], vec[2 * k + 1:2 * k + 2])


def _zero_accs(i, accs):
    @pl.when(i == 0)
    def _():
        for a in accs:
            a[...] = jnp.zeros_like(a)


def _rms_mod_fwd(xs, mods, *, k_shift, n_x, name, rider=None):
    t, d = xs.shape
    tm = _div(t, (256, 128))

    def body(x_ref, mods_ref, h_ref):
        seg = (pl.program_id(0) * tm >= n_x).astype(jnp.int32)
        shift = mods_ref[pl.ds(2 * k_shift + seg, 1), :]
        scale = mods_ref[pl.ds(2 * k_shift + 2 + seg, 1), :]
        h_ref[...] = _rms_mod(x_ref[...], shift, scale).astype(BF16)

    res = _pcall(
        body, grid=(t // tm,),
        in_specs=[pl.BlockSpec((tm, d), lambda i: (i, 0)), pl.BlockSpec((16, d), lambda i: (0, 0))],
        out_specs=[pl.BlockSpec((tm, d), lambda i: (i, 0))],
        out_shape=[jax.ShapeDtypeStruct((t, d), BF16)],
        scratch_shapes=[], semantics=("parallel",), vmem_mb=32, name=name, operands=[xs, mods], rider=rider)
    return res[0] if rider is None else res


def _gate_part(dxv, seg, k_gate, br_ref, gmods_ref, db_ref, dgm_ref):
    r_gate = 2 * k_gate + seg
    db_ref[...] = (dxv * gmods_ref[pl.ds(r_gate, 1), :]).astype(BF16)
    dgm_ref[pl.ds(r_gate, 1), :] += jnp.sum(dxv * br_ref[...].astype(F32), axis=0, keepdims=True)


def _rms_mod_bwd(xs, mods, dh, dres, *, k_shift, n_x, name, rider=None, gate=None, dres_rows=None, out_rows=None):
    t, d = xs.shape
    tm = _div(t, (256, 128))
    n_gate = 2 if gate is not None else 0

    def body(x_ref, mods_ref, dh_ref, dres_ref, *rest):
        gate_in, (dx_ref, dmods_ref), gate_out = rest[:n_gate], rest[n_gate:n_gate + 2], rest[n_gate + 2:]
        i = pl.program_id(0)
        _zero_accs(i, [dmods_ref, *gate_out[1:]])
        seg = (i * tm >= n_x).astype(jnp.int32)
        r_shift = 2 * k_shift + seg
        r_scale = 2 * k_shift + 2 + seg
        shift = mods_ref[pl.ds(r_shift, 1), :]
        scale = mods_ref[pl.ds(r_scale, 1), :]
        _, vjp = jax.vjp(_rms_mod, x_ref[...], shift, scale)
        dx, dshift, dscale = vjp(dh_ref[...].astype(F32))
        dres_v = dres_ref[...]
        if dres_rows is not None:
            dres_v = dres_v * jnp.where(i * tm < dres_rows, 1.0, 0.0)
        dx = dres_v + dx
        if out_rows is None:
            dx_ref[...] = dx
        else:
            @pl.when(i * tm < out_rows)
            def _():
                dx_ref[...] = dx
        dmods_ref[pl.ds(r_shift, 1), :] += dshift
        dmods_ref[pl.ds(r_scale, 1), :] += dscale
        if gate is not None:
            _gate_part(dx, seg, gate[2], *gate_in, *gate_out)

    def clamped(rows):
        return pl.BlockSpec((tm, d), lambda i: (jnp.minimum(i, rows // tm - 1), 0))

    row = pl.BlockSpec((tm, d), lambda i: (i, 0))
    vec = pl.BlockSpec((16, d), lambda i: (0, 0))
    in_specs = [row, vec, row, row if dres_rows is None else clamped(dres_rows)]
    out_specs = [row if out_rows is None else clamped(out_rows), vec]
    out_shape = [jax.ShapeDtypeStruct((t if out_rows is None else out_rows, d), F32),
                 jax.ShapeDtypeStruct((16, d), F32)]
    operands = [xs, mods, dh, dres]
    if gate is not None:
        in_specs += [row, vec]
        out_specs += [row, vec]
        out_shape += [jax.ShapeDtypeStruct((t, d), BF16), jax.ShapeDtypeStruct((16, d), F32)]
        operands += [gate[0], gate[1]]
    return _pcall(body, grid=(t // tm,), in_specs=in_specs, out_specs=out_specs, out_shape=out_shape,
                  scratch_shapes=[], semantics=("arbitrary",), vmem_mb=48, name=name, operands=operands, rider=rider)


def _final_loss(xs, g, target, branch, gmods, *, k_gate, name):
    t, d = xs.shape
    tm = _div(t, (256, 128))

    def loss_fn(x, gv, tgt):
        err = _rms(x) * gv - tgt
        return 0.5 * jnp.sum(jnp.mean(jnp.square(err), axis=-1))

    def body(x_ref, g_ref, t_ref, br_ref, gmods_ref, dx_ref, loss_ref, dg_ref, db_ref, dgm_ref):
        i = pl.program_id(0)
        _zero_accs(i, [loss_ref, dg_ref, dgm_ref])
        val, vjp = jax.vjp(loss_fn, x_ref[...], g_ref[0:1, :], t_ref[...])
        dx, dg, _ = vjp(jnp.ones((), F32))
        dx_ref[...] = dx
        loss_ref[...] += val
        dg_ref[0:1, :] += dg
        _gate_part(dx, 0, k_gate, br_ref, gmods_ref, db_ref, dgm_ref)

    row = pl.BlockSpec((tm, d), lambda i: (i, 0))
    vec16 = pl.BlockSpec((16, d), lambda i: (0, 0))
    return pl.pallas_call(
        body, grid=(t // tm,),
        in_specs=[row, pl.BlockSpec((8, d), lambda i: (0, 0)), row, row, vec16],
        out_specs=[row, pl.BlockSpec((8, 128), lambda i: (0, 0)), pl.BlockSpec((8, d), lambda i: (0, 0)), row, vec16],
        out_shape=[jax.ShapeDtypeStruct((t, d), F32), jax.ShapeDtypeStruct((8, 128), F32),
                   jax.ShapeDtypeStruct((8, d), F32), jax.ShapeDtypeStruct((t, d), BF16),
                   jax.ShapeDtypeStruct((16, d), F32)],
        compiler_params=_params(("arbitrary",), 48), name=name,
    )(xs, g, target, branch, gmods)


def _halo_specs(r, width, col, t):
    h_per = r // HALO

    def prev(i):
        return (jnp.maximum(i * h_per - 1, 0), col)

    def nxt(i):
        return (jnp.minimum((i + 1) * h_per, t // HALO - 1), col)

    return (pl.BlockSpec((HALO, width), prev), pl.BlockSpec((r, width), lambda i: (i, col)),
            pl.BlockSpec((HALO, width), nxt))


def _seg_geometry(i, r, n_x, t):
    row0 = i * r
    in_ctx = row0 >= n_x
    first = jnp.logical_or(row0 == 0, row0 == n_x)
    last = jnp.logical_or(row0 + r == n_x, row0 + r == t)
    seg_start = jnp.where(in_ctx, n_x, 0)
    seg_len = jnp.where(in_ctx, t - n_x, n_x)
    return row0, first, last, seg_start, seg_len


ROW_CHUNK = 64
LANE_CHUNK = 128


def _chunks(width, rows, col0=0):
    lanes = min(LANE_CHUNK, width)
    return [(slice(col0 + c, col0 + c + lanes), r0) for c in range(0, width, lanes) for r0 in range(0, rows, ROW_CHUNK)]


def _pool_count(tpos, w, seg_len):
    return (jnp.minimum(tpos + w // 2, seg_len) - jnp.maximum(tpos - w // 2, 0)).astype(F32)


def _mixer0_fwd(z, conv_w, conv_b, ln_g, ln_b, pool_w, pool_scale, *, n_tap, n_x, name, rider=None):
    t, dc = z.shape[0], z.shape[1] // 3
    n_grp, pg = pool_w.shape[0], pool_w.shape[1]
    r = _div(t - n_x, (256, 128))
    assert n_x % r == 0 and pg * n_grp == dc
    half = n_tap // 2
    assert half < HALO and max(POOL_WINDOWS) // 2 <= HALO

    def body(ap, ac, an, gp, gc, gn, pp, pc, pn, w_ref, cb_ref, lg_ref, lb_ref, pw_ref, ps_ref,
             y_ref, cv_ref, d_ref, uwin, pwin):
        i = pl.program_id(0)
        row0, first, last, seg_start, seg_len = _seg_geometry(i, r, n_x, t)
        keep_prev = jnp.where(first, 0.0, 1.0)
        keep_next = jnp.where(last, 0.0, 1.0)
        uwin[0:HALO, :] = ap[...] * _sigmoid(gp[...]) * keep_prev
        uwin[HALO:HALO + r, :] = ac[...] * _sigmoid(gc[...])
        uwin[HALO + r:, :] = an[...] * _sigmoid(gn[...]) * keep_next
        pwin[0:HALO, :] = pp[...] * keep_prev
        pwin[HALO:HALO + r, :] = pc[...]
        pwin[HALO + r:, :] = pn[...] * keep_next
        for cols, r0 in _chunks(dc, r):
            acc = jnp.zeros((ROW_CHUNK, cols.stop - cols.start), F32) + cb_ref[0:1, cols]
            for k in range(n_tap):
                off = HALO - half + k + r0
                acc = acc + w_ref[k:k + 1, cols] * uwin[off:off + ROW_CHUNK, cols]
            cv_ref[r0:r0 + ROW_CHUNK, cols] = acc
        y_ref[:, 0:dc] = _ln_silu(cv_ref[...], lg_ref[0:1, :], lb_ref[0:1, :]).astype(BF16)
        tpos = row0 - seg_start + lax.broadcasted_iota(jnp.int32, (r, 1), 0)
        for g, w in enumerate(POOL_WINDOWS):
            cnt = _pool_count(tpos, w, seg_len)
            for cols, r0 in _chunks(pg, r, g * pg):
                s = jnp.zeros((ROW_CHUNK, cols.stop - cols.start), F32)
                for j in range(-(w // 2), w // 2):
                    s = s + pwin[HALO + j + r0:HALO + j + r0 + ROW_CHUNK, cols]
                diff = s / cnt[r0:r0 + ROW_CHUNK] - pwin[HALO + r0:HALO + r0 + ROW_CHUNK, cols]
                d_ref[r0:r0 + ROW_CHUNK, cols] = diff.astype(BF16)
            cols = slice(g * pg, (g + 1) * pg)
            pm = jnp.dot(d_ref[:, cols], pw_ref[g].astype(BF16), preferred_element_type=F32)
            y_ref[:, dc + g * pg:dc + (g + 1) * pg] = (pm * ps_ref[0:1, cols]).astype(BF16)

    vec = lambda rows, width: pl.BlockSpec((rows, width), lambda i: (0, 0))
    in_specs = [*_halo_specs(r, dc, 0, t), *_halo_specs(r, dc, 1, t), *_halo_specs(r, dc, 2, t),
                vec(conv_w.shape[0], dc), vec(8, dc), vec(8, dc), vec(8, dc),
                pl.BlockSpec((n_grp, pg, pg), lambda i: (0, 0, 0)), vec(8, dc)]
    return _pcall(
        body, grid=(t // r,), in_specs=in_specs,
        out_specs=[pl.BlockSpec((r, 2 * dc), lambda i: (i, 0)), pl.BlockSpec((r, dc), lambda i: (i, 0)),
                   pl.BlockSpec((r, dc), lambda i: (i, 0))],
        out_shape=[jax.ShapeDtypeStruct((t, 2 * dc), BF16), jax.ShapeDtypeStruct((t, dc), F32),
                   jax.ShapeDtypeStruct((t, dc), BF16)],
        scratch_shapes=[pltpu.VMEM((r + 2 * HALO, dc), F32), pltpu.VMEM((r + 2 * HALO, dc), F32)],
        semantics=("parallel",), vmem_mb=40, name=name,
        operands=[*([z] * 9), conv_w, conv_b, ln_g, ln_b, pool_w, pool_scale], rider=rider)


def _mixer0_bwd(dy, z, cv, dsave, conv_w, ln_g, ln_b, pool_w, pool_scale, *, n_tap, n_x, name, rider=None):
    t, dc = cv.shape
    n_grp, pg = pool_w.shape[0], pool_w.shape[1]
    r = _div(t - n_x, (256, 128))
    half = n_tap // 2
    dyp_, zp_, cvp_ = dy, z, cv
    rw = r + 2 * HALO

    def body(dcp, dcc, dcn, dpp, dpc, dpn, cvp, cvc, cvn, ap, ac, an, gp, gc, gn, d_ref,
             w_ref, lg_ref, lb_ref, pw_ref, ps_ref,
             dz_ref, dw_ref, dcb_ref, dlg_ref, dlb_ref, dpw_ref, dps_ref, uwin, dcvwin, ewin, ddwin):
        i = pl.program_id(0)
        _zero_accs(i, [dw_ref, dcb_ref, dlg_ref, dlb_ref, dpw_ref, dps_ref])
        row0, first, last, seg_start, seg_len = _seg_geometry(i, r, n_x, t)
        keep_prev = jnp.where(first, 0.0, 1.0)
        keep_next = jnp.where(last, 0.0, 1.0)
        lg, lb = lg_ref[0:1, :], lb_ref[0:1, :]
        _, vjp = jax.vjp(_ln_silu, cvc[...], lg, lb)
        dcv, dlg, dlb = vjp(dcc[...])
        dlg_ref[0:1, :] += dlg
        dlb_ref[0:1, :] += dlb
        dcb_ref[0:1, :] += jnp.sum(dcv, axis=0, keepdims=True)
        dcvwin[HALO:HALO + r, :] = dcv
        for halo_cv, halo_dy, keep, lo in ((cvp, dcp, keep_prev, 0), (cvn, dcn, keep_next, HALO + r)):
            _, vjp_h = jax.vjp(lambda v: _ln_silu(v, lg, lb), halo_cv[...])
            dcvwin[lo:lo + HALO, :] = vjp_h(halo_dy[...])[0] * keep
        uwin[0:HALO, :] = ap[...] * _sigmoid(gp[...]) * keep_prev
        uwin[HALO:HALO + r, :] = ac[...] * _sigmoid(gc[...])
        uwin[HALO + r:, :] = an[...] * _sigmoid(gn[...]) * keep_next
        for cols, r0 in _chunks(dc, r):
            du = jnp.zeros((ROW_CHUNK, cols.stop - cols.start), F32)
            for k in range(n_tap):
                off = HALO + half - k + r0
                du = du + w_ref[k:k + 1, cols] * dcvwin[off:off + ROW_CHUNK, cols]
            rows = slice(r0, r0 + ROW_CHUNK)
            sig = _sigmoid(gc[rows, cols])
            dz_ref[rows, cols] = (du * sig).astype(BF16)
            dz_ref[rows, dc + cols.start:dc + cols.stop] = (du * ac[rows, cols] * sig * (1.0 - sig)).astype(BF16)
        for c0 in range(0, dc, LANE_CHUNK):
            cols = slice(c0, c0 + LANE_CHUNK)
            taps = [jnp.zeros((8, LANE_CHUNK), F32) for _ in range(n_tap)]
            for r0 in range(0, r, ROW_CHUNK):
                dcv_c = dcvwin[HALO + r0:HALO + r0 + ROW_CHUNK, cols]
                for k in range(n_tap):
                    off = HALO - half + k + r0
                    prod = dcv_c * uwin[off:off + ROW_CHUNK, cols]
                    taps[k] = taps[k] + functools.reduce(
                        jnp.add, [prod[8 * s:8 * s + 8] for s in range(ROW_CHUNK // 8)])
            for k in range(n_tap):
                dw_ref[k:k + 1, cols] += jnp.sum(taps[k], axis=0, keepdims=True)
        twin = row0 - seg_start - HALO + lax.broadcasted_iota(jnp.int32, (rw, 1), 0)
        for g, w in enumerate(POOL_WINDOWS):
            cols = slice(g * pg, (g + 1) * pg)
            wg = pw_ref[g].astype(BF16)
            scale = ps_ref[0:1, cols]
            dyp_c = dpc[:, cols]
            dpm_win = jnp.concatenate([dpp[:, cols] * keep_prev, dyp_c, dpn[:, cols] * keep_next], axis=0) * scale
            dd_win = lax.dot_general(dpm_win.astype(BF16), wg, (((1,), (1,)), ((), ())), preferred_element_type=F32)
            cnt = jnp.maximum(_pool_count(twin, w, seg_len), 1.0)
            ddwin[:, cols] = dd_win
            ewin[:, cols] = dd_win / cnt
            for ccols, r0 in _chunks(pg, r, g * pg):
                dup = -ddwin[HALO + r0:HALO + r0 + ROW_CHUNK, ccols]
                for j in range(-(w // 2) + 1, w // 2 + 1):
                    dup = dup + ewin[HALO + j + r0:HALO + j + r0 + ROW_CHUNK, ccols]
                dz_ref[r0:r0 + ROW_CHUNK, 2 * dc + ccols.start:2 * dc + ccols.stop] = dup.astype(BF16)
            dsv = d_ref[:, cols]
            pm = jnp.dot(dsv, wg, preferred_element_type=F32)
            dps_ref[0:1, cols] += jnp.sum(dyp_c * pm, axis=0, keepdims=True)
            dpw_ref[g] += lax.dot_general(dsv, (dyp_c * scale).astype(BF16), (((0,), (0,)), ((), ())),
                                          preferred_element_type=F32)

    vec = lambda rows, width: pl.BlockSpec((rows, width), lambda i: (0, 0))
    grp = pl.BlockSpec((n_grp, pg, pg), lambda i: (0, 0, 0))
    in_specs = [*_halo_specs(r, dc, 0, t), *_halo_specs(r, dc, 1, t), *_halo_specs(r, dc, 0, t),
                *_halo_specs(r, dc, 0, t), *_halo_specs(r, dc, 1, t), pl.BlockSpec((r, dc), lambda i: (i, 0)),
                vec(conv_w.shape[0], dc), vec(8, dc), vec(8, dc), grp, vec(8, dc)]
    return _pcall(
        body, grid=(t // r,), in_specs=in_specs,
        out_specs=[pl.BlockSpec((r, 3 * dc), lambda i: (i, 0)), vec(conv_w.shape[0], dc), vec(8, dc), vec(8, dc),
                   vec(8, dc), grp, vec(8, dc)],
        out_shape=[jax.ShapeDtypeStruct((t, 3 * dc), BF16), jax.ShapeDtypeStruct(conv_w.shape, F32),
                   jax.ShapeDtypeStruct((8, dc), F32), jax.ShapeDtypeStruct((8, dc), F32),
                   jax.ShapeDtypeStruct((8, dc), F32), jax.ShapeDtypeStruct(pool_w.shape, F32),
                   jax.ShapeDtypeStruct((8, dc), F32)],
        scratch_shapes=[pltpu.VMEM((rw, dc), F32)] * 4,
        semantics=("arbitrary",), vmem_mb=VMEM_CAP_MB, name=name,
        operands=[dyp_, dyp_, dyp_, dyp_, dyp_, dyp_, cvp_, cvp_, cvp_, zp_, zp_, zp_, zp_, zp_, zp_, dsave,
                  conv_w, ln_g, ln_b, pool_w, pool_scale], rider=rider)


def _swap_halves(x):
    lane = lax.broadcasted_iota(jnp.int32, x.shape, 1)
    quarter = HEAD_DIM // 4
    return jnp.where(lane % (2 * quarter) < quarter,
                     pltpu.roll(x, HEAD_DIM - quarter, 1), pltpu.roll(x, quarter, 1))


def _rope_tables(n_x, n_ctx):
    half = HEAD_DIM // 4
    freqs = ROPE_THETA ** (-jnp.arange(half, dtype=F32) / half)
    tok = jnp.arange(n_x)
    row = (tok // GRID_W).astype(F32)[:, None] * freqs[None, :]
    col = (tok % GRID_W).astype(F32)[:, None] * freqs[None, :]
    cos = jnp.concatenate([jnp.cos(row), jnp.cos(row), jnp.cos(col), jnp.cos(col)], axis=1)
    sin = jnp.concatenate([-jnp.sin(row), jnp.sin(row), -jnp.sin(col), jnp.sin(col)], axis=1)
    cos = jnp.concatenate([cos, jnp.ones((n_ctx, HEAD_DIM), F32)], axis=0)
    sin = jnp.concatenate([sin, jnp.zeros((n_ctx, HEAD_DIM), F32)], axis=0)
    return cos, sin


def _norm_g(x, g):
    return _rms(x) * g


def _qk_prep_fwd(qkv, gq, gk, cos, sin, *, d_q, name):
    t, width = qkv.shape
    d_kv = (width - d_q) // 2
    tm = _div(t, (256, 128))

    def body(qkv_ref, gq_ref, gk_ref, cos_ref, sin_ref, q_ref, k_ref, v_ref):
        cs, sn = cos_ref[...], sin_ref[...]
        for h in range((d_q + d_kv) // HEAD_DIM):
            g = gq_ref[0:1, :] if h * HEAD_DIM < d_q else gk_ref[0:1, :]
            xn = _norm_g(qkv_ref[:, h * HEAD_DIM:(h + 1) * HEAD_DIM], g)
            rot = xn * cs + _swap_halves(xn) * sn
            if h * HEAD_DIM < d_q:
                q_ref[:, h * HEAD_DIM:(h + 1) * HEAD_DIM] = (rot * Q_SCALE_LOG2).astype(BF16)
            else:
                k_ref[:, h * HEAD_DIM - d_q:(h + 1) * HEAD_DIM - d_q] = rot.astype(BF16)
        v_ref[...] = qkv_ref[:, d_q + d_kv:].astype(BF16)

    row = lambda w: pl.BlockSpec((tm, w), lambda i: (i, 0))
    vec = pl.BlockSpec((8, HEAD_DIM), lambda i: (0, 0))
    return pl.pallas_call(
        body, grid=(t // tm,),
        in_specs=[row(width), vec, vec, row(HEAD_DIM), row(HEAD_DIM)],
        out_specs=[row(d_q), row(d_kv), row(d_kv)],
        out_shape=[jax.ShapeDtypeStruct((t, d_q), BF16), jax.ShapeDtypeStruct((t, d_kv), BF16),
                   jax.ShapeDtypeStruct((t, d_kv), BF16)],
        compiler_params=_params(("parallel",), 32), name=name,
    )(qkv, gq, gk, cos, sin)


def _qk_prep_bwd(qkv, dq, dk, dv, gq, gk, cos, sin, *, n_x, name):
    t, width = qkv.shape
    d_q, d_kv = dq.shape[1], dk.shape[1]
    tm = _div(t, (256, 128))
    last_q = n_x // tm - 1

    def body(qkv_ref, dq_ref, dk_ref, dv_ref, gq_ref, gk_ref, cos_ref, sin_ref, out_ref, dgq_ref, dgk_ref):
        i = pl.program_id(0)
        _zero_accs(i, [dgq_ref, dgk_ref])
        is_x = jnp.where(i * tm < n_x, 1.0, 0.0)
        cs, sn = cos_ref[...], sin_ref[...]
        for h in range((d_q + d_kv) // HEAD_DIM):
            sl = slice(h * HEAD_DIM, (h + 1) * HEAD_DIM)
            if h * HEAD_DIM < d_q:
                g, dg_ref, dr = gq_ref[0:1, :], dgq_ref, dq_ref[:, sl] * is_x
            else:
                g, dg_ref = gk_ref[0:1, :], dgk_ref
                dr = dk_ref[:, h * HEAD_DIM - d_q:(h + 1) * HEAD_DIM - d_q]
            dxn = dr * cs + _swap_halves(dr * sn)
            _, vjp = jax.vjp(_norm_g, qkv_ref[:, sl], g)
            dx, dg = vjp(dxn)
            out_ref[:, sl] = dx.astype(BF16)
            dg_ref[0:1, :] += dg
        out_ref[:, d_q + d_kv:] = dv_ref[...].astype(BF16)

    row = lambda w: pl.BlockSpec((tm, w), lambda i: (i, 0))
    vec = pl.BlockSpec((8, HEAD_DIM), lambda i: (0, 0))
    return pl.pallas_call(
        body, grid=(t // tm,),
        in_specs=[row(width), pl.BlockSpec((tm, d_q), lambda i: (jnp.minimum(i, last_q), 0)), row(d_kv), row(d_kv),
                  vec, vec, row(HEAD_DIM), row(HEAD_DIM)],
        out_specs=[row(width), vec, vec],
        out_shape=[jax.ShapeDtypeStruct((t, width), BF16), jax.ShapeDtypeStruct((8, HEAD_DIM), F32),
                   jax.ShapeDtypeStruct((8, HEAD_DIM), F32)],
        compiler_params=_params(("arbitrary",), 40), name=name,
    )(qkv, dq, dk, dv, gq, gk, cos, sin)


ATTN_TQ = (256, 128)
ATTN_TK = (768, 640, 512, 384, 256, 128)


def _attention_fwd(q, k, v_t, *, n_x, name):
    t, d_kv = k.shape
    d_q = q.shape[1]
    kvh = d_kv // HEAD_DIM
    grp = d_q // d_kv
    tq = _div(n_x, ATTN_TQ)
    tk = _div(t, ATTN_TK)
    gw = grp * HEAD_DIM
    n_kv = t // tk
    n_pair = (n_kv - 1) // 2

    def fold8(x):
        return functools.reduce(jnp.add, [x[8 * r:8 * r + 8] for r in range(tk // 8)])

    def body(q_ref, k_ref, vt_ref, o_ref, lse_ref, m_ref, l_ref, acc_ref, qt_ref, s_even, s_odd):
        m_ref[...] = jnp.full_like(m_ref, -jnp.inf)
        l_ref[...] = jnp.zeros_like(l_ref)
        acc_ref[...] = jnp.zeros_like(acc_ref)
        for g in range(grp):
            qt_ref[g] = q_ref[:, g * HEAD_DIM:(g + 1) * HEAD_DIM].T

        def keys(j):
            return pl.ds(pl.multiple_of(j * tk, tk), tk)

        def scores(g, kc):
            return jnp.dot(kc, qt_ref[g], preferred_element_type=F32)

        def chunk(j, s_cur, s_next):
            vt = vt_ref[:, keys(j)]
            kn = k_ref[keys(j + 1), :] if s_next is not None else None
            for g in range(grp):
                s = s_cur[g]
                m_old = m_ref[g]
                m_new = jnp.maximum(m_old, jnp.max(s, axis=0, keepdims=True))
                alpha = jnp.exp2(m_old - m_new)
                p = jnp.exp2(s - m_new)
                if s_next is not None:
                    s_next[g] = scores(g, kn)
                l_ref[g] = alpha * l_ref[g] + fold8(p)
                acc_ref[g] = alpha * acc_ref[g] + jnp.dot(vt, p.astype(BF16), preferred_element_type=F32)
                m_ref[g] = m_new

        k0 = k_ref[keys(0), :]
        for g in range(grp):
            s_even[g] = scores(g, k0)

        def pair(i, carry):
            chunk(2 * i, s_even, s_odd)
            chunk(2 * i + 1, s_odd, s_even)
            return carry

        lax.fori_loop(0, n_pair, pair, 0)
        if n_kv - 2 * n_pair == 2:
            chunk(n_kv - 2, s_even, s_odd)
            chunk(n_kv - 1, s_odd, None)
        else:
            chunk(n_kv - 1, s_even, None)
        for g in range(grp):
            l = jnp.sum(l_ref[g], axis=0, keepdims=True)
            o_ref[:, g * HEAD_DIM:(g + 1) * HEAD_DIM] = (acc_ref[g] / l).T.astype(BF16)
            lse_row = m_ref[g] + jnp.log(l) * LOG2_E
            lse_ref[:, g:g + 1] = jnp.broadcast_to(lse_row, (HEAD_DIM, tq)).T[:, 0:1]

    return pl.pallas_call(
        body, grid=(kvh, n_x // tq),
        in_specs=[pl.BlockSpec((tq, gw), lambda h, i: (i, h)),
                  pl.BlockSpec((t, HEAD_DIM), lambda h, i: (0, h)),
                  pl.BlockSpec((HEAD_DIM, t), lambda h, i: (h, 0))],
        out_specs=[pl.BlockSpec((tq, gw), lambda h, i: (i, h)),
                   pl.BlockSpec((None, tq, grp), lambda h, i: (h, i, 0))],
        out_shape=[jax.ShapeDtypeStruct((n_x, d_q), BF16), jax.ShapeDtypeStruct((kvh, n_x, grp), F32)],
        scratch_shapes=[pltpu.VMEM((grp, 1, tq), F32), pltpu.VMEM((grp, 8, tq), F32),
                        pltpu.VMEM((grp, HEAD_DIM, tq), F32), pltpu.VMEM((grp, HEAD_DIM, tq), BF16),
                        pltpu.VMEM((grp, tk, tq), F32), pltpu.VMEM((grp, tk, tq), F32)],
        compiler_params=_params(("parallel", "arbitrary"), 48), name=name,
    )(q, k, v_t)


def _attention_bwd(q, k, v, o, lse, do, *, n_x, name):
    t, d_kv = k.shape
    d_q = q.shape[1]
    kvh = d_kv // HEAD_DIM
    grp = d_q // d_kv
    tq = _div(n_x, ATTN_TQ)
    tk = _div(t, ATTN_TK)
    gw = grp * HEAD_DIM
    n_q = n_x // tq

    def body(q_ref, k_ref, v_ref, o_ref, lse_ref, do_ref, dq_ref, dkt_ref, dvt_ref, dq_acc, lse_s, delta_s, qt_s,
             dot_s):
        i = pl.program_id(1)
        _zero_accs(i, [dkt_ref, dvt_ref])
        dq_acc[...] = jnp.zeros_like(dq_acc)
        for g in range(grp):
            sl = slice(g * HEAD_DIM, (g + 1) * HEAD_DIM)
            lse_s[g] = lse_ref[:, g:g + 1]
            delta_s[g] = jnp.sum(do_ref[:, sl].astype(F32) * o_ref[:, sl].astype(F32), axis=-1, keepdims=True)
            qt_s[g] = q_ref[:, sl].T
            dot_s[g] = do_ref[:, sl].T

        def step(j, carry):
            start = pl.multiple_of(j * tk, tk)
            kc, vc = k_ref[pl.ds(start, tk), :], v_ref[pl.ds(start, tk), :]
            dkt_part = jnp.zeros((HEAD_DIM, tk), F32)
            dvt_part = jnp.zeros((HEAD_DIM, tk), F32)
            for g in range(grp):
                sl = slice(g * HEAD_DIM, (g + 1) * HEAD_DIM)
                s = lax.dot_general(q_ref[:, sl], kc, (((1,), (1,)), ((), ())), preferred_element_type=F32)
                p = jnp.exp2(s - lse_s[g])
                dp = lax.dot_general(do_ref[:, sl], vc, (((1,), (1,)), ((), ())), preferred_element_type=F32)
                ds = (p * (dp - delta_s[g])).astype(BF16)
                dq_acc[g] += jnp.dot(ds, kc, preferred_element_type=F32)
                dvt_part = dvt_part + jnp.dot(dot_s[g], p.astype(BF16), preferred_element_type=F32)
                dkt_part = dkt_part + jnp.dot(qt_s[g], ds, preferred_element_type=F32)
            dvt_ref[:, pl.ds(start, tk)] += dvt_part
            dkt_ref[:, pl.ds(start, tk)] += dkt_part
            return carry

        lax.fori_loop(0, t // tk, step, 0)
        for g in range(grp):
            dq_ref[:, g * HEAD_DIM:(g + 1) * HEAD_DIM] = dq_acc[g] * ATTN_SCALE

        @pl.when(i == n_q - 1)
        def _():
            dkt_ref[...] = dkt_ref[...] * (1.0 / LOG2_E)

    qspec = pl.BlockSpec((tq, gw), lambda h, i: (i, h))
    kspec = pl.BlockSpec((t, HEAD_DIM), lambda h, i: (0, h))
    ktspec = pl.BlockSpec((HEAD_DIM, t), lambda h, i: (h, 0))
    return pl.pallas_call(
        body, grid=(kvh, n_q),
        in_specs=[qspec, kspec, kspec, qspec, pl.BlockSpec((None, tq, grp), lambda h, i: (h, i, 0)), qspec],
        out_specs=[qspec, ktspec, ktspec],
        out_shape=[jax.ShapeDtypeStruct((n_x, d_q), F32), jax.ShapeDtypeStruct((d_kv, t), F32),
                   jax.ShapeDtypeStruct((d_kv, t), F32)],
        scratch_shapes=[pltpu.VMEM((grp, tq, HEAD_DIM), F32), pltpu.VMEM((grp, tq, 1), F32),
                        pltpu.VMEM((grp, tq, 1), F32), pltpu.VMEM((grp, HEAD_DIM, tq), BF16),
                        pltpu.VMEM((grp, HEAD_DIM, tq), BF16)],
        compiler_params=_params(("parallel", "arbitrary"), 56), name=name,
    )(q, k, v, o, lse, do)


def _whole(body, ins, out_shapes, name):
    return pl.pallas_call(
        body, out_shape=[jax.ShapeDtypeStruct(s, d) for s, d in out_shapes],
        compiler_params=pltpu.CompilerParams(vmem_limit_bytes=40 << 20), name=name)(*ins)


def _silu_rows(x, *, name):
    def body(x_ref, o_ref):
        o_ref[...] = _silu(x_ref[...])
    return _whole(body, [x], [(x.shape, F32)], name)[0]


def _assemble_dmods(gathered, *, name):
    width = gathered.shape[1]

    def body(g_ref, dm0, dm1, db0, db1):
        for l, (dm, db) in enumerate(((dm0, db0), (dm1, db1))):
            ctx = jnp.zeros((1, width), F32)
            tot = jnp.zeros((1, width), F32)
            for q in range(N_DEV):
                row = g_ref[16 * q + 8 * l:16 * q + 8 * l + 1, :]
                dm[q:q + 1, :] = row
                tot = tot + row
                ctx = ctx + g_ref[16 * q + 8 * l + 1:16 * q + 8 * l + 2, :]
            dm[N_DEV:N_DEV + 1, :] = ctx
            dm[N_DEV + 1:, :] = jnp.zeros((16 - N_DEV - 1, width), F32)
            db[...] = jnp.zeros_like(db)
            db[0:1, :] = tot + ctx

    return _whole(body, [gathered], [((16, width), F32), ((16, width), F32), ((8, width), F32), ((8, width), F32)],
                  name)


def _sum_slots(gathered, rows, *, name):
    def body(g_ref, o_ref):
        acc = g_ref[0:rows, :]
        for q in range(1, N_DEV):
            acc = acc + g_ref[q * rows:(q + 1) * rows, :]
        o_ref[...] = acc
    return _whole(body, [gathered], [((rows, gathered.shape[1]), F32)], name)[0]


def _silu_grad(x, dy, *, name):
    def body(x_ref, dy_ref, o_ref):
        _, vjp = jax.vjp(_silu, x_ref[...])
        o_ref[...] = vjp(dy_ref[...])[0]
    return _whole(body, [x, dy], [(x.shape, F32)], name)[0]


def _adamw(w, m, v, *, name, recv=None, grad=None):
    rows, cols = w.shape
    budget = max(8, ADAMW_BLOCK_ELEMS // cols)
    tr = _div(rows, [c for c in (512, 256, 128, 64, 32, 16, 8) if c <= budget] + [rows])
    c1 = 1.0 - ADAM_B1 ** ADAM_STEP
    c2 = 1.0 - ADAM_B2 ** ADAM_STEP

    def body(w_ref, m_ref, v_ref, g_in, g_ref, d_ref, nm_ref, nv_ref):
        if recv is not None:
            g = g_in[0].astype(F32)
            for q in range(1, N_DEV):
                g = g + g_in[q].astype(F32)
        else:
            g = g_in[...]
        nm = ADAM_B1 * m_ref[...] + (1.0 - ADAM_B1) * g
        nv = ADAM_B2 * v_ref[...] + (1.0 - ADAM_B2) * jnp.square(g)
        g_ref[...] = g
        nm_ref[...] = nm
        nv_ref[...] = nv
        d_ref[...] = -ADAM_LR * ((nm / c1) / (jnp.sqrt(nv / c2) + ADAM_EPS) + ADAM_WD * w_ref[...])

    blk = pl.BlockSpec((tr, cols), lambda i: (i, 0))
    g_spec = pl.BlockSpec((N_DEV, tr, cols), lambda i: (0, i, 0)) if recv is not None else blk
    return pl.pallas_call(
        body, grid=(rows // tr,), in_specs=[blk, blk, blk, g_spec], out_specs=[blk] * 4,
        out_shape=[jax.ShapeDtypeStruct((rows, cols), F32)] * 4,
        compiler_params=_params(("parallel",), 48), name=name,
    )(w, m, v, recv if recv is not None else grad)


def _position():
    return tuple(lax.axis_index(a) for a in MESH_AXES)


def _linear(pos):
    return 4 * pos[0] + 2 * pos[1] + pos[2]


def _window(ref, axis, dev, size):
    start = pl.multiple_of(dev * size, size)
    return ref.at[pl.ds(start, size), :] if axis == 0 else ref.at[:, pl.ds(start, size)]


def _all_gather(shards, axes, *, name):
    n = len(shards)
    sizes = [s.shape[ax] for s, ax in zip(shards, axes)]

    def body(*refs):
        src, dst = refs[:n], refs[n:2 * n]
        send_sems, recv_sems, local_sems = refs[2 * n:]
        x, y, c = _position()
        me, sibling = (x, y, c), (x, y, 1 - c)
        chips = [(1 - x, y), (x, 1 - y), (1 - x, 1 - y)]

        def win(k, pos):
            return _window(dst[k], axes[k], _linear(pos), sizes[k])

        def copy(k, sem, block, to, from_src=False):
            return pltpu.make_async_remote_copy(
                src_ref=src[k] if from_src else win(k, block), dst_ref=win(k, block),
                send_sem=send_sems.at[k, sem], recv_sem=recv_sems.at[k, sem],
                device_id=to, device_id_type=MESH_ID)

        mine = [pltpu.make_async_copy(src[k], win(k, me), local_sems.at[k]) for k in range(n)]
        for cp in mine:
            cp.start()
        first = []
        for k in range(n):
            first.append(copy(k, 0, me, sibling, from_src=True))
            first += [copy(k, 1 + j, me, (*chip, c), from_src=True) for j, chip in enumerate(chips)]
        for cp in first:
            cp.start()
        passed = []
        for j, chip in enumerate(chips):
            for k in range(n):
                copy(k, 1 + j, (*chip, c), me).wait_recv()
                fwd = copy(k, 4 + j, (*chip, c), sibling)
                fwd.start()
                passed.append(fwd)
        for k in range(n):
            copy(k, 0, sibling, me).wait_recv()
            for j, chip in enumerate(chips):
                copy(k, 4 + j, (*chip, 1 - c), me).wait_recv()
        for cp in first + passed:
            cp.wait_send()
        for cp in mine:
            cp.wait()

    out_shape = []
    for s, ax in zip(shards, axes):
        full = (s.shape[0] * N_DEV, s.shape[1]) if ax == 0 else (s.shape[0], s.shape[1] * N_DEV)
        out_shape.append(jax.ShapeDtypeStruct(full, s.dtype))
    any_spec = pl.BlockSpec(memory_space=pl.ANY)
    return pl.pallas_call(
        body, in_specs=[any_spec] * n, out_specs=[any_spec] * n, out_shape=out_shape,
        scratch_shapes=[pltpu.SemaphoreType.DMA((n, 7)), pltpu.SemaphoreType.DMA((n, 7)),
                        pltpu.SemaphoreType.DMA((n,))],
        name=name,
    )(*shards)


class Rider:
    def __init__(self, kind, arrays, axes):
        self.kind, self.arrays, self.axes = kind, list(arrays), list(axes)
        self.n = len(self.arrays)
        if kind == "gather":
            self.sizes = [a.shape[ax] for a, ax in zip(self.arrays, self.axes)]
        else:
            self.sizes = [a.shape[ax] // N_DEV for a, ax in zip(self.arrays, self.axes)]

    def out_shape(self):
        shapes = []
        for a, ax, sz in zip(self.arrays, self.axes, self.sizes):
            if self.kind == "gather":
                full = (sz * N_DEV, a.shape[1]) if ax == 0 else (a.shape[0], sz * N_DEV)
                shapes.append(jax.ShapeDtypeStruct(full, a.dtype))
            else:
                shard = (sz, a.shape[1]) if ax == 0 else (a.shape[0], sz)
                shapes.append(jax.ShapeDtypeStruct((N_DEV, *shard), a.dtype))
        return shapes

    def scratch(self):
        return [pltpu.SemaphoreType.DMA((self.n, N_DEV - 1)), pltpu.SemaphoreType.DMA((self.n, N_DEV - 1)),
                pltpu.SemaphoreType.DMA((self.n,))]

    def plan(self, src, dst, send_sems, recv_sems, local_sems):
        x, y, c = me = _position()
        mine = _linear(me)

        def remote(k, sem, src_ref, dst_ref, to):
            return pltpu.make_async_remote_copy(
                src_ref=src_ref, dst_ref=dst_ref, send_sem=send_sems.at[k, sem], recv_sem=recv_sems.at[k, sem],
                device_id=to, device_id_type=MESH_ID)

        ph = dict(local=[], start=[], mid_wait=[], mid_start=[], end_wait=[])
        for k in range(self.n):
            ax, sz = self.axes[k], self.sizes[k]
            if self.kind == "exchange":
                own_src, own_dst = _window(src[k], ax, mine, sz), dst[k].at[mine]
                ph["local"].append(pltpu.make_async_copy(own_src, own_dst, local_sems.at[k]))
                for mask in range(1, N_DEV):
                    to = tuple(1 - p if (mask >> (2 - b)) & 1 else p for b, p in enumerate(me))
                    ph["start"].append(remote(k, mask - 1, _window(src[k], ax, _linear(to), sz), own_dst, to))
                    ph["end_wait"].append(remote(k, mask - 1, own_src, dst[k].at[_linear(to)], to))
            else:
                def win(pos, k=k, ax=ax, sz=sz):
                    return _window(dst[k], ax, _linear(pos), sz)
                sibling = (x, y, 1 - c)
                chips = [(1 - x, y), (x, 1 - y), (1 - x, 1 - y)]
                ph["local"].append(pltpu.make_async_copy(src[k], win(me), local_sems.at[k]))
                ph["start"].append(remote(k, 0, src[k], win(me), sibling))
                ph["end_wait"].append(remote(k, 0, src[k], win(sibling), sibling))
                for j, chip in enumerate(chips):
                    ph["start"].append(remote(k, 1 + j, src[k], win(me), (*chip, c)))
                    ph["mid_wait"].append(remote(k, 1 + j, src[k], win((*chip, c)), (*chip, c)))
                    ph["mid_start"].append(remote(k, 4 + j, win((*chip, c)), win((*chip, c)), sibling))
                    ph["end_wait"].append(remote(k, 4 + j, src[k], win((*chip, 1 - c)), sibling))
        return ph


def _pcall(body, *, grid, in_specs, out_specs, out_shape, scratch_shapes, semantics, vmem_mb, name, operands,
           rider=None):
    if rider is None:
        return pl.pallas_call(body, grid=grid, in_specs=in_specs, out_specs=out_specs, out_shape=out_shape,
                              scratch_shapes=scratch_shapes, compiler_params=_params(semantics, vmem_mb),
                              name=name)(*operands)
    n_in, n_out, n_scr, n = len(in_specs), len(out_specs), len(scratch_shapes), rider.n

    def wrapped(*refs):
        ins, src = refs[:n_in], refs[n_in:n_in + n]
        outs = refs[n_in + n:n_in + n + n_out]
        dst = refs[n_in + n + n_out:n_in + 2 * n + n_out]
        rest = refs[n_in + 2 * n + n_out:]
        scratch, sems = rest[:n_scr], rest[n_scr:]
        step = functools.reduce(lambda acc, ig: acc * ig[1] + ig[0],
                                [(pl.program_id(dim), g) for dim, g in enumerate(grid)], 0)
        n_steps = functools.reduce(lambda a, b: a * b, grid)

        @pl.when(step == 0)
        def _():
            ph = rider.plan(src, dst, *sems)
            for cp in ph["local"] + ph["start"]:
                cp.start()

        body(*ins, *outs, *scratch)

        @pl.when(step == (n_steps * 5) // 8)
        def _():
            ph = rider.plan(src, dst, *sems)
            for cp in ph["mid_wait"]:
                cp.wait_recv()
            for cp in ph["mid_start"]:
                cp.start()

        @pl.when(step == n_steps - 1)
        def _():
            ph = rider.plan(src, dst, *sems)
            for cp in ph["end_wait"]:
                cp.wait_recv()
            for cp in ph["start"] + ph["mid_start"]:
                cp.wait_send()
            for cp in ph["local"]:
                cp.wait()

    any_spec = pl.BlockSpec(memory_space=pl.ANY)
    return pl.pallas_call(
        wrapped, grid=grid, in_specs=list(in_specs) + [any_spec] * n, out_specs=list(out_specs) + [any_spec] * n,
        out_shape=list(out_shape) + rider.out_shape(), scratch_shapes=list(scratch_shapes) + rider.scratch(),
        compiler_params=_params(("arbitrary",) * len(grid), vmem_mb), name=name,
    )(*operands, *rider.arrays)


WEIGHTS = ['c_ctx', 'l0_ada_w', 'l0_ada_b', 'l0_in_w', 'l0_conv_w', 'l0_conv_b', 'l0_conv_ln_g', 'l0_conv_ln_b',
           'l0_pool_w', 'l0_pool_scale', 'l0_out_w', 'l0_mlp_w1', 'l0_mlp_w2', 'l1_ada_w', 'l1_ada_b', 'l1_qkv_w',
           'l1_q_norm_g', 'l1_k_norm_g', 'l1_out_w', 'l1_mlp_w1', 'l1_mlp_w2', 'final_g']
SHARDED = {'l0_in_w': 1, 'l0_out_w': 0, 'l0_mlp_w1': 1, 'l0_mlp_w2': 0,
           'l1_qkv_w': 1, 'l1_out_w': 0, 'l1_mlp_w1': 1, 'l1_mlp_w2': 0}
REPLICATED_SMALL = ['l0_conv_b', 'l0_conv_ln_g', 'l0_conv_ln_b', 'l0_pool_scale', 'l1_q_norm_g', 'l1_k_norm_g',
                    'final_g']


def _row8(v):
    v = v.reshape(1, -1)
    return jnp.pad(v, ((0, 7), (0, 0)))


def _mods16(full, me):
    d = full.shape[1] // 6
    mine = lax.dynamic_slice_in_dim(full, me, 1, axis=0).reshape(6, d)
    ctx = full[N_DEV].reshape(6, d)
    return jnp.pad(jnp.stack([mine, ctx], axis=1).reshape(12, d), ((0, 4), (0, 0)))


def _mlp_fwd(xs, mods, w1, w2, *, n_x, tm, tag, rider1=None, rider2=None):
    t, d = xs.shape
    dff = w1.shape[1]
    h = _rms_mod_fwd(xs, mods, k_shift=3, n_x=n_x, name=f"{tag}_norm2")
    pre, act, *ride1 = _matmul(h, w1, mode="nn", tm=tm, tn=_div(dff, WIDE_TN if tm * WIDE_TN[0] <= WIDE_TILE_ELEMS else WIDE_TN[1:]), tk=d, out_dtypes=[BF16, BF16],
                               name=f"{tag}_mlp1", rider=rider1,
                               epilogue=lambda acc, rows: (acc, jnp.square(jnp.maximum(acc, 0.0))))
    if w2 is None:
        w2 = ride1[-1]
    xo, branch, *ride2 = _matmul(
        act, w2, mode="nn", tm=_div(t, DEEP_TM), tn=_div(d, DEEP_TN), tk=dff, out_dtypes=[F32, BF16],
        name=f"{tag}_mlp2", extras=[("tile", xs), ("vec", mods)], rider=rider2,
        epilogue=lambda acc, rows, res, mv: (res + _seg_pick(mv, 5, rows, n_x) * acc, acc))
    return xo, dict(h=h, pre=pre, act=act, branch=branch, x_in=xs), ride1, ride2


def _exchange_of(items):
    return Rider("exchange", [a for a, _ in items], [ax for _, ax in items]) if items else None


def _mlp_bwd(dxo, dbranch, saved, mods, w1, w2, mixer_branch, *, n_x, tm, tag, ride_dx2=(), ride_dw2=()):
    t, d = dxo.shape
    dff = w1.shape[1]
    tkt = _div(t, TOKEN_TK)
    dpre, *recv_a = _matmul(dbranch, w2, mode="nt", tm=tm, tn=_div(dff, WIDE_TN if tm * WIDE_TN[0] <= WIDE_TILE_ELEMS else WIDE_TN[1:]), tk=d, out_dtypes=[BF16],
                            name=f"{tag}_mlp2_dx", extras=[("tile", saved["pre"])],
                            rider=_exchange_of(list(ride_dx2)),
                            epilogue=lambda acc, rows, pre: (acc * 2.0 * jnp.maximum(pre.astype(F32), 0.0),))
    dw2, *recv_b = _matmul(saved["act"], dbranch, mode="tn", tm=_div(dff, (1024, 512)), tn=_div(d, (1024, 512)),
                           tk=tkt, out_dtypes=[BF16], name=f"{tag}_mlp2_dw", rider=_exchange_of(list(ride_dw2)))
    dh, = _matmul(dpre, w1, mode="nt", tm=_div(t, DEEP_TM), tn=_div(d, DEEP_TN), tk=dff,
                  out_dtypes=[F32], name=f"{tag}_mlp1_dx")
    dw1, recv_dw2 = _matmul(saved["h"], dpre, mode="tn", tm=_div(d, (1024, 512)), tn=_div(dff, (1024, 512)),
                            tk=tkt, out_dtypes=[BF16], name=f"{tag}_mlp1_dw", rider=_exchange_of([(dw2, 0)]))
    dx, dm_norm, dmix, dm_gate = _rms_mod_bwd(saved["x_in"], mods, dh, dxo, k_shift=3, n_x=n_x,
                                              name=f"{tag}_norm2_bwd", gate=(mixer_branch, mods, 2))
    return dx, dw1, dm_norm + dm_gate, dmix, recv_a, recv_b, recv_dw2


def kernel(x, c, ctx, c_ctx, l0_ada_w, l0_ada_b, l0_in_w, l0_conv_w, l0_conv_b, l0_conv_ln_g, l0_conv_ln_b, l0_pool_w, l0_pool_scale, l0_out_w, l0_mlp_w1, l0_mlp_w2, l1_ada_w, l1_ada_b, l1_qkv_w, l1_q_norm_g, l1_k_norm_g, l1_out_w, l1_mlp_w1, l1_mlp_w2, final_g, loss_target, m_c_ctx, m_l0_ada_w, m_l0_ada_b, m_l0_in_w, m_l0_conv_w, m_l0_conv_b, m_l0_conv_ln_g, m_l0_conv_ln_b, m_l0_pool_w, m_l0_pool_scale, m_l0_out_w, m_l0_mlp_w1, m_l0_mlp_w2, m_l1_ada_w, m_l1_ada_b, m_l1_qkv_w, m_l1_q_norm_g, m_l1_k_norm_g, m_l1_out_w, m_l1_mlp_w1, m_l1_mlp_w2, m_final_g, v_c_ctx, v_l0_ada_w, v_l0_ada_b, v_l0_in_w, v_l0_conv_w, v_l0_conv_b, v_l0_conv_ln_g, v_l0_conv_ln_b, v_l0_pool_w, v_l0_pool_scale, v_l0_out_w, v_l0_mlp_w1, v_l0_mlp_w2, v_l1_ada_w, v_l1_ada_b, v_l1_qkv_w, v_l1_q_norm_g, v_l1_k_norm_g, v_l1_out_w, v_l1_mlp_w1, v_l1_mlp_w2, v_final_g):
    p = dict(locals())
    me = _linear(_position())
    n_x, d = x.shape[1], x.shape[2]
    n_ctx = ctx.shape[1]
    t = n_x + n_ctx
    dc = l0_conv_b.shape[0]
    n_tap = l0_conv_w.shape[0]
    d_q = d
    n_mod = l0_ada_b.shape[0] // d
    ada_cols = l0_ada_w.shape[1]
    tm_t = _div(t, (768, 640, 512, 128))
    tm_x = _div(n_x, (1024, 512))

    names = list(SHARDED)
    shard16 = {nm: p[nm].astype(BF16) for nm in names}

    def gather_of(*nms):
        return Rider("gather", [shard16[nm] for nm in nms], [SHARDED[nm] for nm in nms])

    wfull = {}
    n_grp, pg = l0_pool_w.shape[0], l0_pool_w.shape[2]

    c_all = _all_gather([_row8(c)], [0], name="gather_cond")[0].reshape(N_DEV, 8, d)[:, 0]
    cond = jnp.concatenate([c_all, c_ctx.reshape(1, d), jnp.zeros((16 - N_DEV - 1, d), F32)], axis=0)
    s16 = _silu_rows(cond, name="cond_silu")
    mod_shards = []
    for li, (lw, lb) in enumerate(((l0_ada_w, l0_ada_b), (l1_ada_w, l1_ada_b))):
        bias = _row8(lax.dynamic_slice_in_dim(lb, me * ada_cols, ada_cols))
        mod_shards.append(_matmul(s16, lw, mode="nn", tm=16, tn=_div(ada_cols, (512, 384, 256, 128)), tk=d,
                                  out_dtypes=[F32], name=f"l{li}_ada_fwd", extras=[("vec", bias)],
                                  epilogue=lambda acc, rows, b: (acc + b[0:1],))[0])
    mods_full = _all_gather([jnp.concatenate(mod_shards, axis=0)], [1], name="gather_mods")[0]
    mods0, mods1 = _mods16(mods_full[:16], me), _mods16(mods_full[16:], me)

    xs0 = jnp.concatenate([x[0], ctx[0]], axis=0)
    first = Rider("gather", [shard16['l0_in_w'], jnp.pad(l0_conv_w, ((0, 1), (0, 0))), l0_pool_w.reshape(-1, pg)],
                  [SHARDED['l0_in_w'], 1, 0])
    h0, wfull['l0_in_w'], conv_w_full, pool_w_full = _rms_mod_fwd(
        xs0, mods0, k_shift=0, n_x=n_x, name="l0_norm1", rider=first)
    pool_w_full = pool_w_full.reshape(N_DEV, n_grp, pg // N_DEV, pg).transpose(1, 0, 2, 3).reshape(n_grp, pg, pg)
    z, wfull['l0_out_w'] = _matmul(h0, wfull['l0_in_w'], mode="nn", tm=tm_t, tn=_div(3 * dc, (1024, 768, 512, 384)),
                                   tk=d, out_dtypes=[F32], name="l0_in_proj", rider=gather_of('l0_out_w'))
    y0, cv, dsave, wfull['l0_mlp_w1'] = _mixer0_fwd(
        z, conv_w_full, _row8(l0_conv_b), _row8(l0_conv_ln_g), _row8(l0_conv_ln_b), pool_w_full,
        _row8(l0_pool_scale), n_tap=n_tap, n_x=n_x, name="l0_mixer", rider=gather_of('l0_mlp_w1'))
    xs1, mix0, wfull['l1_qkv_w'], wfull['l1_out_w'] = _matmul(
        y0, wfull['l0_out_w'], mode="nn", tm=tm_t, tn=_div(d, (1024, 512)), tk=2 * dc,
        out_dtypes=[F32, BF16], name="l0_out_proj", extras=[("tile", xs0), ("vec", mods0)],
        rider=gather_of('l1_qkv_w', 'l1_out_w'),
        epilogue=lambda acc, rows, res, mv: (res + _seg_pick(mv, 2, rows, n_x) * acc, acc))
    xs2, mlp0, (wfull['l0_mlp_w2'],), (wfull['l1_mlp_w1'],) = _mlp_fwd(
        xs1, mods0, wfull['l0_mlp_w1'], None, n_x=n_x, tm=tm_t, tag="l0",
        rider1=gather_of('l0_mlp_w2'), rider2=gather_of('l1_mlp_w1'))

    h2 = _rms_mod_fwd(xs2, mods1, k_shift=0, n_x=n_x, name="l1_norm1")
    qkv, = _matmul(h2, wfull['l1_qkv_w'], mode="nn", tm=tm_t, tn=_div(l1_qkv_w.shape[1] * N_DEV, (1024, 768, 512)),
                   tk=d, out_dtypes=[F32], name="l1_qkv_proj")
    cos, sin = _rope_tables(n_x, n_ctx)
    gq, gk = _row8(l1_q_norm_g), _row8(l1_k_norm_g)
    q, k, v = _qk_prep_fwd(qkv, gq, gk, cos, sin, d_q=d_q, name="l1_qk_prep")
    o, lse = _attention_fwd(q, k, v.T, n_x=n_x, name="l1_attention")
    x3, mix1 = _matmul(o, wfull['l1_out_w'], mode="nn", tm=tm_x, tn=_div(d, (1024, 512)), tk=d_q,
                       out_dtypes=[F32, BF16], name="l1_out_proj", extras=[("tile", xs2), ("vec", mods1)],
                       epilogue=lambda acc, rows, res, mv: (res + mv[4:5] * acc, acc))
    x4, mlp1, (wfull['l1_mlp_w2'],), _ = _mlp_fwd(x3, mods1, wfull['l1_mlp_w1'], None, n_x=n_x, tm=tm_x, tag="l1",
                                                  rider1=gather_of('l1_mlp_w2'))

    dx4, loss_part, dfinal_g, dbranch1, dmods1 = _final_loss(
        x4, _row8(final_g), loss_target[0], mlp1["branch"], mods1, k_gate=5, name="loss_head")
    loss = lax.psum(loss_part[0, 0], MESH_AXES)

    recv = {}
    dx3, dw1_1, dm, dmix1, _, _, recv['l1_mlp_w2'] = _mlp_bwd(
        dx4, dbranch1, mlp1, mods1, wfull['l1_mlp_w1'], wfull['l1_mlp_w2'], mix1, n_x=n_x, tm=tm_x, tag="l1")
    dmods1 = dmods1 + dm
    do, = _matmul(dmix1, wfull['l1_out_w'], mode="nt", tm=tm_x, tn=_div(d_q, (1024, 512)), tk=d,
                  out_dtypes=[BF16], name="l1_out_dx")
    dw_out1, = _matmul(o, dmix1, mode="tn", tm=_div(d_q, (1024, 512)), tn=_div(d, (1024, 512)),
                       tk=_div(n_x, TOKEN_TK), out_dtypes=[BF16], name="l1_out_dw")
    dq, dk_t, dv_t = _attention_bwd(q, k, v, o, lse, do, n_x=n_x, name="l1_attention_bwd")
    dk, dv = dk_t.T, dv_t.T
    dqkv, dgq, dgk = _qk_prep_bwd(qkv, dq, dk, dv, gq, gk, cos, sin, n_x=n_x, name="l1_qk_prep_bwd")
    tkt = _div(t, TOKEN_TK)
    dh2, recv['l1_out_w'] = _matmul(dqkv, wfull['l1_qkv_w'], mode="nt", tm=tm_t, tn=_div(d, (1024, 512)),
                                    tk=dqkv.shape[1], out_dtypes=[F32], name="l1_qkv_dx",
                                    rider=_exchange_of([(dw_out1, SHARDED['l1_out_w'])]))
    dw_qkv, = _matmul(h2, dqkv, mode="tn", tm=_div(d, (1024, 512)), tn=_div(dqkv.shape[1], (1024, 768, 512)),
                      tk=tkt, out_dtypes=[BF16], name="l1_qkv_dw")
    dxs2, dm, dbranch0, dmods0 = _rms_mod_bwd(xs2, mods1, dh2, dx3, k_shift=0, n_x=n_x, name="l1_norm1_bwd",
                                              dres_rows=n_x, gate=(mlp0["branch"], mods0, 5))
    dmods1 = dmods1 + dm

    dxs1, dw1_0, dm, dmix0, (recv['l1_qkv_w'],), (recv['l1_mlp_w1'],), recv['l0_mlp_w2'] = _mlp_bwd(
        dxs2, dbranch0, mlp0, mods0, wfull['l0_mlp_w1'], wfull['l0_mlp_w2'], mix0, n_x=n_x, tm=tm_t, tag="l0",
        ride_dx2=[(dw_qkv, SHARDED['l1_qkv_w'])], ride_dw2=[(dw1_1, SHARDED['l1_mlp_w1'])])
    dmods0 = dmods0 + dm
    dy0, = _matmul(dmix0, wfull['l0_out_w'], mode="nt", tm=tm_t, tn=_div(2 * dc, (1024, 512)), tk=d,
                   out_dtypes=[F32], name="l0_out_dx")
    dw_out0, = _matmul(y0, dmix0, mode="tn", tm=_div(2 * dc, (1024, 512)), tn=_div(d, (1024, 512)),
                       tk=tkt, out_dtypes=[BF16], name="l0_out_dw")
    dz, dconv_w, dconv_b, dln_g, dln_b, dpool_w, dpool_scale, recv['l0_mlp_w1'] = _mixer0_bwd(
        dy0, z, cv, dsave, conv_w_full, _row8(l0_conv_ln_g), _row8(l0_conv_ln_b), pool_w_full,
        _row8(l0_pool_scale), n_tap=n_tap, n_x=n_x, name="l0_mixer_bwd",
        rider=Rider("exchange", [dw1_0], [SHARDED['l0_mlp_w1']]))
    dw_in0, recv['l0_out_w'] = _matmul(h0, dz, mode="tn", tm=_div(d, (1024, 512)),
                                       tn=_div(3 * dc, (1024, 768, 512, 384)), tk=tkt, out_dtypes=[BF16],
                                       name="l0_in_dw", rider=_exchange_of([(dw_out0, SHARDED['l0_out_w'])]))
    dh0, recv['l0_in_w'] = _matmul(dz, wfull['l0_in_w'], mode="nt", tm=tm_t, tn=_div(d, (1024, 512)), tk=3 * dc,
                                   out_dtypes=[F32], name="l0_in_dx",
                                   rider=_exchange_of([(dw_in0, SHARDED['l0_in_w'])]))
    dx0, dm = _rms_mod_bwd(xs0, mods0, dh0, dxs1, k_shift=0, n_x=n_x, name="l0_norm1_bwd", out_rows=n_x)
    dmods0 = dmods0 + dm
    grad_x = dx0[None]

    def dmod_rows(dm16):
        rows = dm16[:2 * n_mod].reshape(n_mod, 2, d).transpose(1, 0, 2).reshape(2, n_mod * d)
        return jnp.pad(rows, ((0, 6), (0, 0)))
    dm_gathered = _all_gather([jnp.concatenate([dmod_rows(dmods0), dmod_rows(dmods1)], axis=0)], [0],
                              name="gather_dmods")[0]
    dm0, dm1, db0, db1 = _assemble_dmods(dm_gathered, name="assemble_dmods")
    out_g = {'l0_ada_b': db0[0], 'l1_ada_b': db1[0]}
    ds_part = jnp.zeros((16, d), F32)
    for nm, lw, dmf in (('l0_ada_w', l0_ada_w, dm0), ('l1_ada_w', l1_ada_w, dm1)):
        dm_cols = lax.dynamic_slice_in_dim(dmf, me * ada_cols, ada_cols, axis=1)
        out_g[nm], = _matmul(s16, dm_cols, mode="tn", tm=_div(d, (1024, 512)),
                             tn=_div(ada_cols, (512, 384, 256, 128)), tk=16, out_dtypes=[F32], name=f"{nm}_dw")
        ds_part = ds_part + _matmul(dm_cols, lw, mode="nt", tm=16, tn=_div(d, (1024, 512)),
                                    tk=_div(ada_cols, (512, 384, 256, 128)), out_dtypes=[F32], name=f"{nm}_dx")[0]

    small = {'l0_conv_b': dconv_b[0], 'l0_conv_ln_g': dln_g[0], 'l0_conv_ln_b': dln_b[0],
             'l0_pool_scale': dpool_scale[0], 'l1_q_norm_g': dgq[0], 'l1_k_norm_g': dgk[0],
             'final_g': dfinal_g[0], 'dsilu_ctx': ds_part[N_DEV], 'l0_conv_w': dconv_w[:-1].reshape(-1),
             'l0_pool_w': dpool_w.reshape(-1)}
    flat = jnp.concatenate([small[nm] for nm in small])
    rows = -(-flat.shape[0] // 1024) * 8
    packed = jnp.pad(flat, (0, rows * 128 - flat.shape[0])).reshape(rows, 128)
    summed = _sum_slots(_all_gather([packed], [0], name="gather_small_grads")[0], rows,
                        name="sum_small_grads").reshape(-1)
    off = 0
    for nm in small:
        size = small[nm].shape[0]
        small[nm] = summed[off:off + size]
        off += size
    out_g['c_ctx'] = _silu_grad(_row8(c_ctx), _row8(small['dsilu_ctx']), name="c_ctx_grad")[0]
    for nm in REPLICATED_SMALL:
        out_g[nm] = small[nm]
    conv_cols = l0_conv_w.shape[1]
    out_g['l0_conv_w'] = lax.dynamic_slice_in_dim(small['l0_conv_w'].reshape(n_tap, dc), me * conv_cols, conv_cols,
                                                  axis=1)
    out_g['l0_pool_w'] = lax.dynamic_slice_in_dim(small['l0_pool_w'].reshape(n_grp, pg, pg), me * (pg // N_DEV),
                                                  pg // N_DEV, axis=1)

    delta, new_m, new_v = {}, {}, {}
    for nm in names:
        out_g[nm], delta[nm], new_m[nm], new_v[nm] = _adamw(p[nm], p['m_' + nm], p['v_' + nm], recv=recv[nm],
                                                            name=f"adamw_{nm}")
    for nm in ('l0_ada_w', 'l1_ada_w'):
        out_g[nm], delta[nm], new_m[nm], new_v[nm] = _adamw(p[nm], p['m_' + nm], p['v_' + nm], grad=out_g[nm],
                                                            name=f"adamw_{nm}")
    for nm in WEIGHTS:
        if nm in delta:
            continue
        shape = p[nm].shape
        as2d = lambda a: a.reshape(1, -1) if a.ndim == 1 else a.reshape(-1, a.shape[-1])
        res = _adamw(as2d(p[nm]), as2d(p['m_' + nm]), as2d(p['v_' + nm]), grad=as2d(out_g[nm]), name=f"adamw_{nm}")
        out_g[nm], delta[nm], new_m[nm], new_v[nm] = [r.reshape(shape) for r in res]

    return (loss, grad_x, *[out_g[nm] for nm in WEIGHTS], *[delta[nm] for nm in WEIGHTS],
            *[new_m[nm] for nm in WEIGHTS], *[new_v[nm] for nm in WEIGHTS])
```

```python
import functools

import jax
import jax.numpy as jnp
from jax import lax
from jax.experimental import pallas as pl
from jax.experimental.pallas import tpu as pltpu

F32 = jnp.float32
BF16 = jnp.bfloat16
N_DEV = 8
MESH_AXES = ("x", "y", "c")
EPS = 1e-6
HEAD_DIM = 128
POOL_WINDOWS = (2, 4, 8, 16)
GRID_W = 64
ROPE_THETA = 10000.0
ATTN_SCALE = HEAD_DIM ** -0.5
LOG2_E = 1.4426950408889634
Q_SCALE_LOG2 = ATTN_SCALE * LOG2_E
HALO = 16
ADAM_LR, ADAM_B1, ADAM_B2, ADAM_EPS, ADAM_WD, ADAM_STEP = 0.001, 0.9, 0.999, 1e-08, 0.01, 10
VMEM_CAP_MB = 60
ADAMW_BLOCK_ELEMS = 1 << 18
TOKEN_TK = (2816, 2048, 1408, 1024, 768, 640, 512, 128)
DEEP_TM = (384, 512, 256, 128)
DEEP_TN = (512,)
WIDE_TN = (2048, 1024, 512)
WIDE_TILE_ELEMS = 768 * 2048
MESH_ID = pl.DeviceIdType.MESH


def _div(n, prefs):
    for p in prefs:
        if n % p == 0:
            return p
    raise ValueError(f"no tile for {n} in {prefs}")


def _params(sem, vmem_mb):
    return pltpu.CompilerParams(dimension_semantics=sem, vmem_limit_bytes=min(vmem_mb, VMEM_CAP_MB) << 20)


def _sigmoid(x):
    return 1.0 / (1.0 + jnp.exp(-x))


def _silu(x):
    return x * _sigmoid(x)


def _rms(x):
    return x * lax.rsqrt(jnp.mean(x * x, axis=-1, keepdims=True) + EPS)


def _rms_mod(x, shift, scale):
    return _rms(x) * (1.0 + scale) + shift


def _layernorm(x, g, b):
    mu = jnp.mean(x, axis=-1, keepdims=True)
    var = jnp.mean(jnp.square(x - mu), axis=-1, keepdims=True)
    return (x - mu) * lax.rsqrt(var + EPS) * g + b


def _ln_silu(x, g, b):
    return _silu(_layernorm(x, g, b))


def _matmul(a, b, *, mode, tm, tn, tk, out_dtypes, name, extras=(), epilogue=None, rider=None):
    if mode == "tn":
        kdim, m = a.shape
        n = b.shape[1]
    else:
        m, kdim = a.shape
        n = b.shape[0] if mode == "nt" else b.shape[1]
    assert m % tm == 0 and n % tn == 0 and kdim % tk == 0, (name, m, n, kdim, tm, tn, tk)
    nk = kdim // tk
    n_ex = len(extras)
    n_out = len(out_dtypes)

    def body(a_ref, b_ref, *rest):
        ex = rest[:n_ex]
        outs = rest[n_ex:n_ex + n_out]
        acc_ref = rest[n_ex + n_out] if nk > 1 else None
        k = pl.program_id(2)
        av = a_ref[...].astype(BF16)
        bv = b_ref[...].astype(BF16)
        dims = {"nn": ((1,), (0,)), "nt": ((1,), (1,)), "tn": ((0,), (0,))}[mode]
        part = lax.dot_general(av, bv, (dims, ((), ())), preferred_element_type=F32)

        rows = pl.program_id(0) * tm + lax.broadcasted_iota(jnp.int32, (tm, 1), 0)

        def finish(acc):
            res = (acc,) if epilogue is None else epilogue(acc, rows, *[e[...] for e in ex])
            for o, r in zip(outs, res):
                o[...] = r.astype(o.dtype)

        if nk == 1:
            finish(part)
        else:
            @pl.when(k == 0)
            def _():
                acc_ref[...] = part

            @pl.when(k > 0)
            def _():
                acc_ref[...] += part

            @pl.when(k == nk - 1)
            def _():
                finish(acc_ref[...])

    if mode == "tn":
        a_spec = pl.BlockSpec((tk, tm), lambda i, j, k: (k, i))
        b_spec = pl.BlockSpec((tk, tn), lambda i, j, k: (k, j))
    else:
        a_spec = pl.BlockSpec((tm, tk), lambda i, j, k: (i, k))
        b_spec = (pl.BlockSpec((tn, tk), lambda i, j, k: (j, k)) if mode == "nt"
                  else pl.BlockSpec((tk, tn), lambda i, j, k: (k, j)))
    ex_specs, ex_arrays, ex_bytes = [], [], 0
    for kind, arr in extras:
        ex_arrays.append(arr)
        if kind == "tile":
            ex_specs.append(pl.BlockSpec((tm, tn), lambda i, j, k: (i, j)))
            ex_bytes += tm * tn * arr.dtype.itemsize
        else:
            ex_specs.append(pl.BlockSpec((arr.shape[0], tn), lambda i, j, k: (0, j)))
            ex_bytes += arr.shape[0] * tn * 4
    blocks = tm * tk * a.dtype.itemsize + tk * tn * b.dtype.itemsize + ex_bytes
    blocks += sum(tm * tn * jnp.dtype(d).itemsize for d in out_dtypes)
    casts = sum(rows * cols * 2 for arr, rows, cols in ((a, tm, tk), (b, tk, tn)) if arr.dtype != BF16)
    vmem = (2 * blocks + 4 * tm * tn * 4 + casts) // (1 << 20) + 8
    return _pcall(
        body,
        grid=(m // tm, n // tn, nk),
        in_specs=[a_spec, b_spec] + ex_specs,
        out_specs=[pl.BlockSpec((tm, tn), lambda i, j, k: (i, j)) for _ in out_dtypes],
        out_shape=[jax.ShapeDtypeStruct((m, n), d) for d in out_dtypes],
        scratch_shapes=[pltpu.VMEM((tm, tn), F32)] if nk > 1 else [],
        semantics=("parallel", "parallel", "arbitrary"), vmem_mb=vmem, name=name,
        operands=[a, b, *ex_arrays], rider=rider)


def _seg_pick(vec, k, rows, n_x):
    return jnp.where(rows < n_x, vec[2 * k:2 * k + 1], vec[2 * k + 1:2 * k + 2])


def _zero_accs(i, accs):
    @pl.when(i == 0)
    def _():
        for a in accs:
            a[...] = jnp.zeros_like(a)


def _rms_mod_fwd(xs, mods, *, k_shift, n_x, name, rider=None):
    t, d = xs.shape
    tm = _div(t, (256, 128))

    def body(x_ref, mods_ref, h_ref):
        seg = (pl.program_id(0) * tm >= n_x).astype(jnp.int32)
        shift = mods_ref[pl.ds(2 * k_shift + seg, 1), :]
        scale = mods_ref[pl.ds(2 * k_shift + 2 + seg, 1), :]
        h_ref[...] = _rms_mod(x_ref[...], shift, scale).astype(BF16)

    res = _pcall(
        body, grid=(t // tm,),
        in_specs=[pl.BlockSpec((tm, d), lambda i: (i, 0)), pl.BlockSpec((16, d), lambda i: (0, 0))],
        out_specs=[pl.BlockSpec((tm, d), lambda i: (i, 0))],
        out_shape=[jax.ShapeDtypeStruct((t, d), BF16)],
        scratch_shapes=[], semantics=("parallel",), vmem_mb=32, name=name, operands=[xs, mods], rider=rider)
    return res[0] if rider is None else res


def _gate_part(dxv, seg, k_gate, br_ref, gmods_ref, db_ref, dgm_ref):
    r_gate = 2 * k_gate + seg
    db_ref[...] = (dxv * gmods_ref[pl.ds(r_gate, 1), :]).astype(BF16)
    dgm_ref[pl.ds(r_gate, 1), :] += jnp.sum(dxv * br_ref[...].astype(F32), axis=0, keepdims=True)


def _rms_mod_bwd(xs, mods, dh, dres, *, k_shift, n_x, name, rider=None, gate=None, dres_rows=None, out_rows=None):
    t, d = xs.shape
    tm = _div(t, (256, 128))
    n_gate = 2 if gate is not None else 0

    def body(x_ref, mods_ref, dh_ref, dres_ref, *rest):
        gate_in, (dx_ref, dmods_ref), gate_out = rest[:n_gate], rest[n_gate:n_gate + 2], rest[n_gate + 2:]
        i = pl.program_id(0)
        _zero_accs(i, [dmods_ref, *gate_out[1:]])
        seg = (i * tm >= n_x).astype(jnp.int32)
        r_shift = 2 * k_shift + seg
        r_scale = 2 * k_shift + 2 + seg
        shift = mods_ref[pl.ds(r_shift, 1), :]
        scale = mods_ref[pl.ds(r_scale, 1), :]
        _, vjp = jax.vjp(_rms_mod, x_ref[...], shift, scale)
        dx, dshift, dscale = vjp(dh_ref[...].astype(F32))
        dres_v = dres_ref[...]
        if dres_rows is not None:
            dres_v = dres_v * jnp.where(i * tm < dres_rows, 1.0, 0.0)
        dx = dres_v + dx
        if out_rows is None:
            dx_ref[...] = dx
        else:
            @pl.when(i * tm < out_rows)
            def _():
                dx_ref[...] = dx
        dmods_ref[pl.ds(r_shift, 1), :] += dshift
        dmods_ref[pl.ds(r_scale, 1), :] += dscale
        if gate is not None:
            _gate_part(dx, seg, gate[2], *gate_in, *gate_out)

    def clamped(rows):
        return pl.BlockSpec((tm, d), lambda i: (jnp.minimum(i, rows // tm - 1), 0))

    row = pl.BlockSpec((tm, d), lambda i: (i, 0))
    vec = pl.BlockSpec((16, d), lambda i: (0, 0))
    in_specs = [row, vec, row, row if dres_rows is None else clamped(dres_rows)]
    out_specs = [row if out_rows is None else clamped(out_rows), vec]
    out_shape = [jax.ShapeDtypeStruct((t if out_rows is None else out_rows, d), F32),
                 jax.ShapeDtypeStruct((16, d), F32)]
    operands = [xs, mods, dh, dres]
    if gate is not None:
        in_specs += [row, vec]
        out_specs += [row, vec]
        out_shape += [jax.ShapeDtypeStruct((t, d), BF16), jax.ShapeDtypeStruct((16, d), F32)]
        operands += [gate[0], gate[1]]
    return _pcall(body, grid=(t // tm,), in_specs=in_specs, out_specs=out_specs, out_shape=out_shape,
                  scratch_shapes=[], semantics=("arbitrary",), vmem_mb=48, name=name, operands=operands, rider=rider)


def _final_loss(xs, g, target, branch, gmods, *, k_gate, name):
    t, d = xs.shape
    tm = _div(t, (256, 128))

    def loss_fn(x, gv, tgt):
        err = _rms(x) * gv - tgt
        return 0.5 * jnp.sum(jnp.mean(jnp.square(err), axis=-1))

    def body(x_ref, g_ref, t_ref, br_ref, gmods_ref, dx_ref, loss_ref, dg_ref, db_ref, dgm_ref):
        i = pl.program_id(0)
        _zero_accs(i, [loss_ref, dg_ref, dgm_ref])
        val, vjp = jax.vjp(loss_fn, x_ref[...], g_ref[0:1, :], t_ref[...])
        dx, dg, _ = vjp(jnp.ones((), F32))
        dx_ref[...] = dx
        loss_ref[...] += val
        dg_ref[0:1, :] += dg
        _gate_part(dx, 0, k_gate, br_ref, gmods_ref, db_ref, dgm_ref)

    row = pl.BlockSpec((tm, d), lambda i: (i, 0))
    vec16 = pl.BlockSpec((16, d), lambda i: (0, 0))
    return pl.pallas_call(
        body, grid=(t // tm,),
        in_specs=[row, pl.BlockSpec((8, d), lambda i: (0, 0)), row, row, vec16],
        out_specs=[row, pl.BlockSpec((8, 128), lambda i: (0, 0)), pl.BlockSpec((8, d), lambda i: (0, 0)), row, vec16],
        out_shape=[jax.ShapeDtypeStruct((t, d), F32), jax.ShapeDtypeStruct((8, 128), F32),
                   jax.ShapeDtypeStruct((8, d), F32), jax.ShapeDtypeStruct((t, d), BF16),
                   jax.ShapeDtypeStruct((16, d), F32)],
        compiler_params=_params(("arbitrary",), 48), name=name,
    )(xs, g, target, branch, gmods)


def _halo_specs(r, width, col, t):
    h_per = r // HALO

    def prev(i):
        return (jnp.maximum(i * h_per - 1, 0), col)

    def nxt(i):
        return (jnp.minimum((i + 1) * h_per, t // HALO - 1), col)

    return (pl.BlockSpec((HALO, width), prev), pl.BlockSpec((r, width), lambda i: (i, col)),
            pl.BlockSpec((HALO, width), nxt))


def _seg_geometry(i, r, n_x, t):
    row0 = i * r
    in_ctx = row0 >= n_x
    first = jnp.logical_or(row0 == 0, row0 == n_x)
    last = jnp.logical_or(row0 + r == n_x, row0 + r == t)
    seg_start = jnp.where(in_ctx, n_x, 0)
    seg_len = jnp.where(in_ctx, t - n_x, n_x)
    return row0, first, last, seg_start, seg_len


ROW_CHUNK = 64
LANE_CHUNK = 128


def _chunks(width, rows, col0=0):
    lanes = min(LANE_CHUNK, width)
    return [(slice(col0 + c, col0 + c + lanes), r0) for c in range(0, width, lanes) for r0 in range(0, rows, ROW_CHUNK)]


def _pool_count(tpos, w, seg_len):
    return (jnp.minimum(tpos + w // 2, seg_len) - jnp.maximum(tpos - w // 2, 0)).astype(F32)


def _mixer0_fwd(z, conv_w, conv_b, ln_g, ln_b, pool_w, pool_scale, *, n_tap, n_x, name, rider=None):
    t, dc = z.shape[0], z.shape[1] // 3
    n_grp, pg = pool_w.shape[0], pool_w.shape[1]
    r = _div(t - n_x, (256, 128))
    assert n_x % r == 0 and pg * n_grp == dc
    half = n_tap // 2
    assert half < HALO and max(POOL_WINDOWS) // 2 <= HALO

    def body(ap, ac, an, gp, gc, gn, pp, pc, pn, w_ref, cb_ref, lg_ref, lb_ref, pw_ref, ps_ref,
             y_ref, cv_ref, d_ref, uwin, pwin):
        i = pl.program_id(0)
        row0, first, last, seg_start, seg_len = _seg_geometry(i, r, n_x, t)
        keep_prev = jnp.where(first, 0.0, 1.0)
        keep_next = jnp.where(last, 0.0, 1.0)
        uwin[0:HALO, :] = ap[...] * _sigmoid(gp[...]) * keep_prev
        uwin[HALO:HALO + r, :] = ac[...] * _sigmoid(gc[...])
        uwin[HALO + r:, :] = an[...] * _sigmoid(gn[...]) * keep_next
        pwin[0:HALO, :] = pp[...] * keep_prev
        pwin[HALO:HALO + r, :] = pc[...]
        pwin[HALO + r:, :] = pn[...] * keep_next
        for cols, r0 in _chunks(dc, r):
            acc = jnp.zeros((ROW_CHUNK, cols.stop - cols.start), F32) + cb_ref[0:1, cols]
            for k in range(n_tap):
                off = HALO - half + k + r0
                acc = acc + w_ref[k:k + 1, cols] * uwin[off:off + ROW_CHUNK, cols]
            cv_ref[r0:r0 + ROW_CHUNK, cols] = acc
        y_ref[:, 0:dc] = _ln_silu(cv_ref[...], lg_ref[0:1, :], lb_ref[0:1, :]).astype(BF16)
        tpos = row0 - seg_start + lax.broadcasted_iota(jnp.int32, (r, 1), 0)
        for g, w in enumerate(POOL_WINDOWS):
            cnt = _pool_count(tpos, w, seg_len)
            for cols, r0 in _chunks(pg, r, g * pg):
                s = jnp.zeros((ROW_CHUNK, cols.stop - cols.start), F32)
                for j in range(-(w // 2), w // 2):
                    s = s + pwin[HALO + j + r0:HALO + j + r0 + ROW_CHUNK, cols]
                diff = s / cnt[r0:r0 + ROW_CHUNK] - pwin[HALO + r0:HALO + r0 + ROW_CHUNK, cols]
                d_ref[r0:r0 + ROW_CHUNK, cols] = diff.astype(BF16)
            cols = slice(g * pg, (g + 1) * pg)
            pm = jnp.dot(d_ref[:, cols], pw_ref[g].astype(BF16), preferred_element_type=F32)
            y_ref[:, dc + g * pg:dc + (g + 1) * pg] = (pm * ps_ref[0:1, cols]).astype(BF16)

    vec = lambda rows, width: pl.BlockSpec((rows, width), lambda i: (0, 0))
    in_specs = [*_halo_specs(r, dc, 0, t), *_halo_specs(r, dc, 1, t), *_halo_specs(r, dc, 2, t),
                vec(conv_w.shape[0], dc), vec(8, dc), vec(8, dc), vec(8, dc),
                pl.BlockSpec((n_grp, pg, pg), lambda i: (0, 0, 0)), vec(8, dc)]
    return _pcall(
        body, grid=(t // r,), in_specs=in_specs,
        out_specs=[pl.BlockSpec((r, 2 * dc), lambda i: (i, 0)), pl.BlockSpec((r, dc), lambda i: (i, 0)),
                   pl.BlockSpec((r, dc), lambda i: (i, 0))],
        out_shape=[jax.ShapeDtypeStruct((t, 2 * dc), BF16), jax.ShapeDtypeStruct((t, dc), F32),
                   jax.ShapeDtypeStruct((t, dc), BF16)],
        scratch_shapes=[pltpu.VMEM((r + 2 * HALO, dc), F32), pltpu.VMEM((r + 2 * HALO, dc), F32)],
        semantics=("parallel",), vmem_mb=40, name=name,
        operands=[*([z] * 9), conv_w, conv_b, ln_g, ln_b, pool_w, pool_scale], rider=rider)


def _mixer0_bwd(dy, z, cv, dsave, conv_w, ln_g, ln_b, pool_w, pool_scale, *, n_tap, n_x, name, rider=None):
    t, dc = cv.shape
    n_grp, pg = pool_w.shape[0], pool_w.shape[1]
    r = _div(t - n_x, (256, 128))
    half = n_tap // 2
    dyp_, zp_, cvp_ = dy, z, cv
    rw = r + 2 * HALO

    def body(dcp, dcc, dcn, dpp, dpc, dpn, cvp, cvc, cvn, ap, ac, an, gp, gc, gn, d_ref,
             w_ref, lg_ref, lb_ref, pw_ref, ps_ref,
             dz_ref, dw_ref, dcb_ref, dlg_ref, dlb_ref, dpw_ref, dps_ref, uwin, dcvwin, ewin, ddwin):
        i = pl.program_id(0)
        _zero_accs(i, [dw_ref, dcb_ref, dlg_ref, dlb_ref, dpw_ref, dps_ref])
        row0, first, last, seg_start, seg_len = _seg_geometry(i, r, n_x, t)
        keep_prev = jnp.where(first, 0.0, 1.0)
        keep_next = jnp.where(last, 0.0, 1.0)
        lg, lb = lg_ref[0:1, :], lb_ref[0:1, :]
        _, vjp = jax.vjp(_ln_silu, cvc[...], lg, lb)
        dcv, dlg, dlb = vjp(dcc[...])
        dlg_ref[0:1, :] += dlg
        dlb_ref[0:1, :] += dlb
        dcb_ref[0:1, :] += jnp.sum(dcv, axis=0, keepdims=True)
        dcvwin[HALO:HALO + r, :] = dcv
        for halo_cv, halo_dy, keep, lo in ((cvp, dcp, keep_prev, 0), (cvn, dcn, keep_next, HALO + r)):
            _, vjp_h = jax.vjp(lambda v: _ln_silu(v, lg, lb), halo_cv[...])
            dcvwin[lo:lo + HALO, :] = vjp_h(halo_dy[...])[0] * keep
        uwin[0:HALO, :] = ap[...] * _sigmoid(gp[...]) * keep_prev
        uwin[HALO:HALO + r, :] = ac[...] * _sigmoid(gc[...])
        uwin[HALO + r:, :] = an[...] * _sigmoid(gn[...]) * keep_next
        for cols, r0 in _chunks(dc, r):
            du = jnp.zeros((ROW_CHUNK, cols.stop - cols.start), F32)
            for k in range(n_tap):
                off = HALO + half - k + r0
                du = du + w_ref[k:k + 1, cols] * dcvwin[off:off + ROW_CHUNK, cols]
            rows = slice(r0, r0 + ROW_CHUNK)
            sig = _sigmoid(gc[rows, cols])
            dz_ref[rows, cols] = (du * sig).astype(BF16)
            dz_ref[rows, dc + cols.start:dc + cols.stop] = (du * ac[rows, cols] * sig * (1.0 - sig)).astype(BF16)
        for c0 in range(0, dc, LANE_CHUNK):
            cols = slice(c0, c0 + LANE_CHUNK)
            taps = [jnp.zeros((8, LANE_CHUNK), F32) for _ in range(n_tap)]
            for r0 in range(0, r, ROW_CHUNK):
                dcv_c = dcvwin[HALO + r0:HALO + r0 + ROW_CHUNK, cols]
                for k in range(n_tap):
                    off = HALO - half + k + r0
                    prod = dcv_c * uwin[off:off + ROW_CHUNK, cols]
                    taps[k] = taps[k] + functools.reduce(
                        jnp.add, [prod[8 * s:8 * s + 8] for s in range(ROW_CHUNK // 8)])
            for k in range(n_tap):
                dw_ref[k:k + 1, cols] += jnp.sum(taps[k], axis=0, keepdims=True)
        twin = row0 - seg_start - HALO + lax.broadcasted_iota(jnp.int32, (rw, 1), 0)
        for g, w in enumerate(POOL_WINDOWS):
            cols = slice(g * pg, (g + 1) * pg)
            wg = pw_ref[g].astype(BF16)
            scale = ps_ref[0:1, cols]
            dyp_c = dpc[:, cols]
            dpm_win = jnp.concatenate([dpp[:, cols] * keep_prev, dyp_c, dpn[:, cols] * keep_next], axis=0) * scale
            dd_win = lax.dot_general(dpm_win.astype(BF16), wg, (((1,), (1,)), ((), ())), preferred_element_type=F32)
            cnt = jnp.maximum(_pool_count(twin, w, seg_len), 1.0)
            ddwin[:, cols] = dd_win
            ewin[:, cols] = dd_win / cnt
            for ccols, r0 in _chunks(pg, r, g * pg):
                dup = -ddwin[HALO + r0:HALO + r0 + ROW_CHUNK, ccols]
                for j in range(-(w // 2) + 1, w // 2 + 1):
                    dup = dup + ewin[HALO + j + r0:HALO + j + r0 + ROW_CHUNK, ccols]
                dz_ref[r0:r0 + ROW_CHUNK, 2 * dc + ccols.start:2 * dc + ccols.stop] = dup.astype(BF16)
            dsv = d_ref[:, cols]
            pm = jnp.dot(dsv, wg, preferred_element_type=F32)
            dps_ref[0:1, cols] += jnp.sum(dyp_c * pm, axis=0, keepdims=True)
            dpw_ref[g] += lax.dot_general(dsv, (dyp_c * scale).astype(BF16), (((0,), (0,)), ((), ())),
                                          preferred_element_type=F32)

    vec = lambda rows, width: pl.BlockSpec((rows, width), lambda i: (0, 0))
    grp = pl.BlockSpec((n_grp, pg, pg), lambda i: (0, 0, 0))
    in_specs = [*_halo_specs(r, dc, 0, t), *_halo_specs(r, dc, 1, t), *_halo_specs(r, dc, 0, t),
                *_halo_specs(r, dc, 0, t), *_halo_specs(r, dc, 1, t), pl.BlockSpec((r, dc), lambda i: (i, 0)),
                vec(conv_w.shape[0], dc), vec(8, dc), vec(8, dc), grp, vec(8, dc)]
    return _pcall(
        body, grid=(t // r,), in_specs=in_specs,
        out_specs=[pl.BlockSpec((r, 3 * dc), lambda i: (i, 0)), vec(conv_w.shape[0], dc), vec(8, dc), vec(8, dc),
                   vec(8, dc), grp, vec(8, dc)],
        out_shape=[jax.ShapeDtypeStruct((t, 3 * dc), BF16), jax.ShapeDtypeStruct(conv_w.shape, F32),
                   jax.ShapeDtypeStruct((8, dc), F32), jax.ShapeDtypeStruct((8, dc), F32),
                   jax.ShapeDtypeStruct((8, dc), F32), jax.ShapeDtypeStruct(pool_w.shape, F32),
                   jax.ShapeDtypeStruct((8, dc), F32)],
        scratch_shapes=[pltpu.VMEM((rw, dc), F32)] * 4,
        semantics=("arbitrary",), vmem_mb=VMEM_CAP_MB, name=name,
        operands=[dyp_, dyp_, dyp_, dyp_, dyp_, dyp_, cvp_, cvp_, cvp_, zp_, zp_, zp_, zp_, zp_, zp_, dsave,
                  conv_w, ln_g, ln_b, pool_w, pool_scale], rider=rider)


def _swap_halves(x):
    lane = lax.broadcasted_iota(jnp.int32, x.shape, 1)
    quarter = HEAD_DIM // 4
    return jnp.where(lane % (2 * quarter) < quarter,
                     pltpu.roll(x, HEAD_DIM - quarter, 1), pltpu.roll(x, quarter, 1))


def _rope_tables(n_x, n_ctx):
    half = HEAD_DIM // 4
    freqs = ROPE_THETA ** (-jnp.arange(half, dtype=F32) / half)
    tok = jnp.arange(n_x)
    row = (tok // GRID_W).astype(F32)[:, None] * freqs[None, :]
    col = (tok % GRID_W).astype(F32)[:, None] * freqs[None, :]
    cos = jnp.concatenate([jnp.cos(row), jnp.cos(row), jnp.cos(col), jnp.cos(col)], axis=1)
    sin = jnp.concatenate([-jnp.sin(row), jnp.sin(row), -jnp.sin(col), jnp.sin(col)], axis=1)
    cos = jnp.concatenate([cos, jnp.ones((n_ctx, HEAD_DIM), F32)], axis=0)
    sin = jnp.concatenate([sin, jnp.zeros((n_ctx, HEAD_DIM), F32)], axis=0)
    return cos, sin


def _norm_g(x, g):
    return _rms(x) * g


def _qk_prep_fwd(qkv, gq, gk, cos, sin, *, d_q, name):
    t, width = qkv.shape
    d_kv = (width - d_q) // 2
    tm = _div(t, (256, 128))

    def body(qkv_ref, gq_ref, gk_ref, cos_ref, sin_ref, q_ref, k_ref, v_ref):
        cs, sn = cos_ref[...], sin_ref[...]
        for h in range((d_q + d_kv) // HEAD_DIM):
            g = gq_ref[0:1, :] if h * HEAD_DIM < d_q else gk_ref[0:1, :]
            xn = _norm_g(qkv_ref[:, h * HEAD_DIM:(h + 1) * HEAD_DIM], g)
            rot = xn * cs + _swap_halves(xn) * sn
            if h * HEAD_DIM < d_q:
                q_ref[:, h * HEAD_DIM:(h + 1) * HEAD_DIM] = (rot * Q_SCALE_LOG2).astype(BF16)
            else:
                k_ref[:, h * HEAD_DIM - d_q:(h + 1) * HEAD_DIM - d_q] = rot.astype(BF16)
        v_ref[...] = qkv_ref[:, d_q + d_kv:].astype(BF16)

    row = lambda w: pl.BlockSpec((tm, w), lambda i: (i, 0))
    vec = pl.BlockSpec((8, HEAD_DIM), lambda i: (0, 0))
    return pl.pallas_call(
        body, grid=(t // tm,),
        in_specs=[row(width), vec, vec, row(HEAD_DIM), row(HEAD_DIM)],
        out_specs=[row(d_q), row(d_kv), row(d_kv)],
        out_shape=[jax.ShapeDtypeStruct((t, d_q), BF16), jax.ShapeDtypeStruct((t, d_kv), BF16),
                   jax.ShapeDtypeStruct((t, d_kv), BF16)],
        compiler_params=_params(("parallel",), 32), name=name,
    )(qkv, gq, gk, cos, sin)


def _qk_prep_bwd(qkv, dq, dk, dv, gq, gk, cos, sin, *, n_x, name):
    t, width = qkv.shape
    d_q, d_kv = dq.shape[1], dk.shape[1]
    tm = _div(t, (256, 128))
    last_q = n_x // tm - 1

    def body(qkv_ref, dq_ref, dk_ref, dv_ref, gq_ref, gk_ref, cos_ref, sin_ref, out_ref, dgq_ref, dgk_ref):
        i = pl.program_id(0)
        _zero_accs(i, [dgq_ref, dgk_ref])
        is_x = jnp.where(i * tm < n_x, 1.0, 0.0)
        cs, sn = cos_ref[...], sin_ref[...]
        for h in range((d_q + d_kv) // HEAD_DIM):
            sl = slice(h * HEAD_DIM, (h + 1) * HEAD_DIM)
            if h * HEAD_DIM < d_q:
                g, dg_ref, dr = gq_ref[0:1, :], dgq_ref, dq_ref[:, sl] * is_x
            else:
                g, dg_ref = gk_ref[0:1, :], dgk_ref
                dr = dk_ref[:, h * HEAD_DIM - d_q:(h + 1) * HEAD_DIM - d_q]
            dxn = dr * cs + _swap_halves(dr * sn)
            _, vjp = jax.vjp(_norm_g, qkv_ref[:, sl], g)
            dx, dg = vjp(dxn)
            out_ref[:, sl] = dx.astype(BF16)
            dg_ref[0:1, :] += dg
        out_ref[:, d_q + d_kv:] = dv_ref[...].astype(BF16)

    row = lambda w: pl.BlockSpec((tm, w), lambda i: (i, 0))
    vec = pl.BlockSpec((8, HEAD_DIM), lambda i: (0, 0))
    return pl.pallas_call(
        body, grid=(t // tm,),
        in_specs=[row(width), pl.BlockSpec((tm, d_q), lambda i: (jnp.minimum(i, last_q), 0)), row(d_kv), row(d_kv),
                  vec, vec, row(HEAD_DIM), row(HEAD_DIM)],
        out_specs=[row(width), vec, vec],
        out_shape=[jax.ShapeDtypeStruct((t, width), BF16), jax.ShapeDtypeStruct((8, HEAD_DIM), F32),
                   jax.ShapeDtypeStruct((8, HEAD_DIM), F32)],
        compiler_params=_params(("arbitrary",), 40), name=name,
    )(qkv, dq, dk, dv, gq, gk, cos, sin)


ATTN_TQ = (256, 128)
ATTN_TK = (768, 640, 512, 384, 256, 128)


def _attention_fwd(q, k, v_t, *, n_x, name):
    t, d_kv = k.shape
    d_q = q.shape[1]
    kvh = d_kv // HEAD_DIM
    grp = d_q // d_kv
    tq = _div(n_x, ATTN_TQ)
    tk = _div(t, ATTN_TK)
    gw = grp * HEAD_DIM
    n_kv = t // tk
    n_pair = (n_kv - 1) // 2

    def fold8(x):
        return functools.reduce(jnp.add, [x[8 * r:8 * r + 8] for r in range(tk // 8)])

    def body(q_ref, k_ref, vt_ref, o_ref, lse_ref, m_ref, l_ref, acc_ref, qt_ref, s_even, s_odd):
        m_ref[...] = jnp.full_like(m_ref, -jnp.inf)
        l_ref[...] = jnp.zeros_like(l_ref)
        acc_ref[...] = jnp.zeros_like(acc_ref)
        for g in range(grp):
            qt_ref[g] = q_ref[:, g * HEAD_DIM:(g + 1) * HEAD_DIM].T

        def keys(j):
            return pl.ds(pl.multiple_of(j * tk, tk), tk)

        def scores(g, kc):
            return jnp.dot(kc, qt_ref[g], preferred_element_type=F32)

        def chunk(j, s_cur, s_next):
            vt = vt_ref[:, keys(j)]
            kn = k_ref[keys(j + 1), :] if s_next is not None else None
            for g in range(grp):
                s = s_cur[g]
                m_old = m_ref[g]
                m_new = jnp.maximum(m_old, jnp.max(s, axis=0, keepdims=True))
                alpha = jnp.exp2(m_old - m_new)
                p = jnp.exp2(s - m_new)
                if s_next is not None:
                    s_next[g] = scores(g, kn)
                l_ref[g] = alpha * l_ref[g] + fold8(p)
                acc_ref[g] = alpha * acc_ref[g] + jnp.dot(vt, p.astype(BF16), preferred_element_type=F32)
                m_ref[g] = m_new

        k0 = k_ref[keys(0), :]
        for g in range(grp):
            s_even[g] = scores(g, k0)

        def pair(i, carry):
            chunk(2 * i, s_even, s_odd)
            chunk(2 * i + 1, s_odd, s_even)
            return carry

        lax.fori_loop(0, n_pair, pair, 0)
        if n_kv - 2 * n_pair == 2:
            chunk(n_kv - 2, s_even, s_odd)
            chunk(n_kv - 1, s_odd, None)
        else:
            chunk(n_kv - 1, s_even, None)
        for g in range(grp):
            l = jnp.sum(l_ref[g], axis=0, keepdims=True)
            o_ref[:, g * HEAD_DIM:(g + 1) * HEAD_DIM] = (acc_ref[g] / l).T.astype(BF16)
            lse_row = m_ref[g] + jnp.log(l) * LOG2_E
            lse_ref[:, g:g + 1] = jnp.broadcast_to(lse_row, (HEAD_DIM, tq)).T[:, 0:1]

    return pl.pallas_call(
        body, grid=(kvh, n_x // tq),
        in_specs=[pl.BlockSpec((tq, gw), lambda h, i: (i, h)),
                  pl.BlockSpec((t, HEAD_DIM), lambda h, i: (0, h)),
                  pl.BlockSpec((HEAD_DIM, t), lambda h, i: (h, 0))],
        out_specs=[pl.BlockSpec((tq, gw), lambda h, i: (i, h)),
                   pl.BlockSpec((None, tq, grp), lambda h, i: (h, i, 0))],
        out_shape=[jax.ShapeDtypeStruct((n_x, d_q), BF16), jax.ShapeDtypeStruct((kvh, n_x, grp), F32)],
        scratch_shapes=[pltpu.VMEM((grp, 1, tq), F32), pltpu.VMEM((grp, 8, tq), F32),
                        pltpu.VMEM((grp, HEAD_DIM, tq), F32), pltpu.VMEM((grp, HEAD_DIM, tq), BF16),
                        pltpu.VMEM((grp, tk, tq), F32), pltpu.VMEM((grp, tk, tq), F32)],
        compiler_params=_params(("parallel", "arbitrary"), 48), name=name,
    )(q, k, v_t)


def _attention_bwd(q, k, v, o, lse, do, *, n_x, name):
    t, d_kv = k.shape
    d_q = q.shape[1]
    kvh = d_kv // HEAD_DIM
    grp = d_q // d_kv
    tq = _div(n_x, ATTN_TQ)
    tk = _div(t, ATTN_TK)
    gw = grp * HEAD_DIM
    n_q = n_x // tq

    def body(q_ref, k_ref, v_ref, o_ref, lse_ref, do_ref, dq_ref, dkt_ref, dvt_ref, dq_acc, lse_s, delta_s, qt_s,
             dot_s):
        i = pl.program_id(1)
        _zero_accs(i, [dkt_ref, dvt_ref])
        dq_acc[...] = jnp.zeros_like(dq_acc)
        for g in range(grp):
            sl = slice(g * HEAD_DIM, (g + 1) * HEAD_DIM)
            lse_s[g] = lse_ref[:, g:g + 1]
            delta_s[g] = jnp.sum(do_ref[:, sl].astype(F32) * o_ref[:, sl].astype(F32), axis=-1, keepdims=True)
            qt_s[g] = q_ref[:, sl].T
            dot_s[g] = do_ref[:, sl].T

        def step(j, carry):
            start = pl.multiple_of(j * tk, tk)
            kc, vc = k_ref[pl.ds(start, tk), :], v_ref[pl.ds(start, tk), :]
            dkt_part = jnp.zeros((HEAD_DIM, tk), F32)
            dvt_part = jnp.zeros((HEAD_DIM, tk), F32)
            for g in range(grp):
                sl = slice(g * HEAD_DIM, (g + 1) * HEAD_DIM)
                s = lax.dot_general(q_ref[:, sl], kc, (((1,), (1,)), ((), ())), preferred_element_type=F32)
                p = jnp.exp2(s - lse_s[g])
                dp = lax.dot_general(do_ref[:, sl], vc, (((1,), (1,)), ((), ())), preferred_element_type=F32)
                ds = (p * (dp - delta_s[g])).astype(BF16)
                dq_acc[g] += jnp.dot(ds, kc, preferred_element_type=F32)
                dvt_part = dvt_part + jnp.dot(dot_s[g], p.astype(BF16), preferred_element_type=F32)
                dkt_part = dkt_part + jnp.dot(qt_s[g], ds, preferred_element_type=F32)
            dvt_ref[:, pl.ds(start, tk)] += dvt_part
            dkt_ref[:, pl.ds(start, tk)] += dkt_part
            return carry

        lax.fori_loop(0, t // tk, step, 0)
        for g in range(grp):
            dq_ref[:, g * HEAD_DIM:(g + 1) * HEAD_DIM] = dq_acc[g] * ATTN_SCALE

        @pl.when(i == n_q - 1)
        def _():
            dkt_ref[...] = dkt_ref[...] * (1.0 / LOG2_E)

    qspec = pl.BlockSpec((tq, gw), lambda h, i: (i, h))
    kspec = pl.BlockSpec((t, HEAD_DIM), lambda h, i: (0, h))
    ktspec = pl.BlockSpec((HEAD_DIM, t), lambda h, i: (h, 0))
    return pl.pallas_call(
        body, grid=(kvh, n_q),
        in_specs=[qspec, kspec, kspec, qspec, pl.BlockSpec((None, tq, grp), lambda h, i: (h, i, 0)), qspec],
        out_specs=[qspec, ktspec, ktspec],
        out_shape=[jax.ShapeDtypeStruct((n_x, d_q), F32), jax.ShapeDtypeStruct((d_kv, t), F32),
                   jax.ShapeDtypeStruct((d_kv, t), F32)],
        scratch_shapes=[pltpu.VMEM((grp, tq, HEAD_DIM), F32), pltpu.VMEM((grp, tq, 1), F32),
                        pltpu.VMEM((grp, tq, 1), F32), pltpu.VMEM((grp, HEAD_DIM, tq), BF16),
                        pltpu.VMEM((grp, HEAD_DIM, tq), BF16)],
        compiler_params=_params(("parallel", "arbitrary"), 56), name=name,
    )(q, k, v, o, lse, do)


def _whole(body, ins, out_shapes, name):
    return pl.pallas_call(
        body, out_shape=[jax.ShapeDtypeStruct(s, d) for s, d in out_shapes],
        compiler_params=pltpu.CompilerParams(vmem_limit_bytes=40 << 20), name=name)(*ins)


def _silu_rows(x, *, name):
    def body(x_ref, o_ref):
        o_ref[...] = _silu(x_ref[...])
    return _whole(body, [x], [(x.shape, F32)], name)[0]


def _assemble_dmods(gathered, *, name):
    width = gathered.shape[1]

    def body(g_ref, dm0, dm1, db0, db1):
        for l, (dm, db) in enumerate(((dm0, db0), (dm1, db1))):
            ctx = jnp.zeros((1, width), F32)
            tot = jnp.zeros((1, width), F32)
            for q in range(N_DEV):
                row = g_ref[16 * q + 8 * l:16 * q + 8 * l + 1, :]
                dm[q:q + 1, :] = row
                tot = tot + row
                ctx = ctx + g_ref[16 * q + 8 * l + 1:16 * q + 8 * l + 2, :]
            dm[N_DEV:N_DEV + 1, :] = ctx
            dm[N_DEV + 1:, :] = jnp.zeros((16 - N_DEV - 1, width), F32)
            db[...] = jnp.zeros_like(db)
            db[0:1, :] = tot + ctx

    return _whole(body, [gathered], [((16, width), F32), ((16, width), F32), ((8, width), F32), ((8, width), F32)],
                  name)


def _sum_slots(gathered, rows, *, name):
    def body(g_ref, o_ref):
        acc = g_ref[0:rows, :]
        for q in range(1, N_DEV):
            acc = acc + g_ref[q * rows:(q + 1) * rows, :]
        o_ref[...] = acc
    return _whole(body, [gathered], [((rows, gathered.shape[1]), F32)], name)[0]


def _silu_grad(x, dy, *, name):
    def body(x_ref, dy_ref, o_ref):
        _, vjp = jax.vjp(_silu, x_ref[...])
        o_ref[...] = vjp(dy_ref[...])[0]
    return _whole(body, [x, dy], [(x.shape, F32)], name)[0]


def _adamw(w, m, v, *, name, recv=None, grad=None):
    rows, cols = w.shape
    budget = max(8, ADAMW_BLOCK_ELEMS // cols)
    tr = _div(rows, [c for c in (512, 256, 128, 64, 32, 16, 8) if c <= budget] + [rows])
    c1 = 1.0 - ADAM_B1 ** ADAM_STEP
    c2 = 1.0 - ADAM_B2 ** ADAM_STEP

    def body(w_ref, m_ref, v_ref, g_in, g_ref, d_ref, nm_ref, nv_ref):
        if recv is not None:
            g = g_in[0].astype(F32)
            for q in range(1, N_DEV):
                g = g + g_in[q].astype(F32)
        else:
            g = g_in[...]
        nm = ADAM_B1 * m_ref[...] + (1.0 - ADAM_B1) * g
        nv = ADAM_B2 * v_ref[...] + (1.0 - ADAM_B2) * jnp.square(g)
        g_ref[...] = g
        nm_ref[...] = nm
        nv_ref[...] = nv
        d_ref[...] = -ADAM_LR * ((nm / c1) / (jnp.sqrt(nv / c2) + ADAM_EPS) + ADAM_WD * w_ref[...])

    blk = pl.BlockSpec((tr, cols), lambda i: (i, 0))
    g_spec = pl.BlockSpec((N_DEV, tr, cols), lambda i: (0, i, 0)) if recv is not None else blk
    return pl.pallas_call(
        body, grid=(rows // tr,), in_specs=[blk, blk, blk, g_spec], out_specs=[blk] * 4,
        out_shape=[jax.ShapeDtypeStruct((rows, cols), F32)] * 4,
        compiler_params=_params(("parallel",), 48), name=name,
    )(w, m, v, recv if recv is not None else grad)


def _position():
    return tuple(lax.axis_index(a) for a in MESH_AXES)


def _linear(pos):
    return 4 * pos[0] + 2 * pos[1] + pos[2]


def _window(ref, axis, dev, size):
    start = pl.multiple_of(dev * size, size)
    return ref.at[pl.ds(start, size), :] if axis == 0 else ref.at[:, pl.ds(start, size)]


def _all_gather(shards, axes, *, name):
    n = len(shards)
    sizes = [s.shape[ax] for s, ax in zip(shards, axes)]

    def body(*refs):
        src, dst = refs[:n], refs[n:2 * n]
        send_sems, recv_sems, local_sems = refs[2 * n:]
        x, y, c = _position()
        me, sibling = (x, y, c), (x, y, 1 - c)
        chips = [(1 - x, y), (x, 1 - y), (1 - x, 1 - y)]

        def win(k, pos):
            return _window(dst[k], axes[k], _linear(pos), sizes[k])

        def copy(k, sem, block, to, from_src=False):
            return pltpu.make_async_remote_copy(
                src_ref=src[k] if from_src else win(k, block), dst_ref=win(k, block),
                send_sem=send_sems.at[k, sem], recv_sem=recv_sems.at[k, sem],
                device_id=to, device_id_type=MESH_ID)

        mine = [pltpu.make_async_copy(src[k], win(k, me), local_sems.at[k]) for k in range(n)]
        for cp in mine:
            cp.start()
        first = []
        for k in range(n):
            first.append(copy(k, 0, me, sibling, from_src=True))
            first += [copy(k, 1 + j, me, (*chip, c), from_src=True) for j, chip in enumerate(chips)]
        for cp in first:
            cp.start()
        passed = []
        for j, chip in enumerate(chips):
            for k in range(n):
                copy(k, 1 + j, (*chip, c), me).wait_recv()
                fwd = copy(k, 4 + j, (*chip, c), sibling)
                fwd.start()
                passed.append(fwd)
        for k in range(n):
            copy(k, 0, sibling, me).wait_recv()
            for j, chip in enumerate(chips):
                copy(k, 4 + j, (*chip, 1 - c), me).wait_recv()
        for cp in first + passed:
            cp.wait_send()
        for cp in mine:
            cp.wait()

    out_shape = []
    for s, ax in zip(shards, axes):
        full = (s.shape[0] * N_DEV, s.shape[1]) if ax == 0 else (s.shape[0], s.shape[1] * N_DEV)
        out_shape.append(jax.ShapeDtypeStruct(full, s.dtype))
    any_spec = pl.BlockSpec(memory_space=pl.ANY)
    return pl.pallas_call(
        body, in_specs=[any_spec] * n, out_specs=[any_spec] * n, out_shape=out_shape,
        scratch_shapes=[pltpu.SemaphoreType.DMA((n, 7)), pltpu.SemaphoreType.DMA((n, 7)),
                        pltpu.SemaphoreType.DMA((n,))],
        name=name,
    )(*shards)


class Rider:
    def __init__(self, kind, arrays, axes):
        self.kind, self.arrays, self.axes = kind, list(arrays), list(axes)
        self.n = len(self.arrays)
        if kind == "gather":
            self.sizes = [a.shape[ax] for a, ax in zip(self.arrays, self.axes)]
        else:
            self.sizes = [a.shape[ax] // N_DEV for a, ax in zip(self.arrays, self.axes)]

    def out_shape(self):
        shapes = []
        for a, ax, sz in zip(self.arrays, self.axes, self.sizes):
            if self.kind == "gather":
                full = (sz * N_DEV, a.shape[1]) if ax == 0 else (a.shape[0], sz * N_DEV)
                shapes.append(jax.ShapeDtypeStruct(full, a.dtype))
            else:
                shard = (sz, a.shape[1]) if ax == 0 else (a.shape[0], sz)
                shapes.append(jax.ShapeDtypeStruct((N_DEV, *shard), a.dtype))
        return shapes

    def scratch(self):
        return [pltpu.SemaphoreType.DMA((self.n, N_DEV - 1)), pltpu.SemaphoreType.DMA((self.n, N_DEV - 1)),
                pltpu.SemaphoreType.DMA((self.n,))]

    def plan(self, src, dst, send_sems, recv_sems, local_sems):
        x, y, c = me = _position()
        mine = _linear(me)

        def remote(k, sem, src_ref, dst_ref, to):
            return pltpu.make_async_remote_copy(
                src_ref=src_ref, dst_ref=dst_ref, send_sem=send_sems.at[k, sem], recv_sem=recv_sems.at[k, sem],
                device_id=to, device_id_type=MESH_ID)

        ph = dict(local=[], start=[], mid_wait=[], mid_start=[], end_wait=[])
        for k in range(self.n):
            ax, sz = self.axes[k], self.sizes[k]
            if self.kind == "exchange":
                own_src, own_dst = _window(src[k], ax, mine, sz), dst[k].at[mine]
                ph["local"].append(pltpu.make_async_copy(own_src, own_dst, local_sems.at[k]))
                for mask in range(1, N_DEV):
                    to = tuple(1 - p if (mask >> (2 - b)) & 1 else p for b, p in enumerate(me))
                    ph["start"].append(remote(k, mask - 1, _window(src[k], ax, _linear(to), sz), own_dst, to))
                    ph["end_wait"].append(remote(k, mask - 1, own_src, dst[k].at[_linear(to)], to))
            else:
                def win(pos, k=k, ax=ax, sz=sz):
                    return _window(dst[k], ax, _linear(pos), sz)
                sibling = (x, y, 1 - c)
                chips = [(1 - x, y), (x, 1 - y), (1 - x, 1 - y)]
                ph["local"].append(pltpu.make_async_copy(src[k], win(me), local_sems.at[k]))
                ph["start"].append(remote(k, 0, src[k], win(me), sibling))
                ph["end_wait"].append(remote(k, 0, src[k], win(sibling), sibling))
                for j, chip in enumerate(chips):
                    ph["start"].append(remote(k, 1 + j, src[k], win(me), (*chip, c)))
                    ph["mid_wait"].append(remote(k, 1 + j, src[k], win((*chip, c)), (*chip, c)))
                    ph["mid_start"].append(remote(k, 4 + j, win((*chip, c)), win((*chip, c)), sibling))
                    ph["end_wait"].append(remote(k, 4 + j, src[k], win((*chip, 1 - c)), sibling))
        return ph


def _pcall(body, *, grid, in_specs, out_specs, out_shape, scratch_shapes, semantics, vmem_mb, name, operands,
           rider=None):
    if rider is None:
        return pl.pallas_call(body, grid=grid, in_specs=in_specs, out_specs=out_specs, out_shape=out_shape,
                              scratch_shapes=scratch_shapes, compiler_params=_params(semantics, vmem_mb),
                              name=name)(*operands)
    n_in, n_out, n_scr, n = len(in_specs), len(out_specs), len(scratch_shapes), rider.n

    def wrapped(*refs):
        ins, src = refs[:n_in], refs[n_in:n_in + n]
        outs = refs[n_in + n:n_in + n + n_out]
        dst = refs[n_in + n + n_out:n_in + 2 * n + n_out]
        rest = refs[n_in + 2 * n + n_out:]
        scratch, sems = rest[:n_scr], rest[n_scr:]
        step = functools.reduce(lambda acc, ig: acc * ig[1] + ig[0],
                                [(pl.program_id(dim), g) for dim, g in enumerate(grid)], 0)
        n_steps = functools.reduce(lambda a, b: a * b, grid)

        @pl.when(step == 0)
        def _():
            ph = rider.plan(src, dst, *sems)
            for cp in ph["local"] + ph["start"]:
                cp.start()

        body(*ins, *outs, *scratch)

        @pl.when(step == (n_steps * 5) // 8)
        def _():
            ph = rider.plan(src, dst, *sems)
            for cp in ph["mid_wait"]:
                cp.wait_recv()
            for cp in ph["mid_start"]:
                cp.start()

        @pl.when(step == n_steps - 1)
        def _():
            ph = rider.plan(src, dst, *sems)
            for cp in ph["end_wait"]:
                cp.wait_recv()
            for cp in ph["start"] + ph["mid_start"]:
                cp.wait_send()
            for cp in ph["local"]:
                cp.wait()

    any_spec = pl.BlockSpec(memory_space=pl.ANY)
    return pl.pallas_call(
        wrapped, grid=grid, in_specs=list(in_specs) + [any_spec] * n, out_specs=list(out_specs) + [any_spec] * n,
        out_shape=list(out_shape) + rider.out_shape(), scratch_shapes=list(scratch_shapes) + rider.scratch(),
        compiler_params=_params(("arbitrary",) * len(grid), vmem_mb), name=name,
    )(*operands, *rider.arrays)


WEIGHTS = ['c_ctx', 'l0_ada_w', 'l0_ada_b', 'l0_in_w', 'l0_conv_w', 'l0_conv_b', 'l0_conv_ln_g', 'l0_conv_ln_b',
           'l0_pool_w', 'l0_pool_scale', 'l0_out_w', 'l0_mlp_w1', 'l0_mlp_w2', 'l1_ada_w', 'l1_ada_b', 'l1_qkv_w',
           'l1_q_norm_g', 'l1_k_norm_g', 'l1_out_w', 'l1_mlp_w1', 'l1_mlp_w2', 'final_g']
SHARDED = {'l0_in_w': 1, 'l0_out_w': 0, 'l0_mlp_w1': 1, 'l0_mlp_w2': 0,
           'l1_qkv_w': 1, 'l1_out_w': 0, 'l1_mlp_w1': 1, 'l1_mlp_w2': 0}
REPLICATED_SMALL = ['l0_conv_b', 'l0_conv_ln_g', 'l0_conv_ln_b', 'l0_pool_scale', 'l1_q_norm_g', 'l1_k_norm_g',
                    'final_g']


def _row8(v):
    v = v.reshape(1, -1)
    return jnp.pad(v, ((0, 7), (0, 0)))


def _mods16(full, me):
    d = full.shape[1] // 6
    mine = lax.dynamic_slice_in_dim(full, me, 1, axis=0).reshape(6, d)
    ctx = full[N_DEV].reshape(6, d)
    return jnp.pad(jnp.stack([mine, ctx], axis=1).reshape(12, d), ((0, 4), (0, 0)))


def _mlp_fwd(xs, mods, w1, w2, *, n_x, tm, tag, rider1=None, rider2=None):
    t, d = xs.shape
    dff = w1.shape[1]
    h = _rms_mod_fwd(xs, mods, k_shift=3, n_x=n_x, name=f"{tag}_norm2")
    pre, act, *ride1 = _matmul(h, w1, mode="nn", tm=tm, tn=_div(dff, WIDE_TN if tm * WIDE_TN[0] <= WIDE_TILE_ELEMS else WIDE_TN[1:]), tk=d, out_dtypes=[BF16, BF16],
                               name=f"{tag}_mlp1", rider=rider1,
                               epilogue=lambda acc, rows: (acc, jnp.square(jnp.maximum(acc, 0.0))))
    if w2 is None:
        w2 = ride1[-1]
    xo, branch, *ride2 = _matmul(
        act, w2, mode="nn", tm=_div(t, DEEP_TM), tn=_div(d, DEEP_TN), tk=dff, out_dtypes=[F32, BF16],
        name=f"{tag}_mlp2", extras=[("tile", xs), ("vec", mods)], rider=rider2,
        epilogue=lambda acc, rows, res, mv: (res + _seg_pick(mv, 5, rows, n_x) * acc, acc))
    return xo, dict(h=h, pre=pre, act=act, branch=branch, x_in=xs), ride1, ride2


def _exchange_of(items):
    return Rider("exchange", [a for a, _ in items], [ax for _, ax in items]) if items else None


def _mlp_bwd(dxo, dbranch, saved, mods, w1, w2, mixer_branch, *, n_x, tm, tag, ride_dx2=(), ride_dw2=()):
    t, d = dxo.shape
    dff = w1.shape[1]
    tkt = _div(t, TOKEN_TK)
    dpre, *recv_a = _matmul(dbranch, w2, mode="nt", tm=tm, tn=_div(dff, WIDE_TN if tm * WIDE_TN[0] <= WIDE_TILE_ELEMS else WIDE_TN[1:]), tk=d, out_dtypes=[BF16],
                            name=f"{tag}_mlp2_dx", extras=[("tile", saved["pre"])],
                            rider=_exchange_of(list(ride_dx2)),
                            epilogue=lambda acc, rows, pre: (acc * 2.0 * jnp.maximum(pre.astype(F32), 0.0),))
    dw2, *recv_b = _matmul(saved["act"], dbranch, mode="tn", tm=_div(dff, (1024, 512)), tn=_div(d, (1024, 512)),
                           tk=tkt, out_dtypes=[BF16], name=f"{tag}_mlp2_dw", rider=_exchange_of(list(ride_dw2)))
    dh, = _matmul(dpre, w1, mode="nt", tm=_div(t, DEEP_TM), tn=_div(d, DEEP_TN), tk=dff,
                  out_dtypes=[F32], name=f"{tag}_mlp1_dx")
    dw1, recv_dw2 = _matmul(saved["h"], dpre, mode="tn", tm=_div(d, (1024, 512)), tn=_div(dff, (1024, 512)),
                            tk=tkt, out_dtypes=[BF16], name=f"{tag}_mlp1_dw", rider=_exchange_of([(dw2, 0)]))
    dx, dm_norm, dmix, dm_gate = _rms_mod_bwd(saved["x_in"], mods, dh, dxo, k_shift=3, n_x=n_x,
                                              name=f"{tag}_norm2_bwd", gate=(mixer_branch, mods, 2))
    return dx, dw1, dm_norm + dm_gate, dmix, recv_a, recv_b, recv_dw2


def kernel(x, c, ctx, c_ctx, l0_ada_w, l0_ada_b, l0_in_w, l0_conv_w, l0_conv_b, l0_conv_ln_g, l0_conv_ln_b, l0_pool_w, l0_pool_scale, l0_out_w, l0_mlp_w1, l0_mlp_w2, l1_ada_w, l1_ada_b, l1_qkv_w, l1_q_norm_g, l1_k_norm_g, l1_out_w, l1_mlp_w1, l1_mlp_w2, final_g, loss_target, m_c_ctx, m_l0_ada_w, m_l0_ada_b, m_l0_in_w, m_l0_conv_w, m_l0_conv_b, m_l0_conv_ln_g, m_l0_conv_ln_b, m_l0_pool_w, m_l0_pool_scale, m_l0_out_w, m_l0_mlp_w1, m_l0_mlp_w2, m_l1_ada_w, m_l1_ada_b, m_l1_qkv_w, m_l1_q_norm_g, m_l1_k_norm_g, m_l1_out_w, m_l1_mlp_w1, m_l1_mlp_w2, m_final_g, v_c_ctx, v_l0_ada_w, v_l0_ada_b, v_l0_in_w, v_l0_conv_w, v_l0_conv_b, v_l0_conv_ln_g, v_l0_conv_ln_b, v_l0_pool_w, v_l0_pool_scale, v_l0_out_w, v_l0_mlp_w1, v_l0_mlp_w2, v_l1_ada_w, v_l1_ada_b, v_l1_qkv_w, v_l1_q_norm_g, v_l1_k_norm_g, v_l1_out_w, v_l1_mlp_w1, v_l1_mlp_w2, v_final_g):
    p = dict(locals())
    me = _linear(_position())
    n_x, d = x.shape[1], x.shape[2]
    n_ctx = ctx.shape[1]
    t = n_x + n_ctx
    dc = l0_conv_b.shape[0]
    n_tap = l0_conv_w.shape[0]
    d_q = d
    n_mod = l0_ada_b.shape[0] // d
    ada_cols = l0_ada_w.shape[1]
    tm_t = _div(t, (768, 640, 512, 128))
    tm_x = _div(n_x, (1024, 512))

    names = list(SHARDED)
    shard16 = {nm: p[nm].astype(BF16) for nm in names}

    def gather_of(*nms):
        return Rider("gather", [shard16[nm] for nm in nms], [SHARDED[nm] for nm in nms])

    wfull = {}
    n_grp, pg = l0_pool_w.shape[0], l0_pool_w.shape[2]

    c_all = _all_gather([_row8(c)], [0], name="gather_cond")[0].reshape(N_DEV, 8, d)[:, 0]
    cond = jnp.concatenate([c_all, c_ctx.reshape(1, d), jnp.zeros((16 - N_DEV - 1, d), F32)], axis=0)
    s16 = _silu_rows(cond, name="cond_silu")
    mod_shards = []
    for li, (lw, lb) in enumerate(((l0_ada_w, l0_ada_b), (l1_ada_w, l1_ada_b))):
        bias = _row8(lax.dynamic_slice_in_dim(lb, me * ada_cols, ada_cols))
        mod_shards.append(_matmul(s16, lw, mode="nn", tm=16, tn=_div(ada_cols, (512, 384, 256, 128)), tk=d,
                                  out_dtypes=[F32], name=f"l{li}_ada_fwd", extras=[("vec", bias)],
                                  epilogue=lambda acc, rows, b: (acc + b[0:1],))[0])
    mods_full = _all_gather([jnp.concatenate(mod_shards, axis=0)], [1], name="gather_mods")[0]
    mods0, mods1 = _mods16(mods_full[:16], me), _mods16(mods_full[16:], me)

    xs0 = jnp.concatenate([x[0], ctx[0]], axis=0)
    first = Rider("gather", [shard16['l0_in_w'], jnp.pad(l0_conv_w, ((0, 1), (0, 0))), l0_pool_w.reshape(-1, pg)],
                  [SHARDED['l0_in_w'], 1, 0])
    h0, wfull['l0_in_w'], conv_w_full, pool_w_full = _rms_mod_fwd(
        xs0, mods0, k_shift=0, n_x=n_x, name="l0_norm1", rider=first)
    pool_w_full = pool_w_full.reshape(N_DEV, n_grp, pg // N_DEV, pg).transpose(1, 0, 2, 3).reshape(n_grp, pg, pg)
    z, wfull['l0_out_w'] = _matmul(h0, wfull['l0_in_w'], mode="nn", tm=tm_t, tn=_div(3 * dc, (1024, 768, 512, 384)),
                                   tk=d, out_dtypes=[F32], name="l0_in_proj", rider=gather_of('l0_out_w'))
    y0, cv, dsave, wfull['l0_mlp_w1'] = _mixer0_fwd(
        z, conv_w_full, _row8(l0_conv_b), _row8(l0_conv_ln_g), _row8(l0_conv_ln_b), pool_w_full,
        _row8(l0_pool_scale), n_tap=n_tap, n_x=n_x, name="l0_mixer", rider=gather_of('l0_mlp_w1'))
    xs1, mix0, wfull['l1_qkv_w'] = _matmul(
        y0, wfull['l0_out_w'], mode="nn", tm=tm_t, tn=_div(d, (1024, 512)), tk=2 * dc,
        out_dtypes=[F32, BF16], name="l0_out_proj", extras=[("tile", xs0), ("vec", mods0)],
        rider=gather_of('l1_qkv_w'),
        epilogue=lambda acc, rows, res, mv: (res + _seg_pick(mv, 2, rows, n_x) * acc, acc))
    xs2, mlp0, (wfull['l0_mlp_w2'],), (wfull['l1_mlp_w1'], wfull['l1_out_w']) = _mlp_fwd(
        xs1, mods0, wfull['l0_mlp_w1'], None, n_x=n_x, tm=tm_t, tag="l0",
        rider1=gather_of('l0_mlp_w2'), rider2=gather_of('l1_mlp_w1', 'l1_out_w'))

    h2 = _rms_mod_fwd(xs2, mods1, k_shift=0, n_x=n_x, name="l1_norm1")
    qkv, = _matmul(h2, wfull['l1_qkv_w'], mode="nn", tm=tm_t, tn=_div(l1_qkv_w.shape[1] * N_DEV, (1024, 768, 512)),
                   tk=d, out_dtypes=[F32], name="l1_qkv_proj")
    cos, sin = _rope_tables(n_x, n_ctx)
    gq, gk = _row8(l1_q_norm_g), _row8(l1_k_norm_g)
    q, k, v = _qk_prep_fwd(qkv, gq, gk, cos, sin, d_q=d_q, name="l1_qk_prep")
    o, lse = _attention_fwd(q, k, v.T, n_x=n_x, name="l1_attention")
    x3, mix1 = _matmul(o, wfull['l1_out_w'], mode="nn", tm=tm_x, tn=_div(d, (1024, 512)), tk=d_q,
                       out_dtypes=[F32, BF16], name="l1_out_proj", extras=[("tile", xs2), ("vec", mods1)],
                       epilogue=lambda acc, rows, res, mv: (res + mv[4:5] * acc, acc))
    x4, mlp1, (wfull['l1_mlp_w2'],), _ = _mlp_fwd(x3, mods1, wfull['l1_mlp_w1'], None, n_x=n_x, tm=tm_x, tag="l1",
                                                  rider1=gather_of('l1_mlp_w2'))

    dx4, loss_part, dfinal_g, dbranch1, dmods1 = _final_loss(
        x4, _row8(final_g), loss_target[0], mlp1["branch"], mods1, k_gate=5, name="loss_head")
    loss = lax.psum(loss_part[0, 0], MESH_AXES)

    recv = {}
    dx3, dw1_1, dm, dmix1, _, _, recv['l1_mlp_w2'] = _mlp_bwd(
        dx4, dbranch1, mlp1, mods1, wfull['l1_mlp_w1'], wfull['l1_mlp_w2'], mix1, n_x=n_x, tm=tm_x, tag="l1")
    dmods1 = dmods1 + dm
    do, = _matmul(dmix1, wfull['l1_out_w'], mode="nt", tm=tm_x, tn=_div(d_q, (1024, 512)), tk=d,
                  out_dtypes=[BF16], name="l1_out_dx")
    dw_out1, = _matmul(o, dmix1, mode="tn", tm=_div(d_q, (1024, 512)), tn=_div(d, (1024, 512)),
                       tk=_div(n_x, TOKEN_TK), out_dtypes=[BF16], name="l1_out_dw")
    dq, dk_t, dv_t = _attention_bwd(q, k, v, o, lse, do, n_x=n_x, name="l1_attention_bwd")
    dk, dv = dk_t.T, dv_t.T
    dqkv, dgq, dgk = _qk_prep_bwd(qkv, dq, dk, dv, gq, gk, cos, sin, n_x=n_x, name="l1_qk_prep_bwd")
    tkt = _div(t, TOKEN_TK)
    dh2, recv['l1_out_w'] = _matmul(dqkv, wfull['l1_qkv_w'], mode="nt", tm=tm_t, tn=_div(d, (1024, 512)),
                                    tk=dqkv.shape[1], out_dtypes=[F32], name="l1_qkv_dx",
                                    rider=_exchange_of([(dw_out1, SHARDED['l1_out_w'])]))
    dw_qkv, = _matmul(h2, dqkv, mode="tn", tm=_div(d, (1024, 512)), tn=_div(dqkv.shape[1], (1024, 768, 512)),
                      tk=tkt, out_dtypes=[BF16], name="l1_qkv_dw")
    dxs2, dm, dbranch0, dmods0 = _rms_mod_bwd(xs2, mods1, dh2, dx3, k_shift=0, n_x=n_x, name="l1_norm1_bwd",
                                              dres_rows=n_x, gate=(mlp0["branch"], mods0, 5))
    dmods1 = dmods1 + dm

    dxs1, dw1_0, dm, dmix0, (recv['l1_qkv_w'],), (recv['l1_mlp_w1'],), recv['l0_mlp_w2'] = _mlp_bwd(
        dxs2, dbranch0, mlp0, mods0, wfull['l0_mlp_w1'], wfull['l0_mlp_w2'], mix0, n_x=n_x, tm=tm_t, tag="l0",
        ride_dx2=[(dw_qkv, SHARDED['l1_qkv_w'])], ride_dw2=[(dw1_1, SHARDED['l1_mlp_w1'])])
    dmods0 = dmods0 + dm
    dy0, = _matmul(dmix0, wfull['l0_out_w'], mode="nt", tm=tm_t, tn=_div(2 * dc, (1024, 512)), tk=d,
                   out_dtypes=[F32], name="l0_out_dx")
    dw_out0, = _matmul(y0, dmix0, mode="tn", tm=_div(2 * dc, (1024, 512)), tn=_div(d, (1024, 512)),
                       tk=tkt, out_dtypes=[BF16], name="l0_out_dw")
    dz, dconv_w, dconv_b, dln_g, dln_b, dpool_w, dpool_scale, recv['l0_mlp_w1'] = _mixer0_bwd(
        dy0, z, cv, dsave, conv_w_full, _row8(l0_conv_ln_g), _row8(l0_conv_ln_b), pool_w_full,
        _row8(l0_pool_scale), n_tap=n_tap, n_x=n_x, name="l0_mixer_bwd",
        rider=Rider("exchange", [dw1_0], [SHARDED['l0_mlp_w1']]))
    dw_in0, recv['l0_out_w'] = _matmul(h0, dz, mode="tn", tm=_div(d, (1024, 512)),
                                       tn=_div(3 * dc, (1024, 768, 512, 384)), tk=tkt, out_dtypes=[BF16],
                                       name="l0_in_dw", rider=_exchange_of([(dw_out0, SHARDED['l0_out_w'])]))
    dh0, recv['l0_in_w'] = _matmul(dz, wfull['l0_in_w'], mode="nt", tm=tm_t, tn=_div(d, (1024, 512)), tk=3 * dc,
                                   out_dtypes=[F32], name="l0_in_dx",
                                   rider=_exchange_of([(dw_in0, SHARDED['l0_in_w'])]))
    dx0, dm = _rms_mod_bwd(xs0, mods0, dh0, dxs1, k_shift=0, n_x=n_x, name="l0_norm1_bwd", out_rows=n_x)
    dmods0 = dmods0 + dm
    grad_x = dx0[None]

    def dmod_rows(dm16):
        rows = dm16[:2 * n_mod].reshape(n_mod, 2, d).transpose(1, 0, 2).reshape(2, n_mod * d)
        return jnp.pad(rows, ((0, 6), (0, 0)))
    dm_gathered = _all_gather([jnp.concatenate([dmod_rows(dmods0), dmod_rows(dmods1)], axis=0)], [0],
                              name="gather_dmods")[0]
    dm0, dm1, db0, db1 = _assemble_dmods(dm_gathered, name="assemble_dmods")
    out_g = {'l0_ada_b': db0[0], 'l1_ada_b': db1[0]}
    ds_part = jnp.zeros((16, d), F32)
    for nm, lw, dmf in (('l0_ada_w', l0_ada_w, dm0), ('l1_ada_w', l1_ada_w, dm1)):
        dm_cols = lax.dynamic_slice_in_dim(dmf, me * ada_cols, ada_cols, axis=1)
        out_g[nm], = _matmul(s16, dm_cols, mode="tn", tm=_div(d, (1024, 512)),
                             tn=_div(ada_cols, (512, 384, 256, 128)), tk=16, out_dtypes=[F32], name=f"{nm}_dw")
        ds_part = ds_part + _matmul(dm_cols, lw, mode="nt", tm=16, tn=_div(d, (1024, 512)),
                                    tk=_div(ada_cols, (512, 384, 256, 128)), out_dtypes=[F32], name=f"{nm}_dx")[0]

    small = {'l0_conv_b': dconv_b[0], 'l0_conv_ln_g': dln_g[0], 'l0_conv_ln_b': dln_b[0],
             'l0_pool_scale': dpool_scale[0], 'l1_q_norm_g': dgq[0], 'l1_k_norm_g': dgk[0],
             'final_g': dfinal_g[0], 'dsilu_ctx': ds_part[N_DEV], 'l0_conv_w': dconv_w[:-1].reshape(-1),
             'l0_pool_w': dpool_w.reshape(-1)}
    flat = jnp.concatenate([small[nm] for nm in small])
    rows = -(-flat.shape[0] // 1024) * 8
    packed = jnp.pad(flat, (0, rows * 128 - flat.shape[0])).reshape(rows, 128)
    summed = _sum_slots(_all_gather([packed], [0], name="gather_small_grads")[0], rows,
                        name="sum_small_grads").reshape(-1)
    off = 0
    for nm in small:
        size = small[nm].shape[0]
        small[nm] = summed[off:off + size]
        off += size
    out_g['c_ctx'] = _silu_grad(_row8(c_ctx), _row8(small['dsilu_ctx']), name="c_ctx_grad")[0]
    for nm in REPLICATED_SMALL:
        out_g[nm] = small[nm]
    conv_cols = l0_conv_w.shape[1]
    out_g['l0_conv_w'] = lax.dynamic_slice_in_dim(small['l0_conv_w'].reshape(n_tap, dc), me * conv_cols, conv_cols,
                                                  axis=1)
    out_g['l0_pool_w'] = lax.dynamic_slice_in_dim(small['l0_pool_w'].reshape(n_grp, pg, pg), me * (pg // N_DEV),
                                                  pg // N_DEV, axis=1)

    delta, new_m, new_v = {}, {}, {}
    for nm in names:
        out_g[nm], delta[nm], new_m[nm], new_v[nm] = _adamw(p[nm], p['m_' + nm], p['v_' + nm], recv=recv[nm],
                                                            name=f"adamw_{nm}")
    for nm in ('l0_ada_w', 'l1_ada_w'):
        out_g[nm], delta[nm], new_m[nm], new_v[nm] = _adamw(p[nm], p['m_' + nm], p['v_' + nm], grad=out_g[nm],
                                                            name=f"adamw_{nm}")
    for nm in WEIGHTS:
        if nm in delta:
            continue
        shape = p[nm].shape
        as2d = lambda a: a.reshape(1, -1) if a.ndim == 1 else a.reshape(-1, a.shape[-1])
        res = _adamw(as2d(p[nm]), as2d(p['m_' + nm]), as2d(p['v_' + nm]), grad=as2d(out_g[nm]), name=f"adamw_{nm}")
        out_g[nm], delta[nm], new_m[nm], new_v[nm] = [r.reshape(shape) for r in res]

    return (loss, grad_x, *[out_g[nm] for nm in WEIGHTS], *[delta[nm] for nm in WEIGHTS],
            *[new_m[nm] for nm in WEIGHTS], *[new_v[nm] for nm in WEIGHTS])
```

```python
import functools

import jax
import jax.numpy as jnp
from jax import lax
from jax.experimental import pallas as pl
from jax.experimental.pallas import tpu as pltpu

F32 = jnp.float32
BF16 = jnp.bfloat16
N_DEV = 8
MESH_AXES = ("x", "y", "c")
EPS = 1e-6
HEAD_DIM = 128
POOL_WINDOWS = (2, 4, 8, 16)
GRID_W = 64
ROPE_THETA = 10000.0
ATTN_SCALE = HEAD_DIM ** -0.5
LOG2_E = 1.4426950408889634
Q_SCALE_LOG2 = ATTN_SCALE * LOG2_E
HALO = 16
ADAM_LR, ADAM_B1, ADAM_B2, ADAM_EPS, ADAM_WD, ADAM_STEP = 0.001, 0.9, 0.999, 1e-08, 0.01, 10
VMEM_CAP_MB = 60
ADAMW_BLOCK_ELEMS = 1 << 18
TOKEN_TK = (2816, 2048, 1408, 1024, 768, 640, 512, 128)
DEEP_TM = (384, 512, 256, 128)
DEEP_TN = (512,)
WIDE_TN = (2048, 1024, 512)
WIDE_TILE_ELEMS = 768 * 2048
MESH_ID = pl.DeviceIdType.MESH


def _div(n, prefs):
    for p in prefs:
        if n % p == 0:
            return p
    raise ValueError(f"no tile for {n} in {prefs}")


def _params(sem, vmem_mb):
    return pltpu.CompilerParams(dimension_semantics=sem, vmem_limit_bytes=min(vmem_mb, VMEM_CAP_MB) << 20)


def _sigmoid(x):
    return 1.0 / (1.0 + jnp.exp(-x))


def _silu(x):
    return x * _sigmoid(x)


def _rms(x):
    return x * lax.rsqrt(jnp.mean(x * x, axis=-1, keepdims=True) + EPS)


def _rms_mod(x, shift, scale):
    return _rms(x) * (1.0 + scale) + shift


def _layernorm(x, g, b):
    mu = jnp.mean(x, axis=-1, keepdims=True)
    var = jnp.mean(jnp.square(x - mu), axis=-1, keepdims=True)
    return (x - mu) * lax.rsqrt(var + EPS) * g + b


def _ln_silu(x, g, b):
    return _silu(_layernorm(x, g, b))


def _matmul(a, b, *, mode, tm, tn, tk, out_dtypes, name, extras=(), epilogue=None, rider=None):
    if mode == "tn":
        kdim, m = a.shape
        n = b.shape[1]
    else:
        m, kdim = a.shape
        n = b.shape[0] if mode == "nt" else b.shape[1]
    assert m % tm == 0 and n % tn == 0 and kdim % tk == 0, (name, m, n, kdim, tm, tn, tk)
    nk = kdim // tk
    n_ex = len(extras)
    n_out = len(out_dtypes)

    def body(a_ref, b_ref, *rest):
        ex = rest[:n_ex]
        outs = rest[n_ex:n_ex + n_out]
        acc_ref = rest[n_ex + n_out] if nk > 1 else None
        k = pl.program_id(2)
        av = a_ref[...].astype(BF16)
        bv = b_ref[...].astype(BF16)
        dims = {"nn": ((1,), (0,)), "nt": ((1,), (1,)), "tn": ((0,), (0,))}[mode]
        part = lax.dot_general(av, bv, (dims, ((), ())), preferred_element_type=F32)

        rows = pl.program_id(0) * tm + lax.broadcasted_iota(jnp.int32, (tm, 1), 0)

        def finish(acc):
            res = (acc,) if epilogue is None else epilogue(acc, rows, *[e[...] for e in ex])
            for o, r in zip(outs, res):
                o[...] = r.astype(o.dtype)

        if nk == 1:
            finish(part)
        else:
            @pl.when(k == 0)
            def _():
                acc_ref[...] = part

            @pl.when(k > 0)
            def _():
                acc_ref[...] += part

            @pl.when(k == nk - 1)
            def _():
                finish(acc_ref[...])

    if mode == "tn":
        a_spec = pl.BlockSpec((tk, tm), lambda i, j, k: (k, i))
        b_spec = pl.BlockSpec((tk, tn), lambda i, j, k: (k, j))
    else:
        a_spec = pl.BlockSpec((tm, tk), lambda i, j, k: (i, k))
        b_spec = (pl.BlockSpec((tn, tk), lambda i, j, k: (j, k)) if mode == "nt"
                  else pl.BlockSpec((tk, tn), lambda i, j, k: (k, j)))
    ex_specs, ex_arrays, ex_bytes = [], [], 0
    for kind, arr in extras:
        ex_arrays.append(arr)
        if kind == "tile":
            ex_specs.append(pl.BlockSpec((tm, tn), lambda i, j, k: (i, j)))
            ex_bytes += tm * tn * arr.dtype.itemsize
        else:
            ex_specs.append(pl.BlockSpec((arr.shape[0], tn), lambda i, j, k: (0, j)))
            ex_bytes += arr.shape[0] * tn * 4
    blocks = tm * tk * a.dtype.itemsize + tk * tn * b.dtype.itemsize + ex_bytes
    blocks += sum(tm * tn * jnp.dtype(d).itemsize for d in out_dtypes)
    casts = sum(rows * cols * 2 for arr, rows, cols in ((a, tm, tk), (b, tk, tn)) if arr.dtype != BF16)
    vmem = (2 * blocks + 4 * tm * tn * 4 + casts) // (1 << 20) + 8
    return _pcall(
        body,
        grid=(m // tm, n // tn, nk),
        in_specs=[a_spec, b_spec] + ex_specs,
        out_specs=[pl.BlockSpec((tm, tn), lambda i, j, k: (i, j)) for _ in out_dtypes],
        out_shape=[jax.ShapeDtypeStruct((m, n), d) for d in out_dtypes],
        scratch_shapes=[pltpu.VMEM((tm, tn), F32)] if nk > 1 else [],
        semantics=("parallel", "parallel", "arbitrary"), vmem_mb=vmem, name=name,
        operands=[a, b, *ex_arrays], rider=rider)


def _seg_pick(vec, k, rows, n_x):
    return jnp.where(rows < n_x, vec[2 * k:2 * k + 1], vec[2 * k + 1:2 * k + 2])


def _zero_accs(i, accs):
    @pl.when(i == 0)
    def _():
        for a in accs:
            a[...] = jnp.zeros_like(a)


def _rms_mod_fwd(xs, mods, *, k_shift, n_x, name, rider=None):
    t, d = xs.shape
    tm = _div(t, (256, 128))

    def body(x_ref, mods_ref, h_ref):
        seg = (pl.program_id(0) * tm >= n_x).astype(jnp.int32)
        shift = mods_ref[pl.ds(2 * k_shift + seg, 1), :]
        scale = mods_ref[pl.ds(2 * k_shift + 2 + seg, 1), :]
        h_ref[...] = _rms_mod(x_ref[...], shift, scale).astype(BF16)

    res = _pcall(
        body, grid=(t // tm,),
        in_specs=[pl.BlockSpec((tm, d), lambda i: (i, 0)), pl.BlockSpec((16, d), lambda i: (0, 0))],
        out_specs=[pl.BlockSpec((tm, d), lambda i: (i, 0))],
        out_shape=[jax.ShapeDtypeStruct((t, d), BF16)],
        scratch_shapes=[], semantics=("parallel",), vmem_mb=32, name=name, operands=[xs, mods], rider=rider)
    return res[0] if rider is None else res


def _gate_part(dxv, seg, k_gate, br_ref, gmods_ref, db_ref, dgm_ref):
    r_gate = 2 * k_gate + seg
    db_ref[...] = (dxv * gmods_ref[pl.ds(r_gate, 1), :]).astype(BF16)
    dgm_ref[pl.ds(r_gate, 1), :] += jnp.sum(dxv * br_ref[...].astype(F32), axis=0, keepdims=True)


def _rms_mod_bwd(xs, mods, dh, dres, *, k_shift, n_x, name, rider=None, gate=None, dres_rows=None, out_rows=None):
    t, d = xs.shape
    tm = _div(t, (256, 128))
    n_gate = 2 if gate is not None else 0

    def body(x_ref, mods_ref, dh_ref, dres_ref, *rest):
        gate_in, (dx_ref, dmods_ref), gate_out = rest[:n_gate], rest[n_gate:n_gate + 2], rest[n_gate + 2:]
        i = pl.program_id(0)
        _zero_accs(i, [dmods_ref, *gate_out[1:]])
        seg = (i * tm >= n_x).astype(jnp.int32)
        r_shift = 2 * k_shift + seg
        r_scale = 2 * k_shift + 2 + seg
        shift = mods_ref[pl.ds(r_shift, 1), :]
        scale = mods_ref[pl.ds(r_scale, 1), :]
        _, vjp = jax.vjp(_rms_mod, x_ref[...], shift, scale)
        dx, dshift, dscale = vjp(dh_ref[...].astype(F32))
        dres_v = dres_ref[...]
        if dres_rows is not None:
            dres_v = dres_v * jnp.where(i * tm < dres_rows, 1.0, 0.0)
        dx = dres_v + dx
        if out_rows is None:
            dx_ref[...] = dx
        else:
            @pl.when(i * tm < out_rows)
            def _():
                dx_ref[...] = dx
        dmods_ref[pl.ds(r_shift, 1), :] += dshift
        dmods_ref[pl.ds(r_scale, 1), :] += dscale
        if gate is not None:
            _gate_part(dx, seg, gate[2], *gate_in, *gate_out)

    def clamped(rows):
        return pl.BlockSpec((tm, d), lambda i: (jnp.minimum(i, rows // tm - 1), 0))

    row = pl.BlockSpec((tm, d), lambda i: (i, 0))
    vec = pl.BlockSpec((16, d), lambda i: (0, 0))
    in_specs = [row, vec, row, row if dres_rows is None else clamped(dres_rows)]
    out_specs = [row if out_rows is None else clamped(out_rows), vec]
    out_shape = [jax.ShapeDtypeStruct((t if out_rows is None else out_rows, d), F32),
                 jax.ShapeDtypeStruct((16, d), F32)]
    operands = [xs, mods, dh, dres]
    if gate is not None:
        in_specs += [row, vec]
        out_specs += [row, vec]
        out_shape += [jax.ShapeDtypeStruct((t, d), BF16), jax.ShapeDtypeStruct((16, d), F32)]
        operands += [gate[0], gate[1]]
    return _pcall(body, grid=(t // tm,), in_specs=in_specs, out_specs=out_specs, out_shape=out_shape,
                  scratch_shapes=[], semantics=("arbitrary",), vmem_mb=48, name=name, operands=operands, rider=rider)


def _final_loss(xs, g, target, branch, gmods, *, k_gate, name):
    t, d = xs.shape
    tm = _div(t, (256, 128))

    def loss_fn(x, gv, tgt):
        err = _rms(x) * gv - tgt
        return 0.5 * jnp.sum(jnp.mean(jnp.square(err), axis=-1))

    def body(x_ref, g_ref, t_ref, br_ref, gmods_ref, dx_ref, loss_ref, dg_ref, db_ref, dgm_ref):
        i = pl.program_id(0)
        _zero_accs(i, [loss_ref, dg_ref, dgm_ref])
        val, vjp = jax.vjp(loss_fn, x_ref[...], g_ref[0:1, :], t_ref[...])
        dx, dg, _ = vjp(jnp.ones((), F32))
        dx_ref[...] = dx
        loss_ref[...] += val
        dg_ref[0:1, :] += dg
        _gate_part(dx, 0, k_gate, br_ref, gmods_ref, db_ref, dgm_ref)

    row = pl.BlockSpec((tm, d), lambda i: (i, 0))
    vec16 = pl.BlockSpec((16, d), lambda i: (0, 0))
    return pl.pallas_call(
        body, grid=(t // tm,),
        in_specs=[row, pl.BlockSpec((8, d), lambda i: (0, 0)), row, row, vec16],
        out_specs=[row, pl.BlockSpec((8, 128), lambda i: (0, 0)), pl.BlockSpec((8, d), lambda i: (0, 0)), row, vec16],
        out_shape=[jax.ShapeDtypeStruct((t, d), F32), jax.ShapeDtypeStruct((8, 128), F32),
                   jax.ShapeDtypeStruct((8, d), F32), jax.ShapeDtypeStruct((t, d), BF16),
                   jax.ShapeDtypeStruct((16, d), F32)],
        compiler_params=_params(("arbitrary",), 48), name=name,
    )(xs, g, target, branch, gmods)


def _halo_specs(r, width, col, t):
    h_per = r // HALO

    def prev(i):
        return (jnp.maximum(i * h_per - 1, 0), col)

    def nxt(i):
        return (jnp.minimum((i + 1) * h_per, t // HALO - 1), col)

    return (pl.BlockSpec((HALO, width), prev), pl.BlockSpec((r, width), lambda i: (i, col)),
            pl.BlockSpec((HALO, width), nxt))


def _seg_geometry(i, r, n_x, t):
    row0 = i * r
    in_ctx = row0 >= n_x
    first = jnp.logical_or(row0 == 0, row0 == n_x)
    last = jnp.logical_or(row0 + r == n_x, row0 + r == t)
    seg_start = jnp.where(in_ctx, n_x, 0)
    seg_len = jnp.where(in_ctx, t - n_x, n_x)
    return row0, first, last, seg_start, seg_len


ROW_CHUNK = 64
LANE_CHUNK = 128
MIXER_ROWS = (128,)


def _chunks(width, rows, col0=0):
    lanes = min(LANE_CHUNK, width)
    return [(slice(col0 + c, col0 + c + lanes), r0) for c in range(0, width, lanes) for r0 in range(0, rows, ROW_CHUNK)]


def _pool_count(tpos, w, seg_len):
    return (jnp.minimum(tpos + w // 2, seg_len) - jnp.maximum(tpos - w // 2, 0)).astype(F32)


def _mixer0_fwd(z, conv_w, conv_b, ln_g, ln_b, pool_w, pool_scale, *, n_tap, n_x, name, rider=None):
    t, dc = z.shape[0], z.shape[1] // 3
    n_grp, pg = pool_w.shape[0], pool_w.shape[1]
    r = _div(t - n_x, MIXER_ROWS)
    assert n_x % r == 0 and pg * n_grp == dc
    half = n_tap // 2
    assert half < HALO and max(POOL_WINDOWS) // 2 <= HALO

    def body(ap, ac, an, gp, gc, gn, pp, pc, pn, w_ref, cb_ref, lg_ref, lb_ref, pw_ref, ps_ref,
             y_ref, cv_ref, d_ref, uwin, pwin):
        i = pl.program_id(0)
        row0, first, last, seg_start, seg_len = _seg_geometry(i, r, n_x, t)
        keep_prev = jnp.where(first, 0.0, 1.0)
        keep_next = jnp.where(last, 0.0, 1.0)
        uwin[0:HALO, :] = ap[...] * _sigmoid(gp[...]) * keep_prev
        uwin[HALO:HALO + r, :] = ac[...] * _sigmoid(gc[...])
        uwin[HALO + r:, :] = an[...] * _sigmoid(gn[...]) * keep_next
        pwin[0:HALO, :] = pp[...] * keep_prev
        pwin[HALO:HALO + r, :] = pc[...]
        pwin[HALO + r:, :] = pn[...] * keep_next
        for cols, r0 in _chunks(dc, r):
            acc = jnp.zeros((ROW_CHUNK, cols.stop - cols.start), F32) + cb_ref[0:1, cols]
            for k in range(n_tap):
                off = HALO - half + k + r0
                acc = acc + w_ref[k:k + 1, cols] * uwin[off:off + ROW_CHUNK, cols]
            cv_ref[r0:r0 + ROW_CHUNK, cols] = acc
        y_ref[:, 0:dc] = _ln_silu(cv_ref[...], lg_ref[0:1, :], lb_ref[0:1, :]).astype(BF16)
        tpos = row0 - seg_start + lax.broadcasted_iota(jnp.int32, (r, 1), 0)
        for g, w in enumerate(POOL_WINDOWS):
            cnt = _pool_count(tpos, w, seg_len)
            for cols, r0 in _chunks(pg, r, g * pg):
                s = jnp.zeros((ROW_CHUNK, cols.stop - cols.start), F32)
                for j in range(-(w // 2), w // 2):
                    s = s + pwin[HALO + j + r0:HALO + j + r0 + ROW_CHUNK, cols]
                diff = s / cnt[r0:r0 + ROW_CHUNK] - pwin[HALO + r0:HALO + r0 + ROW_CHUNK, cols]
                d_ref[r0:r0 + ROW_CHUNK, cols] = diff.astype(BF16)
            cols = slice(g * pg, (g + 1) * pg)
            pm = jnp.dot(d_ref[:, cols], pw_ref[g].astype(BF16), preferred_element_type=F32)
            y_ref[:, dc + g * pg:dc + (g + 1) * pg] = (pm * ps_ref[0:1, cols]).astype(BF16)

    vec = lambda rows, width: pl.BlockSpec((rows, width), lambda i: (0, 0))
    in_specs = [*_halo_specs(r, dc, 0, t), *_halo_specs(r, dc, 1, t), *_halo_specs(r, dc, 2, t),
                vec(conv_w.shape[0], dc), vec(8, dc), vec(8, dc), vec(8, dc),
                pl.BlockSpec((n_grp, pg, pg), lambda i: (0, 0, 0)), vec(8, dc)]
    return _pcall(
        body, grid=(t // r,), in_specs=in_specs,
        out_specs=[pl.BlockSpec((r, 2 * dc), lambda i: (i, 0)), pl.BlockSpec((r, dc), lambda i: (i, 0)),
                   pl.BlockSpec((r, dc), lambda i: (i, 0))],
        out_shape=[jax.ShapeDtypeStruct((t, 2 * dc), BF16), jax.ShapeDtypeStruct((t, dc), F32),
                   jax.ShapeDtypeStruct((t, dc), BF16)],
        scratch_shapes=[pltpu.VMEM((r + 2 * HALO, dc), F32), pltpu.VMEM((r + 2 * HALO, dc), F32)],
        semantics=("parallel",), vmem_mb=40, name=name,
        operands=[*([z] * 9), conv_w, conv_b, ln_g, ln_b, pool_w, pool_scale], rider=rider)


def _mixer0_bwd(dy, z, cv, dsave, conv_w, ln_g, ln_b, pool_w, pool_scale, *, n_tap, n_x, name, rider=None):
    t, dc = cv.shape
    n_grp, pg = pool_w.shape[0], pool_w.shape[1]
    r = _div(t - n_x, MIXER_ROWS)
    half = n_tap // 2
    dyp_, zp_, cvp_ = dy, z, cv
    rw = r + 2 * HALO

    def body(dcp, dcc, dcn, dpp, dpc, dpn, cvp, cvc, cvn, ap, ac, an, gp, gc, gn, d_ref,
             w_ref, lg_ref, lb_ref, pw_ref, ps_ref,
             dz_ref, dw_ref, dcb_ref, dlg_ref, dlb_ref, dpw_ref, dps_ref, uwin, dcvwin, ewin, ddwin):
        i = pl.program_id(0)
        _zero_accs(i, [dw_ref, dcb_ref, dlg_ref, dlb_ref, dpw_ref, dps_ref])
        row0, first, last, seg_start, seg_len = _seg_geometry(i, r, n_x, t)
        keep_prev = jnp.where(first, 0.0, 1.0)
        keep_next = jnp.where(last, 0.0, 1.0)
        lg, lb = lg_ref[0:1, :], lb_ref[0:1, :]
        _, vjp = jax.vjp(_ln_silu, cvc[...], lg, lb)
        dcv, dlg, dlb = vjp(dcc[...])
        dlg_ref[0:1, :] += dlg
        dlb_ref[0:1, :] += dlb
        dcb_ref[0:1, :] += jnp.sum(dcv, axis=0, keepdims=True)
        dcvwin[HALO:HALO + r, :] = dcv
        for halo_cv, halo_dy, keep, lo in ((cvp, dcp, keep_prev, 0), (cvn, dcn, keep_next, HALO + r)):
            _, vjp_h = jax.vjp(lambda v: _ln_silu(v, lg, lb), halo_cv[...])
            dcvwin[lo:lo + HALO, :] = vjp_h(halo_dy[...])[0] * keep
        uwin[0:HALO, :] = ap[...] * _sigmoid(gp[...]) * keep_prev
        uwin[HALO:HALO + r, :] = ac[...] * _sigmoid(gc[...])
        uwin[HALO + r:, :] = an[...] * _sigmoid(gn[...]) * keep_next
        for cols, r0 in _chunks(dc, r):
            du = jnp.zeros((ROW_CHUNK, cols.stop - cols.start), F32)
            for k in range(n_tap):
                off = HALO + half - k + r0
                du = du + w_ref[k:k + 1, cols] * dcvwin[off:off + ROW_CHUNK, cols]
            rows = slice(r0, r0 + ROW_CHUNK)
            sig = _sigmoid(gc[rows, cols])
            dz_ref[rows, cols] = (du * sig).astype(BF16)
            dz_ref[rows, dc + cols.start:dc + cols.stop] = (du * ac[rows, cols] * sig * (1.0 - sig)).astype(BF16)
        for c0 in range(0, dc, LANE_CHUNK):
            cols = slice(c0, c0 + LANE_CHUNK)
            taps = [jnp.zeros((8, LANE_CHUNK), F32) for _ in range(n_tap)]
            for r0 in range(0, r, ROW_CHUNK):
                dcv_c = dcvwin[HALO + r0:HALO + r0 + ROW_CHUNK, cols]
                for k in range(n_tap):
                    off = HALO - half + k + r0
                    prod = dcv_c * uwin[off:off + ROW_CHUNK, cols]
                    taps[k] = taps[k] + functools.reduce(
                        jnp.add, [prod[8 * s:8 * s + 8] for s in range(ROW_CHUNK // 8)])
            for k in range(n_tap):
                dw_ref[k:k + 1, cols] += jnp.sum(taps[k], axis=0, keepdims=True)
        twin = row0 - seg_start - HALO + lax.broadcasted_iota(jnp.int32, (rw, 1), 0)
        for g, w in enumerate(POOL_WINDOWS):
            cols = slice(g * pg, (g + 1) * pg)
            wg = pw_ref[g].astype(BF16)
            scale = ps_ref[0:1, cols]
            dyp_c = dpc[:, cols]
            dpm_win = jnp.concatenate([dpp[:, cols] * keep_prev, dyp_c, dpn[:, cols] * keep_next], axis=0) * scale
            dd_win = lax.dot_general(dpm_win.astype(BF16), wg, (((1,), (1,)), ((), ())), preferred_element_type=F32)
            cnt = jnp.maximum(_pool_count(twin, w, seg_len), 1.0)
            ddwin[:, cols] = dd_win
            ewin[:, cols] = dd_win / cnt
            for ccols, r0 in _chunks(pg, r, g * pg):
                dup = -ddwin[HALO + r0:HALO + r0 + ROW_CHUNK, ccols]
                for j in range(-(w // 2) + 1, w // 2 + 1):
                    dup = dup + ewin[HALO + j + r0:HALO + j + r0 + ROW_CHUNK, ccols]
                dz_ref[r0:r0 + ROW_CHUNK, 2 * dc + ccols.start:2 * dc + ccols.stop] = dup.astype(BF16)
            dsv = d_ref[:, cols]
            pm = jnp.dot(dsv, wg, preferred_element_type=F32)
            dps_ref[0:1, cols] += jnp.sum(dyp_c * pm, axis=0, keepdims=True)
            dpw_ref[g] += lax.dot_general(dsv, (dyp_c * scale).astype(BF16), (((0,), (0,)), ((), ())),
                                          preferred_element_type=F32)

    vec = lambda rows, width: pl.BlockSpec((rows, width), lambda i: (0, 0))
    grp = pl.BlockSpec((n_grp, pg, pg), lambda i: (0, 0, 0))
    in_specs = [*_halo_specs(r, dc, 0, t), *_halo_specs(r, dc, 1, t), *_halo_specs(r, dc, 0, t),
                *_halo_specs(r, dc, 0, t), *_halo_specs(r, dc, 1, t), pl.BlockSpec((r, dc), lambda i: (i, 0)),
                vec(conv_w.shape[0], dc), vec(8, dc), vec(8, dc), grp, vec(8, dc)]
    return _pcall(
        body, grid=(t // r,), in_specs=in_specs,
        out_specs=[pl.BlockSpec((r, 3 * dc), lambda i: (i, 0)), vec(conv_w.shape[0], dc), vec(8, dc), vec(8, dc),
                   vec(8, dc), grp, vec(8, dc)],
        out_shape=[jax.ShapeDtypeStruct((t, 3 * dc), BF16), jax.ShapeDtypeStruct(conv_w.shape, F32),
                   jax.ShapeDtypeStruct((8, dc), F32), jax.ShapeDtypeStruct((8, dc), F32),
                   jax.ShapeDtypeStruct((8, dc), F32), jax.ShapeDtypeStruct(pool_w.shape, F32),
                   jax.ShapeDtypeStruct((8, dc), F32)],
        scratch_shapes=[pltpu.VMEM((rw, dc), F32)] * 4,
        semantics=("arbitrary",), vmem_mb=VMEM_CAP_MB, name=name,
        operands=[dyp_, dyp_, dyp_, dyp_, dyp_, dyp_, cvp_, cvp_, cvp_, zp_, zp_, zp_, zp_, zp_, zp_, dsave,
                  conv_w, ln_g, ln_b, pool_w, pool_scale], rider=rider)


def _swap_halves(x):
    lane = lax.broadcasted_iota(jnp.int32, x.shape, 1)
    quarter = HEAD_DIM // 4
    return jnp.where(lane % (2 * quarter) < quarter,
                     pltpu.roll(x, HEAD_DIM - quarter, 1), pltpu.roll(x, quarter, 1))


def _rope_tables(n_x, n_ctx):
    half = HEAD_DIM // 4
    freqs = ROPE_THETA ** (-jnp.arange(half, dtype=F32) / half)
    tok = jnp.arange(n_x)
    row = (tok // GRID_W).astype(F32)[:, None] * freqs[None, :]
    col = (tok % GRID_W).astype(F32)[:, None] * freqs[None, :]
    cos = jnp.concatenate([jnp.cos(row), jnp.cos(row), jnp.cos(col), jnp.cos(col)], axis=1)
    sin = jnp.concatenate([-jnp.sin(row), jnp.sin(row), -jnp.sin(col), jnp.sin(col)], axis=1)
    cos = jnp.concatenate([cos, jnp.ones((n_ctx, HEAD_DIM), F32)], axis=0)
    sin = jnp.concatenate([sin, jnp.zeros((n_ctx, HEAD_DIM), F32)], axis=0)
    return cos, sin


def _norm_g(x, g):
    return _rms(x) * g


def _qk_prep_fwd(qkv, gq, gk, cos, sin, *, d_q, name):
    t, width = qkv.shape
    d_kv = (width - d_q) // 2
    tm = _div(t, (256, 128))

    def body(qkv_ref, gq_ref, gk_ref, cos_ref, sin_ref, q_ref, k_ref, v_ref):
        cs, sn = cos_ref[...], sin_ref[...]
        for h in range((d_q + d_kv) // HEAD_DIM):
            g = gq_ref[0:1, :] if h * HEAD_DIM < d_q else gk_ref[0:1, :]
            xn = _norm_g(qkv_ref[:, h * HEAD_DIM:(h + 1) * HEAD_DIM], g)
            rot = xn * cs + _swap_halves(xn) * sn
            if h * HEAD_DIM < d_q:
                q_ref[:, h * HEAD_DIM:(h + 1) * HEAD_DIM] = (rot * Q_SCALE_LOG2).astype(BF16)
            else:
                k_ref[:, h * HEAD_DIM - d_q:(h + 1) * HEAD_DIM - d_q] = rot.astype(BF16)
        v_ref[...] = qkv_ref[:, d_q + d_kv:].astype(BF16)

    row = lambda w: pl.BlockSpec((tm, w), lambda i: (i, 0))
    vec = pl.BlockSpec((8, HEAD_DIM), lambda i: (0, 0))
    return pl.pallas_call(
        body, grid=(t // tm,),
        in_specs=[row(width), vec, vec, row(HEAD_DIM), row(HEAD_DIM)],
        out_specs=[row(d_q), row(d_kv), row(d_kv)],
        out_shape=[jax.ShapeDtypeStruct((t, d_q), BF16), jax.ShapeDtypeStruct((t, d_kv), BF16),
                   jax.ShapeDtypeStruct((t, d_kv), BF16)],
        compiler_params=_params(("parallel",), 32), name=name,
    )(qkv, gq, gk, cos, sin)


def _qk_prep_bwd(qkv, dq, dk, dv, gq, gk, cos, sin, *, n_x, name):
    t, width = qkv.shape
    d_q, d_kv = dq.shape[1], dk.shape[1]
    tm = _div(t, (256, 128))
    last_q = n_x // tm - 1

    def body(qkv_ref, dq_ref, dk_ref, dv_ref, gq_ref, gk_ref, cos_ref, sin_ref, out_ref, dgq_ref, dgk_ref):
        i = pl.program_id(0)
        _zero_accs(i, [dgq_ref, dgk_ref])
        is_x = jnp.where(i * tm < n_x, 1.0, 0.0)
        cs, sn = cos_ref[...], sin_ref[...]
        for h in range((d_q + d_kv) // HEAD_DIM):
            sl = slice(h * HEAD_DIM, (h + 1) * HEAD_DIM)
            if h * HEAD_DIM < d_q:
                g, dg_ref, dr = gq_ref[0:1, :], dgq_ref, dq_ref[:, sl] * is_x
            else:
                g, dg_ref = gk_ref[0:1, :], dgk_ref
                dr = dk_ref[:, h * HEAD_DIM - d_q:(h + 1) * HEAD_DIM - d_q]
            dxn = dr * cs + _swap_halves(dr * sn)
            _, vjp = jax.vjp(_norm_g, qkv_ref[:, sl], g)
            dx, dg = vjp(dxn)
            out_ref[:, sl] = dx.astype(BF16)
            dg_ref[0:1, :] += dg
        out_ref[:, d_q + d_kv:] = dv_ref[...].astype(BF16)

    row = lambda w: pl.BlockSpec((tm, w), lambda i: (i, 0))
    vec = pl.BlockSpec((8, HEAD_DIM), lambda i: (0, 0))
    return pl.pallas_call(
        body, grid=(t // tm,),
        in_specs=[row(width), pl.BlockSpec((tm, d_q), lambda i: (jnp.minimum(i, last_q), 0)), row(d_kv), row(d_kv),
                  vec, vec, row(HEAD_DIM), row(HEAD_DIM)],
        out_specs=[row(width), vec, vec],
        out_shape=[jax.ShapeDtypeStruct((t, width), BF16), jax.ShapeDtypeStruct((8, HEAD_DIM), F32),
                   jax.ShapeDtypeStruct((8, HEAD_DIM), F32)],
        compiler_params=_params(("arbitrary",), 40), name=name,
    )(qkv, dq, dk, dv, gq, gk, cos, sin)


ATTN_TQ = (256, 128)
ATTN_TK = (768, 640, 512, 384, 256, 128)


def _attention_fwd(q, k, v_t, *, n_x, name):
    t, d_kv = k.shape
    d_q = q.shape[1]
    kvh = d_kv // HEAD_DIM
    grp = d_q // d_kv
    tq = _div(n_x, ATTN_TQ)
    tk = _div(t, ATTN_TK)
    gw = grp * HEAD_DIM
    n_kv = t // tk
    n_pair = (n_kv - 1) // 2

    def fold8(x):
        return functools.reduce(jnp.add, [x[8 * r:8 * r + 8] for r in range(tk // 8)])

    def body(q_ref, k_ref, vt_ref, o_ref, lse_ref, m_ref, l_ref, acc_ref, qt_ref, s_even, s_odd):
        m_ref[...] = jnp.full_like(m_ref, -jnp.inf)
        l_ref[...] = jnp.zeros_like(l_ref)
        acc_ref[...] = jnp.zeros_like(acc_ref)
        for g in range(grp):
            qt_ref[g] = q_ref[:, g * HEAD_DIM:(g + 1) * HEAD_DIM].T

        def keys(j):
            return pl.ds(pl.multiple_of(j * tk, tk), tk)

        def scores(g, kc):
            return jnp.dot(kc, qt_ref[g], preferred_element_type=F32)

        def chunk(j, s_cur, s_next):
            vt = vt_ref[:, keys(j)]
            kn = k_ref[keys(j + 1), :] if s_next is not None else None
            for g in range(grp):
                s = s_cur[g]
                m_old = m_ref[g]
                m_new = jnp.maximum(m_old, jnp.max(s, axis=0, keepdims=True))
                alpha = jnp.exp2(m_old - m_new)
                p = jnp.exp2(s - m_new)
                if s_next is not None:
                    s_next[g] = scores(g, kn)
                l_ref[g] = alpha * l_ref[g] + fold8(p)
                acc_ref[g] = alpha * acc_ref[g] + jnp.dot(vt, p.astype(BF16), preferred_element_type=F32)
                m_ref[g] = m_new

        k0 = k_ref[keys(0), :]
        for g in range(grp):
            s_even[g] = scores(g, k0)

        def pair(i, carry):
            chunk(2 * i, s_even, s_odd)
            chunk(2 * i + 1, s_odd, s_even)
            return carry

        lax.fori_loop(0, n_pair, pair, 0)
        if n_kv - 2 * n_pair == 2:
            chunk(n_kv - 2, s_even, s_odd)
            chunk(n_kv - 1, s_odd, None)
        else:
            chunk(n_kv - 1, s_even, None)
        for g in range(grp):
            l = jnp.sum(l_ref[g], axis=0, keepdims=True)
            o_ref[:, g * HEAD_DIM:(g + 1) * HEAD_DIM] = (acc_ref[g] / l).T.astype(BF16)
            lse_row = m_ref[g] + jnp.log(l) * LOG2_E
            lse_ref[:, g:g + 1] = jnp.broadcast_to(lse_row, (HEAD_DIM, tq)).T[:, 0:1]

    return pl.pallas_call(
        body, grid=(kvh, n_x // tq),
        in_specs=[pl.BlockSpec((tq, gw), lambda h, i: (i, h)),
                  pl.BlockSpec((t, HEAD_DIM), lambda h, i: (0, h)),
                  pl.BlockSpec((HEAD_DIM, t), lambda h, i: (h, 0))],
        out_specs=[pl.BlockSpec((tq, gw), lambda h, i: (i, h)),
                   pl.BlockSpec((None, tq, grp), lambda h, i: (h, i, 0))],
        out_shape=[jax.ShapeDtypeStruct((n_x, d_q), BF16), jax.ShapeDtypeStruct((kvh, n_x, grp), F32)],
        scratch_shapes=[pltpu.VMEM((grp, 1, tq), F32), pltpu.VMEM((grp, 8, tq), F32),
                        pltpu.VMEM((grp, HEAD_DIM, tq), F32), pltpu.VMEM((grp, HEAD_DIM, tq), BF16),
                        pltpu.VMEM((grp, tk, tq), F32), pltpu.VMEM((grp, tk, tq), F32)],
        compiler_params=_params(("parallel", "arbitrary"), 48), name=name,
    )(q, k, v_t)


def _attention_bwd(q, k, v, o, lse, do, *, n_x, name):
    t, d_kv = k.shape
    d_q = q.shape[1]
    kvh = d_kv // HEAD_DIM
    grp = d_q // d_kv
    tq = _div(n_x, ATTN_TQ)
    tk = _div(t, ATTN_TK)
    gw = grp * HEAD_DIM
    n_q = n_x // tq

    def body(q_ref, k_ref, v_ref, o_ref, lse_ref, do_ref, dq_ref, dkt_ref, dvt_ref, dq_acc, lse_s, delta_s, qt_s,
             dot_s):
        i = pl.program_id(1)
        _zero_accs(i, [dkt_ref, dvt_ref])
        dq_acc[...] = jnp.zeros_like(dq_acc)
        for g in range(grp):
            sl = slice(g * HEAD_DIM, (g + 1) * HEAD_DIM)
            lse_s[g] = lse_ref[:, g:g + 1]
            delta_s[g] = jnp.sum(do_ref[:, sl].astype(F32) * o_ref[:, sl].astype(F32), axis=-1, keepdims=True)
            qt_s[g] = q_ref[:, sl].T
            dot_s[g] = do_ref[:, sl].T

        def step(j, carry):
            start = pl.multiple_of(j * tk, tk)
            kc, vc = k_ref[pl.ds(start, tk), :], v_ref[pl.ds(start, tk), :]
            dkt_part = jnp.zeros((HEAD_DIM, tk), F32)
            dvt_part = jnp.zeros((HEAD_DIM, tk), F32)
            for g in range(grp):
                sl = slice(g * HEAD_DIM, (g + 1) * HEAD_DIM)
                s = lax.dot_general(q_ref[:, sl], kc, (((1,), (1,)), ((), ())), preferred_element_type=F32)
                p = jnp.exp2(s - lse_s[g])
                dp = lax.dot_general(do_ref[:, sl], vc, (((1,), (1,)), ((), ())), preferred_element_type=F32)
                ds = (p * (dp - delta_s[g])).astype(BF16)
                dq_acc[g] += jnp.dot(ds, kc, preferred_element_type=F32)
                dvt_part = dvt_part + jnp.dot(dot_s[g], p.astype(BF16), preferred_element_type=F32)
                dkt_part = dkt_part + jnp.dot(qt_s[g], ds, preferred_element_type=F32)
            dvt_ref[:, pl.ds(start, tk)] += dvt_part
            dkt_ref[:, pl.ds(start, tk)] += dkt_part
            return carry

        lax.fori_loop(0, t // tk, step, 0)
        for g in range(grp):
            dq_ref[:, g * HEAD_DIM:(g + 1) * HEAD_DIM] = dq_acc[g] * ATTN_SCALE

        @pl.when(i == n_q - 1)
        def _():
            dkt_ref[...] = dkt_ref[...] * (1.0 / LOG2_E)

    qspec = pl.BlockSpec((tq, gw), lambda h, i: (i, h))
    kspec = pl.BlockSpec((t, HEAD_DIM), lambda h, i: (0, h))
    ktspec = pl.BlockSpec((HEAD_DIM, t), lambda h, i: (h, 0))
    return pl.pallas_call(
        body, grid=(kvh, n_q),
        in_specs=[qspec, kspec, kspec, qspec, pl.BlockSpec((None, tq, grp), lambda h, i: (h, i, 0)), qspec],
        out_specs=[qspec, ktspec, ktspec],
        out_shape=[jax.ShapeDtypeStruct((n_x, d_q), F32), jax.ShapeDtypeStruct((d_kv, t), F32),
                   jax.ShapeDtypeStruct((d_kv, t), F32)],
        scratch_shapes=[pltpu.VMEM((grp, tq, HEAD_DIM), F32), pltpu.VMEM((grp, tq, 1), F32),
                        pltpu.VMEM((grp, tq, 1), F32), pltpu.VMEM((grp, HEAD_DIM, tq), BF16),
                        pltpu.VMEM((grp, HEAD_DIM, tq), BF16)],
        compiler_params=_params(("parallel", "arbitrary"), 56), name=name,
    )(q, k, v, o, lse, do)


def _whole(body, ins, out_shapes, name):
    return pl.pallas_call(
        body, out_shape=[jax.ShapeDtypeStruct(s, d) for s, d in out_shapes],
        compiler_params=pltpu.CompilerParams(vmem_limit_bytes=40 << 20), name=name)(*ins)


def _silu_rows(x, *, name):
    def body(x_ref, o_ref):
        o_ref[...] = _silu(x_ref[...])
    return _whole(body, [x], [(x.shape, F32)], name)[0]


def _assemble_dmods(gathered, *, name):
    width = gathered.shape[1]

    def body(g_ref, dm0, dm1, db0, db1):
        for l, (dm, db) in enumerate(((dm0, db0), (dm1, db1))):
            ctx = jnp.zeros((1, width), F32)
            tot = jnp.zeros((1, width), F32)
            for q in range(N_DEV):
                row = g_ref[16 * q + 8 * l:16 * q + 8 * l + 1, :]
                dm[q:q + 1, :] = row
                tot = tot + row
                ctx = ctx + g_ref[16 * q + 8 * l + 1:16 * q + 8 * l + 2, :]
            dm[N_DEV:N_DEV + 1, :] = ctx
            dm[N_DEV + 1:, :] = jnp.zeros((16 - N_DEV - 1, width), F32)
            db[...] = jnp.zeros_like(db)
            db[0:1, :] = tot + ctx

    return _whole(body, [gathered], [((16, width), F32), ((16, width), F32), ((8, width), F32), ((8, width), F32)],
                  name)


def _sum_slots(gathered, rows, *, name):
    def body(g_ref, o_ref):
        acc = g_ref[0:rows, :]
        for q in range(1, N_DEV):
            acc = acc + g_ref[q * rows:(q + 1) * rows, :]
        o_ref[...] = acc
    return _whole(body, [gathered], [((rows, gathered.shape[1]), F32)], name)[0]


def _silu_grad(x, dy, *, name):
    def body(x_ref, dy_ref, o_ref):
        _, vjp = jax.vjp(_silu, x_ref[...])
        o_ref[...] = vjp(dy_ref[...])[0]
    return _whole(body, [x, dy], [(x.shape, F32)], name)[0]


def _adamw(w, m, v, *, name, recv=None, grad=None):
    rows, cols = w.shape
    budget = max(8, ADAMW_BLOCK_ELEMS // cols)
    tr = _div(rows, [c for c in (512, 256, 128, 64, 32, 16, 8) if c <= budget] + [rows])
    c1 = 1.0 - ADAM_B1 ** ADAM_STEP
    c2 = 1.0 - ADAM_B2 ** ADAM_STEP

    def body(w_ref, m_ref, v_ref, g_in, g_ref, d_ref, nm_ref, nv_ref):
        if recv is not None:
            g = g_in[0].astype(F32)
            for q in range(1, N_DEV):
                g = g + g_in[q].astype(F32)
        else:
            g = g_in[...]
        nm = ADAM_B1 * m_ref[...] + (1.0 - ADAM_B1) * g
        nv = ADAM_B2 * v_ref[...] + (1.0 - ADAM_B2) * jnp.square(g)
        g_ref[...] = g
        nm_ref[...] = nm
        nv_ref[...] = nv
        d_ref[...] = -ADAM_LR * ((nm / c1) / (jnp.sqrt(nv / c2) + ADAM_EPS) + ADAM_WD * w_ref[...])

    blk = pl.BlockSpec((tr, cols), lambda i: (i, 0))
    g_spec = pl.BlockSpec((N_DEV, tr, cols), lambda i: (0, i, 0)) if recv is not None else blk
    return pl.pallas_call(
        body, grid=(rows // tr,), in_specs=[blk, blk, blk, g_spec], out_specs=[blk] * 4,
        out_shape=[jax.ShapeDtypeStruct((rows, cols), F32)] * 4,
        compiler_params=_params(("parallel",), 48), name=name,
    )(w, m, v, recv if recv is not None else grad)


def _position():
    return tuple(lax.axis_index(a) for a in MESH_AXES)


def _linear(pos):
    return 4 * pos[0] + 2 * pos[1] + pos[2]


def _window(ref, axis, dev, size):
    start = pl.multiple_of(dev * size, size)
    return ref.at[pl.ds(start, size), :] if axis == 0 else ref.at[:, pl.ds(start, size)]


def _all_gather(shards, axes, *, name):
    n = len(shards)
    sizes = [s.shape[ax] for s, ax in zip(shards, axes)]

    def body(*refs):
        src, dst = refs[:n], refs[n:2 * n]
        send_sems, recv_sems, local_sems = refs[2 * n:]
        x, y, c = _position()
        me, sibling = (x, y, c), (x, y, 1 - c)
        chips = [(1 - x, y), (x, 1 - y), (1 - x, 1 - y)]

        def win(k, pos):
            return _window(dst[k], axes[k], _linear(pos), sizes[k])

        def copy(k, sem, block, to, from_src=False):
            return pltpu.make_async_remote_copy(
                src_ref=src[k] if from_src else win(k, block), dst_ref=win(k, block),
                send_sem=send_sems.at[k, sem], recv_sem=recv_sems.at[k, sem],
                device_id=to, device_id_type=MESH_ID)

        mine = [pltpu.make_async_copy(src[k], win(k, me), local_sems.at[k]) for k in range(n)]
        for cp in mine:
            cp.start()
        first = []
        for k in range(n):
            first.append(copy(k, 0, me, sibling, from_src=True))
            first += [copy(k, 1 + j, me, (*chip, c), from_src=True) for j, chip in enumerate(chips)]
        for cp in first:
            cp.start()
        passed = []
        for j, chip in enumerate(chips):
            for k in range(n):
                copy(k, 1 + j, (*chip, c), me).wait_recv()
                fwd = copy(k, 4 + j, (*chip, c), sibling)
                fwd.start()
                passed.append(fwd)
        for k in range(n):
            copy(k, 0, sibling, me).wait_recv()
            for j, chip in enumerate(chips):
                copy(k, 4 + j, (*chip, 1 - c), me).wait_recv()
        for cp in first + passed:
            cp.wait_send()
        for cp in mine:
            cp.wait()

    out_shape = []
    for s, ax in zip(shards, axes):
        full = (s.shape[0] * N_DEV, s.shape[1]) if ax == 0 else (s.shape[0], s.shape[1] * N_DEV)
        out_shape.append(jax.ShapeDtypeStruct(full, s.dtype))
    any_spec = pl.BlockSpec(memory_space=pl.ANY)
    return pl.pallas_call(
        body, in_specs=[any_spec] * n, out_specs=[any_spec] * n, out_shape=out_shape,
        scratch_shapes=[pltpu.SemaphoreType.DMA((n, 7)), pltpu.SemaphoreType.DMA((n, 7)),
                        pltpu.SemaphoreType.DMA((n,))],
        name=name,
    )(*shards)


class Rider:
    def __init__(self, kind, arrays, axes):
        self.kind, self.arrays, self.axes = kind, list(arrays), list(axes)
        self.n = len(self.arrays)
        if kind == "gather":
            self.sizes = [a.shape[ax] for a, ax in zip(self.arrays, self.axes)]
        else:
            self.sizes = [a.shape[ax] // N_DEV for a, ax in zip(self.arrays, self.axes)]

    def out_shape(self):
        shapes = []
        for a, ax, sz in zip(self.arrays, self.axes, self.sizes):
            if self.kind == "gather":
                full = (sz * N_DEV, a.shape[1]) if ax == 0 else (a.shape[0], sz * N_DEV)
                shapes.append(jax.ShapeDtypeStruct(full, a.dtype))
            else:
                shard = (sz, a.shape[1]) if ax == 0 else (a.shape[0], sz)
                shapes.append(jax.ShapeDtypeStruct((N_DEV, *shard), a.dtype))
        return shapes

    def scratch(self):
        return [pltpu.SemaphoreType.DMA((self.n, N_DEV - 1)), pltpu.SemaphoreType.DMA((self.n, N_DEV - 1)),
                pltpu.SemaphoreType.DMA((self.n,))]

    def plan(self, src, dst, send_sems, recv_sems, local_sems):
        x, y, c = me = _position()
        mine = _linear(me)

        def remote(k, sem, src_ref, dst_ref, to):
            return pltpu.make_async_remote_copy(
                src_ref=src_ref, dst_ref=dst_ref, send_sem=send_sems.at[k, sem], recv_sem=recv_sems.at[k, sem],
                device_id=to, device_id_type=MESH_ID)

        ph = dict(local=[], start=[], mid_wait=[], mid_start=[], end_wait=[])
        for k in range(self.n):
            ax, sz = self.axes[k], self.sizes[k]
            if self.kind == "exchange":
                own_src, own_dst = _window(src[k], ax, mine, sz), dst[k].at[mine]
                ph["local"].append(pltpu.make_async_copy(own_src, own_dst, local_sems.at[k]))
                for mask in range(1, N_DEV):
                    to = tuple(1 - p if (mask >> (2 - b)) & 1 else p for b, p in enumerate(me))
                    ph["start"].append(remote(k, mask - 1, _window(src[k], ax, _linear(to), sz), own_dst, to))
                    ph["end_wait"].append(remote(k, mask - 1, own_src, dst[k].at[_linear(to)], to))
            else:
                def win(pos, k=k, ax=ax, sz=sz):
                    return _window(dst[k], ax, _linear(pos), sz)
                sibling = (x, y, 1 - c)
                chips = [(1 - x, y), (x, 1 - y), (1 - x, 1 - y)]
                ph["local"].append(pltpu.make_async_copy(src[k], win(me), local_sems.at[k]))
                ph["start"].append(remote(k, 0, src[k], win(me), sibling))
                ph["end_wait"].append(remote(k, 0, src[k], win(sibling), sibling))
                for j, chip in enumerate(chips):
                    ph["start"].append(remote(k, 1 + j, src[k], win(me), (*chip, c)))
                    ph["mid_wait"].append(remote(k, 1 + j, src[k], win((*chip, c)), (*chip, c)))
                    ph["mid_start"].append(remote(k, 4 + j, win((*chip, c)), win((*chip, c)), sibling))
                    ph["end_wait"].append(remote(k, 4 + j, src[k], win((*chip, 1 - c)), sibling))
        return ph


def _pcall(body, *, grid, in_specs, out_specs, out_shape, scratch_shapes, semantics, vmem_mb, name, operands,
           rider=None):
    if rider is None:
        return pl.pallas_call(body, grid=grid, in_specs=in_specs, out_specs=out_specs, out_shape=out_shape,
                              scratch_shapes=scratch_shapes, compiler_params=_params(semantics, vmem_mb),
                              name=name)(*operands)
    n_in, n_out, n_scr, n = len(in_specs), len(out_specs), len(scratch_shapes), rider.n

    def wrapped(*refs):
        ins, src = refs[:n_in], refs[n_in:n_in + n]
        outs = refs[n_in + n:n_in + n + n_out]
        dst = refs[n_in + n + n_out:n_in + 2 * n + n_out]
        rest = refs[n_in + 2 * n + n_out:]
        scratch, sems = rest[:n_scr], rest[n_scr:]
        step = functools.reduce(lambda acc, ig: acc * ig[1] + ig[0],
                                [(pl.program_id(dim), g) for dim, g in enumerate(grid)], 0)
        n_steps = functools.reduce(lambda a, b: a * b, grid)

        @pl.when(step == 0)
        def _():
            ph = rider.plan(src, dst, *sems)
            for cp in ph["local"] + ph["start"]:
                cp.start()

        body(*ins, *outs, *scratch)

        @pl.when(step == (n_steps * 5) // 8)
        def _():
            ph = rider.plan(src, dst, *sems)
            for cp in ph["mid_wait"]:
                cp.wait_recv()
            for cp in ph["mid_start"]:
                cp.start()

        @pl.when(step == n_steps - 1)
        def _():
            ph = rider.plan(src, dst, *sems)
            for cp in ph["end_wait"]:
                cp.wait_recv()
            for cp in ph["start"] + ph["mid_start"]:
                cp.wait_send()
            for cp in ph["local"]:
                cp.wait()

    any_spec = pl.BlockSpec(memory_space=pl.ANY)
    return pl.pallas_call(
        wrapped, grid=grid, in_specs=list(in_specs) + [any_spec] * n, out_specs=list(out_specs) + [any_spec] * n,
        out_shape=list(out_shape) + rider.out_shape(), scratch_shapes=list(scratch_shapes) + rider.scratch(),
        compiler_params=_params(("arbitrary",) * len(grid), vmem_mb), name=name,
    )(*operands, *rider.arrays)


WEIGHTS = ['c_ctx', 'l0_ada_w', 'l0_ada_b', 'l0_in_w', 'l0_conv_w', 'l0_conv_b', 'l0_conv_ln_g', 'l0_conv_ln_b',
           'l0_pool_w', 'l0_pool_scale', 'l0_out_w', 'l0_mlp_w1', 'l0_mlp_w2', 'l1_ada_w', 'l1_ada_b', 'l1_qkv_w',
           'l1_q_norm_g', 'l1_k_norm_g', 'l1_out_w', 'l1_mlp_w1', 'l1_mlp_w2', 'final_g']
SHARDED = {'l0_in_w': 1, 'l0_out_w': 0, 'l0_mlp_w1': 1, 'l0_mlp_w2': 0,
           'l1_qkv_w': 1, 'l1_out_w': 0, 'l1_mlp_w1': 1, 'l1_mlp_w2': 0}
REPLICATED_SMALL = ['l0_conv_b', 'l0_conv_ln_g', 'l0_conv_ln_b', 'l0_pool_scale', 'l1_q_norm_g', 'l1_k_norm_g',
                    'final_g']


def _row8(v):
    v = v.reshape(1, -1)
    return jnp.pad(v, ((0, 7), (0, 0)))


def _mods16(full, me):
    d = full.shape[1] // 6
    mine = lax.dynamic_slice_in_dim(full, me, 1, axis=0).reshape(6, d)
    ctx = full[N_DEV].reshape(6, d)
    return jnp.pad(jnp.stack([mine, ctx], axis=1).reshape(12, d), ((0, 4), (0, 0)))


def _mlp_fwd(xs, mods, w1, w2, *, n_x, tm, tag, rider1=None, rider2=None):
    t, d = xs.shape
    dff = w1.shape[1]
    h = _rms_mod_fwd(xs, mods, k_shift=3, n_x=n_x, name=f"{tag}_norm2")
    pre, act, *ride1 = _matmul(h, w1, mode="nn", tm=tm, tn=_div(dff, WIDE_TN if tm * WIDE_TN[0] <= WIDE_TILE_ELEMS else WIDE_TN[1:]), tk=d, out_dtypes=[BF16, BF16],
                               name=f"{tag}_mlp1", rider=rider1,
                               epilogue=lambda acc, rows: (acc, jnp.square(jnp.maximum(acc, 0.0))))
    if w2 is None:
        w2 = ride1[-1]
    xo, branch, *ride2 = _matmul(
        act, w2, mode="nn", tm=_div(t, DEEP_TM), tn=_div(d, DEEP_TN), tk=dff, out_dtypes=[F32, BF16],
        name=f"{tag}_mlp2", extras=[("tile", xs), ("vec", mods)], rider=rider2,
        epilogue=lambda acc, rows, res, mv: (res + _seg_pick(mv, 5, rows, n_x) * acc, acc))
    return xo, dict(h=h, pre=pre, act=act, branch=branch, x_in=xs), ride1, ride2


def _exchange_of(items):
    return Rider("exchange", [a for a, _ in items], [ax for _, ax in items]) if items else None


def _mlp_bwd(dxo, dbranch, saved, mods, w1, w2, mixer_branch, *, n_x, tm, tag, ride_dx2=(), ride_dw2=()):
    t, d = dxo.shape
    dff = w1.shape[1]
    tkt = _div(t, TOKEN_TK)
    dpre, *recv_a = _matmul(dbranch, w2, mode="nt", tm=tm, tn=_div(dff, WIDE_TN if tm * WIDE_TN[0] <= WIDE_TILE_ELEMS else WIDE_TN[1:]), tk=d, out_dtypes=[BF16],
                            name=f"{tag}_mlp2_dx", extras=[("tile", saved["pre"])],
                            rider=_exchange_of(list(ride_dx2)),
                            epilogue=lambda acc, rows, pre: (acc * 2.0 * jnp.maximum(pre.astype(F32), 0.0),))
    dw2, *recv_b = _matmul(saved["act"], dbranch, mode="tn", tm=_div(dff, (1024, 512)), tn=_div(d, (1024, 512)),
                           tk=tkt, out_dtypes=[BF16], name=f"{tag}_mlp2_dw", rider=_exchange_of(list(ride_dw2)))
    dh, = _matmul(dpre, w1, mode="nt", tm=_div(t, DEEP_TM), tn=_div(d, DEEP_TN), tk=dff,
                  out_dtypes=[F32], name=f"{tag}_mlp1_dx")
    dw1, recv_dw2 = _matmul(saved["h"], dpre, mode="tn", tm=_div(d, (1024, 512)), tn=_div(dff, (1024, 512)),
                            tk=tkt, out_dtypes=[BF16], name=f"{tag}_mlp1_dw", rider=_exchange_of([(dw2, 0)]))
    dx, dm_norm, dmix, dm_gate = _rms_mod_bwd(saved["x_in"], mods, dh, dxo, k_shift=3, n_x=n_x,
                                              name=f"{tag}_norm2_bwd", gate=(mixer_branch, mods, 2))
    return dx, dw1, dm_norm + dm_gate, dmix, recv_a, recv_b, recv_dw2


def kernel(x, c, ctx, c_ctx, l0_ada_w, l0_ada_b, l0_in_w, l0_conv_w, l0_conv_b, l0_conv_ln_g, l0_conv_ln_b, l0_pool_w, l0_pool_scale, l0_out_w, l0_mlp_w1, l0_mlp_w2, l1_ada_w, l1_ada_b, l1_qkv_w, l1_q_norm_g, l1_k_norm_g, l1_out_w, l1_mlp_w1, l1_mlp_w2, final_g, loss_target, m_c_ctx, m_l0_ada_w, m_l0_ada_b, m_l0_in_w, m_l0_conv_w, m_l0_conv_b, m_l0_conv_ln_g, m_l0_conv_ln_b, m_l0_pool_w, m_l0_pool_scale, m_l0_out_w, m_l0_mlp_w1, m_l0_mlp_w2, m_l1_ada_w, m_l1_ada_b, m_l1_qkv_w, m_l1_q_norm_g, m_l1_k_norm_g, m_l1_out_w, m_l1_mlp_w1, m_l1_mlp_w2, m_final_g, v_c_ctx, v_l0_ada_w, v_l0_ada_b, v_l0_in_w, v_l0_conv_w, v_l0_conv_b, v_l0_conv_ln_g, v_l0_conv_ln_b, v_l0_pool_w, v_l0_pool_scale, v_l0_out_w, v_l0_mlp_w1, v_l0_mlp_w2, v_l1_ada_w, v_l1_ada_b, v_l1_qkv_w, v_l1_q_norm_g, v_l1_k_norm_g, v_l1_out_w, v_l1_mlp_w1, v_l1_mlp_w2, v_final_g):
    p = dict(locals())
    me = _linear(_position())
    n_x, d = x.shape[1], x.shape[2]
    n_ctx = ctx.shape[1]
    t = n_x + n_ctx
    dc = l0_conv_b.shape[0]
    n_tap = l0_conv_w.shape[0]
    d_q = d
    n_mod = l0_ada_b.shape[0] // d
    ada_cols = l0_ada_w.shape[1]
    tm_t = _div(t, (768, 640, 512, 128))
    tm_x = _div(n_x, (1024, 512))

    names = list(SHARDED)
    shard16 = {nm: p[nm].astype(BF16) for nm in names}

    def gather_of(*nms):
        return Rider("gather", [shard16[nm] for nm in nms], [SHARDED[nm] for nm in nms])

    wfull = {}
    n_grp, pg = l0_pool_w.shape[0], l0_pool_w.shape[2]

    c_all = _all_gather([_row8(c)], [0], name="gather_cond")[0].reshape(N_DEV, 8, d)[:, 0]
    cond = jnp.concatenate([c_all, c_ctx.reshape(1, d), jnp.zeros((16 - N_DEV - 1, d), F32)], axis=0)
    s16 = _silu_rows(cond, name="cond_silu")
    mod_shards = []
    for li, (lw, lb) in enumerate(((l0_ada_w, l0_ada_b), (l1_ada_w, l1_ada_b))):
        bias = _row8(lax.dynamic_slice_in_dim(lb, me * ada_cols, ada_cols))
        mod_shards.append(_matmul(s16, lw, mode="nn", tm=16, tn=_div(ada_cols, (512, 384, 256, 128)), tk=d,
                                  out_dtypes=[F32], name=f"l{li}_ada_fwd", extras=[("vec", bias)],
                                  epilogue=lambda acc, rows, b: (acc + b[0:1],))[0])
    mods_full = _all_gather([jnp.concatenate(mod_shards, axis=0)], [1], name="gather_mods")[0]
    mods0, mods1 = _mods16(mods_full[:16], me), _mods16(mods_full[16:], me)

    xs0 = jnp.concatenate([x[0], ctx[0]], axis=0)
    first = Rider("gather", [shard16['l0_in_w'], jnp.pad(l0_conv_w, ((0, 1), (0, 0))), l0_pool_w.reshape(-1, pg)],
                  [SHARDED['l0_in_w'], 1, 0])
    h0, wfull['l0_in_w'], conv_w_full, pool_w_full = _rms_mod_fwd(
        xs0, mods0, k_shift=0, n_x=n_x, name="l0_norm1", rider=first)
    pool_w_full = pool_w_full.reshape(N_DEV, n_grp, pg // N_DEV, pg).transpose(1, 0, 2, 3).reshape(n_grp, pg, pg)
    z, wfull['l0_out_w'] = _matmul(h0, wfull['l0_in_w'], mode="nn", tm=tm_t, tn=_div(3 * dc, (1024, 768, 512, 384)),
                                   tk=d, out_dtypes=[F32], name="l0_in_proj", rider=gather_of('l0_out_w'))
    y0, cv, dsave, wfull['l0_mlp_w1'] = _mixer0_fwd(
        z, conv_w_full, _row8(l0_conv_b), _row8(l0_conv_ln_g), _row8(l0_conv_ln_b), pool_w_full,
        _row8(l0_pool_scale), n_tap=n_tap, n_x=n_x, name="l0_mixer", rider=gather_of('l0_mlp_w1'))
    xs1, mix0, wfull['l1_qkv_w'] = _matmul(
        y0, wfull['l0_out_w'], mode="nn", tm=tm_t, tn=_div(d, (1024, 512)), tk=2 * dc,
        out_dtypes=[F32, BF16], name="l0_out_proj", extras=[("tile", xs0), ("vec", mods0)],
        rider=gather_of('l1_qkv_w'),
        epilogue=lambda acc, rows, res, mv: (res + _seg_pick(mv, 2, rows, n_x) * acc, acc))
    xs2, mlp0, (wfull['l0_mlp_w2'],), (wfull['l1_mlp_w1'], wfull['l1_out_w']) = _mlp_fwd(
        xs1, mods0, wfull['l0_mlp_w1'], None, n_x=n_x, tm=tm_t, tag="l0",
        rider1=gather_of('l0_mlp_w2'), rider2=gather_of('l1_mlp_w1', 'l1_out_w'))

    h2 = _rms_mod_fwd(xs2, mods1, k_shift=0, n_x=n_x, name="l1_norm1")
    qkv, = _matmul(h2, wfull['l1_qkv_w'], mode="nn", tm=tm_t, tn=_div(l1_qkv_w.shape[1] * N_DEV, (1024, 768, 512)),
                   tk=d, out_dtypes=[F32], name="l1_qkv_proj")
    cos, sin = _rope_tables(n_x, n_ctx)
    gq, gk = _row8(l1_q_norm_g), _row8(l1_k_norm_g)
    q, k, v = _qk_prep_fwd(qkv, gq, gk, cos, sin, d_q=d_q, name="l1_qk_prep")
    o, lse = _attention_fwd(q, k, v.T, n_x=n_x, name="l1_attention")
    x3, mix1 = _matmul(o, wfull['l1_out_w'], mode="nn", tm=tm_x, tn=_div(d, (1024, 512)), tk=d_q,
                       out_dtypes=[F32, BF16], name="l1_out_proj", extras=[("tile", xs2), ("vec", mods1)],
                       epilogue=lambda acc, rows, res, mv: (res + mv[4:5] * acc, acc))
    x4, mlp1, (wfull['l1_mlp_w2'],), _ = _mlp_fwd(x3, mods1, wfull['l1_mlp_w1'], None, n_x=n_x, tm=tm_x, tag="l1",
                                                  rider1=gather_of('l1_mlp_w2'))

    dx4, loss_part, dfinal_g, dbranch1, dmods1 = _final_loss(
        x4, _row8(final_g), loss_target[0], mlp1["branch"], mods1, k_gate=5, name="loss_head")
    loss = lax.psum(loss_part[0, 0], MESH_AXES)

    recv = {}
    dx3, dw1_1, dm, dmix1, _, _, recv['l1_mlp_w2'] = _mlp_bwd(
        dx4, dbranch1, mlp1, mods1, wfull['l1_mlp_w1'], wfull['l1_mlp_w2'], mix1, n_x=n_x, tm=tm_x, tag="l1")
    dmods1 = dmods1 + dm
    do, = _matmul(dmix1, wfull['l1_out_w'], mode="nt", tm=tm_x, tn=_div(d_q, (1024, 512)), tk=d,
                  out_dtypes=[BF16], name="l1_out_dx")
    dw_out1, = _matmul(o, dmix1, mode="tn", tm=_div(d_q, (1024, 512)), tn=_div(d, (1024, 512)),
                       tk=_div(n_x, TOKEN_TK), out_dtypes=[BF16], name="l1_out_dw")
    dq, dk_t, dv_t = _attention_bwd(q, k, v, o, lse, do, n_x=n_x, name="l1_attention_bwd")
    dk, dv = dk_t.T, dv_t.T
    dqkv, dgq, dgk = _qk_prep_bwd(qkv, dq, dk, dv, gq, gk, cos, sin, n_x=n_x, name="l1_qk_prep_bwd")
    tkt = _div(t, TOKEN_TK)
    dh2, recv['l1_out_w'] = _matmul(dqkv, wfull['l1_qkv_w'], mode="nt", tm=tm_t, tn=_div(d, (1024, 512)),
                                    tk=dqkv.shape[1], out_dtypes=[F32], name="l1_qkv_dx",
                                    rider=_exchange_of([(dw_out1, SHARDED['l1_out_w'])]))
    dw_qkv, = _matmul(h2, dqkv, mode="tn", tm=_div(d, (1024, 512)), tn=_div(dqkv.shape[1], (1024, 768, 512)),
                      tk=tkt, out_dtypes=[BF16], name="l1_qkv_dw")
    dxs2, dm, dbranch0, dmods0 = _rms_mod_bwd(xs2, mods1, dh2, dx3, k_shift=0, n_x=n_x, name="l1_norm1_bwd",
                                              dres_rows=n_x, gate=(mlp0["branch"], mods0, 5))
    dmods1 = dmods1 + dm

    dxs1, dw1_0, dm, dmix0, (recv['l1_qkv_w'],), (recv['l1_mlp_w1'],), recv['l0_mlp_w2'] = _mlp_bwd(
        dxs2, dbranch0, mlp0, mods0, wfull['l0_mlp_w1'], wfull['l0_mlp_w2'], mix0, n_x=n_x, tm=tm_t, tag="l0",
        ride_dx2=[(dw_qkv, SHARDED['l1_qkv_w'])], ride_dw2=[(dw1_1, SHARDED['l1_mlp_w1'])])
    dmods0 = dmods0 + dm
    dy0, = _matmul(dmix0, wfull['l0_out_w'], mode="nt", tm=tm_t, tn=_div(2 * dc, (1024, 512)), tk=d,
                   out_dtypes=[F32], name="l0_out_dx")
    dw_out0, = _matmul(y0, dmix0, mode="tn", tm=_div(2 * dc, (1024, 512)), tn=_div(d, (1024, 512)),
                       tk=tkt, out_dtypes=[BF16], name="l0_out_dw")
    dz, dconv_w, dconv_b, dln_g, dln_b, dpool_w, dpool_scale, recv['l0_mlp_w1'] = _mixer0_bwd(
        dy0, z, cv, dsave, conv_w_full, _row8(l0_conv_ln_g), _row8(l0_conv_ln_b), pool_w_full,
        _row8(l0_pool_scale), n_tap=n_tap, n_x=n_x, name="l0_mixer_bwd",
        rider=Rider("exchange", [dw1_0], [SHARDED['l0_mlp_w1']]))
    dw_in0, recv['l0_out_w'] = _matmul(h0, dz, mode="tn", tm=_div(d, (1024, 512)),
                                       tn=_div(3 * dc, (1024, 768, 512, 384)), tk=tkt, out_dtypes=[BF16],
                                       name="l0_in_dw", rider=_exchange_of([(dw_out0, SHARDED['l0_out_w'])]))
    dh0, recv['l0_in_w'] = _matmul(dz, wfull['l0_in_w'], mode="nt", tm=tm_t, tn=_div(d, (1024, 512)), tk=3 * dc,
                                   out_dtypes=[F32], name="l0_in_dx",
                                   rider=_exchange_of([(dw_in0, SHARDED['l0_in_w'])]))
    dx0, dm = _rms_mod_bwd(xs0, mods0, dh0, dxs1, k_shift=0, n_x=n_x, name="l0_norm1_bwd", out_rows=n_x)
    dmods0 = dmods0 + dm
    grad_x = dx0[None]

    def dmod_rows(dm16):
        rows = dm16[:2 * n_mod].reshape(n_mod, 2, d).transpose(1, 0, 2).reshape(2, n_mod * d)
        return jnp.pad(rows, ((0, 6), (0, 0)))
    dm_gathered = _all_gather([jnp.concatenate([dmod_rows(dmods0), dmod_rows(dmods1)], axis=0)], [0],
                              name="gather_dmods")[0]
    dm0, dm1, db0, db1 = _assemble_dmods(dm_gathered, name="assemble_dmods")
    out_g = {'l0_ada_b': db0[0], 'l1_ada_b': db1[0]}
    ds_part = jnp.zeros((16, d), F32)
    for nm, lw, dmf in (('l0_ada_w', l0_ada_w, dm0), ('l1_ada_w', l1_ada_w, dm1)):
        dm_cols = lax.dynamic_slice_in_dim(dmf, me * ada_cols, ada_cols, axis=1)
        out_g[nm], = _matmul(s16, dm_cols, mode="tn", tm=_div(d, (1024, 512)),
                             tn=_div(ada_cols, (512, 384, 256, 128)), tk=16, out_dtypes=[F32], name=f"{nm}_dw")
        ds_part = ds_part + _matmul(dm_cols, lw, mode="nt", tm=16, tn=_div(d, (1024, 512)),
                                    tk=_div(ada_cols, (512, 384, 256, 128)), out_dtypes=[F32], name=f"{nm}_dx")[0]

    small = {'l0_conv_b': dconv_b[0], 'l0_conv_ln_g': dln_g[0], 'l0_conv_ln_b': dln_b[0],
             'l0_pool_scale': dpool_scale[0], 'l1_q_norm_g': dgq[0], 'l1_k_norm_g': dgk[0],
             'final_g': dfinal_g[0], 'dsilu_ctx': ds_part[N_DEV], 'l0_conv_w': dconv_w[:-1].reshape(-1),
             'l0_pool_w': dpool_w.reshape(-1)}
    flat = jnp.concatenate([small[nm] for nm in small])
    rows = -(-flat.shape[0] // 1024) * 8
    packed = jnp.pad(flat, (0, rows * 128 - flat.shape[0])).reshape(rows, 128)
    summed = _sum_slots(_all_gather([packed], [0], name="gather_small_grads")[0], rows,
                        name="sum_small_grads").reshape(-1)
    off = 0
    for nm in small:
        size = small[nm].shape[0]
        small[nm] = summed[off:off + size]
        off += size
    out_g['c_ctx'] = _silu_grad(_row8(c_ctx), _row8(small['dsilu_ctx']), name="c_ctx_grad")[0]
    for nm in REPLICATED_SMALL:
        out_g[nm] = small[nm]
    conv_cols = l0_conv_w.shape[1]
    out_g['l0_conv_w'] = lax.dynamic_slice_in_dim(small['l0_conv_w'].reshape(n_tap, dc), me * conv_cols, conv_cols,
                                                  axis=1)
    out_g['l0_pool_w'] = lax.dynamic_slice_in_dim(small['l0_pool_w'].reshape(n_grp, pg, pg), me * (pg // N_DEV),
                                                  pg // N_DEV, axis=1)

    delta, new_m, new_v = {}, {}, {}
    for nm in names:
        out_g[nm], delta[nm], new_m[nm], new_v[nm] = _adamw(p[nm], p['m_' + nm], p['v_' + nm], recv=recv[nm],
                                                            name=f"adamw_{nm}")
    for nm in ('l0_ada_w', 'l1_ada_w'):
        out_g[nm], delta[nm], new_m[nm], new_v[nm] = _adamw(p[nm], p['m_' + nm], p['v_' + nm], grad=out_g[nm],
                                                            name=f"adamw_{nm}")
    for nm in WEIGHTS:
        if nm in delta:
            continue
        shape = p[nm].shape
        as2d = lambda a: a.reshape(1, -1) if a.ndim == 1 else a.reshape(-1, a.shape[-1])
        res = _adamw(as2d(p[nm]), as2d(p['m_' + nm]), as2d(p['v_' + nm]), grad=as2d(out_g[nm]), name=f"adamw_{nm}")
        out_g[nm], delta[nm], new_m[nm], new_v[nm] = [r.reshape(shape) for r in res]

    return (loss, grad_x, *[out_g[nm] for nm in WEIGHTS], *[delta[nm] for nm in WEIGHTS],
            *[new_m[nm] for nm in WEIGHTS], *[new_v[nm] for nm in WEIGHTS])
```
